```python
import math
import jax
import jax.numpy as jnp
from jax import lax
import numpy as np

D_MODEL = 1024
BATCH = 4
SEQ = 8192
DEPTH = 1
DEC_BATCH = 8
DEC_SEQ = 32
PAST_LEN = 2048

CHUNK = 64
D_MIX = D_MODEL
H_A = 8
DQK = 32
DV = 2 * DQK
D_ATT = H_A * DV
SSM_HEADDIM = 64
D_SSM = D_MIX - D_ATT
H_S = D_SSM // SSM_HEADDIM
SSM_GROUPS = 2
HEADS_PER_GROUP = H_S // SSM_GROUPS
D_STATE = 128
CONV_W = 4
CONV_DIM = D_SSM + 2 * SSM_GROUPS * D_STATE
SSD_CHUNK = CHUNK
D_QK_ALL = H_A * 2 * DQK
SPLITS = (D_QK_ALL, 2 * D_QK_ALL, 2 * D_QK_ALL + D_ATT, 2 * D_QK_ALL + D_ATT + D_SSM,
          2 * D_QK_ALL + D_ATT + D_SSM + CONV_DIM)
D_IN_PROJ = 2 * D_QK_ALL + D_ATT + D_SSM + CONV_DIM + H_S
N_BUCKETS = 32
MAX_DISTANCE = 128
Q_BLOCK = 128
N_EXPERTS = 32
TOP_K = 4
D_FF = D_MODEL
SWIGLU_LIMIT = 7.0
SWIGLU_ALPHA = 1.702
MOE_BLOCK = 256
LN_EPS = 1e-5
RMS_EPS = 1e-5
NEG_INF = -1e30
DEEPNORM_ALPHA = (2.0 * DEPTH) ** 0.25
DEEPNORM_BETA = (8.0 * DEPTH) ** -0.25

kernel_name = 'hybrid_diffattn_ssd_moe_stream'


def layer_norm(x, g, b):
    xf = x.astype(jnp.float32)
    mu = jnp.mean(xf, axis=-1, keepdims=True)
    var = jnp.mean(jnp.square(xf - mu), axis=-1, keepdims=True)
    return ((xf - mu) * lax.rsqrt(var + LN_EPS) * g.astype(jnp.float32) + b.astype(jnp.float32)).astype(x.dtype)


def t5_bucket(rel):
    half = N_BUCKETS // 2
    max_exact = half // 2
    n = jnp.abs(rel)
    large = max_exact + (jnp.log(jnp.maximum(n, 1).astype(jnp.float32) / max_exact)
                         / math.log(MAX_DISTANCE / max_exact) * (half - max_exact)).astype(jnp.int32)
    large = jnp.minimum(large, half - 1)
    return jnp.where(rel > 0, half, 0) + jnp.where(n < max_exact, n, large)


def diff_attention_core(q, k, v, q_pos, k_pos, lam, rel_bias):
    b, m, h, _ = q.shape
    n = k.shape[1]
    qf = q.astype(jnp.float32).reshape(b, m, h, 2, DQK) * (DQK ** -0.5)
    kf = k.astype(jnp.float32).reshape(b, n, h, 2, DQK)
    s = jnp.einsum('bqhcd,bkhcd->cbhqk', qf, kf)
    bias = jnp.transpose(rel_bias.astype(jnp.float32)[t5_bucket(k_pos[None, :] - q_pos[:, None])], (2, 0, 1))
    allowed = (k_pos[None, :] // CHUNK) <= (q_pos[:, None] // CHUNK)
    s = jnp.where(allowed, s + bias, NEG_INF)
    p = jax.nn.softmax(s, axis=-1)
    w = p[0] - lam * p[1]
    return jnp.einsum('bhqk,bkhd->bqhd', w, v.astype(jnp.float32))


def diff_attention_prompt(q, k, v, lam, rel_bias):
    b, L, h, dqk = q.shape
    nb = L // Q_BLOCK
    qb = q.reshape(b, nb, Q_BLOCK, h, dqk).transpose(1, 0, 2, 3, 4)
    k_pos = jnp.arange(L, dtype=jnp.int32)

    def one_block(args):
        q_blk, i = args
        q_pos = i * Q_BLOCK + jnp.arange(Q_BLOCK, dtype=jnp.int32)
        return diff_attention_core(q_blk, k, v, q_pos, k_pos, lam, rel_bias)

    o = lax.map(one_block, (qb, jnp.arange(nb, dtype=jnp.int32)))
    return o.transpose(1, 0, 2, 3, 4).reshape(b, L, h, DV)


def causal_conv(xbc, conv_prev, w, bias):
    L = xbc.shape[1]
    xp = jnp.concatenate([conv_prev.astype(xbc.dtype), xbc], axis=1)
    y = sum(xp[:, i:i + L] * w[i] for i in range(CONV_W)) + bias
    return jax.nn.silu(y), xp[:, xp.shape[1] - (CONV_W - 1):]


def ssd_scan(x, dt, a, bm, cm, h0, chunk):
    b, L = x.shape[:2]
    nc = L // chunk
    mask = jnp.tril(jnp.ones((chunk, chunk), dtype=bool))[None, :, :, None, None]

    def to_chunks(t):
        return t.reshape((b, nc, chunk) + t.shape[2:]).swapaxes(0, 1)

    def step(h, inp):
        xc, dtc, bc, cc = inp
        acs = jnp.cumsum(dtc * a, axis=1)
        seg = acs[:, :, None] - acs[:, None, :]
        lmat = jnp.exp(jnp.where(mask, seg, -jnp.inf))
        cb = jnp.einsum('bign,bjgn->bijg', cc, bc)
        y = jnp.einsum('bijg,bijge,bjgep->bigep', cb, lmat, dtc[..., None] * xc)
        y = y + jnp.einsum('bign,bgepn->bigep', cc, h) * jnp.exp(acs)[..., None]
        decay = jnp.exp(acs[:, -1:] - acs) * dtc
        h = jnp.exp(acs[:, -1])[..., None, None] * h + jnp.einsum('bjgn,bjge,bjgep->bgepn', bc, decay, xc)
        return h, y

    h, ys = lax.scan(step, h0, (to_chunks(x), to_chunks(dt), to_chunks(bm), to_chunks(cm)))
    return ys.swapaxes(0, 1).reshape(x.shape), h


def moe_ffn(h, w_router, b_router, w_gu, b_gu, w_down, b_down):
    T = h.shape[0]
    logits = (h @ w_router + b_router).astype(jnp.float32)
    top_logit, top_e = lax.top_k(logits, TOP_K)
    gate = jax.nn.softmax(top_logit, axis=-1)
    n_assign = T * TOP_K
    flat_e = top_e.reshape(-1)
    order = jnp.argsort(flat_e)
    e_sorted = flat_e[order]
    counts = jnp.bincount(flat_e, length=N_EXPERTS)
    padded = (counts + MOE_BLOCK - 1) // MOE_BLOCK * MOE_BLOCK
    pad_end = jnp.cumsum(padded)
    pad_start = pad_end - padded
    start = jnp.cumsum(counts) - counts
    dest = pad_start[e_sorted] + jnp.arange(n_assign, dtype=jnp.int32) - start[e_sorted]
    n_blocks = -(-n_assign // MOE_BLOCK) + N_EXPERTS
    n_rows = n_blocks * MOE_BLOCK
    row_tok = jnp.full((n_rows,), T, jnp.int32).at[dest].set((order // TOP_K).astype(jnp.int32))
    row_gate = jnp.zeros((n_rows,), jnp.float32).at[dest].set(gate.reshape(-1)[order])
    block_start = jnp.arange(n_blocks, dtype=jnp.int32) * MOE_BLOCK
    block_e = jnp.minimum(jnp.searchsorted(pad_end, block_start, side='right'), N_EXPERTS - 1)
    h_pad = jnp.concatenate([h, jnp.zeros((1, h.shape[1]), h.dtype)], axis=0)

    def expert_block(args):
        tok_blk, e = args
        xb = h_pad[tok_blk]
        gu = xb @ w_gu[e] + b_gu[e]
        g, u = gu[:, 0::2], gu[:, 1::2]
        g = jnp.minimum(g, SWIGLU_LIMIT)
        u = jnp.clip(u, -SWIGLU_LIMIT, SWIGLU_LIMIT)
        act = (u + 1.0) * g * jax.nn.sigmoid(SWIGLU_ALPHA * g)
        return act @ w_down[e] + b_down[e]

    ys = lax.map(expert_block, (row_tok.reshape(n_blocks, MOE_BLOCK), block_e)).reshape(n_rows, -1)
    out = jnp.zeros((T + 1, h.shape[1]), jnp.float32).at[row_tok].add(ys.astype(jnp.float32) * row_gate[:, None])
    return out[:T].astype(h.dtype)


def trunk_layer(x, layer, p, rel_bias, k_past, v_past, h0, conv0):
    b, L, _ = x.shape
    lam_init = 0.8 - 0.6 * math.exp(-0.3 * layer)
    proj = jnp.einsum('bld,de->ble', x, p['w_in'])
    q, k, v, z, xbc, dt_raw = jnp.split(proj, SPLITS, axis=-1)
    q = q.reshape(b, L, H_A, 2 * DQK)
    k = k.reshape(b, L, H_A, 2 * DQK)
    v = v.reshape(b, L, H_A, DV)
    lam = (jnp.exp(jnp.sum(p['lq1'].astype(jnp.float32) * p['lk1'].astype(jnp.float32)))
           - jnp.exp(jnp.sum(p['lq2'].astype(jnp.float32) * p['lk2'].astype(jnp.float32))) + lam_init)
    if k_past is None:
        o = diff_attention_prompt(q, k, v, lam, rel_bias)
    else:
        past = k_past.shape[1]
        k_all = jnp.concatenate([k_past.astype(k.dtype), k], axis=1)
        v_all = jnp.concatenate([v_past.astype(v.dtype), v], axis=1)
        o = diff_attention_core(q, k_all, v_all, past + jnp.arange(L, dtype=jnp.int32),
                                jnp.arange(past + L, dtype=jnp.int32), lam, rel_bias)
    o = o * lax.rsqrt(jnp.mean(o * o, axis=-1, keepdims=True) + RMS_EPS) * p['attn_g'].astype(jnp.float32) * (1.0 - lam_init)
    att_out = o.reshape(b, L, D_ATT).astype(x.dtype)
    xbc_act, conv_new = causal_conv(xbc, conv0, p['conv_w'], p['conv_b'])
    xs, bm, cm = jnp.split(xbc_act.astype(jnp.float32), [D_SSM, D_SSM + SSM_GROUPS * D_STATE], axis=-1)
    xs = xs.reshape(b, L, SSM_GROUPS, HEADS_PER_GROUP, SSM_HEADDIM)
    bm = bm.reshape(b, L, SSM_GROUPS, D_STATE)
    cm = cm.reshape(b, L, SSM_GROUPS, D_STATE)
    dt = jax.nn.softplus(dt_raw.astype(jnp.float32) + p['dt_bias'].astype(jnp.float32)).reshape(b, L, SSM_GROUPS, HEADS_PER_GROUP)
    a = -jnp.exp(p['a_log'].astype(jnp.float32)).reshape(SSM_GROUPS, HEADS_PER_GROUP)
    h_init = h0.astype(jnp.float32).reshape(b, SSM_GROUPS, HEADS_PER_GROUP, SSM_HEADDIM, D_STATE)
    y, h_new = ssd_scan(xs, dt, a, bm, cm, h_init, min(SSD_CHUNK, L))
    y = y + p['d_skip'].astype(jnp.float32).reshape(SSM_GROUPS, HEADS_PER_GROUP, 1) * xs
    y = y.reshape(b, L, D_SSM) * jax.nn.silu(z.astype(jnp.float32))
    yg = y.reshape(b, L, SSM_GROUPS, D_SSM // SSM_GROUPS)
    yg = yg * lax.rsqrt(jnp.mean(yg * yg, axis=-1, keepdims=True) + RMS_EPS)
    ssm_out = (yg.reshape(b, L, D_SSM) * p['ssm_g'].astype(jnp.float32)).astype(x.dtype)
    mix = jnp.einsum('ble,ed->bld', jnp.concatenate([att_out, ssm_out], axis=-1), p['w_o'])
    x = layer_norm(DEEPNORM_ALPHA * x + mix, p['ln1_g'], p['ln1_b'])
    ff = moe_ffn(x.reshape(b * L, D_MODEL), p['w_router'], p['b_router'], p['w_gu'], p['b_gu'],
                 p['w_down'], p['b_down']).reshape(b, L, D_MODEL)
    x = layer_norm(DEEPNORM_ALPHA * x + ff, p['ln2_g'], p['ln2_b'])
    return x, k, v, h_new.reshape(b, H_S, SSM_HEADDIM, D_STATE), conv_new


def setup_inputs(seed: int = 0) -> dict:
    key = jax.random.key(seed)
    ks = jax.random.split(key, 32)
    f32 = jnp.float32

    def nrm(k, shape, scale):
        return scale * jax.random.normal(k, shape, f32)

    dt0 = jnp.exp(jax.random.uniform(ks[15], (DEPTH, H_S), f32, math.log(1e-3), math.log(1e-1)))
    return {
        'x_prompt': nrm(ks[0], (BATCH, SEQ, D_MODEL), 1.0),
        'x_sample': nrm(ks[1], (DEC_BATCH, DEC_SEQ, D_MODEL), 1.0),
        'cache_k': nrm(ks[2], (DEPTH, DEC_BATCH, PAST_LEN, H_A, 2 * DQK), 1.0),
        'cache_v': nrm(ks[3], (DEPTH, DEC_BATCH, PAST_LEN, H_A, DV), 1.0),
        'state_ssm': nrm(ks[4], (DEPTH, DEC_BATCH, H_S, SSM_HEADDIM, D_STATE), 0.1),
        'state_conv': nrm(ks[5], (DEPTH, DEC_BATCH, CONV_W - 1, CONV_DIM), 1.0),
        'rel_bias': nrm(ks[6], (N_BUCKETS, H_A), 0.5),
        'w_in': nrm(ks[7], (DEPTH, D_MODEL, D_IN_PROJ), D_MODEL ** -0.5),
        'lambda_q1': nrm(ks[8], (DEPTH, DQK), 0.1),
        'lambda_k1': nrm(ks[9], (DEPTH, DQK), 0.1),
        'lambda_q2': nrm(ks[10], (DEPTH, DQK), 0.1),
        'lambda_k2': nrm(ks[11], (DEPTH, DQK), 0.1),
        'attn_norm_g': 1.0 + nrm(ks[12], (DEPTH, DV), 0.02),
        'conv_w': nrm(ks[13], (DEPTH, CONV_W, CONV_DIM), CONV_W ** -0.5),
        'conv_b': nrm(ks[14], (DEPTH, CONV_DIM), 0.02),
        'dt_bias': dt0 + jnp.log(-jnp.expm1(-dt0)),
        'a_log': jnp.log(jax.random.uniform(ks[16], (DEPTH, H_S), f32, 1.0, 16.0)),
        'd_skip': 1.0 + nrm(ks[17], (DEPTH, H_S), 0.1),
        'ssm_norm_g': 1.0 + nrm(ks[18], (DEPTH, D_SSM), 0.02),
        'w_o': nrm(ks[19], (DEPTH, D_MIX, D_MODEL), D_MIX ** -0.5 * DEEPNORM_BETA),
        'ln1_g': 1.0 + nrm(ks[20], (DEPTH, D_MODEL), 0.02),
        'ln1_b': nrm(ks[21], (DEPTH, D_MODEL), 0.02),
        'w_router': nrm(ks[22], (DEPTH, D_MODEL, N_EXPERTS), D_MODEL ** -0.5),
        'b_router': nrm(ks[23], (DEPTH, N_EXPERTS), 0.01),
        'w_gate_up': nrm(ks[24], (DEPTH, N_EXPERTS, D_MODEL, 2 * D_FF), D_MODEL ** -0.5),
        'b_gate_up': nrm(ks[25], (DEPTH, N_EXPERTS, 2 * D_FF), 0.02),
        'w_down': nrm(ks[26], (DEPTH, N_EXPERTS, D_FF, D_MODEL), D_FF ** -0.5 * DEEPNORM_BETA),
        'b_down': nrm(ks[27], (DEPTH, N_EXPERTS, D_MODEL), 0.02),
        'ln2_g': 1.0 + nrm(ks[28], (DEPTH, D_MODEL), 0.02),
        'ln2_b': nrm(ks[29], (DEPTH, D_MODEL), 0.02),
    }


def reference(x_prompt, x_sample, cache_k, cache_v, state_ssm, state_conv, rel_bias, w_in,
              lambda_q1, lambda_k1, lambda_q2, lambda_k2, attn_norm_g, conv_w, conv_b, dt_bias,
              a_log, d_skip, ssm_norm_g, w_o, ln1_g, ln1_b, w_router, b_router, w_gate_up,
              b_gate_up, w_down, b_down, ln2_g, ln2_b):
    yp, ys = x_prompt, x_sample
    bp = x_prompt.shape[0]
    kp_l, vp_l, hp_l, cp_l, ks_l, vs_l, hs_l, cs_l = [], [], [], [], [], [], [], []
    for l in range(DEPTH):
        p = {'w_in': w_in[l], 'lq1': lambda_q1[l], 'lk1': lambda_k1[l], 'lq2': lambda_q2[l],
             'lk2': lambda_k2[l], 'attn_g': attn_norm_g[l], 'conv_w': conv_w[l], 'conv_b': conv_b[l],
             'dt_bias': dt_bias[l], 'a_log': a_log[l], 'd_skip': d_skip[l], 'ssm_g': ssm_norm_g[l],
             'w_o': w_o[l], 'ln1_g': ln1_g[l], 'ln1_b': ln1_b[l], 'w_router': w_router[l],
             'b_router': b_router[l], 'w_gu': w_gate_up[l], 'b_gu': b_gate_up[l],
             'w_down': w_down[l], 'b_down': b_down[l], 'ln2_g': ln2_g[l], 'ln2_b': ln2_b[l]}
        h0 = jnp.zeros((bp, H_S, SSM_HEADDIM, D_STATE), jnp.float32)
        c0 = jnp.zeros((bp, CONV_W - 1, CONV_DIM), x_prompt.dtype)
        yp, kp, vp, hp, cp = trunk_layer(yp, l, p, rel_bias, None, None, h0, c0)
        ys, ksm, vsm, hsm, csm = trunk_layer(ys, l, p, rel_bias, cache_k[l], cache_v[l], state_ssm[l], state_conv[l])
        kp_l.append(kp); vp_l.append(vp); hp_l.append(hp); cp_l.append(cp)
        ks_l.append(ksm); vs_l.append(vsm); hs_l.append(hsm); cs_l.append(csm)
    k_rows_prompt = jnp.stack(kp_l)
    v_rows_prompt = jnp.stack(vp_l)
    ssm_prompt = jnp.stack(hp_l)
    conv_prompt = jnp.stack(cp_l)
    k_rows_sample = jnp.stack(ks_l)
    v_rows_sample = jnp.stack(vs_l)
    ssm_sample = jnp.stack(hs_l)
    conv_sample = jnp.stack(cs_l)
    return (yp, ys, k_rows_prompt, v_rows_prompt, ssm_prompt, conv_prompt,
            k_rows_sample, v_rows_sample, ssm_sample, conv_sample)
```

```python
import functools
import math

import numpy as np
import jax
import jax.numpy as jnp
from jax import lax
from jax.experimental import pallas as pl
from jax.experimental.pallas import tpu as pltpu

f32 = jnp.float32
bf16 = jnp.bfloat16
i32 = jnp.int32

CHUNK = 64
H_A = 8
DQK = 32
DV = 2 * DQK
D_ATT = H_A * DV
SSM_HEADDIM = 64
H_S = 8
D_SSM = H_S * SSM_HEADDIM
SSM_GROUPS = 2
HEADS_PER_GROUP = H_S // SSM_GROUPS
D_STATE = 128
CONV_W = 4
CONV_DIM = D_SSM + 2 * SSM_GROUPS * D_STATE
D_QK_ALL = H_A * 2 * DQK
N_BUCKETS = 32
MAX_DISTANCE = 128
N_EXPERTS = 32
TOP_K = 4
SWIGLU_LIMIT = 7.0
SWIGLU_ALPHA = 1.702
MOE_BLOCK = 256
LN_EPS = 1e-5
RMS_EPS = 1e-5
NEG_INF = -1e30
DEPTH = 1
DEEPNORM_ALPHA = (2.0 * DEPTH) ** 0.25

LANES = 128
SUBLANES = 8
BF16_ROWS = 16
VMEM_LIMIT = 48 * 1024 * 1024

ROW_TILE = 512
ATT_TILE = 256
SSD_TILE = 256
V_ROWS = DV + BF16_ROWS
COMB_TILE = 256


def _params(semantics):
    return pltpu.CompilerParams(dimension_semantics=semantics, vmem_limit_bytes=VMEM_LIMIT)


def _dot(a, b):
    return jnp.dot(a, b, preferred_element_type=f32)


def _dot_nt(a, b):
    return lax.dot_general(a, b, (((1,), (1,)), ((), ())), preferred_element_type=f32)


def _dot_tn(a, b):
    return lax.dot_general(a, b, (((0,), (0,)), ((), ())), preferred_element_type=f32)


def _split3(a):
    hi = a.astype(bf16)
    r1 = a - hi.astype(f32)
    mid = r1.astype(bf16)
    lo = (r1 - mid.astype(f32)).astype(bf16)
    return hi, mid, lo


def _dot_f32_lhs(a, b_exact):
    hi, mid, lo = _split3(a)
    return _dot(hi, b_exact) + _dot(mid, b_exact) + _dot(lo, b_exact)


def _dot_f32_rhs(a_exact, b):
    hi, mid, lo = _split3(b)
    return _dot(a_exact, hi) + _dot(a_exact, mid) + _dot(a_exact, lo)


def _softplus(x):
    return jnp.maximum(x, 0.0) + jnp.log1p(jnp.exp(-jnp.abs(x)))


def _sigmoid(x):
    return 1.0 / (1.0 + jnp.exp(-x))


def _layer_norm(y, g, b):
    mu = jnp.mean(y, axis=-1, keepdims=True)
    yc = y - mu
    var = jnp.mean(yc * yc, axis=-1, keepdims=True)
    return yc * lax.rsqrt(var + LN_EPS) * g + b


def _lambda(lq1_ref, lk1_ref, lq2_ref, lk2_ref, lam_init):
    s1 = jnp.sum(lq1_ref[...] * lk1_ref[...], axis=-1, keepdims=True)
    s2 = jnp.sum(lq2_ref[...] * lk2_ref[...], axis=-1, keepdims=True)
    return jnp.exp(s1) - jnp.exp(s2) + lam_init


def _in_proj_kernel(x_ref, wk_ref, wv_ref, wz_ref, wx_ref, wdt_ref, wdtT_ref, wq_ref, wvT_ref, *outs, prompt):
    xb = x_ref[...].astype(bf16)
    k_ref, v_ref, z_ref, xbc_ref, dt_ref, dtT_ref = outs[:6]
    k = _dot(xb, wk_ref[...])
    k_ref[...] = k
    v_ref[...] = _dot(xb, wv_ref[...])
    z_ref[...] = _dot(xb, wz_ref[...])
    xbc_ref[...] = _dot(xb, wx_ref[...])
    dt_ref[...] = _dot(xb, wdt_ref[...])
    dtT_ref[...] = _dot_nt(wdtT_ref[...], xb)
    scale = DQK ** -0.5
    if prompt:
        kb_ref, qT_ref, v1T_ref = outs[6:]
        kb_ref[...] = k.astype(bf16)
        qT_ref[...] = (_dot_nt(wq_ref[...], xb) * scale).astype(bf16)
        vT = _dot_nt(wvT_ref[...], xb).astype(bf16)
        ones = jnp.ones((BF16_ROWS, vT.shape[1]), bf16)
        for h in range(H_A):
            v1T_ref[h * V_ROWS:h * V_ROWS + DV, :] = vT[h * DV:(h + 1) * DV, :]
            v1T_ref[h * V_ROWS + DV:(h + 1) * V_ROWS, :] = ones
    else:
        (q_ref,) = outs[6:]
        q_ref[...] = (_dot(xb, wq_ref[...]) * scale).astype(bf16)


def _in_proj(x2d, w, prompt):
    T, D = x2d.shape
    tl = min(ROW_TILE, T)
    grid = (T // tl,)
    row = lambda n: pl.BlockSpec((tl, n), lambda i: (i, 0))
    col = lambda n: pl.BlockSpec((n, tl), lambda i: (0, i))
    full = lambda a: pl.BlockSpec(a.shape, lambda i: (0,) * a.ndim)
    wq = w['wqT'] if prompt else w['wq']
    ins = [x2d, w['wk'], w['wv'], w['wz'], w['wx'], w['wdt'], w['wdtT'], wq, w['wvT']]
    in_specs = [row(D)] + [full(a) for a in ins[1:]]
    out_shape = [jax.ShapeDtypeStruct((T, D_QK_ALL), f32), jax.ShapeDtypeStruct((T, D_ATT), f32),
                 jax.ShapeDtypeStruct((T, D_SSM), f32), jax.ShapeDtypeStruct((T, CONV_DIM), f32),
                 jax.ShapeDtypeStruct((T, LANES), f32), jax.ShapeDtypeStruct((BF16_ROWS, T), f32)]
    out_specs = [row(D_QK_ALL), row(D_ATT), row(D_SSM), row(CONV_DIM), row(LANES), col(BF16_ROWS)]
    if prompt:
        out_shape += [jax.ShapeDtypeStruct((T, D_QK_ALL), bf16), jax.ShapeDtypeStruct((D_QK_ALL, T), bf16),
                      jax.ShapeDtypeStruct((H_A * V_ROWS, T), bf16)]
        out_specs += [row(D_QK_ALL), col(D_QK_ALL), col(H_A * V_ROWS)]
    else:
        out_shape += [jax.ShapeDtypeStruct((T, D_QK_ALL), bf16)]
        out_specs += [row(D_QK_ALL)]
    return pl.pallas_call(
        functools.partial(_in_proj_kernel, prompt=prompt),
        grid=grid, in_specs=in_specs, out_specs=out_specs, out_shape=out_shape,
        compiler_params=_params(("parallel",)),
    )(*ins)


def _t5_bucket(rel):
    half = N_BUCKETS // 2
    max_exact = half // 2
    n = jnp.abs(rel)
    large = max_exact + (jnp.log(jnp.maximum(n, 1).astype(f32) / max_exact)
                         / math.log(MAX_DISTANCE / max_exact) * (half - max_exact)).astype(i32)
    large = jnp.minimum(large, half - 1)
    return jnp.where(rel > 0, half, 0) + jnp.where(n < max_exact, n, large)


def _far_bucket(min_dist):
    half = N_BUCKETS // 2
    max_exact = half // 2
    v = max_exact + int(math.log(min_dist / max_exact) / math.log(MAX_DISTANCE / max_exact) * (half - max_exact))
    return half - 1 if (min_dist >= max_exact and v - 1 >= half - 1) else None


def _prompt_bias_tiles(rel_bias, ta):
    j = jnp.arange(ta, dtype=i32)[:, None]
    i = jnp.arange(ta, dtype=i32)[None, :]
    tiles = []
    for d in range(2):
        rel = j - i - d * ta
        b = rel_bias.astype(f32)[_t5_bucket(rel)]
        allowed = ((j // CHUNK) <= (i // CHUNK)) if d == 0 else jnp.ones((ta, ta), bool)
        tiles.append(jnp.where(allowed[..., None], b, NEG_INF))
    return jnp.transpose(jnp.stack(tiles), (3, 0, 1, 2))


def _sample_bias(rel_bias, past, s):
    q_pos = past + jnp.arange(s, dtype=i32)
    k_pos = jnp.arange(past + s, dtype=i32)
    rel = k_pos[None, :] - q_pos[:, None]
    b = rel_bias.astype(f32)[_t5_bucket(rel)]
    allowed = (k_pos[None, :] // CHUNK) <= (q_pos[:, None] // CHUNK)
    return jnp.transpose(jnp.where(allowed[..., None], b, NEG_INF), (2, 0, 1))


def _attn_prompt_kernel(qi_tab, ki_tab, qT_ref, k_ref, v1T_ref, bias_ref, cfar_ref,
                        lq1_ref, lk1_ref, lq2_ref, lk2_ref, g_ref, o_ref,
                        m_sc, acc_sc, oT_sc, *, lam_init):
    p = pl.program_id(1)
    qi = qi_tab[p]
    ki = ki_tab[p]
    d = qi - ki
    ta = k_ref.shape[0]

    @pl.when(ki == 0)
    def _():
        m_sc[...] = jnp.full(m_sc.shape, NEG_INF, f32)
        acc_sc[...] = jnp.zeros(acc_sc.shape, f32)

    row = lax.broadcasted_iota(i32, (LANES, ta), 0)

    def update(near):
        for g in range(2 * H_A):
            h = g // 2
            hp = h // 2
            gi = g % 4
            kt = k_ref[:, hp * LANES:(hp + 1) * LANES]
            qt = qT_ref[hp * LANES:(hp + 1) * LANES, :]
            qm = jnp.where((row >= gi * DQK) & (row < (gi + 1) * DQK), qt, jnp.zeros_like(qt))
            s = _dot(kt, qm)
            m_old = m_sc[g]
            if near:
                s = s + bias_ref[h, jnp.minimum(d, 1)]
                m_new = jnp.maximum(m_old, jnp.max(s, axis=0, keepdims=True))
                shift = m_new
            else:
                c = cfar_ref[h]
                m_new = jnp.maximum(m_old, jnp.max(s, axis=0, keepdims=True) + c)
                shift = m_new - c
            alpha = jnp.exp(m_old - m_new)
            pT = jnp.exp(s - shift).astype(bf16)
            pv = _dot(v1T_ref[h * V_ROWS:(h + 1) * V_ROWS, :], pT)
            acc_sc[g] = alpha * acc_sc[g] + pv
            m_sc[g] = m_new

    @pl.when(d < 2)
    def _():
        update(True)

    @pl.when(d >= 2)
    def _():
        update(False)

    @pl.when(d == 0)
    def _():
        lam = _lambda(lq1_ref, lk1_ref, lq2_ref, lk2_ref, lam_init)
        for h in range(H_A):
            a0 = acc_sc[2 * h]
            a1 = acc_sc[2 * h + 1]
            o = a0[:DV] / a0[DV:DV + 1] - lam * (a1[:DV] / a1[DV:DV + 1])
            o = o * lax.rsqrt(jnp.mean(o * o, axis=0, keepdims=True) + RMS_EPS) * g_ref[...] * (1.0 - lam_init)
            oT_sc[h * DV:(h + 1) * DV, :] = o
        o_ref[...] = oT_sc[...].T.astype(bf16)


def _attn_prompt(qT, kb, v1T, rel_bias, lam_args, attn_g, B, L, lam_init):
    ta = min(ATT_TILE, L)
    assert L % ta == 0 and ta % CHUNK == 0
    nt = L // ta
    far = _far_bucket(ta + 1)
    assert nt <= 2 or far is not None, "key tiles two or more behind must share one bias bucket"
    far = far if far is not None else N_BUCKETS // 2 - 1
    pairs = [(q, k) for q in range(nt) for k in range(q + 1)]
    qi_tab = jnp.asarray([q for q, _ in pairs], i32)
    ki_tab = jnp.asarray([k for _, k in pairs], i32)
    bias = _prompt_bias_tiles(rel_bias, ta)
    cfar = jnp.broadcast_to(rel_bias.astype(f32)[far][:, None, None], (H_A, 1, ta))
    g_col = attn_g.astype(f32).reshape(DV, 1)
    T = B * L
    full = lambda a: pl.BlockSpec(a.shape, lambda b, p, qt, kt: (0,) * a.ndim)
    grid_spec = pltpu.PrefetchScalarGridSpec(
        num_scalar_prefetch=2, grid=(B, len(pairs)),
        in_specs=[
            pl.BlockSpec((D_QK_ALL, ta), lambda b, p, qt, kt: (0, b * nt + qt[p])),
            pl.BlockSpec((ta, D_QK_ALL), lambda b, p, qt, kt: (b * nt + kt[p], 0)),
            pl.BlockSpec((H_A * V_ROWS, ta), lambda b, p, qt, kt: (0, b * nt + kt[p])),
            full(bias), full(cfar)] + [full(a) for a in lam_args] + [full(g_col)],
        out_specs=pl.BlockSpec((ta, D_ATT), lambda b, p, qt, kt: (b * nt + qt[p], 0)),
        scratch_shapes=[pltpu.VMEM((2 * H_A, 1, ta), f32), pltpu.VMEM((2 * H_A, V_ROWS, ta), f32),
                        pltpu.VMEM((D_ATT, ta), f32)])
    return pl.pallas_call(
        functools.partial(_attn_prompt_kernel, lam_init=lam_init),
        grid_spec=grid_spec, out_shape=jax.ShapeDtypeStruct((T, D_ATT), bf16),
        compiler_params=_params(("parallel", "arbitrary")),
    )(qi_tab, ki_tab, qT, kb, v1T, bias, cfar, *lam_args, g_col)


def _attn_sample_kernel(q_ref, kn_ref, vn_ref, kc_ref, vc_ref, bc_ref, bn_ref,
                        lq1_ref, lk1_ref, lq2_ref, lk2_ref, g2_ref, o_ref, *, lam_init):
    s_len = q_ref.shape[0]
    lane = lax.broadcasted_iota(i32, (s_len, LANES), 1)
    lam = _lambda(lq1_ref, lk1_ref, lq2_ref, lk2_ref, lam_init)
    for hp in range(H_A // 2):
        sl = slice(hp * LANES, (hp + 1) * LANES)
        qt = q_ref[:, sl]
        kct = kc_ref[0, :, sl].astype(bf16)
        vct = vc_ref[0, :, sl].astype(bf16)
        knt = kn_ref[:, sl].astype(bf16)
        vnt = vn_ref[:, sl].astype(bf16)
        outs = []
        for hh in range(2):
            h = 2 * hp + hh
            oc = []
            for c in range(2):
                gi = 2 * hh + c
                qm = jnp.where((lane >= gi * DQK) & (lane < (gi + 1) * DQK), qt, jnp.zeros_like(qt))
                s1 = _dot_nt(qm, kct) + bc_ref[h]
                s2 = _dot_nt(qm, knt) + bn_ref[h]
                m = jnp.maximum(jnp.max(s1, axis=-1, keepdims=True), jnp.max(s2, axis=-1, keepdims=True))
                p1 = jnp.exp(s1 - m)
                p2 = jnp.exp(s2 - m)
                l = jnp.sum(p1, axis=-1, keepdims=True) + jnp.sum(p2, axis=-1, keepdims=True)
                pv = _dot(p1.astype(bf16), vct) + _dot(p2.astype(bf16), vnt)
                oc.append(pv / l)
            o = oc[0] - lam * oc[1]
            in_head = (lane >= hh * DV) & (lane < (hh + 1) * DV)
            ms = jnp.sum(jnp.where(in_head, o * o, 0.0), axis=-1, keepdims=True) * (1.0 / DV)
            outs.append(o * lax.rsqrt(ms + RMS_EPS))
        ot = jnp.where(lane < DV, outs[0], outs[1]) * g2_ref[...] * (1.0 - lam_init)
        o_ref[:, sl] = ot.astype(bf16)


def _attn_sample(q, k_new, v_new, cache_k, cache_v, rel_bias, lam_args, attn_g, lam_init):
    nb, past = cache_k.shape[0], cache_k.shape[1]
    T = q.shape[0]
    s = T // nb
    kc = cache_k.reshape(nb, past, D_QK_ALL)
    vc = cache_v.reshape(nb, past, D_ATT)
    bias = _sample_bias(rel_bias, past, s)
    bc, bn = bias[:, :, :past], bias[:, :, past:]
    g2 = jnp.tile(attn_g.astype(f32), 2).reshape(1, LANES)
    full = lambda a: pl.BlockSpec(a.shape, lambda b: (0,) * a.ndim)
    row = lambda n: pl.BlockSpec((s, n), lambda b: (b, 0))
    return pl.pallas_call(
        functools.partial(_attn_sample_kernel, lam_init=lam_init),
        grid=(nb,),
        in_specs=[row(D_QK_ALL), row(D_QK_ALL), row(D_ATT),
                  pl.BlockSpec((1, past, D_QK_ALL), lambda b: (b, 0, 0)),
                  pl.BlockSpec((1, past, D_ATT), lambda b: (b, 0, 0)),
                  full(bc), full(bn)] + [full(a) for a in lam_args] + [full(g2)],
        out_specs=row(D_ATT), out_shape=jax.ShapeDtypeStruct((T, D_ATT), bf16),
        compiler_params=_params(("parallel",)),
    )(q, k_new, v_new, kc, vc, bc, bn, *lam_args, g2)


def _ssd_kernel(z_ref, xbc_ref, dt_ref, dtT_ref, h0_ref, c0_ref, cw_ref, cb_ref, dtb_ref, dtbT_ref,
                alog_ref, alogT_ref, dsk_ref, g_ref, y_ref, hout_ref, cout_ref, xpad_sc, h_sc):
    q = xbc_ref.shape[0]
    c = pl.program_id(1)
    gw = HEADS_PER_GROUP * SSM_HEADDIM

    @pl.when(c == 0)
    def _():
        xpad_sc[0:SUBLANES] = c0_ref[0]
        h_sc[...] = h0_ref[0]

    xpad_sc[SUBLANES:SUBLANES + q] = xbc_ref[...]
    first = SUBLANES - (CONV_W - 1)
    conv = cb_ref[...]
    for i in range(CONV_W):
        conv = conv + xpad_sc[first + i:first + i + q] * cw_ref[i:i + 1, :]
    tail = xpad_sc[q:q + SUBLANES]
    xpad_sc[0:SUBLANES] = tail
    cout_ref[0] = tail
    act = conv * _sigmoid(conv)
    xs = act[:, :D_SSM]

    lane = lax.broadcasted_iota(i32, (1, LANES), 1)
    a_row = jnp.where(lane < H_S, -jnp.exp(alog_ref[...]), 0.0)
    dt = _softplus(dt_ref[...] + dtb_ref[...])
    ii = lax.broadcasted_iota(i32, (q, q), 0)
    jj = lax.broadcasted_iota(i32, (q, q), 1)
    causal = jj <= ii
    acs = _dot_f32_rhs(causal.astype(bf16), dt * a_row)
    sub = lax.broadcasted_iota(i32, (BF16_ROWS, 1), 0)
    a_col = jnp.where(sub < H_S, -jnp.exp(alogT_ref[...]), 0.0)
    dtT = _softplus(dtT_ref[0] + dtbT_ref[...])
    acsT = _dot_f32_lhs(dtT * a_col, (ii <= jj).astype(bf16))

    er = lax.broadcasted_iota(i32, (LANES, D_SSM), 0)
    ec = lax.broadcasted_iota(i32, (LANES, D_SSM), 1)
    expand = (ec // SSM_HEADDIM == er).astype(bf16)
    dt_x = _dot_f32_lhs(dt, expand)
    acs_x = _dot_f32_lhs(acs, expand)
    e_acs = jnp.exp(acs_x)
    acs_last = acs_x[q - 1:q, :]
    decay = jnp.exp(acs_last - acs_x) * dt_x
    dtx = xs * dt_x
    xd = xs * decay
    glane = lax.broadcasted_iota(i32, (1, gw), 1)

    ys = []
    for g in range(SSM_GROUPS):
        gs = slice(g * gw, (g + 1) * gw)
        bg = act[:, D_SSM + g * D_STATE:D_SSM + (g + 1) * D_STATE].astype(bf16)
        cg = act[:, D_SSM + (SSM_GROUPS + g) * D_STATE:D_SSM + (SSM_GROUPS + g + 1) * D_STATE].astype(bf16)
        cb = _dot_nt(cg, bg)
        h_old = h_sc[g]
        yg = _dot(cg, h_old.astype(bf16)) * e_acs[:, gs]
        dtx_g = dtx[:, gs]
        for e4 in range(HEADS_PER_GROUP):
            e = g * HEADS_PER_GROUP + e4
            seg = acs[:, e:e + 1] - acsT[e:e + 1, :]
            lmat = jnp.exp(jnp.where(causal, seg, -jnp.inf))
            rhs = jnp.where(glane // SSM_HEADDIM == e4, dtx_g, 0.0).astype(bf16)
            yg = yg + _dot((cb * lmat).astype(bf16), rhs)
        h_sc[g] = h_old * jnp.exp(acs_last[:, gs]) + _dot_tn(bg, xd[:, gs].astype(bf16))
        ys.append(yg)
    y = jnp.concatenate(ys, axis=1) + dsk_ref[...] * xs
    zz = z_ref[...]
    y = y * (zz * _sigmoid(zz))
    for g in range(SSM_GROUPS):
        gs = slice(g * gw, (g + 1) * gw)
        yg = y[:, gs]
        r = lax.rsqrt(jnp.mean(yg * yg, axis=-1, keepdims=True) + RMS_EPS)
        y_ref[:, gs] = (yg * r * g_ref[:, gs]).astype(bf16)
    hout_ref[0] = h_sc[...]


def _ssd(z, xbc, dt, dtT, h0, conv0, pw, B, L):
    q = min(SSD_TILE, L)
    nc = L // q
    gw = HEADS_PER_GROUP * SSM_HEADDIM
    dtT3 = dtT.reshape(BF16_ROWS, B, L).transpose(1, 0, 2)
    hT0 = h0.astype(f32).reshape(B, SSM_GROUPS, gw, D_STATE).transpose(0, 1, 3, 2)
    c0 = jnp.pad(conv0.astype(f32), ((0, 0), (SUBLANES - (CONV_W - 1), 0), (0, 0)))
    full = lambda a: pl.BlockSpec(a.shape, lambda b, c: (0,) * a.ndim)
    row = lambda n: pl.BlockSpec((q, n), lambda b, c: (b * nc + c, 0))
    params = [pw['conv_w'], pw['conv_b'], pw['dt_bias'], pw['dt_biasT'], pw['a_log'], pw['a_logT'],
              pw['d_skip'], pw['ssm_g']]
    y, hT, ctail = pl.pallas_call(
        _ssd_kernel, grid=(B, nc),
        in_specs=[row(D_SSM), row(CONV_DIM), row(LANES),
                  pl.BlockSpec((1, BF16_ROWS, q), lambda b, c: (b, 0, c)),
                  pl.BlockSpec((1, SSM_GROUPS, D_STATE, gw), lambda b, c: (b, 0, 0, 0)),
                  pl.BlockSpec((1, SUBLANES, CONV_DIM), lambda b, c: (b, 0, 0))] + [full(a) for a in params],
        out_specs=[row(D_SSM),
                   pl.BlockSpec((1, SSM_GROUPS, D_STATE, gw), lambda b, c: (b, 0, 0, 0)),
                   pl.BlockSpec((1, SUBLANES, CONV_DIM), lambda b, c: (b, 0, 0))],
        out_shape=[jax.ShapeDtypeStruct((B * L, D_SSM), bf16),
                   jax.ShapeDtypeStruct((B, SSM_GROUPS, D_STATE, gw), f32),
                   jax.ShapeDtypeStruct((B, SUBLANES, CONV_DIM), f32)],
        scratch_shapes=[pltpu.VMEM((q + SUBLANES, CONV_DIM), f32), pltpu.VMEM((SSM_GROUPS, D_STATE, gw), f32)],
        compiler_params=_params(("parallel", "arbitrary")),
    )(z, xbc, dt, dtT3, hT0, c0, *params)
    h_new = hT.transpose(0, 1, 3, 2).reshape(B, H_S, SSM_HEADDIM, D_STATE)
    return y, h_new, ctail[:, SUBLANES - (CONV_W - 1):, :]


def _mix_router_kernel(att_ref, ssm_ref, x_ref, woa_ref, wos_ref, g_ref, b_ref, wrh_ref, wrl_ref, br_ref,
                       x1_ref, te_ref, gate_ref):
    mix = _dot(att_ref[...], woa_ref[...]) + _dot(ssm_ref[...], wos_ref[...])
    x1 = _layer_norm(DEEPNORM_ALPHA * x_ref[...] + mix, g_ref[...], b_ref[...])
    x1_ref[...] = x1
    hi = x1.astype(bf16)
    lo = (x1 - hi.astype(f32)).astype(bf16)
    logits = _dot(hi, wrh_ref[...]) + _dot(lo, wrh_ref[...]) + _dot(hi, wrl_ref[...]) + br_ref[...]
    lane = lax.broadcasted_iota(i32, logits.shape, 1)
    vals, idxs = [], []
    for _ in range(TOP_K):
        m = jnp.max(logits, axis=-1, keepdims=True)
        idx = jnp.min(jnp.where(logits == m, lane, LANES), axis=-1, keepdims=True)
        vals.append(m)
        idxs.append(idx)
        logits = jnp.where(lane == idx, -jnp.inf, logits)
    es = [jnp.exp(v - vals[0]) for v in vals]
    tot = es[0]
    for e in es[1:]:
        tot = tot + e
    te = jnp.zeros(logits.shape, i32)
    gate = jnp.zeros(logits.shape, f32)
    for k in range(TOP_K):
        te = jnp.where(lane == k, idxs[k], te)
        gate = jnp.where(lane == k, es[k] / tot, gate)
    te_ref[...] = te
    gate_ref[...] = gate


def _mix_router(att, ssm, x2d, w):
    T, D = x2d.shape
    tl = min(ROW_TILE, T)
    row = lambda n: pl.BlockSpec((tl, n), lambda i: (i, 0))
    full = lambda a: pl.BlockSpec(a.shape, lambda i: (0,) * a.ndim)
    ws = [w['wo_att'], w['wo_ssm'], w['ln1_g'], w['ln1_b'], w['wr_hi'], w['wr_lo'], w['b_router']]
    return pl.pallas_call(
        _mix_router_kernel, grid=(T // tl,),
        in_specs=[row(D_ATT), row(D_SSM), row(D)] + [full(a) for a in ws],
        out_specs=[row(D), row(LANES), row(LANES)],
        out_shape=[jax.ShapeDtypeStruct((T, D), f32), jax.ShapeDtypeStruct((T, LANES), i32),
                   jax.ShapeDtypeStruct((T, LANES), f32)],
        compiler_params=_params(("parallel",)),
    )(att, ssm, x2d, *ws)


def _route(top_e):
    T = top_e.shape[0]
    n_assign = T * TOP_K
    flat_e = top_e.reshape(-1)
    order = jnp.argsort(flat_e)
    e_sorted = flat_e[order]
    counts = jnp.bincount(flat_e, length=N_EXPERTS)
    padded = (counts + MOE_BLOCK - 1) // MOE_BLOCK * MOE_BLOCK
    pad_end = jnp.cumsum(padded)
    pad_start = pad_end - padded
    start = jnp.cumsum(counts) - counts
    dest = (pad_start[e_sorted] + jnp.arange(n_assign, dtype=i32) - start[e_sorted]).astype(i32)
    n_blocks = -(-n_assign // MOE_BLOCK) + N_EXPERTS
    row_tok = jnp.zeros((n_blocks * MOE_BLOCK,), i32).at[dest].set((order // TOP_K).astype(i32))
    slot = jnp.zeros((n_assign,), i32).at[order].set(dest)
    block_start = jnp.arange(n_blocks, dtype=i32) * MOE_BLOCK
    block_e = jnp.minimum(jnp.searchsorted(pad_end, block_start, side='right'), N_EXPERTS - 1).astype(i32)
    n_used = (pad_end[-1] // MOE_BLOCK).astype(i32).reshape(1)
    return row_tok.reshape(n_blocks, MOE_BLOCK), block_e, n_used, slot.reshape(T, TOP_K)


def _row_gather(idx_ref, n, src_hbm, dst, sem):
    def body(r, carry):
        pltpu.make_async_copy(src_hbm.at[pl.ds(idx_ref[0, 0, r], 1)], dst.at[pl.ds(r, 1)], sem).start()
        return carry
    lax.fori_loop(0, n, body, 0, unroll=8)


def _ffn_kernel(be_ref, nu_ref, tok_ref, tokn_ref, x_hbm, wg_ref, wu_ref, bg_ref, bu_ref, wd_ref, bd_ref,
                y_ref, xbuf, sem):
    i = pl.program_id(0)
    n_used = nu_ref[0]
    slot = i % 2

    @pl.when(i == 0)
    def _():
        _row_gather(tok_ref, MOE_BLOCK, x_hbm, xbuf.at[0], sem.at[0])

    @pl.when(i + 1 < n_used)
    def _():
        _row_gather(tokn_ref, MOE_BLOCK, x_hbm, xbuf.at[1 - slot], sem.at[1 - slot])

    @pl.when(i < n_used)
    def _():
        pltpu.make_async_copy(xbuf.at[slot], xbuf.at[slot], sem.at[slot]).wait()
        xb = xbuf[slot].astype(bf16)
        g = _dot(xb, wg_ref[0]) + bg_ref[0]
        u = _dot(xb, wu_ref[0]) + bu_ref[0]
        g = jnp.minimum(g, SWIGLU_LIMIT)
        u = jnp.clip(u, -SWIGLU_LIMIT, SWIGLU_LIMIT)
        act = (u + 1.0) * g * _sigmoid(SWIGLU_ALPHA * g)
        y_ref[...] = _dot(act.astype(bf16), wd_ref[0]) + bd_ref[0]

    @pl.when(i >= n_used)
    def _():
        y_ref[...] = jnp.zeros(y_ref.shape, f32)


def _expert_ffn(x1, row_tok, block_e, n_used, w):
    n_blocks = row_tok.shape[0]
    D = x1.shape[1]
    F = w['w_g'].shape[2]
    last = n_blocks - 1
    grid_spec = pltpu.PrefetchScalarGridSpec(
        num_scalar_prefetch=2, grid=(n_blocks,),
        in_specs=[
            pl.BlockSpec((1, 1, MOE_BLOCK), lambda i, be, nu: (i, 0, 0), memory_space=pltpu.SMEM),
            pl.BlockSpec((1, 1, MOE_BLOCK), lambda i, be, nu: (jnp.minimum(i + 1, last), 0, 0),
                         memory_space=pltpu.SMEM),
            pl.BlockSpec(memory_space=pl.ANY),
            pl.BlockSpec((1, D, F), lambda i, be, nu: (be[i], 0, 0)),
            pl.BlockSpec((1, D, F), lambda i, be, nu: (be[i], 0, 0)),
            pl.BlockSpec((1, 1, F), lambda i, be, nu: (be[i], 0, 0)),
            pl.BlockSpec((1, 1, F), lambda i, be, nu: (be[i], 0, 0)),
            pl.BlockSpec((1, F, D), lambda i, be, nu: (be[i], 0, 0)),
            pl.BlockSpec((1, 1, D), lambda i, be, nu: (be[i], 0, 0))],
        out_specs=pl.BlockSpec((MOE_BLOCK, D), lambda i, be, nu: (i, 0)),
        scratch_shapes=[pltpu.VMEM((2, MOE_BLOCK, D), f32), pltpu.SemaphoreType.DMA((2,))])
    return pl.pallas_call(
        _ffn_kernel, grid_spec=grid_spec,
        out_shape=jax.ShapeDtypeStruct((n_blocks * MOE_BLOCK, D), f32),
        compiler_params=_params(("arbitrary",)),
    )(block_e, n_used, row_tok[:, None, :], row_tok[:, None, :], x1, w['w_g'], w['w_u'], w['b_g'], w['b_u'], w['w_d'], w['b_d'])


def _combine_kernel(slot_ref, slotn_ref, ys_hbm, x1_ref, gate_ref, g_ref, b_ref, y_ref, buf, sem):
    i = pl.program_id(0)
    n = pl.num_programs(0)
    tl = x1_ref.shape[0]
    cur = i % 2

    @pl.when(i == 0)
    def _():
        _row_gather(slot_ref, TOP_K * tl, ys_hbm, buf.at[0], sem.at[0])

    @pl.when(i + 1 < n)
    def _():
        _row_gather(slotn_ref, TOP_K * tl, ys_hbm, buf.at[1 - cur], sem.at[1 - cur])

    pltpu.make_async_copy(buf.at[cur], buf.at[cur], sem.at[cur]).wait()
    gate = gate_ref[...]
    ff = gate[:, 0:1] * buf[cur, 0:tl]
    for k in range(1, TOP_K):
        ff = ff + gate[:, k:k + 1] * buf[cur, k * tl:(k + 1) * tl]
    y_ref[...] = _layer_norm(DEEPNORM_ALPHA * x1_ref[...] + ff, g_ref[...], b_ref[...])


def _combine(ys, slot, x1, gate, ln_g, ln_b):
    T, D = x1.shape
    tl = min(COMB_TILE, T)
    nt = T // tl
    slot_tiles = slot.reshape(nt, tl, TOP_K).transpose(0, 2, 1).reshape(nt, 1, TOP_K * tl)
    row = lambda n: pl.BlockSpec((tl, n), lambda i: (i, 0))
    full = lambda a: pl.BlockSpec(a.shape, lambda i: (0,) * a.ndim)
    return pl.pallas_call(
        _combine_kernel, grid=(nt,),
        in_specs=[pl.BlockSpec((1, 1, TOP_K * tl), lambda i: (i, 0, 0), memory_space=pltpu.SMEM),
                  pl.BlockSpec((1, 1, TOP_K * tl), lambda i: (jnp.minimum(i + 1, nt - 1), 0, 0),
                               memory_space=pltpu.SMEM),
                  pl.BlockSpec(memory_space=pl.ANY), row(D), row(LANES), full(ln_g), full(ln_b)],
        out_specs=row(D), out_shape=jax.ShapeDtypeStruct((T, D), f32),
        scratch_shapes=[pltpu.VMEM((2, TOP_K * tl, D), f32), pltpu.SemaphoreType.DMA((2,))],
        compiler_params=_params(("arbitrary",)),
    )(slot_tiles, slot_tiles, ys, x1, gate, ln_g, ln_b)


def _prep_weights(l, w_in, conv_w, conv_b, dt_bias, a_log, d_skip, ssm_norm_g, w_o, ln1_g, ln1_b,
                  w_router, b_router, w_gate_up, b_gate_up, w_down, b_down, ln2_g, ln2_b):
    wi = w_in[l]
    c0, c1, c2, c3 = D_QK_ALL, 2 * D_QK_ALL, 2 * D_QK_ALL + D_ATT, 2 * D_QK_ALL + D_ATT + D_SSM
    c4 = c3 + CONV_DIM
    wdt = wi[:, c4:c4 + H_S]
    pad_lane = lambda v, fill=0.0: jnp.pad(v.astype(f32).reshape(1, -1), ((0, 0), (0, LANES - v.shape[-1])),
                                            constant_values=fill)
    pad_col = lambda v: jnp.pad(v.astype(f32).reshape(-1, 1), ((0, BF16_ROWS - v.shape[-1]), (0, 0)))
    wr = jnp.pad(w_router[l].astype(f32), ((0, 0), (0, LANES - N_EXPERTS)))
    wr_hi = wr.astype(bf16)
    wgu = w_gate_up[l]
    return {
        'wq': wi[:, :c0].astype(bf16), 'wqT': wi[:, :c0].T.astype(bf16),
        'wk': wi[:, c0:c1].astype(bf16), 'wv': wi[:, c1:c2].astype(bf16), 'wvT': wi[:, c1:c2].T.astype(bf16),
        'wz': wi[:, c2:c3].astype(bf16), 'wx': wi[:, c3:c4].astype(bf16),
        'wdt': jnp.pad(wdt, ((0, 0), (0, LANES - H_S))).astype(bf16),
        'wdtT': jnp.pad(wdt.T, ((0, BF16_ROWS - H_S), (0, 0))).astype(bf16),
        'conv_w': conv_w[l].astype(f32), 'conv_b': conv_b[l].astype(f32).reshape(1, -1),
        'dt_bias': pad_lane(dt_bias[l]), 'dt_biasT': pad_col(dt_bias[l]),
        'a_log': pad_lane(a_log[l]), 'a_logT': pad_col(a_log[l]),
        'd_skip': jnp.repeat(d_skip[l].astype(f32), SSM_HEADDIM).reshape(1, -1),
        'ssm_g': ssm_norm_g[l].astype(f32).reshape(1, -1),
        'wo_att': w_o[l][:D_ATT].astype(bf16), 'wo_ssm': w_o[l][D_ATT:].astype(bf16),
        'ln1_g': ln1_g[l].astype(f32).reshape(1, -1), 'ln1_b': ln1_b[l].astype(f32).reshape(1, -1),
        'wr_hi': wr_hi, 'wr_lo': (wr - wr_hi.astype(f32)).astype(bf16),
        'b_router': pad_lane(b_router[l], NEG_INF),
        'w_g': wgu[:, :, 0::2].astype(bf16), 'w_u': wgu[:, :, 1::2].astype(bf16),
        'b_g': b_gate_up[l][:, None, 0::2].astype(f32), 'b_u': b_gate_up[l][:, None, 1::2].astype(f32),
        'w_d': w_down[l].astype(bf16), 'b_d': b_down[l][:, None, :].astype(f32),
        'ln2_g': ln2_g[l].astype(f32).reshape(1, -1), 'ln2_b': ln2_b[l].astype(f32).reshape(1, -1),
    }


def _trunk_layer(x, l, w, rel_bias, lam_args, attn_g, k_past, v_past, h0, conv0):
    B, L, D = x.shape
    x2d = x.reshape(B * L, D)
    lam_init = 0.8 - 0.6 * math.exp(-0.3 * l)
    prompt = k_past is None
    proj = _in_proj(x2d, w, prompt)
    k, v, z, xbc, dt, dtT = proj[:6]
    if prompt:
        kb, qT, v1T = proj[6:]
        att = _attn_prompt(qT, kb, v1T, rel_bias, lam_args, attn_g, B, L, lam_init)
    else:
        att = _attn_sample(proj[6], k, v, k_past, v_past, rel_bias, lam_args, attn_g, lam_init)
    ssm, h_new, conv_new = _ssd(z, xbc, dt, dtT, h0, conv0, w, B, L)
    x1, top_e, gate = _mix_router(att, ssm, x2d, w)
    row_tok, block_e, n_used, slot = _route(top_e[:, :TOP_K])
    ys = _expert_ffn(x1, row_tok, block_e, n_used, w)
    y = _combine(ys, slot, x1, gate, w['ln2_g'], w['ln2_b'])
    return (y.reshape(B, L, D), k.reshape(B, L, H_A, 2 * DQK), v.reshape(B, L, H_A, DV), h_new, conv_new)


def kernel(x_prompt, x_sample, cache_k, cache_v, state_ssm, state_conv, rel_bias, w_in, lambda_q1, lambda_k1, lambda_q2, lambda_k2, attn_norm_g, conv_w, conv_b, dt_bias, a_log, d_skip, ssm_norm_g, w_o, ln1_g, ln1_b, w_router, b_router, w_gate_up, b_gate_up, w_down, b_down, ln2_g, ln2_b):
    yp, ys = x_prompt, x_sample
    bp = x_prompt.shape[0]
    depth = w_in.shape[0]
    outs = [[] for _ in range(8)]
    for l in range(depth):
        w = _prep_weights(l, w_in, conv_w, conv_b, dt_bias, a_log, d_skip, ssm_norm_g, w_o, ln1_g, ln1_b,
                          w_router, b_router, w_gate_up, b_gate_up, w_down, b_down, ln2_g, ln2_b)
        lam_args = [a[l].astype(f32).reshape(1, -1) for a in (lambda_q1, lambda_k1, lambda_q2, lambda_k2)]
        h0 = jnp.zeros((bp, H_S, SSM_HEADDIM, D_STATE), f32)
        c0 = jnp.zeros((bp, CONV_W - 1, CONV_DIM), f32)
        yp, kp, vp, hp, cp = _trunk_layer(yp, l, w, rel_bias, lam_args, attn_norm_g[l], None, None, h0, c0)
        ys, ks, vs, hs, cs = _trunk_layer(ys, l, w, rel_bias, lam_args, attn_norm_g[l], cache_k[l], cache_v[l],
                                          state_ssm[l], state_conv[l])
        for lst, a in zip(outs, (kp, vp, hp, cp, ks, vs, hs, cs)):
            lst.append(a)
    return (yp, ys) + tuple(jnp.stack(o) for o in outs)
```

```python
import functools
import math

import numpy as np
import jax
import jax.numpy as jnp
from jax import lax
from jax.experimental import pallas as pl
from jax.experimental.pallas import tpu as pltpu

f32 = jnp.float32
bf16 = jnp.bfloat16
i32 = jnp.int32

CHUNK = 64
H_A = 8
DQK = 32
DV = 2 * DQK
D_ATT = H_A * DV
SSM_HEADDIM = 64
H_S = 8
D_SSM = H_S * SSM_HEADDIM
SSM_GROUPS = 2
HEADS_PER_GROUP = H_S // SSM_GROUPS
D_STATE = 128
CONV_W = 4
CONV_DIM = D_SSM + 2 * SSM_GROUPS * D_STATE
D_QK_ALL = H_A * 2 * DQK
N_BUCKETS = 32
MAX_DISTANCE = 128
N_EXPERTS = 32
TOP_K = 4
SWIGLU_LIMIT = 7.0
SWIGLU_ALPHA = 1.702
MOE_BLOCK = 256
LN_EPS = 1e-5
RMS_EPS = 1e-5
NEG_INF = -1e30
DEPTH = 1
DEEPNORM_ALPHA = (2.0 * DEPTH) ** 0.25
LOG2E = math.log2(math.e)

LANES = 128
SUBLANES = 8
BF16_ROWS = 16
VMEM_LIMIT = 48 * 1024 * 1024

ROW_TILE = 512
ATT_TILE = 256
SSD_TILE = 256
V_ROWS = DV + BF16_ROWS
COMB_TILE = 256


def _params(semantics):
    return pltpu.CompilerParams(dimension_semantics=semantics, vmem_limit_bytes=VMEM_LIMIT)


def _dot(a, b):
    return jnp.dot(a, b, preferred_element_type=f32)


def _dot_nt(a, b):
    return lax.dot_general(a, b, (((1,), (1,)), ((), ())), preferred_element_type=f32)


def _dot_tn(a, b):
    return lax.dot_general(a, b, (((0,), (0,)), ((), ())), preferred_element_type=f32)


def _split3(a):
    hi = a.astype(bf16)
    r1 = a - hi.astype(f32)
    mid = r1.astype(bf16)
    lo = (r1 - mid.astype(f32)).astype(bf16)
    return hi, mid, lo


def _dot_f32_lhs(a, b_exact):
    hi, mid, lo = _split3(a)
    return _dot(hi, b_exact) + _dot(mid, b_exact) + _dot(lo, b_exact)


def _dot_f32_rhs(a_exact, b):
    hi, mid, lo = _split3(b)
    return _dot(a_exact, hi) + _dot(a_exact, mid) + _dot(a_exact, lo)


def _softplus(x):
    return jnp.maximum(x, 0.0) + jnp.log1p(jnp.exp(-jnp.abs(x)))


def _sigmoid(x):
    return 1.0 / (1.0 + jnp.exp(-x))


def _layer_norm(y, g, b):
    mu = jnp.mean(y, axis=-1, keepdims=True)
    yc = y - mu
    var = jnp.mean(yc * yc, axis=-1, keepdims=True)
    return yc * lax.rsqrt(var + LN_EPS) * g + b


def _lambda(lq1_ref, lk1_ref, lq2_ref, lk2_ref, lam_init):
    s1 = jnp.sum(lq1_ref[...] * lk1_ref[...], axis=-1, keepdims=True)
    s2 = jnp.sum(lq2_ref[...] * lk2_ref[...], axis=-1, keepdims=True)
    return jnp.exp(s1) - jnp.exp(s2) + lam_init


def _in_proj_kernel(x_ref, wk_ref, wv_ref, wz_ref, wx_ref, wdt_ref, wdtT_ref, wq_ref, wvT_ref, *outs, prompt):
    xb = x_ref[...].astype(bf16)
    k_ref, v_ref, z_ref, xbc_ref, dt_ref, dtT_ref = outs[:6]
    k = _dot(xb, wk_ref[...])
    k_ref[...] = k
    v_ref[...] = _dot(xb, wv_ref[...])
    z_ref[...] = _dot(xb, wz_ref[...])
    xbc_ref[...] = _dot(xb, wx_ref[...])
    dt_ref[...] = _dot(xb, wdt_ref[...])
    dtT_ref[...] = _dot_nt(wdtT_ref[...], xb)
    scale = DQK ** -0.5
    if prompt:
        kb_ref, qT_ref, v1T_ref = outs[6:]
        kb_ref[...] = k.astype(bf16)
        qT_ref[...] = (_dot_nt(wq_ref[...], xb) * (scale * LOG2E)).astype(bf16)
        vT = _dot_nt(wvT_ref[...], xb).astype(bf16)
        ones = jnp.ones((BF16_ROWS, vT.shape[1]), bf16)
        for h in range(H_A):
            v1T_ref[h * V_ROWS:h * V_ROWS + DV, :] = vT[h * DV:(h + 1) * DV, :]
            v1T_ref[h * V_ROWS + DV:(h + 1) * V_ROWS, :] = ones
    else:
        (q_ref,) = outs[6:]
        q_ref[...] = (_dot(xb, wq_ref[...]) * scale).astype(bf16)


def _in_proj(x2d, w, prompt):
    T, D = x2d.shape
    tl = min(ROW_TILE, T)
    grid = (T // tl,)
    row = lambda n: pl.BlockSpec((tl, n), lambda i: (i, 0))
    col = lambda n: pl.BlockSpec((n, tl), lambda i: (0, i))
    full = lambda a: pl.BlockSpec(a.shape, lambda i: (0,) * a.ndim)
    wq = w['wqT'] if prompt else w['wq']
    ins = [x2d, w['wk'], w['wv'], w['wz'], w['wx'], w['wdt'], w['wdtT'], wq, w['wvT']]
    in_specs = [row(D)] + [full(a) for a in ins[1:]]
    out_shape = [jax.ShapeDtypeStruct((T, D_QK_ALL), f32), jax.ShapeDtypeStruct((T, D_ATT), f32),
                 jax.ShapeDtypeStruct((T, D_SSM), f32), jax.ShapeDtypeStruct((T, CONV_DIM), f32),
                 jax.ShapeDtypeStruct((T, LANES), f32), jax.ShapeDtypeStruct((BF16_ROWS, T), f32)]
    out_specs = [row(D_QK_ALL), row(D_ATT), row(D_SSM), row(CONV_DIM), row(LANES), col(BF16_ROWS)]
    if prompt:
        out_shape += [jax.ShapeDtypeStruct((T, D_QK_ALL), bf16), jax.ShapeDtypeStruct((D_QK_ALL, T), bf16),
                      jax.ShapeDtypeStruct((H_A * V_ROWS, T), bf16)]
        out_specs += [row(D_QK_ALL), col(D_QK_ALL), col(H_A * V_ROWS)]
    else:
        out_shape += [jax.ShapeDtypeStruct((T, D_QK_ALL), bf16)]
        out_specs += [row(D_QK_ALL)]
    return pl.pallas_call(
        functools.partial(_in_proj_kernel, prompt=prompt),
        grid=grid, in_specs=in_specs, out_specs=out_specs, out_shape=out_shape,
        compiler_params=_params(("parallel",)),
    )(*ins)


def _t5_bucket(rel):
    half = N_BUCKETS // 2
    max_exact = half // 2
    n = jnp.abs(rel)
    large = max_exact + (jnp.log(jnp.maximum(n, 1).astype(f32) / max_exact)
                         / math.log(MAX_DISTANCE / max_exact) * (half - max_exact)).astype(i32)
    large = jnp.minimum(large, half - 1)
    return jnp.where(rel > 0, half, 0) + jnp.where(n < max_exact, n, large)


def _far_bucket(min_dist):
    half = N_BUCKETS // 2
    max_exact = half // 2
    v = max_exact + int(math.log(min_dist / max_exact) / math.log(MAX_DISTANCE / max_exact) * (half - max_exact))
    return half - 1 if (min_dist >= max_exact and v - 1 >= half - 1) else None


def _prompt_bias_tiles(rel_bias, ta):
    j = jnp.arange(ta, dtype=i32)[:, None]
    i = jnp.arange(ta, dtype=i32)[None, :]
    tiles = []
    for d in range(2):
        rel = j - i - d * ta
        b = rel_bias.astype(f32)[_t5_bucket(rel)]
        allowed = ((j // CHUNK) <= (i // CHUNK)) if d == 0 else jnp.ones((ta, ta), bool)
        tiles.append(jnp.where(allowed[..., None], b, NEG_INF))
    return jnp.transpose(jnp.stack(tiles), (3, 0, 1, 2))


def _sample_bias(rel_bias, past, s):
    q_pos = past + jnp.arange(s, dtype=i32)
    k_pos = jnp.arange(past + s, dtype=i32)
    rel = k_pos[None, :] - q_pos[:, None]
    b = rel_bias.astype(f32)[_t5_bucket(rel)]
    allowed = (k_pos[None, :] // CHUNK) <= (q_pos[:, None] // CHUNK)
    return jnp.transpose(jnp.where(allowed[..., None], b, NEG_INF), (2, 0, 1))


def _attn_prompt_kernel(qi_tab, ki_tab, qT_ref, k_ref, v1T_ref, bias_ref, cfar_ref,
                        lq1_ref, lk1_ref, lq2_ref, lk2_ref, g_ref, o_ref,
                        qm_sc, m_sc, acc_sc, oT_sc, *, lam_init):
    p = pl.program_id(1)
    qi = qi_tab[p]
    ki = ki_tab[p]
    d = qi - ki
    ta = k_ref.shape[0]
    n_hp = H_A // 2

    @pl.when(ki == 0)
    def _():
        m_sc[...] = jnp.full(m_sc.shape, NEG_INF, f32)
        acc_sc[...] = jnp.zeros(acc_sc.shape, f32)
        row = lax.broadcasted_iota(i32, (LANES, ta), 0)
        for hp in range(n_hp):
            qt = qT_ref[hp * LANES:(hp + 1) * LANES, :]
            for gi in range(4):
                keep = (row >= gi * DQK) & (row < (gi + 1) * DQK)
                qm_sc[hp, :, gi * ta:(gi + 1) * ta] = jnp.where(keep, qt, jnp.zeros_like(qt))

    def update(near):
        for hp in range(n_hp):
            s = _dot(k_ref[:, hp * LANES:(hp + 1) * LANES], qm_sc[hp])
            m_old = m_sc[hp]
            if near:
                dd = jnp.minimum(d, 1)
                b0 = bias_ref[2 * hp, dd]
                b1 = bias_ref[2 * hp + 1, dd]
                s = s + jnp.concatenate([b0, b0, b1, b1], axis=1)
                m_new = jnp.maximum(m_old, jnp.max(s, axis=0, keepdims=True))
                shift = m_new
            else:
                c = cfar_ref[hp]
                m_new = jnp.maximum(m_old, jnp.max(s, axis=0, keepdims=True) + c)
                shift = m_new - c
            alpha = jnp.exp2(m_old - m_new)
            pT = jnp.exp2(s - shift).astype(bf16)
            for hh in range(2):
                h = 2 * hp + hh
                cols = slice(hh * 2 * ta, (hh + 1) * 2 * ta)
                pv = _dot(v1T_ref[h * V_ROWS:(h + 1) * V_ROWS, :], pT[:, cols])
                acc_sc[h] = alpha[:, cols] * acc_sc[h] + pv
            m_sc[hp] = m_new

    @pl.when(d < 2)
    def _():
        update(True)

    @pl.when(d >= 2)
    def _():
        update(False)

    @pl.when(d == 0)
    def _():
        lam = _lambda(lq1_ref, lk1_ref, lq2_ref, lk2_ref, lam_init)
        for h in range(H_A):
            a = acc_sc[h]
            a0 = a[:, :ta]
            a1 = a[:, ta:]
            o = a0[:DV] / a0[DV:DV + 1] - lam * (a1[:DV] / a1[DV:DV + 1])
            o = o * lax.rsqrt(jnp.mean(o * o, axis=0, keepdims=True) + RMS_EPS) * g_ref[...] * (1.0 - lam_init)
            oT_sc[h * DV:(h + 1) * DV, :] = o
        o_ref[...] = oT_sc[...].T.astype(bf16)


def _attn_prompt(qT, kb, v1T, rel_bias, lam_args, attn_g, B, L, lam_init):
    ta = min(ATT_TILE, L)
    assert L % ta == 0 and ta % CHUNK == 0
    nt = L // ta
    far = _far_bucket(ta + 1)
    assert nt <= 2 or far is not None, "key tiles two or more behind must share one bias bucket"
    far = far if far is not None else N_BUCKETS // 2 - 1
    pairs = [(q, k) for q in range(nt) for k in range(q + 1)]
    qi_tab = jnp.asarray([q for q, _ in pairs], i32)
    ki_tab = jnp.asarray([k for _, k in pairs], i32)
    bias = _prompt_bias_tiles(rel_bias, ta) * LOG2E
    cfar = jnp.repeat((rel_bias.astype(f32)[far] * LOG2E).reshape(H_A // 2, 2), 2 * ta, axis=1)[:, None, :]
    g_col = attn_g.astype(f32).reshape(DV, 1)
    T = B * L
    n_hp = H_A // 2
    full = lambda a: pl.BlockSpec(a.shape, lambda b, p, qt, kt: (0,) * a.ndim)
    grid_spec = pltpu.PrefetchScalarGridSpec(
        num_scalar_prefetch=2, grid=(B, len(pairs)),
        in_specs=[
            pl.BlockSpec((D_QK_ALL, ta), lambda b, p, qt, kt: (0, b * nt + qt[p])),
            pl.BlockSpec((ta, D_QK_ALL), lambda b, p, qt, kt: (b * nt + kt[p], 0)),
            pl.BlockSpec((H_A * V_ROWS, ta), lambda b, p, qt, kt: (0, b * nt + kt[p])),
            full(bias), full(cfar)] + [full(a) for a in lam_args] + [full(g_col)],
        out_specs=pl.BlockSpec((ta, D_ATT), lambda b, p, qt, kt: (b * nt + qt[p], 0)),
        scratch_shapes=[pltpu.VMEM((n_hp, LANES, 4 * ta), bf16), pltpu.VMEM((n_hp, 1, 4 * ta), f32),
                        pltpu.VMEM((H_A, V_ROWS, 2 * ta), f32), pltpu.VMEM((D_ATT, ta), f32)])
    return pl.pallas_call(
        functools.partial(_attn_prompt_kernel, lam_init=lam_init),
        grid_spec=grid_spec, out_shape=jax.ShapeDtypeStruct((T, D_ATT), bf16),
        compiler_params=_params(("parallel", "arbitrary")),
    )(qi_tab, ki_tab, qT, kb, v1T, bias, cfar, *lam_args, g_col)


def _attn_sample_kernel(q_ref, kn_ref, vn_ref, kc_ref, vc_ref, bc_ref, bn_ref,
                        lq1_ref, lk1_ref, lq2_ref, lk2_ref, g2_ref, o_ref, *, lam_init):
    s_len = q_ref.shape[0]
    lane = lax.broadcasted_iota(i32, (s_len, LANES), 1)
    lam = _lambda(lq1_ref, lk1_ref, lq2_ref, lk2_ref, lam_init)
    for hp in range(H_A // 2):
        sl = slice(hp * LANES, (hp + 1) * LANES)
        qt = q_ref[:, sl]
        kct = kc_ref[0, :, sl].astype(bf16)
        vct = vc_ref[0, :, sl].astype(bf16)
        knt = kn_ref[:, sl].astype(bf16)
        vnt = vn_ref[:, sl].astype(bf16)
        outs = []
        for hh in range(2):
            h = 2 * hp + hh
            oc = []
            for c in range(2):
                gi = 2 * hh + c
                qm = jnp.where((lane >= gi * DQK) & (lane < (gi + 1) * DQK), qt, jnp.zeros_like(qt))
                s1 = _dot_nt(qm, kct) + bc_ref[h]
                s2 = _dot_nt(qm, knt) + bn_ref[h]
                m = jnp.maximum(jnp.max(s1, axis=-1, keepdims=True), jnp.max(s2, axis=-1, keepdims=True))
                p1 = jnp.exp(s1 - m)
                p2 = jnp.exp(s2 - m)
                l = jnp.sum(p1, axis=-1, keepdims=True) + jnp.sum(p2, axis=-1, keepdims=True)
                pv = _dot(p1.astype(bf16), vct) + _dot(p2.astype(bf16), vnt)
                oc.append(pv / l)
            o = oc[0] - lam * oc[1]
            in_head = (lane >= hh * DV) & (lane < (hh + 1) * DV)
            ms = jnp.sum(jnp.where(in_head, o * o, 0.0), axis=-1, keepdims=True) * (1.0 / DV)
            outs.append(o * lax.rsqrt(ms + RMS_EPS))
        ot = jnp.where(lane < DV, outs[0], outs[1]) * g2_ref[...] * (1.0 - lam_init)
        o_ref[:, sl] = ot.astype(bf16)


def _attn_sample(q, k_new, v_new, cache_k, cache_v, rel_bias, lam_args, attn_g, lam_init):
    nb, past = cache_k.shape[0], cache_k.shape[1]
    T = q.shape[0]
    s = T // nb
    kc = cache_k.reshape(nb, past, D_QK_ALL)
    vc = cache_v.reshape(nb, past, D_ATT)
    bias = _sample_bias(rel_bias, past, s)
    bc, bn = bias[:, :, :past], bias[:, :, past:]
    g2 = jnp.tile(attn_g.astype(f32), 2).reshape(1, LANES)
    full = lambda a: pl.BlockSpec(a.shape, lambda b: (0,) * a.ndim)
    row = lambda n: pl.BlockSpec((s, n), lambda b: (b, 0))
    return pl.pallas_call(
        functools.partial(_attn_sample_kernel, lam_init=lam_init),
        grid=(nb,),
        in_specs=[row(D_QK_ALL), row(D_QK_ALL), row(D_ATT),
                  pl.BlockSpec((1, past, D_QK_ALL), lambda b: (b, 0, 0)),
                  pl.BlockSpec((1, past, D_ATT), lambda b: (b, 0, 0)),
                  full(bc), full(bn)] + [full(a) for a in lam_args] + [full(g2)],
        out_specs=row(D_ATT), out_shape=jax.ShapeDtypeStruct((T, D_ATT), bf16),
        compiler_params=_params(("parallel",)),
    )(q, k_new, v_new, kc, vc, bc, bn, *lam_args, g2)


def _ssd_kernel(z_ref, xbc_ref, dt_ref, dtT_ref, h0_ref, c0_ref, cw_ref, cb_ref, dtb_ref, dtbT_ref,
                alog_ref, alogT_ref, dsk_ref, g_ref, y_ref, hout_ref, cout_ref, xpad_sc, h_sc):
    q = xbc_ref.shape[0]
    c = pl.program_id(1)
    gw = HEADS_PER_GROUP * SSM_HEADDIM

    @pl.when(c == 0)
    def _():
        xpad_sc[0:SUBLANES] = c0_ref[0]
        h_sc[...] = h0_ref[0]

    xpad_sc[SUBLANES:SUBLANES + q] = xbc_ref[...]
    first = SUBLANES - (CONV_W - 1)
    conv = cb_ref[...]
    for i in range(CONV_W):
        conv = conv + xpad_sc[first + i:first + i + q] * cw_ref[i:i + 1, :]
    tail = xpad_sc[q:q + SUBLANES]
    xpad_sc[0:SUBLANES] = tail
    cout_ref[0] = tail
    act = conv * _sigmoid(conv)
    xs = act[:, :D_SSM]

    lane = lax.broadcasted_iota(i32, (1, LANES), 1)
    a_row = jnp.where(lane < H_S, -jnp.exp(alog_ref[...]), 0.0)
    dt = _softplus(dt_ref[...] + dtb_ref[...])
    ii = lax.broadcasted_iota(i32, (q, q), 0)
    jj = lax.broadcasted_iota(i32, (q, q), 1)
    causal = jj <= ii
    acs = _dot_f32_rhs(causal.astype(bf16), dt * a_row)
    sub = lax.broadcasted_iota(i32, (BF16_ROWS, 1), 0)
    a_col = jnp.where(sub < H_S, -jnp.exp(alogT_ref[...]), 0.0)
    dtT = _softplus(dtT_ref[0] + dtbT_ref[...])
    acsT = _dot_f32_lhs(dtT * a_col, (ii <= jj).astype(bf16))

    er = lax.broadcasted_iota(i32, (LANES, D_SSM), 0)
    ec = lax.broadcasted_iota(i32, (LANES, D_SSM), 1)
    expand = (ec // SSM_HEADDIM == er).astype(bf16)
    dt_x = _dot_f32_lhs(dt, expand)
    acs_x = _dot_f32_lhs(acs, expand)
    e_acs = jnp.exp(acs_x)
    acs_last = acs_x[q - 1:q, :]
    decay = jnp.exp(acs_last - acs_x) * dt_x
    dtx = xs * dt_x
    xd = xs * decay
    glane = lax.broadcasted_iota(i32, (1, gw), 1)

    ys = []
    for g in range(SSM_GROUPS):
        gs = slice(g * gw, (g + 1) * gw)
        bg = act[:, D_SSM + g * D_STATE:D_SSM + (g + 1) * D_STATE].astype(bf16)
        cg = act[:, D_SSM + (SSM_GROUPS + g) * D_STATE:D_SSM + (SSM_GROUPS + g + 1) * D_STATE].astype(bf16)
        cb = _dot_nt(cg, bg)
        h_old = h_sc[g]
        yg = _dot(cg, h_old.astype(bf16)) * e_acs[:, gs]
        dtx_g = dtx[:, gs]
        for e4 in range(HEADS_PER_GROUP):
            e = g * HEADS_PER_GROUP + e4
            seg = acs[:, e:e + 1] - acsT[e:e + 1, :]
            lmat = jnp.exp(jnp.where(causal, seg, -jnp.inf))
            rhs = jnp.where(glane // SSM_HEADDIM == e4, dtx_g, 0.0).astype(bf16)
            yg = yg + _dot((cb * lmat).astype(bf16), rhs)
        h_sc[g] = h_old * jnp.exp(acs_last[:, gs]) + _dot_tn(bg, xd[:, gs].astype(bf16))
        ys.append(yg)
    y = jnp.concatenate(ys, axis=1) + dsk_ref[...] * xs
    zz = z_ref[...]
    y = y * (zz * _sigmoid(zz))
    for g in range(SSM_GROUPS):
        gs = slice(g * gw, (g + 1) * gw)
        yg = y[:, gs]
        r = lax.rsqrt(jnp.mean(yg * yg, axis=-1, keepdims=True) + RMS_EPS)
        y_ref[:, gs] = (yg * r * g_ref[:, gs]).astype(bf16)
    hout_ref[0] = h_sc[...]


def _ssd(z, xbc, dt, dtT, h0, conv0, pw, B, L):
    q = min(SSD_TILE, L)
    nc = L // q
    gw = HEADS_PER_GROUP * SSM_HEADDIM
    dtT3 = dtT.reshape(BF16_ROWS, B, L).transpose(1, 0, 2)
    hT0 = h0.astype(f32).reshape(B, SSM_GROUPS, gw, D_STATE).transpose(0, 1, 3, 2)
    c0 = jnp.pad(conv0.astype(f32), ((0, 0), (SUBLANES - (CONV_W - 1), 0), (0, 0)))
    full = lambda a: pl.BlockSpec(a.shape, lambda b, c: (0,) * a.ndim)
    row = lambda n: pl.BlockSpec((q, n), lambda b, c: (b * nc + c, 0))
    params = [pw['conv_w'], pw['conv_b'], pw['dt_bias'], pw['dt_biasT'], pw['a_log'], pw['a_logT'],
              pw['d_skip'], pw['ssm_g']]
    y, hT, ctail = pl.pallas_call(
        _ssd_kernel, grid=(B, nc),
        in_specs=[row(D_SSM), row(CONV_DIM), row(LANES),
                  pl.BlockSpec((1, BF16_ROWS, q), lambda b, c: (b, 0, c)),
                  pl.BlockSpec((1, SSM_GROUPS, D_STATE, gw), lambda b, c: (b, 0, 0, 0)),
                  pl.BlockSpec((1, SUBLANES, CONV_DIM), lambda b, c: (b, 0, 0))] + [full(a) for a in params],
        out_specs=[row(D_SSM),
                   pl.BlockSpec((1, SSM_GROUPS, D_STATE, gw), lambda b, c: (b, 0, 0, 0)),
                   pl.BlockSpec((1, SUBLANES, CONV_DIM), lambda b, c: (b, 0, 0))],
        out_shape=[jax.ShapeDtypeStruct((B * L, D_SSM), bf16),
                   jax.ShapeDtypeStruct((B, SSM_GROUPS, D_STATE, gw), f32),
                   jax.ShapeDtypeStruct((B, SUBLANES, CONV_DIM), f32)],
        scratch_shapes=[pltpu.VMEM((q + SUBLANES, CONV_DIM), f32), pltpu.VMEM((SSM_GROUPS, D_STATE, gw), f32)],
        compiler_params=_params(("parallel", "arbitrary")),
    )(z, xbc, dt, dtT3, hT0, c0, *params)
    h_new = hT.transpose(0, 1, 3, 2).reshape(B, H_S, SSM_HEADDIM, D_STATE)
    return y, h_new, ctail[:, SUBLANES - (CONV_W - 1):, :]


def _mix_router_kernel(att_ref, ssm_ref, x_ref, woa_ref, wos_ref, g_ref, b_ref, wrh_ref, wrl_ref, br_ref,
                       x1_ref, te_ref, gate_ref, cnt_ref, run_sc):
    @pl.when(pl.program_id(0) == 0)
    def _():
        run_sc[...] = jnp.zeros(run_sc.shape, f32)

    mix = _dot(att_ref[...], woa_ref[...]) + _dot(ssm_ref[...], wos_ref[...])
    x1 = _layer_norm(DEEPNORM_ALPHA * x_ref[...] + mix, g_ref[...], b_ref[...])
    x1_ref[...] = x1
    hi = x1.astype(bf16)
    lo = (x1 - hi.astype(f32)).astype(bf16)
    logits = _dot(hi, wrh_ref[...]) + _dot(lo, wrh_ref[...]) + _dot(hi, wrl_ref[...]) + br_ref[...]
    lane = lax.broadcasted_iota(i32, logits.shape, 1)
    vals, idxs = [], []
    for _ in range(TOP_K):
        m = jnp.max(logits, axis=-1, keepdims=True)
        idx = jnp.min(jnp.where(logits == m, lane, LANES), axis=-1, keepdims=True)
        vals.append(m)
        idxs.append(idx)
        logits = jnp.where(lane == idx, -jnp.inf, logits)
    es = [jnp.exp(v - vals[0]) for v in vals]
    tot = es[0]
    for e in es[1:]:
        tot = tot + e
    tl = logits.shape[0]
    chosen = jnp.zeros(logits.shape, f32)
    for k in range(TOP_K):
        chosen = chosen + (lane == idxs[k]).astype(f32)
    ii = lax.broadcasted_iota(i32, (tl, tl), 0)
    jj = lax.broadcasted_iota(i32, (tl, tl), 1)
    before = _dot((jj < ii).astype(bf16), chosen.astype(bf16)) + run_sc[...]
    te = jnp.zeros(logits.shape, i32)
    gate = jnp.zeros(logits.shape, f32)
    for k in range(TOP_K):
        rank = jnp.sum(jnp.where(lane == idxs[k], before, 0.0), axis=-1, keepdims=True).astype(i32)
        te = jnp.where(lane == k, idxs[k], te)
        te = jnp.where(lane == TOP_K + k, rank, te)
        gate = jnp.where(lane == k, es[k] / tot, gate)
    te_ref[...] = te
    gate_ref[...] = gate
    run_sc[...] = run_sc[...] + jnp.sum(chosen, axis=0, keepdims=True)
    cnt_ref[...] = run_sc[...]


def _mix_router(att, ssm, x2d, w):
    T, D = x2d.shape
    tl = min(ROW_TILE, T)
    row = lambda n: pl.BlockSpec((tl, n), lambda i: (i, 0))
    full = lambda a: pl.BlockSpec(a.shape, lambda i: (0,) * a.ndim)
    ws = [w['wo_att'], w['wo_ssm'], w['ln1_g'], w['ln1_b'], w['wr_hi'], w['wr_lo'], w['b_router']]
    return pl.pallas_call(
        _mix_router_kernel, grid=(T // tl,),
        in_specs=[row(D_ATT), row(D_SSM), row(D)] + [full(a) for a in ws],
        out_specs=[row(D), row(LANES), row(LANES), pl.BlockSpec((1, LANES), lambda i: (0, 0))],
        out_shape=[jax.ShapeDtypeStruct((T, D), f32), jax.ShapeDtypeStruct((T, LANES), i32),
                   jax.ShapeDtypeStruct((T, LANES), f32), jax.ShapeDtypeStruct((1, LANES), f32)],
        scratch_shapes=[pltpu.VMEM((1, LANES), f32)],
        compiler_params=_params(("arbitrary",)),
    )(att, ssm, x2d, *ws)


def _route(te, counts_f):
    T = te.shape[0]
    n_assign = T * TOP_K
    counts = counts_f[0, :N_EXPERTS].astype(i32)
    padded = (counts + MOE_BLOCK - 1) // MOE_BLOCK * MOE_BLOCK
    pad_end = jnp.cumsum(padded)
    pad_start = pad_end - padded
    top_e, rank = te[:, :TOP_K], te[:, TOP_K:2 * TOP_K]
    experts = jnp.arange(N_EXPERTS, dtype=i32)
    slot = rank + jnp.sum(jnp.where(top_e[..., None] == experts, pad_start, 0), axis=-1)
    n_blocks = -(-n_assign // MOE_BLOCK) + N_EXPERTS
    block_start = jnp.arange(n_blocks, dtype=i32) * MOE_BLOCK
    block_e = jnp.minimum(jnp.sum(block_start[:, None] >= pad_end[None, :], axis=-1), N_EXPERTS - 1).astype(i32)
    n_used = (pad_end[-1] // MOE_BLOCK).astype(i32).reshape(1)
    return block_e, n_used, slot.astype(i32), n_blocks


def _slot_tiles(slot, tl):
    T = slot.shape[0]
    return slot.reshape(T // tl, tl, TOP_K).transpose(0, 2, 1).reshape(T // tl, 1, TOP_K * tl)


def _dispatch_kernel(slot_ref, x_hbm, xs_in_hbm, xs_hbm, sem):
    del xs_in_hbm
    i = pl.program_id(0)
    n = pl.num_programs(0)
    n_copies = slot_ref.shape[-1]
    tl = n_copies // TOP_K
    cur = i % 2

    def body(r, carry):
        src = x_hbm.at[pl.ds(i * tl + r, 1)]
        for k in range(TOP_K):
            pltpu.make_async_copy(src, xs_hbm.at[pl.ds(slot_ref[0, 0, k * tl + r], 1)], sem.at[cur]).start()
        return carry
    lax.fori_loop(0, tl, body, 0, unroll=4)

    def wait_tile(s):
        pltpu.make_async_copy(xs_hbm.at[pl.ds(0, n_copies)], xs_hbm.at[pl.ds(0, n_copies)], sem.at[s]).wait()

    @pl.when(i > 0)
    def _():
        wait_tile(1 - cur)

    @pl.when(i == n - 1)
    def _():
        wait_tile(cur)


def _dispatch(x1, slot, n_blocks):
    T, D = x1.shape
    tl = min(ROW_TILE, T)
    tiles = _slot_tiles(slot, tl)
    xs0 = jnp.zeros((n_blocks * MOE_BLOCK, D), f32)
    return pl.pallas_call(
        _dispatch_kernel, grid=(T // tl,),
        in_specs=[pl.BlockSpec((1, 1, TOP_K * tl), lambda i: (i, 0, 0), memory_space=pltpu.SMEM),
                  pl.BlockSpec(memory_space=pl.ANY), pl.BlockSpec(memory_space=pl.ANY)],
        out_specs=pl.BlockSpec(memory_space=pl.ANY),
        out_shape=jax.ShapeDtypeStruct(xs0.shape, f32),
        scratch_shapes=[pltpu.SemaphoreType.DMA((2,))],
        input_output_aliases={2: 0},
        compiler_params=_params(("arbitrary",)),
    )(tiles, x1, xs0)


def _deinterleave_kernel(w_ref, g_ref, u_ref):
    wb = w_ref[0].astype(bf16)
    tn = g_ref.shape[-1]
    r = lax.broadcasted_iota(i32, (2 * tn, tn), 0)
    c = lax.broadcasted_iota(i32, (2 * tn, tn), 1)
    g_ref[0] = _dot(wb, (r == 2 * c).astype(bf16)).astype(bf16)
    u_ref[0] = _dot(wb, (r == 2 * c + 1).astype(bf16)).astype(bf16)


def _deinterleave(w_gu):
    E, D, F2 = w_gu.shape
    tn = 2 * LANES
    out = jax.ShapeDtypeStruct((E, D, F2 // 2), bf16)
    return pl.pallas_call(
        _deinterleave_kernel, grid=(E, F2 // (2 * tn)),
        in_specs=[pl.BlockSpec((1, D, 2 * tn), lambda e, j: (e, 0, j))],
        out_specs=[pl.BlockSpec((1, D, tn), lambda e, j: (e, 0, j))] * 2,
        out_shape=[out, out],
        compiler_params=_params(("parallel", "parallel")),
    )(w_gu)


def _row_gather(idx_ref, n, src_hbm, dst, sem):
    def body(r, carry):
        pltpu.make_async_copy(src_hbm.at[pl.ds(idx_ref[0, 0, r], 1)], dst.at[pl.ds(r, 1)], sem).start()
        return carry
    lax.fori_loop(0, n, body, 0, unroll=8)


def _ffn_kernel(be_ref, nu_ref, xs_ref, wg_ref, wu_ref, bg_ref, bu_ref, wd_ref, bd_ref, y_ref):
    i = pl.program_id(0)
    n_used = nu_ref[0]

    @pl.when(i < n_used)
    def _():
        xb = xs_ref[...].astype(bf16)
        g = _dot(xb, wg_ref[0]) + bg_ref[0]
        u = _dot(xb, wu_ref[0]) + bu_ref[0]
        g = jnp.minimum(g, SWIGLU_LIMIT)
        u = jnp.clip(u, -SWIGLU_LIMIT, SWIGLU_LIMIT)
        act = (u + 1.0) * g * _sigmoid(SWIGLU_ALPHA * g)
        y_ref[...] = _dot(act.astype(bf16), wd_ref[0]) + bd_ref[0]

    @pl.when(i >= n_used)
    def _():
        y_ref[...] = jnp.zeros(y_ref.shape, f32)


def _expert_ffn(xs, block_e, n_used, w):
    D = xs.shape[1]
    n_blocks = xs.shape[0] // MOE_BLOCK
    F = w['w_g'].shape[2]
    grid_spec = pltpu.PrefetchScalarGridSpec(
        num_scalar_prefetch=2, grid=(n_blocks,),
        in_specs=[
            pl.BlockSpec((MOE_BLOCK, D), lambda i, be, nu: (jnp.minimum(i, nu[0] - 1), 0)),
            pl.BlockSpec((1, D, F), lambda i, be, nu: (be[i], 0, 0)),
            pl.BlockSpec((1, D, F), lambda i, be, nu: (be[i], 0, 0)),
            pl.BlockSpec((1, 1, F), lambda i, be, nu: (be[i], 0, 0)),
            pl.BlockSpec((1, 1, F), lambda i, be, nu: (be[i], 0, 0)),
            pl.BlockSpec((1, F, D), lambda i, be, nu: (be[i], 0, 0)),
            pl.BlockSpec((1, 1, D), lambda i, be, nu: (be[i], 0, 0))],
        out_specs=pl.BlockSpec((MOE_BLOCK, D), lambda i, be, nu: (i, 0)))
    return pl.pallas_call(
        _ffn_kernel, grid_spec=grid_spec,
        out_shape=jax.ShapeDtypeStruct((n_blocks * MOE_BLOCK, D), f32),
        compiler_params=_params(("arbitrary",)),
    )(block_e, n_used, xs, w['w_g'], w['w_u'], w['b_g'], w['b_u'], w['w_d'], w['b_d'])


def _combine_kernel(slot_ref, slotn_ref, ys_hbm, x1_ref, gate_ref, g_ref, b_ref, y_ref, buf, sem):
    i = pl.program_id(0)
    n = pl.num_programs(0)
    tl = x1_ref.shape[0]
    cur = i % 2

    @pl.when(i == 0)
    def _():
        _row_gather(slot_ref, TOP_K * tl, ys_hbm, buf.at[0], sem.at[0])

    @pl.when(i + 1 < n)
    def _():
        _row_gather(slotn_ref, TOP_K * tl, ys_hbm, buf.at[1 - cur], sem.at[1 - cur])

    pltpu.make_async_copy(buf.at[cur], buf.at[cur], sem.at[cur]).wait()
    gate = gate_ref[...]
    ff = gate[:, 0:1] * buf[cur, 0:tl]
    for k in range(1, TOP_K):
        ff = ff + gate[:, k:k + 1] * buf[cur, k * tl:(k + 1) * tl]
    y_ref[...] = _layer_norm(DEEPNORM_ALPHA * x1_ref[...] + ff, g_ref[...], b_ref[...])


def _combine(ys, slot, x1, gate, ln_g, ln_b):
    T, D = x1.shape
    tl = min(COMB_TILE, T)
    nt = T // tl
    slot_tiles = _slot_tiles(slot, tl)
    row = lambda n: pl.BlockSpec((tl, n), lambda i: (i, 0))
    full = lambda a: pl.BlockSpec(a.shape, lambda i: (0,) * a.ndim)
    return pl.pallas_call(
        _combine_kernel, grid=(nt,),
        in_specs=[pl.BlockSpec((1, 1, TOP_K * tl), lambda i: (i, 0, 0), memory_space=pltpu.SMEM),
                  pl.BlockSpec((1, 1, TOP_K * tl), lambda i: (jnp.minimum(i + 1, nt - 1), 0, 0),
                               memory_space=pltpu.SMEM),
                  pl.BlockSpec(memory_space=pl.ANY), row(D), row(LANES), full(ln_g), full(ln_b)],
        out_specs=row(D), out_shape=jax.ShapeDtypeStruct((T, D), f32),
        scratch_shapes=[pltpu.VMEM((2, TOP_K * tl, D), f32), pltpu.SemaphoreType.DMA((2,))],
        compiler_params=_params(("arbitrary",)),
    )(slot_tiles, slot_tiles, ys, x1, gate, ln_g, ln_b)


def _prep_weights(l, w_in, conv_w, conv_b, dt_bias, a_log, d_skip, ssm_norm_g, w_o, ln1_g, ln1_b,
                  w_router, b_router, w_gate_up, b_gate_up, w_down, b_down, ln2_g, ln2_b):
    wi = w_in[l]
    c0, c1, c2, c3 = D_QK_ALL, 2 * D_QK_ALL, 2 * D_QK_ALL + D_ATT, 2 * D_QK_ALL + D_ATT + D_SSM
    c4 = c3 + CONV_DIM
    wdt = wi[:, c4:c4 + H_S]
    pad_lane = lambda v, fill=0.0: jnp.pad(v.astype(f32).reshape(1, -1), ((0, 0), (0, LANES - v.shape[-1])),
                                            constant_values=fill)
    pad_col = lambda v: jnp.pad(v.astype(f32).reshape(-1, 1), ((0, BF16_ROWS - v.shape[-1]), (0, 0)))
    wr = jnp.pad(w_router[l].astype(f32), ((0, 0), (0, LANES - N_EXPERTS)))
    wr_hi = wr.astype(bf16)
    wgu = w_gate_up[l]
    return {
        'wq': wi[:, :c0].astype(bf16), 'wqT': wi[:, :c0].T.astype(bf16),
        'wk': wi[:, c0:c1].astype(bf16), 'wv': wi[:, c1:c2].astype(bf16), 'wvT': wi[:, c1:c2].T.astype(bf16),
        'wz': wi[:, c2:c3].astype(bf16), 'wx': wi[:, c3:c4].astype(bf16),
        'wdt': jnp.pad(wdt, ((0, 0), (0, LANES - H_S))).astype(bf16),
        'wdtT': jnp.pad(wdt.T, ((0, BF16_ROWS - H_S), (0, 0))).astype(bf16),
        'conv_w': conv_w[l].astype(f32), 'conv_b': conv_b[l].astype(f32).reshape(1, -1),
        'dt_bias': pad_lane(dt_bias[l]), 'dt_biasT': pad_col(dt_bias[l]),
        'a_log': pad_lane(a_log[l]), 'a_logT': pad_col(a_log[l]),
        'd_skip': jnp.repeat(d_skip[l].astype(f32), SSM_HEADDIM).reshape(1, -1),
        'ssm_g': ssm_norm_g[l].astype(f32).reshape(1, -1),
        'wo_att': w_o[l][:D_ATT].astype(bf16), 'wo_ssm': w_o[l][D_ATT:].astype(bf16),
        'ln1_g': ln1_g[l].astype(f32).reshape(1, -1), 'ln1_b': ln1_b[l].astype(f32).reshape(1, -1),
        'wr_hi': wr_hi, 'wr_lo': (wr - wr_hi.astype(f32)).astype(bf16),
        'b_router': pad_lane(b_router[l], NEG_INF),
        'w_gu': wgu,
        'b_g': b_gate_up[l][:, None, 0::2].astype(f32), 'b_u': b_gate_up[l][:, None, 1::2].astype(f32),
        'w_d': w_down[l].astype(bf16), 'b_d': b_down[l][:, None, :].astype(f32),
        'ln2_g': ln2_g[l].astype(f32).reshape(1, -1), 'ln2_b': ln2_b[l].astype(f32).reshape(1, -1),
    }


def _trunk_layer(x, l, w, rel_bias, lam_args, attn_g, k_past, v_past, h0, conv0):
    B, L, D = x.shape
    x2d = x.reshape(B * L, D)
    lam_init = 0.8 - 0.6 * math.exp(-0.3 * l)
    prompt = k_past is None
    proj = _in_proj(x2d, w, prompt)
    k, v, z, xbc, dt, dtT = proj[:6]
    if prompt:
        kb, qT, v1T = proj[6:]
        att = _attn_prompt(qT, kb, v1T, rel_bias, lam_args, attn_g, B, L, lam_init)
    else:
        att = _attn_sample(proj[6], k, v, k_past, v_past, rel_bias, lam_args, attn_g, lam_init)
    ssm, h_new, conv_new = _ssd(z, xbc, dt, dtT, h0, conv0, w, B, L)
    x1, te, gate, counts = _mix_router(att, ssm, x2d, w)
    block_e, n_used, slot, n_blocks = _route(te, counts)
    xs = _dispatch(x1, slot, n_blocks)
    ys = _expert_ffn(xs, block_e, n_used, w)
    y = _combine(ys, slot, x1, gate, w['ln2_g'], w['ln2_b'])
    return (y.reshape(B, L, D), k.reshape(B, L, H_A, 2 * DQK), v.reshape(B, L, H_A, DV), h_new, conv_new)


def kernel(x_prompt, x_sample, cache_k, cache_v, state_ssm, state_conv, rel_bias, w_in, lambda_q1, lambda_k1, lambda_q2, lambda_k2, attn_norm_g, conv_w, conv_b, dt_bias, a_log, d_skip, ssm_norm_g, w_o, ln1_g, ln1_b, w_router, b_router, w_gate_up, b_gate_up, w_down, b_down, ln2_g, ln2_b):
    yp, ys = x_prompt, x_sample
    bp = x_prompt.shape[0]
    depth = w_in.shape[0]
    outs = [[] for _ in range(8)]
    for l in range(depth):
        w = _prep_weights(l, w_in, conv_w, conv_b, dt_bias, a_log, d_skip, ssm_norm_g, w_o, ln1_g, ln1_b,
                          w_router, b_router, w_gate_up, b_gate_up, w_down, b_down, ln2_g, ln2_b)
        w['w_g'], w['w_u'] = _deinterleave(w.pop('w_gu'))
        lam_args = [a[l].astype(f32).reshape(1, -1) for a in (lambda_q1, lambda_k1, lambda_q2, lambda_k2)]
        h0 = jnp.zeros((bp, H_S, SSM_HEADDIM, D_STATE), f32)
        c0 = jnp.zeros((bp, CONV_W - 1, CONV_DIM), f32)
        yp, kp, vp, hp, cp = _trunk_layer(yp, l, w, rel_bias, lam_args, attn_norm_g[l], None, None, h0, c0)
        ys, ks, vs, hs, cs = _trunk_layer(ys, l, w, rel_bias, lam_args, attn_norm_g[l], cache_k[l], cache_v[l],
                                          state_ssm[l], state_conv[l])
        for lst, a in zip(outs, (kp, vp, hp, cp, ks, vs, hs, cs)):
            lst.append(a)
    return (yp, ys) + tuple(jnp.stack(o) for o in outs)
```

```python
import functools
import math

import numpy as np
import jax
import jax.numpy as jnp
from jax import lax
from jax.experimental import pallas as pl
from jax.experimental.pallas import tpu as pltpu

f32 = jnp.float32
bf16 = jnp.bfloat16
i32 = jnp.int32

CHUNK = 64
H_A = 8
DQK = 32
DV = 2 * DQK
D_ATT = H_A * DV
SSM_HEADDIM = 64
H_S = 8
D_SSM = H_S * SSM_HEADDIM
SSM_GROUPS = 2
HEADS_PER_GROUP = H_S // SSM_GROUPS
D_STATE = 128
CONV_W = 4
CONV_DIM = D_SSM + 2 * SSM_GROUPS * D_STATE
D_QK_ALL = H_A * 2 * DQK
N_BUCKETS = 32
MAX_DISTANCE = 128
N_EXPERTS = 32
TOP_K = 4
SWIGLU_LIMIT = 7.0
SWIGLU_ALPHA = 1.702
MOE_BLOCK = 256
LN_EPS = 1e-5
RMS_EPS = 1e-5
NEG_INF = -1e30
DEPTH = 1
DEEPNORM_ALPHA = (2.0 * DEPTH) ** 0.25
LOG2E = math.log2(math.e)

LANES = 128
SUBLANES = 8
BF16_ROWS = 16
VMEM_LIMIT = 48 * 1024 * 1024

ROW_TILE = 512
ATT_TILE = 512
ATT_SUB = 256
SSD_TILE = 256
V_ROWS = DV + BF16_ROWS
COMB_TILE = 256


def _params(semantics):
    return pltpu.CompilerParams(dimension_semantics=semantics, vmem_limit_bytes=VMEM_LIMIT)


def _dot(a, b):
    return jnp.dot(a, b, preferred_element_type=f32)


def _dot_nt(a, b):
    return lax.dot_general(a, b, (((1,), (1,)), ((), ())), preferred_element_type=f32)


def _dot_tn(a, b):
    return lax.dot_general(a, b, (((0,), (0,)), ((), ())), preferred_element_type=f32)


def _split3(a):
    hi = a.astype(bf16)
    r1 = a - hi.astype(f32)
    mid = r1.astype(bf16)
    lo = (r1 - mid.astype(f32)).astype(bf16)
    return hi, mid, lo


def _dot_f32_lhs(a, b_exact):
    hi, mid, lo = _split3(a)
    return _dot(hi, b_exact) + _dot(mid, b_exact) + _dot(lo, b_exact)


def _dot_f32_rhs(a_exact, b):
    hi, mid, lo = _split3(b)
    return _dot(a_exact, hi) + _dot(a_exact, mid) + _dot(a_exact, lo)


def _softplus(x):
    return jnp.maximum(x, 0.0) + jnp.log1p(jnp.exp(-jnp.abs(x)))


def _sigmoid(x):
    return 1.0 / (1.0 + jnp.exp(-x))


def _layer_norm(y, g, b):
    mu = jnp.mean(y, axis=-1, keepdims=True)
    yc = y - mu
    var = jnp.mean(yc * yc, axis=-1, keepdims=True)
    return yc * lax.rsqrt(var + LN_EPS) * g + b


def _lambda(lq1_ref, lk1_ref, lq2_ref, lk2_ref, lam_init):
    s1 = jnp.sum(lq1_ref[...] * lk1_ref[...], axis=-1, keepdims=True)
    s2 = jnp.sum(lq2_ref[...] * lk2_ref[...], axis=-1, keepdims=True)
    return jnp.exp(s1) - jnp.exp(s2) + lam_init


def _in_proj_kernel(x_ref, wk_ref, wv_ref, wz_ref, wx_ref, wdt_ref, wdtT_ref, wq_ref, wvT_ref, *outs, prompt):
    xb = x_ref[...].astype(bf16)
    k_ref, v_ref, z_ref, xbc_ref, dt_ref, dtT_ref = outs[:6]
    k = _dot(xb, wk_ref[...])
    k_ref[...] = k
    v_ref[...] = _dot(xb, wv_ref[...])
    z_ref[...] = _dot(xb, wz_ref[...])
    xbc_ref[...] = _dot(xb, wx_ref[...])
    dt_ref[...] = _dot(xb, wdt_ref[...])
    dtT_ref[...] = _dot_nt(wdtT_ref[...], xb)
    scale = DQK ** -0.5
    if prompt:
        kb_ref, qT_ref, v1T_ref = outs[6:]
        kb_ref[...] = k.astype(bf16)
        qT_ref[...] = (_dot_nt(wq_ref[...], xb) * (scale * LOG2E)).astype(bf16)
        vT = _dot_nt(wvT_ref[...], xb).astype(bf16)
        ones = jnp.ones((BF16_ROWS, vT.shape[1]), bf16)
        for h in range(H_A):
            v1T_ref[h * V_ROWS:h * V_ROWS + DV, :] = vT[h * DV:(h + 1) * DV, :]
            v1T_ref[h * V_ROWS + DV:(h + 1) * V_ROWS, :] = ones
    else:
        (q_ref,) = outs[6:]
        q_ref[...] = (_dot(xb, wq_ref[...]) * scale).astype(bf16)


def _in_proj(x2d, w, prompt):
    T, D = x2d.shape
    tl = min(ROW_TILE, T)
    grid = (T // tl,)
    row = lambda n: pl.BlockSpec((tl, n), lambda i: (i, 0))
    col = lambda n: pl.BlockSpec((n, tl), lambda i: (0, i))
    full = lambda a: pl.BlockSpec(a.shape, lambda i: (0,) * a.ndim)
    wq = w['wqT'] if prompt else w['wq']
    ins = [x2d, w['wk'], w['wv'], w['wz'], w['wx'], w['wdt'], w['wdtT'], wq, w['wvT']]
    in_specs = [row(D)] + [full(a) for a in ins[1:]]
    out_shape = [jax.ShapeDtypeStruct((T, D_QK_ALL), f32), jax.ShapeDtypeStruct((T, D_ATT), f32),
                 jax.ShapeDtypeStruct((T, D_SSM), f32), jax.ShapeDtypeStruct((T, CONV_DIM), f32),
                 jax.ShapeDtypeStruct((T, LANES), f32), jax.ShapeDtypeStruct((BF16_ROWS, T), f32)]
    out_specs = [row(D_QK_ALL), row(D_ATT), row(D_SSM), row(CONV_DIM), row(LANES), col(BF16_ROWS)]
    if prompt:
        out_shape += [jax.ShapeDtypeStruct((T, D_QK_ALL), bf16), jax.ShapeDtypeStruct((D_QK_ALL, T), bf16),
                      jax.ShapeDtypeStruct((H_A * V_ROWS, T), bf16)]
        out_specs += [row(D_QK_ALL), col(D_QK_ALL), col(H_A * V_ROWS)]
    else:
        out_shape += [jax.ShapeDtypeStruct((T, D_QK_ALL), bf16)]
        out_specs += [row(D_QK_ALL)]
    return pl.pallas_call(
        functools.partial(_in_proj_kernel, prompt=prompt),
        grid=grid, in_specs=in_specs, out_specs=out_specs, out_shape=out_shape,
        compiler_params=_params(("parallel",)),
    )(*ins)


def _t5_bucket(rel):
    half = N_BUCKETS // 2
    max_exact = half // 2
    n = jnp.abs(rel)
    large = max_exact + (jnp.log(jnp.maximum(n, 1).astype(f32) / max_exact)
                         / math.log(MAX_DISTANCE / max_exact) * (half - max_exact)).astype(i32)
    large = jnp.minimum(large, half - 1)
    return jnp.where(rel > 0, half, 0) + jnp.where(n < max_exact, n, large)


def _far_bucket(min_dist):
    half = N_BUCKETS // 2
    max_exact = half // 2
    v = max_exact + int(math.log(min_dist / max_exact) / math.log(MAX_DISTANCE / max_exact) * (half - max_exact))
    return half - 1 if (min_dist >= max_exact and v - 1 >= half - 1) else None


def _bias_lookup(rel_bias, bucket):
    onehot = (bucket[..., None] == jnp.arange(N_BUCKETS, dtype=i32)).astype(f32)
    return jnp.dot(onehot, rel_bias.astype(f32), precision=lax.Precision.HIGHEST)


def _prompt_bias_tiles(rel_bias, ta):
    j = jnp.arange(ta, dtype=i32)[:, None]
    i = jnp.arange(ta, dtype=i32)[None, :]
    tiles = []
    for d in range(2):
        rel = j - i - d * ta
        b = _bias_lookup(rel_bias, _t5_bucket(rel))
        allowed = ((j // CHUNK) <= (i // CHUNK)) if d == 0 else jnp.ones((ta, ta), bool)
        tiles.append(jnp.where(allowed[..., None], b, NEG_INF))
    return jnp.transpose(jnp.stack(tiles), (3, 0, 1, 2))


def _sample_bias(rel_bias, past, s):
    q_pos = past + jnp.arange(s, dtype=i32)
    k_pos = jnp.arange(past + s, dtype=i32)
    rel = k_pos[None, :] - q_pos[:, None]
    b = _bias_lookup(rel_bias, _t5_bucket(rel))
    allowed = (k_pos[None, :] // CHUNK) <= (q_pos[:, None] // CHUNK)
    return jnp.transpose(jnp.where(allowed[..., None], b, NEG_INF), (2, 0, 1))


def _attn_prompt_kernel(qi_tab, ki_tab, qT_ref, k_ref, v1T_ref, bias_ref, cfar_ref,
                        lq1_ref, lk1_ref, lq2_ref, lk2_ref, g_ref, o_ref,
                        qm_sc, m_sc, acc_sc, oT_sc, *, lam_init):
    p = pl.program_id(1)
    qi = qi_tab[p]
    ki = ki_tab[p]
    d = qi - ki
    ta = k_ref.shape[0]
    n_hp = H_A // 2

    @pl.when(ki == 0)
    def _():
        m_sc[...] = jnp.full(m_sc.shape, NEG_INF, f32)
        acc_sc[...] = jnp.zeros(acc_sc.shape, f32)
        row = lax.broadcasted_iota(i32, (LANES, ta), 0)
        for hp in range(n_hp):
            qt = qT_ref[hp * LANES:(hp + 1) * LANES, :]
            for gi in range(4):
                keep = (row >= gi * DQK) & (row < (gi + 1) * DQK)
                qm_sc[hp, :, gi * ta:(gi + 1) * ta] = jnp.where(keep, qt, jnp.zeros_like(qt))

    def scores(hp):
        return _dot(k_ref[:, hp * LANES:(hp + 1) * LANES], qm_sc[hp])

    nb = ta // ATT_SUB

    def near_bias(hp, step_d):
        rows = []
        for jb in range(nb):
            cols = []
            for g in range(4):
                for ib in range(nb):
                    dd = step_d * nb + ib - jb
                    if dd < 0:
                        blk = jnp.full((ATT_SUB, ATT_SUB), NEG_INF * LOG2E, f32)
                    elif dd < 2:
                        blk = bias_ref[2 * hp + g // 2, dd]
                    else:
                        lo = g * ta + ib * ATT_SUB
                        blk = jnp.broadcast_to(cfar_ref[hp][:, lo:lo + ATT_SUB], (ATT_SUB, ATT_SUB))
                    cols.append(blk)
            rows.append(jnp.concatenate(cols, axis=1))
        return jnp.concatenate(rows, axis=0)

    def update(step_d):
        s_next = scores(0)
        for hp in range(n_hp):
            s = s_next
            if hp + 1 < n_hp:
                s_next = scores(hp + 1)
            m_old = m_sc[hp]
            if step_d is not None:
                s = s + near_bias(hp, step_d)
                m_new = jnp.maximum(m_old, jnp.max(s, axis=0, keepdims=True))
                shift = m_new
            else:
                c = cfar_ref[hp]
                m_new = jnp.maximum(m_old, jnp.max(s, axis=0, keepdims=True) + c)
                shift = m_new - c
            alpha = jnp.exp2(m_old - m_new)
            pT = jnp.exp2(s - shift).astype(bf16)
            for hh in range(2):
                h = 2 * hp + hh
                cols = slice(hh * 2 * ta, (hh + 1) * 2 * ta)
                pv = _dot(v1T_ref[h * V_ROWS:(h + 1) * V_ROWS, :], pT[:, cols])
                acc_sc[h] = alpha[:, cols] * acc_sc[h] + pv
            m_sc[hp] = m_new

    @pl.when(d == 0)
    def _():
        update(0)

    @pl.when(d == 1)
    def _():
        update(1)

    @pl.when(d >= 2)
    def _():
        update(None)

    @pl.when(d == 0)
    def _():
        lam = _lambda(lq1_ref, lk1_ref, lq2_ref, lk2_ref, lam_init)
        for h in range(H_A):
            a = acc_sc[h]
            a0 = a[:, :ta]
            a1 = a[:, ta:]
            o = a0[:DV] / a0[DV:DV + 1] - lam * (a1[:DV] / a1[DV:DV + 1])
            o = o * lax.rsqrt(jnp.mean(o * o, axis=0, keepdims=True) + RMS_EPS) * g_ref[...] * (1.0 - lam_init)
            oT_sc[h * DV:(h + 1) * DV, :] = o
        o_ref[...] = oT_sc[...].T.astype(bf16)


def _attn_prompt(qT, kb, v1T, rel_bias, lam_args, attn_g, B, L, lam_init):
    ta = min(ATT_TILE, L)
    assert L % ta == 0 and ta % ATT_SUB == 0 and ATT_SUB % CHUNK == 0
    nt = L // ta
    far = _far_bucket(ATT_SUB + 1)
    assert L <= 2 * ATT_SUB or far is not None, "key blocks two or more behind must share one bias bucket"
    far = far if far is not None else N_BUCKETS // 2 - 1
    pairs = [(q, k) for q in range(nt) for k in range(q + 1)]
    qi_tab = jnp.asarray([q for q, _ in pairs], i32)
    ki_tab = jnp.asarray([k for _, k in pairs], i32)
    bias = _prompt_bias_tiles(rel_bias, ATT_SUB) * LOG2E
    cfar = jnp.repeat((rel_bias.astype(f32)[far] * LOG2E).reshape(H_A // 2, 2), 2 * ta, axis=1)[:, None, :]
    g_col = attn_g.astype(f32).reshape(DV, 1)
    T = B * L
    n_hp = H_A // 2
    full = lambda a: pl.BlockSpec(a.shape, lambda b, p, qt, kt: (0,) * a.ndim)
    grid_spec = pltpu.PrefetchScalarGridSpec(
        num_scalar_prefetch=2, grid=(B, len(pairs)),
        in_specs=[
            pl.BlockSpec((D_QK_ALL, ta), lambda b, p, qt, kt: (0, b * nt + qt[p])),
            pl.BlockSpec((ta, D_QK_ALL), lambda b, p, qt, kt: (b * nt + kt[p], 0)),
            pl.BlockSpec((H_A * V_ROWS, ta), lambda b, p, qt, kt: (0, b * nt + kt[p])),
            full(bias), full(cfar)] + [full(a) for a in lam_args] + [full(g_col)],
        out_specs=pl.BlockSpec((ta, D_ATT), lambda b, p, qt, kt: (b * nt + qt[p], 0)),
        scratch_shapes=[pltpu.VMEM((n_hp, LANES, 4 * ta), bf16), pltpu.VMEM((n_hp, 1, 4 * ta), f32),
                        pltpu.VMEM((H_A, V_ROWS, 2 * ta), f32), pltpu.VMEM((D_ATT, ta), f32)])
    return pl.pallas_call(
        functools.partial(_attn_prompt_kernel, lam_init=lam_init),
        grid_spec=grid_spec, out_shape=jax.ShapeDtypeStruct((T, D_ATT), bf16),
        compiler_params=_params(("parallel", "arbitrary")),
    )(qi_tab, ki_tab, qT, kb, v1T, bias, cfar, *lam_args, g_col)


def _attn_sample_kernel(q_ref, kn_ref, vn_ref, kc_ref, vc_ref, bc_ref, bn_ref,
                        lq1_ref, lk1_ref, lq2_ref, lk2_ref, g2_ref, o_ref, *, lam_init):
    s_len = q_ref.shape[0]
    lane = lax.broadcasted_iota(i32, (s_len, LANES), 1)
    lam = _lambda(lq1_ref, lk1_ref, lq2_ref, lk2_ref, lam_init)
    for hp in range(H_A // 2):
        sl = slice(hp * LANES, (hp + 1) * LANES)
        qt = q_ref[:, sl]
        kct = kc_ref[0, :, sl].astype(bf16)
        vct = vc_ref[0, :, sl].astype(bf16)
        knt = kn_ref[:, sl].astype(bf16)
        vnt = vn_ref[:, sl].astype(bf16)
        outs = []
        for hh in range(2):
            h = 2 * hp + hh
            oc = []
            for c in range(2):
                gi = 2 * hh + c
                qm = jnp.where((lane >= gi * DQK) & (lane < (gi + 1) * DQK), qt, jnp.zeros_like(qt))
                s1 = _dot_nt(qm, kct) + bc_ref[h]
                s2 = _dot_nt(qm, knt) + bn_ref[h]
                m = jnp.maximum(jnp.max(s1, axis=-1, keepdims=True), jnp.max(s2, axis=-1, keepdims=True))
                p1 = jnp.exp(s1 - m)
                p2 = jnp.exp(s2 - m)
                l = jnp.sum(p1, axis=-1, keepdims=True) + jnp.sum(p2, axis=-1, keepdims=True)
                pv = _dot(p1.astype(bf16), vct) + _dot(p2.astype(bf16), vnt)
                oc.append(pv / l)
            o = oc[0] - lam * oc[1]
            in_head = (lane >= hh * DV) & (lane < (hh + 1) * DV)
            ms = jnp.sum(jnp.where(in_head, o * o, 0.0), axis=-1, keepdims=True) * (1.0 / DV)
            outs.append(o * lax.rsqrt(ms + RMS_EPS))
        ot = jnp.where(lane < DV, outs[0], outs[1]) * g2_ref[...] * (1.0 - lam_init)
        o_ref[:, sl] = ot.astype(bf16)


def _attn_sample(q, k_new, v_new, cache_k, cache_v, rel_bias, lam_args, attn_g, lam_init):
    nb, past = cache_k.shape[0], cache_k.shape[1]
    T = q.shape[0]
    s = T // nb
    kc = cache_k.reshape(nb, past, D_QK_ALL)
    vc = cache_v.reshape(nb, past, D_ATT)
    bias = _sample_bias(rel_bias, past, s)
    bc, bn = bias[:, :, :past], bias[:, :, past:]
    g2 = jnp.tile(attn_g.astype(f32), 2).reshape(1, LANES)
    full = lambda a: pl.BlockSpec(a.shape, lambda b: (0,) * a.ndim)
    row = lambda n: pl.BlockSpec((s, n), lambda b: (b, 0))
    return pl.pallas_call(
        functools.partial(_attn_sample_kernel, lam_init=lam_init),
        grid=(nb,),
        in_specs=[row(D_QK_ALL), row(D_QK_ALL), row(D_ATT),
                  pl.BlockSpec((1, past, D_QK_ALL), lambda b: (b, 0, 0)),
                  pl.BlockSpec((1, past, D_ATT), lambda b: (b, 0, 0)),
                  full(bc), full(bn)] + [full(a) for a in lam_args] + [full(g2)],
        out_specs=row(D_ATT), out_shape=jax.ShapeDtypeStruct((T, D_ATT), bf16),
        compiler_params=_params(("parallel",)),
    )(q, k_new, v_new, kc, vc, bc, bn, *lam_args, g2)


def _ssd_kernel(z_ref, xbc_ref, dt_ref, dtT_ref, h0_ref, c0_ref, cw_ref, cb_ref, dtb_ref, dtbT_ref,
                alog_ref, alogT_ref, dsk_ref, g_ref, y_ref, hout_ref, cout_ref, xpad_sc, h_sc):
    q = xbc_ref.shape[0]
    c = pl.program_id(1)
    gw = HEADS_PER_GROUP * SSM_HEADDIM

    @pl.when(c == 0)
    def _():
        xpad_sc[0:SUBLANES] = c0_ref[0]
        h_sc[...] = h0_ref[0]

    xpad_sc[SUBLANES:SUBLANES + q] = xbc_ref[...]
    first = SUBLANES - (CONV_W - 1)
    conv = cb_ref[...]
    for i in range(CONV_W):
        conv = conv + xpad_sc[first + i:first + i + q] * cw_ref[i:i + 1, :]
    tail = xpad_sc[q:q + SUBLANES]
    xpad_sc[0:SUBLANES] = tail
    cout_ref[0] = tail
    act = conv * _sigmoid(conv)
    xs = act[:, :D_SSM]

    lane = lax.broadcasted_iota(i32, (1, LANES), 1)
    a_row = jnp.where(lane < H_S, -jnp.exp(alog_ref[...]), 0.0)
    dt = _softplus(dt_ref[...] + dtb_ref[...])
    ii = lax.broadcasted_iota(i32, (q, q), 0)
    jj = lax.broadcasted_iota(i32, (q, q), 1)
    causal = jj <= ii
    acs = _dot_f32_rhs(causal.astype(bf16), dt * a_row)
    sub = lax.broadcasted_iota(i32, (BF16_ROWS, 1), 0)
    a_col = jnp.where(sub < H_S, -jnp.exp(alogT_ref[...]), 0.0)
    dtT = _softplus(dtT_ref[0] + dtbT_ref[...])
    acsT = _dot_f32_lhs(dtT * a_col, (ii <= jj).astype(bf16))

    er = lax.broadcasted_iota(i32, (LANES, D_SSM), 0)
    ec = lax.broadcasted_iota(i32, (LANES, D_SSM), 1)
    expand = (ec // SSM_HEADDIM == er).astype(bf16)
    dt_x = _dot_f32_lhs(dt, expand)
    acs_x = _dot_f32_lhs(acs, expand)
    e_acs = jnp.exp(acs_x)
    acs_last = acs_x[q - 1:q, :]
    decay = jnp.exp(acs_last - acs_x) * dt_x
    dtx = xs * dt_x
    xd = xs * decay
    glane = lax.broadcasted_iota(i32, (1, gw), 1)

    ys = []
    for g in range(SSM_GROUPS):
        gs = slice(g * gw, (g + 1) * gw)
        bg = act[:, D_SSM + g * D_STATE:D_SSM + (g + 1) * D_STATE].astype(bf16)
        cg = act[:, D_SSM + (SSM_GROUPS + g) * D_STATE:D_SSM + (SSM_GROUPS + g + 1) * D_STATE].astype(bf16)
        cb = _dot_nt(cg, bg)
        h_old = h_sc[g]
        yg = _dot(cg, h_old.astype(bf16)) * e_acs[:, gs]
        dtx_g = dtx[:, gs]
        for e4 in range(HEADS_PER_GROUP):
            e = g * HEADS_PER_GROUP + e4
            seg = acs[:, e:e + 1] - acsT[e:e + 1, :]
            lmat = jnp.exp(jnp.where(causal, seg, -jnp.inf))
            rhs = jnp.where(glane // SSM_HEADDIM == e4, dtx_g, 0.0).astype(bf16)
            yg = yg + _dot((cb * lmat).astype(bf16), rhs)
        h_sc[g] = h_old * jnp.exp(acs_last[:, gs]) + _dot_tn(bg, xd[:, gs].astype(bf16))
        ys.append(yg)
    y = jnp.concatenate(ys, axis=1) + dsk_ref[...] * xs
    zz = z_ref[...]
    y = y * (zz * _sigmoid(zz))
    for g in range(SSM_GROUPS):
        gs = slice(g * gw, (g + 1) * gw)
        yg = y[:, gs]
        r = lax.rsqrt(jnp.mean(yg * yg, axis=-1, keepdims=True) + RMS_EPS)
        y_ref[:, gs] = (yg * r * g_ref[:, gs]).astype(bf16)
    hout_ref[0] = h_sc[...]


def _ssd(z, xbc, dt, dtT, h0, conv0, pw, B, L):
    q = min(SSD_TILE, L)
    nc = L // q
    gw = HEADS_PER_GROUP * SSM_HEADDIM
    dtT3 = dtT.reshape(BF16_ROWS, B, L).transpose(1, 0, 2)
    hT0 = h0.astype(f32).reshape(B, SSM_GROUPS, gw, D_STATE).transpose(0, 1, 3, 2)
    c0 = jnp.pad(conv0.astype(f32), ((0, 0), (SUBLANES - (CONV_W - 1), 0), (0, 0)))
    full = lambda a: pl.BlockSpec(a.shape, lambda b, c: (0,) * a.ndim)
    row = lambda n: pl.BlockSpec((q, n), lambda b, c: (b * nc + c, 0))
    params = [pw['conv_w'], pw['conv_b'], pw['dt_bias'], pw['dt_biasT'], pw['a_log'], pw['a_logT'],
              pw['d_skip'], pw['ssm_g']]
    y, hT, ctail = pl.pallas_call(
        _ssd_kernel, grid=(B, nc),
        in_specs=[row(D_SSM), row(CONV_DIM), row(LANES),
                  pl.BlockSpec((1, BF16_ROWS, q), lambda b, c: (b, 0, c)),
                  pl.BlockSpec((1, SSM_GROUPS, D_STATE, gw), lambda b, c: (b, 0, 0, 0)),
                  pl.BlockSpec((1, SUBLANES, CONV_DIM), lambda b, c: (b, 0, 0))] + [full(a) for a in params],
        out_specs=[row(D_SSM),
                   pl.BlockSpec((1, SSM_GROUPS, D_STATE, gw), lambda b, c: (b, 0, 0, 0)),
                   pl.BlockSpec((1, SUBLANES, CONV_DIM), lambda b, c: (b, 0, 0))],
        out_shape=[jax.ShapeDtypeStruct((B * L, D_SSM), bf16),
                   jax.ShapeDtypeStruct((B, SSM_GROUPS, D_STATE, gw), f32),
                   jax.ShapeDtypeStruct((B, SUBLANES, CONV_DIM), f32)],
        scratch_shapes=[pltpu.VMEM((q + SUBLANES, CONV_DIM), f32), pltpu.VMEM((SSM_GROUPS, D_STATE, gw), f32)],
        compiler_params=_params(("parallel", "arbitrary")),
    )(z, xbc, dt, dtT3, hT0, c0, *params)
    h_new = hT.transpose(0, 1, 3, 2).reshape(B, H_S, SSM_HEADDIM, D_STATE)
    return y, h_new, ctail[:, SUBLANES - (CONV_W - 1):, :]


def _mix_router_kernel(att_ref, ssm_ref, x_ref, woa_ref, wos_ref, g_ref, b_ref, wrh_ref, wrl_ref, br_ref,
                       x1_ref, te_ref, gate_ref, cnt_ref, run_sc):
    @pl.when(pl.program_id(0) == 0)
    def _():
        run_sc[...] = jnp.zeros(run_sc.shape, f32)

    mix = _dot(att_ref[...], woa_ref[...]) + _dot(ssm_ref[...], wos_ref[...])
    x1 = _layer_norm(DEEPNORM_ALPHA * x_ref[...] + mix, g_ref[...], b_ref[...])
    x1_ref[...] = x1
    hi = x1.astype(bf16)
    lo = (x1 - hi.astype(f32)).astype(bf16)
    logits = _dot(hi, wrh_ref[...]) + _dot(lo, wrh_ref[...]) + _dot(hi, wrl_ref[...]) + br_ref[...]
    lane = lax.broadcasted_iota(i32, logits.shape, 1)
    vals, idxs = [], []
    for _ in range(TOP_K):
        m = jnp.max(logits, axis=-1, keepdims=True)
        idx = jnp.min(jnp.where(logits == m, lane, LANES), axis=-1, keepdims=True)
        vals.append(m)
        idxs.append(idx)
        logits = jnp.where(lane == idx, -jnp.inf, logits)
    es = [jnp.exp(v - vals[0]) for v in vals]
    tot = es[0]
    for e in es[1:]:
        tot = tot + e
    tl = logits.shape[0]
    chosen = jnp.zeros(logits.shape, f32)
    for k in range(TOP_K):
        chosen = chosen + (lane == idxs[k]).astype(f32)
    ii = lax.broadcasted_iota(i32, (tl, tl), 0)
    jj = lax.broadcasted_iota(i32, (tl, tl), 1)
    before = _dot((jj < ii).astype(bf16), chosen.astype(bf16)) + run_sc[...]
    te = jnp.zeros(logits.shape, i32)
    gate = jnp.zeros(logits.shape, f32)
    for k in range(TOP_K):
        rank = jnp.sum(jnp.where(lane == idxs[k], before, 0.0), axis=-1, keepdims=True).astype(i32)
        te = jnp.where(lane == k, idxs[k], te)
        te = jnp.where(lane == TOP_K + k, rank, te)
        gate = jnp.where(lane == k, es[k] / tot, gate)
    te_ref[...] = te
    gate_ref[...] = gate
    run_sc[...] = run_sc[...] + jnp.sum(chosen, axis=0, keepdims=True)
    cnt_ref[...] = run_sc[...]


def _mix_router(att, ssm, x2d, w):
    T, D = x2d.shape
    tl = min(ROW_TILE, T)
    row = lambda n: pl.BlockSpec((tl, n), lambda i: (i, 0))
    full = lambda a: pl.BlockSpec(a.shape, lambda i: (0,) * a.ndim)
    ws = [w['wo_att'], w['wo_ssm'], w['ln1_g'], w['ln1_b'], w['wr_hi'], w['wr_lo'], w['b_router']]
    return pl.pallas_call(
        _mix_router_kernel, grid=(T // tl,),
        in_specs=[row(D_ATT), row(D_SSM), row(D)] + [full(a) for a in ws],
        out_specs=[row(D), row(LANES), row(LANES), pl.BlockSpec((1, LANES), lambda i: (0, 0))],
        out_shape=[jax.ShapeDtypeStruct((T, D), f32), jax.ShapeDtypeStruct((T, LANES), i32),
                   jax.ShapeDtypeStruct((T, LANES), f32), jax.ShapeDtypeStruct((1, LANES), f32)],
        scratch_shapes=[pltpu.VMEM((1, LANES), f32)],
        compiler_params=_params(("arbitrary",)),
    )(att, ssm, x2d, *ws)


def _route(te, counts_f):
    T = te.shape[0]
    n_assign = T * TOP_K
    counts = counts_f[0, :N_EXPERTS].astype(i32)
    padded = (counts + MOE_BLOCK - 1) // MOE_BLOCK * MOE_BLOCK
    pad_end = jnp.cumsum(padded)
    pad_start = pad_end - padded
    top_e, rank = te[:, :TOP_K], te[:, TOP_K:2 * TOP_K]
    experts = jnp.arange(N_EXPERTS, dtype=i32)
    slot = rank + jnp.sum(jnp.where(top_e[..., None] == experts, pad_start, 0), axis=-1)
    n_blocks = -(-n_assign // MOE_BLOCK) + N_EXPERTS
    block_start = jnp.arange(n_blocks, dtype=i32) * MOE_BLOCK
    block_e = jnp.minimum(jnp.sum(block_start[:, None] >= pad_end[None, :], axis=-1), N_EXPERTS - 1).astype(i32)
    n_used = (pad_end[-1] // MOE_BLOCK).astype(i32).reshape(1)
    return block_e, n_used, slot.astype(i32), n_blocks


def _slot_tiles(slot, tl):
    T = slot.shape[0]
    return slot.reshape(T // tl, tl, TOP_K).transpose(0, 2, 1).reshape(T // tl, 1, TOP_K * tl)


def _dispatch_kernel(slot_ref, x_ref, xs_in_hbm, xs_hbm, sem):
    del xs_in_hbm
    n_copies = slot_ref.shape[-1]
    tl = n_copies // TOP_K

    def body(r, carry):
        src = x_ref.at[pl.ds(r, 1)]
        for k in range(TOP_K):
            pltpu.make_async_copy(src, xs_hbm.at[pl.ds(slot_ref[0, 0, k * tl + r], 1)], sem.at[0]).start()
        return carry
    lax.fori_loop(0, tl, body, 0, unroll=4)
    pltpu.make_async_copy(xs_hbm.at[pl.ds(0, n_copies)], xs_hbm.at[pl.ds(0, n_copies)], sem.at[0]).wait()


def _dispatch(x1, slot, n_blocks):
    T, D = x1.shape
    tl = min(ROW_TILE, T)
    tiles = _slot_tiles(slot, tl)
    xs0 = jnp.zeros((n_blocks * MOE_BLOCK, D), f32)
    return pl.pallas_call(
        _dispatch_kernel, grid=(T // tl,),
        in_specs=[pl.BlockSpec((1, 1, TOP_K * tl), lambda i: (i, 0, 0), memory_space=pltpu.SMEM),
                  pl.BlockSpec((tl, D), lambda i: (i, 0)), pl.BlockSpec(memory_space=pl.ANY)],
        out_specs=pl.BlockSpec(memory_space=pl.ANY),
        out_shape=jax.ShapeDtypeStruct(xs0.shape, f32),
        scratch_shapes=[pltpu.SemaphoreType.DMA((1,))],
        input_output_aliases={2: 0},
        compiler_params=_params(("arbitrary",)),
    )(tiles, x1, xs0)


def _deinterleave_kernel(w_ref, g_ref, u_ref):
    wb = w_ref[0].astype(bf16)
    tn = g_ref.shape[-1]
    r = lax.broadcasted_iota(i32, (2 * tn, tn), 0)
    c = lax.broadcasted_iota(i32, (2 * tn, tn), 1)
    g_ref[0] = _dot(wb, (r == 2 * c).astype(bf16)).astype(bf16)
    u_ref[0] = _dot(wb, (r == 2 * c + 1).astype(bf16)).astype(bf16)


def _deinterleave(w_gu):
    E, D, F2 = w_gu.shape
    tn = 2 * LANES
    out = jax.ShapeDtypeStruct((E, D, F2 // 2), bf16)
    return pl.pallas_call(
        _deinterleave_kernel, grid=(E, F2 // (2 * tn)),
        in_specs=[pl.BlockSpec((1, D, 2 * tn), lambda e, j: (e, 0, j))],
        out_specs=[pl.BlockSpec((1, D, tn), lambda e, j: (e, 0, j))] * 2,
        out_shape=[out, out],
        compiler_params=_params(("parallel", "parallel")),
    )(w_gu)


def _row_gather(idx_ref, n, src_hbm, dst, sem):
    def body(r, carry):
        pltpu.make_async_copy(src_hbm.at[pl.ds(idx_ref[0, 0, r], 1)], dst.at[pl.ds(r, 1)], sem).start()
        return carry
    lax.fori_loop(0, n, body, 0, unroll=8)


def _ffn_kernel(be_ref, nu_ref, xs_ref, wg_ref, wu_ref, bg_ref, bu_ref, wd_ref, bd_ref, y_ref):
    i = pl.program_id(0)
    n_used = nu_ref[0]

    @pl.when(i < n_used)
    def _():
        xb = xs_ref[...].astype(bf16)
        g = _dot(xb, wg_ref[0]) + bg_ref[0]
        u = _dot(xb, wu_ref[0]) + bu_ref[0]
        g = jnp.minimum(g, SWIGLU_LIMIT)
        u = jnp.clip(u, -SWIGLU_LIMIT, SWIGLU_LIMIT)
        act = (u + 1.0) * g * _sigmoid(SWIGLU_ALPHA * g)
        y_ref[...] = _dot(act.astype(bf16), wd_ref[0]) + bd_ref[0]

    @pl.when(i >= n_used)
    def _():
        y_ref[...] = jnp.zeros(y_ref.shape, f32)


def _expert_ffn(xs, block_e, n_used, w):
    D = xs.shape[1]
    n_blocks = xs.shape[0] // MOE_BLOCK
    F = w['w_g'].shape[2]
    grid_spec = pltpu.PrefetchScalarGridSpec(
        num_scalar_prefetch=2, grid=(n_blocks,),
        in_specs=[
            pl.BlockSpec((MOE_BLOCK, D), lambda i, be, nu: (jnp.minimum(i, nu[0] - 1), 0)),
            pl.BlockSpec((1, D, F), lambda i, be, nu: (be[i], 0, 0)),
            pl.BlockSpec((1, D, F), lambda i, be, nu: (be[i], 0, 0)),
            pl.BlockSpec((1, 1, F), lambda i, be, nu: (be[i], 0, 0)),
            pl.BlockSpec((1, 1, F), lambda i, be, nu: (be[i], 0, 0)),
            pl.BlockSpec((1, F, D), lambda i, be, nu: (be[i], 0, 0)),
            pl.BlockSpec((1, 1, D), lambda i, be, nu: (be[i], 0, 0))],
        out_specs=pl.BlockSpec((MOE_BLOCK, D), lambda i, be, nu: (i, 0)))
    return pl.pallas_call(
        _ffn_kernel, grid_spec=grid_spec,
        out_shape=jax.ShapeDtypeStruct((n_blocks * MOE_BLOCK, D), f32),
        compiler_params=_params(("arbitrary",)),
    )(block_e, n_used, xs, w['w_g'], w['w_u'], w['b_g'], w['b_u'], w['w_d'], w['b_d'])


def _combine_kernel(slot_ref, slotn_ref, ys_hbm, x1_ref, gate_ref, g_ref, b_ref, y_ref, buf, sem):
    i = pl.program_id(0)
    n = pl.num_programs(0)
    tl = x1_ref.shape[0]
    cur = i % 2

    @pl.when(i == 0)
    def _():
        _row_gather(slot_ref, TOP_K * tl, ys_hbm, buf.at[0], sem.at[0])

    @pl.when(i + 1 < n)
    def _():
        _row_gather(slotn_ref, TOP_K * tl, ys_hbm, buf.at[1 - cur], sem.at[1 - cur])

    pltpu.make_async_copy(buf.at[cur], buf.at[cur], sem.at[cur]).wait()
    gate = gate_ref[...]
    ff = gate[:, 0:1] * buf[cur, 0:tl]
    for k in range(1, TOP_K):
        ff = ff + gate[:, k:k + 1] * buf[cur, k * tl:(k + 1) * tl]
    y_ref[...] = _layer_norm(DEEPNORM_ALPHA * x1_ref[...] + ff, g_ref[...], b_ref[...])


def _combine(ys, slot, x1, gate, ln_g, ln_b):
    T, D = x1.shape
    tl = min(COMB_TILE, T)
    nt = T // tl
    slot_tiles = _slot_tiles(slot, tl)
    row = lambda n: pl.BlockSpec((tl, n), lambda i: (i, 0))
    full = lambda a: pl.BlockSpec(a.shape, lambda i: (0,) * a.ndim)
    return pl.pallas_call(
        _combine_kernel, grid=(nt,),
        in_specs=[pl.BlockSpec((1, 1, TOP_K * tl), lambda i: (i, 0, 0), memory_space=pltpu.SMEM),
                  pl.BlockSpec((1, 1, TOP_K * tl), lambda i: (jnp.minimum(i + 1, nt - 1), 0, 0),
                               memory_space=pltpu.SMEM),
                  pl.BlockSpec(memory_space=pl.ANY), row(D), row(LANES), full(ln_g), full(ln_b)],
        out_specs=row(D), out_shape=jax.ShapeDtypeStruct((T, D), f32),
        scratch_shapes=[pltpu.VMEM((2, TOP_K * tl, D), f32), pltpu.SemaphoreType.DMA((2,))],
        compiler_params=_params(("arbitrary",)),
    )(slot_tiles, slot_tiles, ys, x1, gate, ln_g, ln_b)


def _prep_weights(l, w_in, conv_w, conv_b, dt_bias, a_log, d_skip, ssm_norm_g, w_o, ln1_g, ln1_b,
                  w_router, b_router, w_gate_up, b_gate_up, w_down, b_down, ln2_g, ln2_b):
    wi = w_in[l]
    c0, c1, c2, c3 = D_QK_ALL, 2 * D_QK_ALL, 2 * D_QK_ALL + D_ATT, 2 * D_QK_ALL + D_ATT + D_SSM
    c4 = c3 + CONV_DIM
    wdt = wi[:, c4:c4 + H_S]
    pad_lane = lambda v, fill=0.0: jnp.pad(v.astype(f32).reshape(1, -1), ((0, 0), (0, LANES - v.shape[-1])),
                                            constant_values=fill)
    pad_col = lambda v: jnp.pad(v.astype(f32).reshape(-1, 1), ((0, BF16_ROWS - v.shape[-1]), (0, 0)))
    wr = jnp.pad(w_router[l].astype(f32), ((0, 0), (0, LANES - N_EXPERTS)))
    wr_hi = wr.astype(bf16)
    wgu = w_gate_up[l]
    return {
        'wq': wi[:, :c0].astype(bf16), 'wqT': wi[:, :c0].T.astype(bf16),
        'wk': wi[:, c0:c1].astype(bf16), 'wv': wi[:, c1:c2].astype(bf16), 'wvT': wi[:, c1:c2].T.astype(bf16),
        'wz': wi[:, c2:c3].astype(bf16), 'wx': wi[:, c3:c4].astype(bf16),
        'wdt': jnp.pad(wdt, ((0, 0), (0, LANES - H_S))).astype(bf16),
        'wdtT': jnp.pad(wdt.T, ((0, BF16_ROWS - H_S), (0, 0))).astype(bf16),
        'conv_w': conv_w[l].astype(f32), 'conv_b': conv_b[l].astype(f32).reshape(1, -1),
        'dt_bias': pad_lane(dt_bias[l]), 'dt_biasT': pad_col(dt_bias[l]),
        'a_log': pad_lane(a_log[l]), 'a_logT': pad_col(a_log[l]),
        'd_skip': jnp.repeat(d_skip[l].astype(f32), SSM_HEADDIM).reshape(1, -1),
        'ssm_g': ssm_norm_g[l].astype(f32).reshape(1, -1),
        'wo_att': w_o[l][:D_ATT].astype(bf16), 'wo_ssm': w_o[l][D_ATT:].astype(bf16),
        'ln1_g': ln1_g[l].astype(f32).reshape(1, -1), 'ln1_b': ln1_b[l].astype(f32).reshape(1, -1),
        'wr_hi': wr_hi, 'wr_lo': (wr - wr_hi.astype(f32)).astype(bf16),
        'b_router': pad_lane(b_router[l], NEG_INF),
        'w_gu': wgu,
        'b_g': b_gate_up[l][:, None, 0::2].astype(f32), 'b_u': b_gate_up[l][:, None, 1::2].astype(f32),
        'w_d': w_down[l].astype(bf16), 'b_d': b_down[l][:, None, :].astype(f32),
        'ln2_g': ln2_g[l].astype(f32).reshape(1, -1), 'ln2_b': ln2_b[l].astype(f32).reshape(1, -1),
    }


def _trunk_layer(x, l, w, rel_bias, lam_args, attn_g, k_past, v_past, h0, conv0):
    B, L, D = x.shape
    x2d = x.reshape(B * L, D)
    lam_init = 0.8 - 0.6 * math.exp(-0.3 * l)
    prompt = k_past is None
    proj = _in_proj(x2d, w, prompt)
    k, v, z, xbc, dt, dtT = proj[:6]
    if prompt:
        kb, qT, v1T = proj[6:]
        att = _attn_prompt(qT, kb, v1T, rel_bias, lam_args, attn_g, B, L, lam_init)
    else:
        att = _attn_sample(proj[6], k, v, k_past, v_past, rel_bias, lam_args, attn_g, lam_init)
    ssm, h_new, conv_new = _ssd(z, xbc, dt, dtT, h0, conv0, w, B, L)
    x1, te, gate, counts = _mix_router(att, ssm, x2d, w)
    block_e, n_used, slot, n_blocks = _route(te, counts)
    xs = _dispatch(x1, slot, n_blocks)
    ys = _expert_ffn(xs, block_e, n_used, w)
    y = _combine(ys, slot, x1, gate, w['ln2_g'], w['ln2_b'])
    return (y.reshape(B, L, D), k.reshape(B, L, H_A, 2 * DQK), v.reshape(B, L, H_A, DV), h_new, conv_new)


def kernel(x_prompt, x_sample, cache_k, cache_v, state_ssm, state_conv, rel_bias, w_in, lambda_q1, lambda_k1, lambda_q2, lambda_k2, attn_norm_g, conv_w, conv_b, dt_bias, a_log, d_skip, ssm_norm_g, w_o, ln1_g, ln1_b, w_router, b_router, w_gate_up, b_gate_up, w_down, b_down, ln2_g, ln2_b):
    yp, ys = x_prompt, x_sample
    bp = x_prompt.shape[0]
    depth = w_in.shape[0]
    outs = [[] for _ in range(8)]
    for l in range(depth):
        w = _prep_weights(l, w_in, conv_w, conv_b, dt_bias, a_log, d_skip, ssm_norm_g, w_o, ln1_g, ln1_b,
                          w_router, b_router, w_gate_up, b_gate_up, w_down, b_down, ln2_g, ln2_b)
        w['w_g'], w['w_u'] = _deinterleave(w.pop('w_gu'))
        lam_args = [a[l].astype(f32).reshape(1, -1) for a in (lambda_q1, lambda_k1, lambda_q2, lambda_k2)]
        h0 = jnp.zeros((bp, H_S, SSM_HEADDIM, D_STATE), f32)
        c0 = jnp.zeros((bp, CONV_W - 1, CONV_DIM), f32)
        yp, kp, vp, hp, cp = _trunk_layer(yp, l, w, rel_bias, lam_args, attn_norm_g[l], None, None, h0, c0)
        ys, ks, vs, hs, cs = _trunk_layer(ys, l, w, rel_bias, lam_args, attn_norm_g[l], cache_k[l], cache_v[l],
                                          state_ssm[l], state_conv[l])
        for lst, a in zip(outs, (kp, vp, hp, cp, ks, vs, hs, cs)):
            lst.append(a)
    return (yp, ys) + tuple(jnp.stack(o) for o in outs)
```

```python
import functools
import math

import numpy as np
import jax
import jax.numpy as jnp
from jax import lax
from jax.experimental import pallas as pl
from jax.experimental.pallas import tpu as pltpu

f32 = jnp.float32
bf16 = jnp.bfloat16
i32 = jnp.int32

CHUNK = 64
H_A = 8
DQK = 32
DV = 2 * DQK
D_ATT = H_A * DV
SSM_HEADDIM = 64
H_S = 8
D_SSM = H_S * SSM_HEADDIM
SSM_GROUPS = 2
HEADS_PER_GROUP = H_S // SSM_GROUPS
D_STATE = 128
CONV_W = 4
CONV_DIM = D_SSM + 2 * SSM_GROUPS * D_STATE
D_QK_ALL = H_A * 2 * DQK
N_BUCKETS = 32
MAX_DISTANCE = 128
N_EXPERTS = 32
TOP_K = 4
SWIGLU_LIMIT = 7.0
SWIGLU_ALPHA = 1.702
MOE_BLOCK = 256
LN_EPS = 1e-5
RMS_EPS = 1e-5
NEG_INF = -1e30
DEPTH = 1
DEEPNORM_ALPHA = (2.0 * DEPTH) ** 0.25
LOG2E = math.log2(math.e)

LANES = 128
SUBLANES = 8
BF16_ROWS = 16
VMEM_LIMIT = 48 * 1024 * 1024

ROW_TILE = 512
ATT_TILE = 512
ATT_SUB = 256
SSD_TILE = 256
V_ROWS = DV + BF16_ROWS
COMB_TILE = 256
FFN_BLOCKS_PER_STEP = 2


def _params(semantics):
    return pltpu.CompilerParams(dimension_semantics=semantics, vmem_limit_bytes=VMEM_LIMIT)


def _dot(a, b):
    return jnp.dot(a, b, preferred_element_type=f32)


def _dot_nt(a, b):
    return lax.dot_general(a, b, (((1,), (1,)), ((), ())), preferred_element_type=f32)


def _dot_tn(a, b):
    return lax.dot_general(a, b, (((0,), (0,)), ((), ())), preferred_element_type=f32)


def _split3(a):
    hi = a.astype(bf16)
    r1 = a - hi.astype(f32)
    mid = r1.astype(bf16)
    lo = (r1 - mid.astype(f32)).astype(bf16)
    return hi, mid, lo


def _dot_f32_lhs(a, b_exact):
    hi, mid, lo = _split3(a)
    return _dot(hi, b_exact) + _dot(mid, b_exact) + _dot(lo, b_exact)


def _dot_f32_rhs(a_exact, b):
    hi, mid, lo = _split3(b)
    return _dot(a_exact, hi) + _dot(a_exact, mid) + _dot(a_exact, lo)


def _softplus(x):
    return jnp.maximum(x, 0.0) + jnp.log1p(jnp.exp(-jnp.abs(x)))


def _sigmoid(x):
    return 1.0 / (1.0 + jnp.exp(-x))


def _layer_norm(y, g, b):
    mu = jnp.mean(y, axis=-1, keepdims=True)
    yc = y - mu
    var = jnp.mean(yc * yc, axis=-1, keepdims=True)
    return yc * lax.rsqrt(var + LN_EPS) * g + b


def _lambda(lq1_ref, lk1_ref, lq2_ref, lk2_ref, lam_init):
    s1 = jnp.sum(lq1_ref[...] * lk1_ref[...], axis=-1, keepdims=True)
    s2 = jnp.sum(lq2_ref[...] * lk2_ref[...], axis=-1, keepdims=True)
    return jnp.exp(s1) - jnp.exp(s2) + lam_init


def _in_proj_kernel(x_ref, wk_ref, wv_ref, wz_ref, wx_ref, wdt_ref, wdtT_ref, wq_ref, wvT_ref, *outs, prompt):
    xb = x_ref[...].astype(bf16)
    k_ref, v_ref, z_ref, xbc_ref, dt_ref, dtT_ref = outs[:6]
    k = _dot(xb, wk_ref[...])
    k_ref[...] = k
    v_ref[...] = _dot(xb, wv_ref[...])
    z_ref[...] = _dot(xb, wz_ref[...])
    xbc_ref[...] = _dot(xb, wx_ref[...])
    dt_ref[...] = _dot(xb, wdt_ref[...])
    dtT_ref[...] = _dot_nt(wdtT_ref[...], xb)
    scale = DQK ** -0.5
    if prompt:
        kb_ref, qT_ref, v1T_ref = outs[6:]
        kb_ref[...] = k.astype(bf16)
        qT_ref[...] = (_dot_nt(wq_ref[...], xb) * (scale * LOG2E)).astype(bf16)
        vT = _dot_nt(wvT_ref[...], xb).astype(bf16)
        ones = jnp.ones((BF16_ROWS, vT.shape[1]), bf16)
        for h in range(H_A):
            v1T_ref[h * V_ROWS:h * V_ROWS + DV, :] = vT[h * DV:(h + 1) * DV, :]
            v1T_ref[h * V_ROWS + DV:(h + 1) * V_ROWS, :] = ones
    else:
        (q_ref,) = outs[6:]
        q_ref[...] = (_dot(xb, wq_ref[...]) * scale).astype(bf16)


def _in_proj(x2d, w, prompt):
    T, D = x2d.shape
    tl = min(ROW_TILE, T)
    grid = (T // tl,)
    row = lambda n: pl.BlockSpec((tl, n), lambda i: (i, 0))
    col = lambda n: pl.BlockSpec((n, tl), lambda i: (0, i))
    full = lambda a: pl.BlockSpec(a.shape, lambda i: (0,) * a.ndim)
    wq = w['wqT'] if prompt else w['wq']
    ins = [x2d, w['wk'], w['wv'], w['wz'], w['wx'], w['wdt'], w['wdtT'], wq, w['wvT']]
    in_specs = [row(D)] + [full(a) for a in ins[1:]]
    out_shape = [jax.ShapeDtypeStruct((T, D_QK_ALL), f32), jax.ShapeDtypeStruct((T, D_ATT), f32),
                 jax.ShapeDtypeStruct((T, D_SSM), f32), jax.ShapeDtypeStruct((T, CONV_DIM), f32),
                 jax.ShapeDtypeStruct((T, LANES), f32), jax.ShapeDtypeStruct((BF16_ROWS, T), f32)]
    out_specs = [row(D_QK_ALL), row(D_ATT), row(D_SSM), row(CONV_DIM), row(LANES), col(BF16_ROWS)]
    if prompt:
        out_shape += [jax.ShapeDtypeStruct((T, D_QK_ALL), bf16), jax.ShapeDtypeStruct((D_QK_ALL, T), bf16),
                      jax.ShapeDtypeStruct((H_A * V_ROWS, T), bf16)]
        out_specs += [row(D_QK_ALL), col(D_QK_ALL), col(H_A * V_ROWS)]
    else:
        out_shape += [jax.ShapeDtypeStruct((T, D_QK_ALL), bf16)]
        out_specs += [row(D_QK_ALL)]
    return pl.pallas_call(
        functools.partial(_in_proj_kernel, prompt=prompt),
        grid=grid, in_specs=in_specs, out_specs=out_specs, out_shape=out_shape,
        compiler_params=_params(("parallel",)),
    )(*ins)


def _t5_bucket(rel):
    half = N_BUCKETS // 2
    max_exact = half // 2
    n = jnp.abs(rel)
    large = max_exact + (jnp.log(jnp.maximum(n, 1).astype(f32) / max_exact)
                         / math.log(MAX_DISTANCE / max_exact) * (half - max_exact)).astype(i32)
    large = jnp.minimum(large, half - 1)
    return jnp.where(rel > 0, half, 0) + jnp.where(n < max_exact, n, large)


def _far_bucket(min_dist):
    half = N_BUCKETS // 2
    max_exact = half // 2
    v = max_exact + int(math.log(min_dist / max_exact) / math.log(MAX_DISTANCE / max_exact) * (half - max_exact))
    return half - 1 if (min_dist >= max_exact and v - 1 >= half - 1) else None


def _bias_lookup(rel_bias, bucket):
    onehot = (bucket[..., None] == jnp.arange(N_BUCKETS, dtype=i32)).astype(f32)
    return jnp.dot(onehot, rel_bias.astype(f32), precision=lax.Precision.HIGHEST)


def _prompt_bias_tiles(rel_bias, ta):
    j = jnp.arange(ta, dtype=i32)[:, None]
    i = jnp.arange(ta, dtype=i32)[None, :]
    tiles = []
    for d in range(2):
        rel = j - i - d * ta
        b = _bias_lookup(rel_bias, _t5_bucket(rel))
        allowed = ((j // CHUNK) <= (i // CHUNK)) if d == 0 else jnp.ones((ta, ta), bool)
        tiles.append(jnp.where(allowed[..., None], b, NEG_INF))
    return jnp.transpose(jnp.stack(tiles), (3, 0, 1, 2))


def _sample_bias(rel_bias, past, s):
    q_pos = past + jnp.arange(s, dtype=i32)
    k_pos = jnp.arange(past + s, dtype=i32)
    rel = k_pos[None, :] - q_pos[:, None]
    b = _bias_lookup(rel_bias, _t5_bucket(rel))
    allowed = (k_pos[None, :] // CHUNK) <= (q_pos[:, None] // CHUNK)
    return jnp.transpose(jnp.where(allowed[..., None], b, NEG_INF), (2, 0, 1))


def _attn_prompt_kernel(qi_tab, ki_tab, qT_ref, k_ref, v1T_ref, bias_ref, cfar_ref,
                        lq1_ref, lk1_ref, lq2_ref, lk2_ref, g_ref, o_ref,
                        qm_sc, m_sc, acc_sc, oT_sc, *, lam_init):
    p = pl.program_id(1)
    qi = qi_tab[p]
    ki = ki_tab[p]
    d = qi - ki
    ta = k_ref.shape[0]
    n_hp = H_A // 2

    @pl.when(ki == 0)
    def _():
        m_sc[...] = jnp.full(m_sc.shape, NEG_INF, f32)
        acc_sc[...] = jnp.zeros(acc_sc.shape, f32)
        row = lax.broadcasted_iota(i32, (LANES, ta), 0)
        for hp in range(n_hp):
            qt = qT_ref[hp * LANES:(hp + 1) * LANES, :]
            for gi in range(4):
                keep = (row >= gi * DQK) & (row < (gi + 1) * DQK)
                qm_sc[hp, :, gi * ta:(gi + 1) * ta] = jnp.where(keep, qt, jnp.zeros_like(qt))

    def scores(hp):
        return _dot(k_ref[:, hp * LANES:(hp + 1) * LANES], qm_sc[hp])

    nb = ta // ATT_SUB

    def near_bias(hp, step_d):
        rows = []
        for jb in range(nb):
            cols = []
            for g in range(4):
                for ib in range(nb):
                    dd = step_d * nb + ib - jb
                    if dd < 0:
                        blk = jnp.full((ATT_SUB, ATT_SUB), NEG_INF * LOG2E, f32)
                    elif dd < 2:
                        blk = bias_ref[2 * hp + g // 2, dd]
                    else:
                        lo = g * ta + ib * ATT_SUB
                        blk = jnp.broadcast_to(cfar_ref[hp][:, lo:lo + ATT_SUB], (ATT_SUB, ATT_SUB))
                    cols.append(blk)
            rows.append(jnp.concatenate(cols, axis=1))
        return jnp.concatenate(rows, axis=0)

    def update(step_d):
        s_next = scores(0)
        for hp in range(n_hp):
            s = s_next
            if hp + 1 < n_hp:
                s_next = scores(hp + 1)
            m_old = m_sc[hp]
            if step_d is not None:
                s = s + near_bias(hp, step_d)
                m_new = jnp.maximum(m_old, jnp.max(s, axis=0, keepdims=True))
                shift = m_new
            else:
                c = cfar_ref[hp]
                m_new = jnp.maximum(m_old, jnp.max(s, axis=0, keepdims=True) + c)
                shift = m_new - c
            alpha = jnp.exp2(m_old - m_new)
            pT = jnp.exp2(s - shift).astype(bf16)
            for hh in range(2):
                h = 2 * hp + hh
                cols = slice(hh * 2 * ta, (hh + 1) * 2 * ta)
                pv = _dot(v1T_ref[h * V_ROWS:(h + 1) * V_ROWS, :], pT[:, cols])
                acc_sc[h] = alpha[:, cols] * acc_sc[h] + pv
            m_sc[hp] = m_new

    @pl.when(d == 0)
    def _():
        update(0)

    @pl.when(d == 1)
    def _():
        update(1)

    @pl.when(d >= 2)
    def _():
        update(None)

    @pl.when(d == 0)
    def _():
        lam = _lambda(lq1_ref, lk1_ref, lq2_ref, lk2_ref, lam_init)
        for h in range(H_A):
            a = acc_sc[h]
            a0 = a[:, :ta]
            a1 = a[:, ta:]
            o = a0[:DV] / a0[DV:DV + 1] - lam * (a1[:DV] / a1[DV:DV + 1])
            o = o * lax.rsqrt(jnp.mean(o * o, axis=0, keepdims=True) + RMS_EPS) * g_ref[...] * (1.0 - lam_init)
            oT_sc[h * DV:(h + 1) * DV, :] = o
        o_ref[...] = oT_sc[...].T.astype(bf16)


def _attn_prompt(qT, kb, v1T, rel_bias, lam_args, attn_g, B, L, lam_init):
    ta = min(ATT_TILE, L)
    assert L % ta == 0 and ta % ATT_SUB == 0 and ATT_SUB % CHUNK == 0
    nt = L // ta
    far = _far_bucket(ATT_SUB + 1)
    assert L <= 2 * ATT_SUB or far is not None, "key blocks two or more behind must share one bias bucket"
    far = far if far is not None else N_BUCKETS // 2 - 1
    pairs = [(q, k) for q in range(nt) for k in range(q + 1)]
    qi_tab = jnp.asarray([q for q, _ in pairs], i32)
    ki_tab = jnp.asarray([k for _, k in pairs], i32)
    bias = _prompt_bias_tiles(rel_bias, ATT_SUB) * LOG2E
    cfar = jnp.repeat((rel_bias.astype(f32)[far] * LOG2E).reshape(H_A // 2, 2), 2 * ta, axis=1)[:, None, :]
    g_col = attn_g.astype(f32).reshape(DV, 1)
    T = B * L
    n_hp = H_A // 2
    full = lambda a: pl.BlockSpec(a.shape, lambda b, p, qt, kt: (0,) * a.ndim)
    grid_spec = pltpu.PrefetchScalarGridSpec(
        num_scalar_prefetch=2, grid=(B, len(pairs)),
        in_specs=[
            pl.BlockSpec((D_QK_ALL, ta), lambda b, p, qt, kt: (0, b * nt + qt[p])),
            pl.BlockSpec((ta, D_QK_ALL), lambda b, p, qt, kt: (b * nt + kt[p], 0)),
            pl.BlockSpec((H_A * V_ROWS, ta), lambda b, p, qt, kt: (0, b * nt + kt[p])),
            full(bias), full(cfar)] + [full(a) for a in lam_args] + [full(g_col)],
        out_specs=pl.BlockSpec((ta, D_ATT), lambda b, p, qt, kt: (b * nt + qt[p], 0)),
        scratch_shapes=[pltpu.VMEM((n_hp, LANES, 4 * ta), bf16), pltpu.VMEM((n_hp, 1, 4 * ta), f32),
                        pltpu.VMEM((H_A, V_ROWS, 2 * ta), f32), pltpu.VMEM((D_ATT, ta), f32)])
    return pl.pallas_call(
        functools.partial(_attn_prompt_kernel, lam_init=lam_init),
        grid_spec=grid_spec, out_shape=jax.ShapeDtypeStruct((T, D_ATT), bf16),
        compiler_params=_params(("parallel", "arbitrary")),
    )(qi_tab, ki_tab, qT, kb, v1T, bias, cfar, *lam_args, g_col)


def _attn_sample_kernel(q_ref, kn_ref, vn_ref, kc_ref, vc_ref, bc_ref, bn_ref,
                        lq1_ref, lk1_ref, lq2_ref, lk2_ref, g2_ref, o_ref, *, lam_init):
    s_len = q_ref.shape[0]
    lane = lax.broadcasted_iota(i32, (s_len, LANES), 1)
    lam = _lambda(lq1_ref, lk1_ref, lq2_ref, lk2_ref, lam_init)
    for hp in range(H_A // 2):
        sl = slice(hp * LANES, (hp + 1) * LANES)
        qt = q_ref[:, sl]
        kct = kc_ref[0, :, sl].astype(bf16)
        vct = vc_ref[0, :, sl].astype(bf16)
        knt = kn_ref[:, sl].astype(bf16)
        vnt = vn_ref[:, sl].astype(bf16)
        outs = []
        for hh in range(2):
            h = 2 * hp + hh
            oc = []
            for c in range(2):
                gi = 2 * hh + c
                qm = jnp.where((lane >= gi * DQK) & (lane < (gi + 1) * DQK), qt, jnp.zeros_like(qt))
                s1 = _dot_nt(qm, kct) + bc_ref[h]
                s2 = _dot_nt(qm, knt) + bn_ref[h]
                m = jnp.maximum(jnp.max(s1, axis=-1, keepdims=True), jnp.max(s2, axis=-1, keepdims=True))
                p1 = jnp.exp(s1 - m)
                p2 = jnp.exp(s2 - m)
                l = jnp.sum(p1, axis=-1, keepdims=True) + jnp.sum(p2, axis=-1, keepdims=True)
                pv = _dot(p1.astype(bf16), vct) + _dot(p2.astype(bf16), vnt)
                oc.append(pv / l)
            o = oc[0] - lam * oc[1]
            in_head = (lane >= hh * DV) & (lane < (hh + 1) * DV)
            ms = jnp.sum(jnp.where(in_head, o * o, 0.0), axis=-1, keepdims=True) * (1.0 / DV)
            outs.append(o * lax.rsqrt(ms + RMS_EPS))
        ot = jnp.where(lane < DV, outs[0], outs[1]) * g2_ref[...] * (1.0 - lam_init)
        o_ref[:, sl] = ot.astype(bf16)


def _attn_sample(q, k_new, v_new, cache_k, cache_v, rel_bias, lam_args, attn_g, lam_init):
    nb, past = cache_k.shape[0], cache_k.shape[1]
    T = q.shape[0]
    s = T // nb
    kc = cache_k.reshape(nb, past, D_QK_ALL)
    vc = cache_v.reshape(nb, past, D_ATT)
    bias = _sample_bias(rel_bias, past, s)
    bc, bn = bias[:, :, :past], bias[:, :, past:]
    g2 = jnp.tile(attn_g.astype(f32), 2).reshape(1, LANES)
    full = lambda a: pl.BlockSpec(a.shape, lambda b: (0,) * a.ndim)
    row = lambda n: pl.BlockSpec((s, n), lambda b: (b, 0))
    return pl.pallas_call(
        functools.partial(_attn_sample_kernel, lam_init=lam_init),
        grid=(nb,),
        in_specs=[row(D_QK_ALL), row(D_QK_ALL), row(D_ATT),
                  pl.BlockSpec((1, past, D_QK_ALL), lambda b: (b, 0, 0)),
                  pl.BlockSpec((1, past, D_ATT), lambda b: (b, 0, 0)),
                  full(bc), full(bn)] + [full(a) for a in lam_args] + [full(g2)],
        out_specs=row(D_ATT), out_shape=jax.ShapeDtypeStruct((T, D_ATT), bf16),
        compiler_params=_params(("parallel",)),
    )(q, k_new, v_new, kc, vc, bc, bn, *lam_args, g2)


def _ssd_kernel(z_ref, xbc_ref, dt_ref, dtT_ref, h0_ref, c0_ref, cw_ref, cb_ref, dtb_ref, dtbT_ref,
                alog_ref, alogT_ref, dsk_ref, g_ref, y_ref, hout_ref, cout_ref, xpad_sc, h_sc):
    q = xbc_ref.shape[0]
    c = pl.program_id(1)
    gw = HEADS_PER_GROUP * SSM_HEADDIM

    @pl.when(c == 0)
    def _():
        xpad_sc[0:SUBLANES] = c0_ref[0]
        h_sc[...] = h0_ref[0]

    xpad_sc[SUBLANES:SUBLANES + q] = xbc_ref[...]
    first = SUBLANES - (CONV_W - 1)
    conv = cb_ref[...]
    for i in range(CONV_W):
        conv = conv + xpad_sc[first + i:first + i + q] * cw_ref[i:i + 1, :]
    tail = xpad_sc[q:q + SUBLANES]
    xpad_sc[0:SUBLANES] = tail
    cout_ref[0] = tail
    act = conv * _sigmoid(conv)
    xs = act[:, :D_SSM]

    lane = lax.broadcasted_iota(i32, (1, LANES), 1)
    a_row = jnp.where(lane < H_S, -jnp.exp(alog_ref[...]), 0.0)
    dt = _softplus(dt_ref[...] + dtb_ref[...])
    ii = lax.broadcasted_iota(i32, (q, q), 0)
    jj = lax.broadcasted_iota(i32, (q, q), 1)
    causal = jj <= ii
    acs = _dot_f32_rhs(causal.astype(bf16), dt * a_row)
    sub = lax.broadcasted_iota(i32, (BF16_ROWS, 1), 0)
    a_col = jnp.where(sub < H_S, -jnp.exp(alogT_ref[...]), 0.0)
    dtT = _softplus(dtT_ref[0] + dtbT_ref[...])
    acsT = _dot_f32_lhs(dtT * a_col, (ii <= jj).astype(bf16))

    er = lax.broadcasted_iota(i32, (LANES, D_SSM), 0)
    ec = lax.broadcasted_iota(i32, (LANES, D_SSM), 1)
    expand = (ec // SSM_HEADDIM == er).astype(bf16)
    dt_x = _dot_f32_lhs(dt, expand)
    acs_x = _dot_f32_lhs(acs, expand)
    e_acs = jnp.exp(acs_x)
    acs_last = acs_x[q - 1:q, :]
    decay = jnp.exp(acs_last - acs_x) * dt_x
    dtx = xs * dt_x
    xd = xs * decay
    glane = lax.broadcasted_iota(i32, (1, gw), 1)

    ys = []
    for g in range(SSM_GROUPS):
        gs = slice(g * gw, (g + 1) * gw)
        bg = act[:, D_SSM + g * D_STATE:D_SSM + (g + 1) * D_STATE].astype(bf16)
        cg = act[:, D_SSM + (SSM_GROUPS + g) * D_STATE:D_SSM + (SSM_GROUPS + g + 1) * D_STATE].astype(bf16)
        cb = _dot_nt(cg, bg)
        h_old = h_sc[g]
        yg = _dot(cg, h_old.astype(bf16)) * e_acs[:, gs]
        dtx_g = dtx[:, gs]
        for e4 in range(HEADS_PER_GROUP):
            e = g * HEADS_PER_GROUP + e4
            seg = acs[:, e:e + 1] - acsT[e:e + 1, :]
            lmat = jnp.exp(jnp.where(causal, seg, -jnp.inf))
            rhs = jnp.where(glane // SSM_HEADDIM == e4, dtx_g, 0.0).astype(bf16)
            yg = yg + _dot((cb * lmat).astype(bf16), rhs)
        h_sc[g] = h_old * jnp.exp(acs_last[:, gs]) + _dot_tn(bg, xd[:, gs].astype(bf16))
        ys.append(yg)
    y = jnp.concatenate(ys, axis=1) + dsk_ref[...] * xs
    zz = z_ref[...]
    y = y * (zz * _sigmoid(zz))
    for g in range(SSM_GROUPS):
        gs = slice(g * gw, (g + 1) * gw)
        yg = y[:, gs]
        r = lax.rsqrt(jnp.mean(yg * yg, axis=-1, keepdims=True) + RMS_EPS)
        y_ref[:, gs] = (yg * r * g_ref[:, gs]).astype(bf16)
    hout_ref[0] = h_sc[...]


def _ssd(z, xbc, dt, dtT, h0, conv0, pw, B, L):
    q = min(SSD_TILE, L)
    nc = L // q
    gw = HEADS_PER_GROUP * SSM_HEADDIM
    dtT3 = dtT.reshape(BF16_ROWS, B, L).transpose(1, 0, 2)
    hT0 = h0.astype(f32).reshape(B, SSM_GROUPS, gw, D_STATE).transpose(0, 1, 3, 2)
    c0 = jnp.pad(conv0.astype(f32), ((0, 0), (SUBLANES - (CONV_W - 1), 0), (0, 0)))
    full = lambda a: pl.BlockSpec(a.shape, lambda b, c: (0,) * a.ndim)
    row = lambda n: pl.BlockSpec((q, n), lambda b, c: (b * nc + c, 0))
    params = [pw['conv_w'], pw['conv_b'], pw['dt_bias'], pw['dt_biasT'], pw['a_log'], pw['a_logT'],
              pw['d_skip'], pw['ssm_g']]
    y, hT, ctail = pl.pallas_call(
        _ssd_kernel, grid=(B, nc),
        in_specs=[row(D_SSM), row(CONV_DIM), row(LANES),
                  pl.BlockSpec((1, BF16_ROWS, q), lambda b, c: (b, 0, c)),
                  pl.BlockSpec((1, SSM_GROUPS, D_STATE, gw), lambda b, c: (b, 0, 0, 0)),
                  pl.BlockSpec((1, SUBLANES, CONV_DIM), lambda b, c: (b, 0, 0))] + [full(a) for a in params],
        out_specs=[row(D_SSM),
                   pl.BlockSpec((1, SSM_GROUPS, D_STATE, gw), lambda b, c: (b, 0, 0, 0)),
                   pl.BlockSpec((1, SUBLANES, CONV_DIM), lambda b, c: (b, 0, 0))],
        out_shape=[jax.ShapeDtypeStruct((B * L, D_SSM), bf16),
                   jax.ShapeDtypeStruct((B, SSM_GROUPS, D_STATE, gw), f32),
                   jax.ShapeDtypeStruct((B, SUBLANES, CONV_DIM), f32)],
        scratch_shapes=[pltpu.VMEM((q + SUBLANES, CONV_DIM), f32), pltpu.VMEM((SSM_GROUPS, D_STATE, gw), f32)],
        compiler_params=_params(("parallel", "arbitrary")),
    )(z, xbc, dt, dtT3, hT0, c0, *params)
    h_new = hT.transpose(0, 1, 3, 2).reshape(B, H_S, SSM_HEADDIM, D_STATE)
    return y, h_new, ctail[:, SUBLANES - (CONV_W - 1):, :]


def _mix_router_kernel(att_ref, ssm_ref, x_ref, woa_ref, wos_ref, g_ref, b_ref, wrh_ref, wrhl_ref, br_ref,
                       x1_ref, te_ref, gate_ref, cnt_ref, run_sc):
    @pl.when(pl.program_id(0) == 0)
    def _():
        run_sc[...] = jnp.zeros(run_sc.shape, f32)

    mix = _dot(att_ref[...], woa_ref[...]) + _dot(ssm_ref[...], wos_ref[...])
    x1 = _layer_norm(DEEPNORM_ALPHA * x_ref[...] + mix, g_ref[...], b_ref[...])
    x1_ref[...] = x1
    hi = x1.astype(bf16)
    lo = (x1 - hi.astype(f32)).astype(bf16)
    both = _dot(hi, wrhl_ref[...])
    logits = both[:, :LANES] + both[:, LANES:] + _dot(lo, wrh_ref[...]) + br_ref[...]
    lane =lax.broadcasted_iota(i32, logits.shape, 1)
    vals, idxs = [], []
    for _ in range(TOP_K):
        m = jnp.max(logits, axis=-1, keepdims=True)
        idx = jnp.min(jnp.where(logits == m, lane, LANES), axis=-1, keepdims=True)
        vals.append(m)
        idxs.append(idx)
        logits = jnp.where(lane == idx, -jnp.inf, logits)
    es = [jnp.exp(v - vals[0]) for v in vals]
    tot = es[0]
    for e in es[1:]:
        tot = tot + e
    tl = logits.shape[0]
    chosen = jnp.zeros(logits.shape, f32)
    for k in range(TOP_K):
        chosen = chosen + (lane == idxs[k]).astype(f32)
    ii = lax.broadcasted_iota(i32, (tl, tl), 0)
    jj = lax.broadcasted_iota(i32, (tl, tl), 1)
    before = _dot((jj < ii).astype(bf16), chosen.astype(bf16)) + run_sc[...]
    te = jnp.zeros(logits.shape, i32)
    gate = jnp.zeros(logits.shape, f32)
    for k in range(TOP_K):
        rank = jnp.sum(jnp.where(lane == idxs[k], before, 0.0), axis=-1, keepdims=True).astype(i32)
        te = jnp.where(lane == k, idxs[k], te)
        te = jnp.where(lane == TOP_K + k, rank, te)
        gate = jnp.where(lane == k, es[k] / tot, gate)
    te_ref[...] = te
    gate_ref[...] = gate
    run_sc[...] = run_sc[...] + jnp.sum(chosen, axis=0, keepdims=True)
    cnt_ref[...] = run_sc[...]


def _mix_router(att, ssm, x2d, w):
    T, D = x2d.shape
    tl = min(ROW_TILE, T)
    row = lambda n: pl.BlockSpec((tl, n), lambda i: (i, 0))
    full = lambda a: pl.BlockSpec(a.shape, lambda i: (0,) * a.ndim)
    ws = [w['wo_att'], w['wo_ssm'], w['ln1_g'], w['ln1_b'], w['wr_hi'], w['wr_hi_lo'], w['b_router']]
    return pl.pallas_call(
        _mix_router_kernel, grid=(T // tl,),
        in_specs=[row(D_ATT), row(D_SSM), row(D)] + [full(a) for a in ws],
        out_specs=[row(D), row(LANES), row(LANES), pl.BlockSpec((1, LANES), lambda i: (0, 0))],
        out_shape=[jax.ShapeDtypeStruct((T, D), f32), jax.ShapeDtypeStruct((T, LANES), i32),
                   jax.ShapeDtypeStruct((T, LANES), f32), jax.ShapeDtypeStruct((1, LANES), f32)],
        scratch_shapes=[pltpu.VMEM((1, LANES), f32)],
        compiler_params=_params(("arbitrary",)),
    )(att, ssm, x2d, *ws)


def _route(te, counts_f):
    T = te.shape[0]
    n_assign = T * TOP_K
    counts = counts_f[0, :N_EXPERTS].astype(i32)
    padded = (counts + MOE_BLOCK - 1) // MOE_BLOCK * MOE_BLOCK
    pad_end = jnp.cumsum(padded)
    pad_start = pad_end - padded
    top_e, rank = te[:, :TOP_K], te[:, TOP_K:2 * TOP_K]
    experts = jnp.arange(N_EXPERTS, dtype=i32)
    slot = rank + jnp.sum(jnp.where(top_e[..., None] == experts, pad_start, 0), axis=-1)
    n_blocks = -(-n_assign // MOE_BLOCK) + N_EXPERTS
    n_blocks += n_blocks % FFN_BLOCKS_PER_STEP
    block_start = jnp.arange(n_blocks, dtype=i32) * MOE_BLOCK
    block_e = jnp.minimum(jnp.sum(block_start[:, None] >= pad_end[None, :], axis=-1), N_EXPERTS - 1).astype(i32)
    n_used = (pad_end[-1] // MOE_BLOCK).astype(i32).reshape(1)
    last_block = jnp.where(padded > 0, pad_end // MOE_BLOCK - 1, -1).astype(i32)
    return block_e, n_used, last_block, slot.astype(i32), n_blocks


def _slot_tiles(slot, tl):
    T = slot.shape[0]
    return slot.reshape(T // tl, tl, TOP_K).transpose(0, 2, 1).reshape(T // tl, 1, TOP_K * tl)


def _dispatch_kernel(last_ref, nu_ref, slot_ref, x_ref, xs_hbm, zero_sc, sem):
    n_copies = slot_ref.shape[-1]
    tl = n_copies // TOP_K
    n_blocks = xs_hbm.shape[0] // MOE_BLOCK

    @pl.when(pl.program_id(0) == 0)
    def _():
        zero_sc[...] = jnp.zeros(zero_sc.shape, f32)

        def zero_block(b):
            return pltpu.make_async_copy(zero_sc, xs_hbm.at[pl.ds(b * MOE_BLOCK, MOE_BLOCK)], sem.at[1])

        def each_zero_block(fn):
            def per_expert(e, carry):
                @pl.when(last_ref[e] >= 0)
                def _():
                    fn(zero_block(last_ref[e]))
                return carry
            lax.fori_loop(0, N_EXPERTS, per_expert, 0)

            def per_unused(b, carry):
                fn(zero_block(b))
                return carry
            lax.fori_loop(nu_ref[0], n_blocks, per_unused, 0)

        each_zero_block(lambda c: c.start())
        each_zero_block(lambda c: c.wait())

    def body(r, carry):
        src = x_ref.at[pl.ds(r, 1)]
        for k in range(TOP_K):
            pltpu.make_async_copy(src, xs_hbm.at[pl.ds(slot_ref[0, 0, k * tl + r], 1)], sem.at[0]).start()
        return carry
    lax.fori_loop(0, tl, body, 0, unroll=4)
    pltpu.make_async_copy(xs_hbm.at[pl.ds(0, n_copies)], xs_hbm.at[pl.ds(0, n_copies)], sem.at[0]).wait()


def _dispatch(x1, slot, last_block, n_used, n_blocks):
    T, D = x1.shape
    tl = min(ROW_TILE, T)
    tiles = _slot_tiles(slot, tl)
    grid_spec = pltpu.PrefetchScalarGridSpec(
        num_scalar_prefetch=2, grid=(T // tl,),
        in_specs=[pl.BlockSpec((1, 1, TOP_K * tl), lambda i, lb, nu: (i, 0, 0), memory_space=pltpu.SMEM),
                  pl.BlockSpec((tl, D), lambda i, lb, nu: (i, 0))],
        out_specs=pl.BlockSpec(memory_space=pl.ANY),
        scratch_shapes=[pltpu.VMEM((MOE_BLOCK, D), f32), pltpu.SemaphoreType.DMA((2,))])
    return pl.pallas_call(
        _dispatch_kernel, grid_spec=grid_spec,
        out_shape=jax.ShapeDtypeStruct((n_blocks * MOE_BLOCK, D), f32),
        compiler_params=_params(("arbitrary",)),
    )(last_block, n_used, tiles, x1)


def _deinterleave_kernel(w_ref, g_ref, u_ref):
    wb = w_ref[0].astype(bf16)
    tn = g_ref.shape[-1]
    r = lax.broadcasted_iota(i32, (2 * tn, tn), 0)
    c = lax.broadcasted_iota(i32, (2 * tn, tn), 1)
    g_ref[0] = _dot(wb, (r == 2 * c).astype(bf16)).astype(bf16)
    u_ref[0] = _dot(wb, (r == 2 * c + 1).astype(bf16)).astype(bf16)


def _deinterleave(w_gu):
    E, D, F2 = w_gu.shape
    tn = 2 * LANES
    out = jax.ShapeDtypeStruct((E, D, F2 // 2), bf16)
    return pl.pallas_call(
        _deinterleave_kernel, grid=(E, F2 // (2 * tn)),
        in_specs=[pl.BlockSpec((1, D, 2 * tn), lambda e, j: (e, 0, j))],
        out_specs=[pl.BlockSpec((1, D, tn), lambda e, j: (e, 0, j))] * 2,
        out_shape=[out, out],
        compiler_params=_params(("parallel", "parallel")),
    )(w_gu)


def _row_gather(idx_ref, n, src_hbm, dst, sem):
    def body(r, carry):
        pltpu.make_async_copy(src_hbm.at[pl.ds(idx_ref[0, 0, r], 1)], dst.at[pl.ds(r, 1)], sem).start()
        return carry
    lax.fori_loop(0, n, body, 0, unroll=8)


def _ffn_kernel(be_ref, nu_ref, xs_ref, *refs):
    del be_ref
    y_ref = refs[-1]
    wsets = [refs[6 * j:6 * j + 6] for j in range(FFN_BLOCKS_PER_STEP)]
    first = pl.program_id(0) * FFN_BLOCKS_PER_STEP
    n_live = jnp.clip(nu_ref[0] - first, 0, FFN_BLOCKS_PER_STEP)

    def block(j):
        wg_ref, wu_ref, bg_ref, bu_ref, wd_ref, bd_ref = wsets[j]
        rows = pl.ds(j * MOE_BLOCK, MOE_BLOCK)
        xb = xs_ref[rows, :].astype(bf16)
        g = _dot(xb, wg_ref[0]) + bg_ref[0]
        u = _dot(xb, wu_ref[0]) + bu_ref[0]
        g = jnp.minimum(g, SWIGLU_LIMIT)
        u = jnp.clip(u, -SWIGLU_LIMIT, SWIGLU_LIMIT)
        act = (u + 1.0) * g * _sigmoid(SWIGLU_ALPHA * g)
        y_ref[rows, :] = _dot(act.astype(bf16), wd_ref[0]) + bd_ref[0]

    for live in range(FFN_BLOCKS_PER_STEP + 1):
        @pl.when(n_live == live)
        def _(live=live):
            for j in range(live):
                block(j)
            for j in range(live, FFN_BLOCKS_PER_STEP):
                y_ref[pl.ds(j * MOE_BLOCK, MOE_BLOCK), :] = jnp.zeros((MOE_BLOCK, y_ref.shape[1]), f32)


def _expert_ffn(xs, block_e, n_used, w):
    n_rows, D = xs.shape
    nps = FFN_BLOCKS_PER_STEP
    n_steps = n_rows // (MOE_BLOCK * nps)
    F = w['w_g'].shape[2]
    wspecs, wargs = [], []
    for j in range(nps):
        pick = lambda i, be, nu, j=j: (be[i * nps + j], 0, 0)
        wspecs += [pl.BlockSpec((1, D, F), pick), pl.BlockSpec((1, D, F), pick), pl.BlockSpec((1, 1, F), pick),
                   pl.BlockSpec((1, 1, F), pick), pl.BlockSpec((1, F, D), pick), pl.BlockSpec((1, 1, D), pick)]
        wargs += [w['w_g'], w['w_u'], w['b_g'], w['b_u'], w['w_d'], w['b_d']]
    grid_spec = pltpu.PrefetchScalarGridSpec(
        num_scalar_prefetch=2, grid=(n_steps,),
        in_specs=[pl.BlockSpec((nps * MOE_BLOCK, D),
                               lambda i, be, nu: (jnp.minimum(i, (nu[0] - 1) // nps), 0))] + wspecs,
        out_specs=pl.BlockSpec((nps * MOE_BLOCK, D), lambda i, be, nu: (i, 0)))
    return pl.pallas_call(
        _ffn_kernel, grid_spec=grid_spec,
        out_shape=jax.ShapeDtypeStruct((n_rows, D), f32),
        compiler_params=_params(("arbitrary",)),
    )(block_e, n_used, xs, *wargs)


def _combine_kernel(slot_ref, slotn_ref, ys_hbm, x1_ref, gate_ref, g_ref, b_ref, y_ref, buf, sem):
    i = pl.program_id(0)
    n = pl.num_programs(0)
    tl = x1_ref.shape[0]
    cur = i % 2

    @pl.when(i == 0)
    def _():
        _row_gather(slot_ref, TOP_K * tl, ys_hbm, buf.at[0], sem.at[0])

    @pl.when(i + 1 < n)
    def _():
        _row_gather(slotn_ref, TOP_K * tl, ys_hbm, buf.at[1 - cur], sem.at[1 - cur])

    pltpu.make_async_copy(buf.at[cur], buf.at[cur], sem.at[cur]).wait()
    gate = gate_ref[...]
    ff = gate[:, 0:1] * buf[cur, 0:tl]
    for k in range(1, TOP_K):
        ff = ff + gate[:, k:k + 1] * buf[cur, k * tl:(k + 1) * tl]
    y_ref[...] = _layer_norm(DEEPNORM_ALPHA * x1_ref[...] + ff, g_ref[...], b_ref[...])


def _combine(ys, slot, x1, gate, ln_g, ln_b):
    T, D = x1.shape
    tl = min(COMB_TILE, T)
    nt = T // tl
    slot_tiles = _slot_tiles(slot, tl)
    row = lambda n: pl.BlockSpec((tl, n), lambda i: (i, 0))
    full = lambda a: pl.BlockSpec(a.shape, lambda i: (0,) * a.ndim)
    return pl.pallas_call(
        _combine_kernel, grid=(nt,),
        in_specs=[pl.BlockSpec((1, 1, TOP_K * tl), lambda i: (i, 0, 0), memory_space=pltpu.SMEM),
                  pl.BlockSpec((1, 1, TOP_K * tl), lambda i: (jnp.minimum(i + 1, nt - 1), 0, 0),
                               memory_space=pltpu.SMEM),
                  pl.BlockSpec(memory_space=pl.ANY), row(D), row(LANES), full(ln_g), full(ln_b)],
        out_specs=row(D), out_shape=jax.ShapeDtypeStruct((T, D), f32),
        scratch_shapes=[pltpu.VMEM((2, TOP_K * tl, D), f32), pltpu.SemaphoreType.DMA((2,))],
        compiler_params=_params(("arbitrary",)),
    )(slot_tiles, slot_tiles, ys, x1, gate, ln_g, ln_b)


def _prep_weights(l, w_in, conv_w, conv_b, dt_bias, a_log, d_skip, ssm_norm_g, w_o, ln1_g, ln1_b,
                  w_router, b_router, w_gate_up, b_gate_up, w_down, b_down, ln2_g, ln2_b):
    wi = w_in[l]
    c0, c1, c2, c3 = D_QK_ALL, 2 * D_QK_ALL, 2 * D_QK_ALL + D_ATT, 2 * D_QK_ALL + D_ATT + D_SSM
    c4 = c3 + CONV_DIM
    wdt = wi[:, c4:c4 + H_S]
    pad_lane = lambda v, fill=0.0: jnp.pad(v.astype(f32).reshape(1, -1), ((0, 0), (0, LANES - v.shape[-1])),
                                            constant_values=fill)
    pad_col = lambda v: jnp.pad(v.astype(f32).reshape(-1, 1), ((0, BF16_ROWS - v.shape[-1]), (0, 0)))
    wr = jnp.pad(w_router[l].astype(f32), ((0, 0), (0, LANES - N_EXPERTS)))
    wr_hi = wr.astype(bf16)
    wgu = w_gate_up[l]
    return {
        'wq': wi[:, :c0].astype(bf16), 'wqT': wi[:, :c0].T.astype(bf16),
        'wk': wi[:, c0:c1].astype(bf16), 'wv': wi[:, c1:c2].astype(bf16), 'wvT': wi[:, c1:c2].T.astype(bf16),
        'wz': wi[:, c2:c3].astype(bf16), 'wx': wi[:, c3:c4].astype(bf16),
        'wdt': jnp.pad(wdt, ((0, 0), (0, LANES - H_S))).astype(bf16),
        'wdtT': jnp.pad(wdt.T, ((0, BF16_ROWS - H_S), (0, 0))).astype(bf16),
        'conv_w': conv_w[l].astype(f32), 'conv_b': conv_b[l].astype(f32).reshape(1, -1),
        'dt_bias': pad_lane(dt_bias[l]), 'dt_biasT': pad_col(dt_bias[l]),
        'a_log': pad_lane(a_log[l]), 'a_logT': pad_col(a_log[l]),
        'd_skip': jnp.repeat(d_skip[l].astype(f32), SSM_HEADDIM).reshape(1, -1),
        'ssm_g': ssm_norm_g[l].astype(f32).reshape(1, -1),
        'wo_att': w_o[l][:D_ATT].astype(bf16), 'wo_ssm': w_o[l][D_ATT:].astype(bf16),
        'ln1_g': ln1_g[l].astype(f32).reshape(1, -1), 'ln1_b': ln1_b[l].astype(f32).reshape(1, -1),
        'wr_hi': wr_hi, 'wr_hi_lo': jnp.concatenate([wr_hi, (wr - wr_hi.astype(f32)).astype(bf16)], axis=1),
        'b_router': pad_lane(b_router[l], NEG_INF),
        'w_gu': wgu,
        'b_g': b_gate_up[l][:, None, 0::2].astype(f32), 'b_u': b_gate_up[l][:, None, 1::2].astype(f32),
        'w_d': w_down[l].astype(bf16), 'b_d': b_down[l][:, None, :].astype(f32),
        'ln2_g': ln2_g[l].astype(f32).reshape(1, -1), 'ln2_b': ln2_b[l].astype(f32).reshape(1, -1),
    }


def _trunk_layer(x, l, w, rel_bias, lam_args, attn_g, k_past, v_past, h0, conv0):
    B, L, D = x.shape
    x2d = x.reshape(B * L, D)
    lam_init = 0.8 - 0.6 * math.exp(-0.3 * l)
    prompt = k_past is None
    proj = _in_proj(x2d, w, prompt)
    k, v, z, xbc, dt, dtT = proj[:6]
    if prompt:
        kb, qT, v1T = proj[6:]
        att = _attn_prompt(qT, kb, v1T, rel_bias, lam_args, attn_g, B, L, lam_init)
    else:
        att = _attn_sample(proj[6], k, v, k_past, v_past, rel_bias, lam_args, attn_g, lam_init)
    ssm, h_new, conv_new = _ssd(z, xbc, dt, dtT, h0, conv0, w, B, L)
    x1, te, gate, counts = _mix_router(att, ssm, x2d, w)
    block_e, n_used, last_block, slot, n_blocks = _route(te, counts)
    xs = _dispatch(x1, slot, last_block, n_used, n_blocks)
    ys = _expert_ffn(xs, block_e, n_used, w)
    y = _combine(ys, slot, x1, gate, w['ln2_g'], w['ln2_b'])
    return (y.reshape(B, L, D), k.reshape(B, L, H_A, 2 * DQK), v.reshape(B, L, H_A, DV), h_new, conv_new)


def kernel(x_prompt, x_sample, cache_k, cache_v, state_ssm, state_conv, rel_bias, w_in, lambda_q1, lambda_k1, lambda_q2, lambda_k2, attn_norm_g, conv_w, conv_b, dt_bias, a_log, d_skip, ssm_norm_g, w_o, ln1_g, ln1_b, w_router, b_router, w_gate_up, b_gate_up, w_down, b_down, ln2_g, ln2_b):
    yp, ys = x_prompt, x_sample
    bp = x_prompt.shape[0]
    depth = w_in.shape[0]
    outs = [[] for _ in range(8)]
    for l in range(depth):
        w = _prep_weights(l, w_in, conv_w, conv_b, dt_bias, a_log, d_skip, ssm_norm_g, w_o, ln1_g, ln1_b,
                          w_router, b_router, w_gate_up, b_gate_up, w_down, b_down, ln2_g, ln2_b)
        w['w_g'], w['w_u'] = _deinterleave(w.pop('w_gu'))
        lam_args = [a[l].astype(f32).reshape(1, -1) for a in (lambda_q1, lambda_k1, lambda_q2, lambda_k2)]
        h0 = jnp.zeros((bp, H_S, SSM_HEADDIM, D_STATE), f32)
        c0 = jnp.zeros((bp, CONV_W - 1, CONV_DIM), f32)
        yp, kp, vp, hp, cp = _trunk_layer(yp, l, w, rel_bias, lam_args, attn_norm_g[l], None, None, h0, c0)
        ys, ks, vs, hs, cs = _trunk_layer(ys, l, w, rel_bias, lam_args, attn_norm_g[l], cache_k[l], cache_v[l],
                                          state_ssm[l], state_conv[l])
        for lst, a in zip(outs, (kp, vp, hp, cp, ks, vs, hs, cs)):
            lst.append(a)
    return (yp, ys) + tuple(jnp.stack(o) for o in outs)
```

```python
import functools
import math

import numpy as np
import jax
import jax.numpy as jnp
from jax import lax
from jax.experimental import pallas as pl
from jax.experimental.pallas import tpu as pltpu
from jax.experimental.pallas import tpu_sc as plsc

f32 = jnp.float32
bf16 = jnp.bfloat16
i32 = jnp.int32

CHUNK = 64
H_A = 8
DQK = 32
DV = 2 * DQK
D_ATT = H_A * DV
SSM_HEADDIM = 64
H_S = 8
D_SSM = H_S * SSM_HEADDIM
SSM_GROUPS = 2
HEADS_PER_GROUP = H_S // SSM_GROUPS
D_STATE = 128
CONV_W = 4
CONV_DIM = D_SSM + 2 * SSM_GROUPS * D_STATE
D_QK_ALL = H_A * 2 * DQK
N_BUCKETS = 32
MAX_DISTANCE = 128
N_EXPERTS = 32
TOP_K = 4
SWIGLU_LIMIT = 7.0
SWIGLU_ALPHA = 1.702
MOE_BLOCK = 256
LN_EPS = 1e-5
RMS_EPS = 1e-5
NEG_INF = -1e30
DEPTH = 1
DEEPNORM_ALPHA = (2.0 * DEPTH) ** 0.25
LOG2E = math.log2(math.e)

LANES = 128
SUBLANES = 8
BF16_ROWS = 16
VMEM_LIMIT = 48 * 1024 * 1024

ROW_TILE = 512
ATT_TILE = 512
ATT_SUB = 256
SSD_TILE = 256
V_ROWS = DV + BF16_ROWS
COMB_TILE = 256
FFN_BLOCKS_PER_STEP = 2


def _params(semantics):
    return pltpu.CompilerParams(dimension_semantics=semantics, vmem_limit_bytes=VMEM_LIMIT)


def _dot(a, b):
    return jnp.dot(a, b, preferred_element_type=f32)


def _dot_nt(a, b):
    return lax.dot_general(a, b, (((1,), (1,)), ((), ())), preferred_element_type=f32)


def _dot_tn(a, b):
    return lax.dot_general(a, b, (((0,), (0,)), ((), ())), preferred_element_type=f32)


def _split3(a):
    hi = a.astype(bf16)
    r1 = a - hi.astype(f32)
    mid = r1.astype(bf16)
    lo = (r1 - mid.astype(f32)).astype(bf16)
    return hi, mid, lo


def _dot_f32_lhs(a, b_exact):
    hi, mid, lo = _split3(a)
    return _dot(hi, b_exact) + _dot(mid, b_exact) + _dot(lo, b_exact)


def _dot_f32_rhs(a_exact, b):
    hi, mid, lo = _split3(b)
    return _dot(a_exact, hi) + _dot(a_exact, mid) + _dot(a_exact, lo)


def _softplus(x):
    return jnp.maximum(x, 0.0) + jnp.log1p(jnp.exp(-jnp.abs(x)))


def _sigmoid(x):
    return 1.0 / (1.0 + jnp.exp(-x))


def _layer_norm(y, g, b):
    mu = jnp.mean(y, axis=-1, keepdims=True)
    yc = y - mu
    var = jnp.mean(yc * yc, axis=-1, keepdims=True)
    return yc * lax.rsqrt(var + LN_EPS) * g + b


def _lambda(lq1_ref, lk1_ref, lq2_ref, lk2_ref, lam_init):
    s1 = jnp.sum(lq1_ref[...] * lk1_ref[...], axis=-1, keepdims=True)
    s2 = jnp.sum(lq2_ref[...] * lk2_ref[...], axis=-1, keepdims=True)
    return jnp.exp(s1) - jnp.exp(s2) + lam_init


def _in_proj_kernel(x_ref, wk_ref, wv_ref, wz_ref, wx_ref, wdt_ref, wdtT_ref, wq_ref, wvT_ref, *outs, prompt):
    xb = x_ref[...].astype(bf16)
    k_ref, v_ref, z_ref, xbc_ref, dt_ref, dtT_ref = outs[:6]
    k = _dot(xb, wk_ref[...])
    k_ref[...] = k
    v_ref[...] = _dot(xb, wv_ref[...])
    z_ref[...] = _dot(xb, wz_ref[...])
    xbc_ref[...] = _dot(xb, wx_ref[...])
    dt_ref[...] = _dot(xb, wdt_ref[...])
    dtT_ref[...] = _dot_nt(wdtT_ref[...], xb)
    scale = DQK ** -0.5
    if prompt:
        kb_ref, qT_ref, v1T_ref = outs[6:]
        kb_ref[...] = k.astype(bf16)
        qT_ref[...] = (_dot_nt(wq_ref[...], xb) * (scale * LOG2E)).astype(bf16)
        vT = _dot_nt(wvT_ref[...], xb).astype(bf16)
        ones = jnp.ones((BF16_ROWS, vT.shape[1]), bf16)
        for h in range(H_A):
            v1T_ref[h * V_ROWS:h * V_ROWS + DV, :] = vT[h * DV:(h + 1) * DV, :]
            v1T_ref[h * V_ROWS + DV:(h + 1) * V_ROWS, :] = ones
    else:
        (q_ref,) = outs[6:]
        q_ref[...] = (_dot(xb, wq_ref[...]) * scale).astype(bf16)


def _in_proj(x2d, w, prompt):
    T, D = x2d.shape
    tl = min(ROW_TILE, T)
    grid = (T // tl,)
    row = lambda n: pl.BlockSpec((tl, n), lambda i: (i, 0))
    col = lambda n: pl.BlockSpec((n, tl), lambda i: (0, i))
    full = lambda a: pl.BlockSpec(a.shape, lambda i: (0,) * a.ndim)
    wq = w['wqT'] if prompt else w['wq']
    ins = [x2d, w['wk'], w['wv'], w['wz'], w['wx'], w['wdt'], w['wdtT'], wq, w['wvT']]
    in_specs = [row(D)] + [full(a) for a in ins[1:]]
    out_shape = [jax.ShapeDtypeStruct((T, D_QK_ALL), f32), jax.ShapeDtypeStruct((T, D_ATT), f32),
                 jax.ShapeDtypeStruct((T, D_SSM), f32), jax.ShapeDtypeStruct((T, CONV_DIM), f32),
                 jax.ShapeDtypeStruct((T, LANES), f32), jax.ShapeDtypeStruct((BF16_ROWS, T), f32)]
    out_specs = [row(D_QK_ALL), row(D_ATT), row(D_SSM), row(CONV_DIM), row(LANES), col(BF16_ROWS)]
    if prompt:
        out_shape += [jax.ShapeDtypeStruct((T, D_QK_ALL), bf16), jax.ShapeDtypeStruct((D_QK_ALL, T), bf16),
                      jax.ShapeDtypeStruct((H_A * V_ROWS, T), bf16)]
        out_specs += [row(D_QK_ALL), col(D_QK_ALL), col(H_A * V_ROWS)]
    else:
        out_shape += [jax.ShapeDtypeStruct((T, D_QK_ALL), bf16)]
        out_specs += [row(D_QK_ALL)]
    return pl.pallas_call(
        functools.partial(_in_proj_kernel, prompt=prompt),
        grid=grid, in_specs=in_specs, out_specs=out_specs, out_shape=out_shape,
        compiler_params=_params(("parallel",)),
    )(*ins)


def _t5_bucket(rel):
    half = N_BUCKETS // 2
    max_exact = half // 2
    n = jnp.abs(rel)
    large = max_exact + (jnp.log(jnp.maximum(n, 1).astype(f32) / max_exact)
                         / math.log(MAX_DISTANCE / max_exact) * (half - max_exact)).astype(i32)
    large = jnp.minimum(large, half - 1)
    return jnp.where(rel > 0, half, 0) + jnp.where(n < max_exact, n, large)


def _far_bucket(min_dist):
    half = N_BUCKETS // 2
    max_exact = half // 2
    v = max_exact + int(math.log(min_dist / max_exact) / math.log(MAX_DISTANCE / max_exact) * (half - max_exact))
    return half - 1 if (min_dist >= max_exact and v - 1 >= half - 1) else None


def _bias_lookup(rel_bias, bucket):
    onehot = (bucket[..., None] == jnp.arange(N_BUCKETS, dtype=i32)).astype(f32)
    return jnp.dot(onehot, rel_bias.astype(f32), precision=lax.Precision.HIGHEST)


def _prompt_bias_tiles(rel_bias, ta):
    j = jnp.arange(ta, dtype=i32)[:, None]
    i = jnp.arange(ta, dtype=i32)[None, :]
    tiles = []
    for d in range(2):
        rel = j - i - d * ta
        b = _bias_lookup(rel_bias, _t5_bucket(rel))
        allowed = ((j // CHUNK) <= (i // CHUNK)) if d == 0 else jnp.ones((ta, ta), bool)
        tiles.append(jnp.where(allowed[..., None], b, NEG_INF))
    return jnp.transpose(jnp.stack(tiles), (3, 0, 1, 2))


def _sample_bias(rel_bias, past, s):
    q_pos = past + jnp.arange(s, dtype=i32)
    k_pos = jnp.arange(past + s, dtype=i32)
    rel = k_pos[None, :] - q_pos[:, None]
    b = _bias_lookup(rel_bias, _t5_bucket(rel))
    allowed = (k_pos[None, :] // CHUNK) <= (q_pos[:, None] // CHUNK)
    return jnp.transpose(jnp.where(allowed[..., None], b, NEG_INF), (2, 0, 1))


def _attn_prompt_kernel(qi_tab, ki_tab, qT_ref, k_ref, v1T_ref, bias_ref, cfar_ref,
                        lq1_ref, lk1_ref, lq2_ref, lk2_ref, g_ref, o_ref,
                        qm_sc, m_sc, acc_sc, oT_sc, *, lam_init):
    p = pl.program_id(1)
    qi = qi_tab[p]
    ki = ki_tab[p]
    d = qi - ki
    ta = k_ref.shape[0]
    n_hp = H_A // 2

    @pl.when(ki == 0)
    def _():
        m_sc[...] = jnp.full(m_sc.shape, NEG_INF, f32)
        acc_sc[...] = jnp.zeros(acc_sc.shape, f32)
        row = lax.broadcasted_iota(i32, (LANES, ta), 0)
        for hp in range(n_hp):
            qt = qT_ref[hp * LANES:(hp + 1) * LANES, :]
            for gi in range(4):
                keep = (row >= gi * DQK) & (row < (gi + 1) * DQK)
                qm_sc[hp, :, gi * ta:(gi + 1) * ta] = jnp.where(keep, qt, jnp.zeros_like(qt))

    def scores(hp):
        return _dot(k_ref[:, hp * LANES:(hp + 1) * LANES], qm_sc[hp])

    nb = ta // ATT_SUB

    def near_bias(hp, step_d):
        rows = []
        for jb in range(nb):
            cols = []
            for g in range(4):
                for ib in range(nb):
                    dd = step_d * nb + ib - jb
                    if dd < 0:
                        blk = jnp.full((ATT_SUB, ATT_SUB), NEG_INF * LOG2E, f32)
                    elif dd < 2:
                        blk = bias_ref[2 * hp + g // 2, dd]
                    else:
                        lo = g * ta + ib * ATT_SUB
                        blk = jnp.broadcast_to(cfar_ref[hp][:, lo:lo + ATT_SUB], (ATT_SUB, ATT_SUB))
                    cols.append(blk)
            rows.append(jnp.concatenate(cols, axis=1))
        return jnp.concatenate(rows, axis=0)

    def update(step_d):
        s_next = scores(0)
        for hp in range(n_hp):
            s = s_next
            if hp + 1 < n_hp:
                s_next = scores(hp + 1)
            m_old = m_sc[hp]
            if step_d is not None:
                s = s + near_bias(hp, step_d)
                m_new = jnp.maximum(m_old, jnp.max(s, axis=0, keepdims=True))
                shift = m_new
            else:
                c = cfar_ref[hp]
                m_new = jnp.maximum(m_old, jnp.max(s, axis=0, keepdims=True) + c)
                shift = m_new - c
            alpha = jnp.exp2(m_old - m_new)
            pT = jnp.exp2(s - shift).astype(bf16)
            for hh in range(2):
                h = 2 * hp + hh
                cols = slice(hh * 2 * ta, (hh + 1) * 2 * ta)
                pv = _dot(v1T_ref[h * V_ROWS:(h + 1) * V_ROWS, :], pT[:, cols])
                acc_sc[h] = alpha[:, cols] * acc_sc[h] + pv
            m_sc[hp] = m_new

    @pl.when(d == 0)
    def _():
        update(0)

    @pl.when(d == 1)
    def _():
        update(1)

    @pl.when(d >= 2)
    def _():
        update(None)

    @pl.when(d == 0)
    def _():
        lam = _lambda(lq1_ref, lk1_ref, lq2_ref, lk2_ref, lam_init)
        for h in range(H_A):
            a = acc_sc[h]
            a0 = a[:, :ta]
            a1 = a[:, ta:]
            o = a0[:DV] / a0[DV:DV + 1] - lam * (a1[:DV] / a1[DV:DV + 1])
            o = o * lax.rsqrt(jnp.mean(o * o, axis=0, keepdims=True) + RMS_EPS) * g_ref[...] * (1.0 - lam_init)
            oT_sc[h * DV:(h + 1) * DV, :] = o
        o_ref[...] = oT_sc[...].T.astype(bf16)


def _attn_prompt(qT, kb, v1T, rel_bias, lam_args, attn_g, B, L, lam_init):
    ta = min(ATT_TILE, L)
    assert L % ta == 0 and ta % ATT_SUB == 0 and ATT_SUB % CHUNK == 0
    nt = L // ta
    far = _far_bucket(ATT_SUB + 1)
    assert L <= 2 * ATT_SUB or far is not None, "key blocks two or more behind must share one bias bucket"
    far = far if far is not None else N_BUCKETS // 2 - 1
    pairs = [(q, k) for q in range(nt) for k in range(q + 1)]
    qi_tab = jnp.asarray([q for q, _ in pairs], i32)
    ki_tab = jnp.asarray([k for _, k in pairs], i32)
    bias = _prompt_bias_tiles(rel_bias, ATT_SUB) * LOG2E
    cfar = jnp.repeat((rel_bias.astype(f32)[far] * LOG2E).reshape(H_A // 2, 2), 2 * ta, axis=1)[:, None, :]
    g_col = attn_g.astype(f32).reshape(DV, 1)
    T = B * L
    n_hp = H_A // 2
    full = lambda a: pl.BlockSpec(a.shape, lambda b, p, qt, kt: (0,) * a.ndim)
    grid_spec = pltpu.PrefetchScalarGridSpec(
        num_scalar_prefetch=2, grid=(B, len(pairs)),
        in_specs=[
            pl.BlockSpec((D_QK_ALL, ta), lambda b, p, qt, kt: (0, b * nt + qt[p])),
            pl.BlockSpec((ta, D_QK_ALL), lambda b, p, qt, kt: (b * nt + kt[p], 0)),
            pl.BlockSpec((H_A * V_ROWS, ta), lambda b, p, qt, kt: (0, b * nt + kt[p])),
            full(bias), full(cfar)] + [full(a) for a in lam_args] + [full(g_col)],
        out_specs=pl.BlockSpec((ta, D_ATT), lambda b, p, qt, kt: (b * nt + qt[p], 0)),
        scratch_shapes=[pltpu.VMEM((n_hp, LANES, 4 * ta), bf16), pltpu.VMEM((n_hp, 1, 4 * ta), f32),
                        pltpu.VMEM((H_A, V_ROWS, 2 * ta), f32), pltpu.VMEM((D_ATT, ta), f32)])
    return pl.pallas_call(
        functools.partial(_attn_prompt_kernel, lam_init=lam_init),
        grid_spec=grid_spec, out_shape=jax.ShapeDtypeStruct((T, D_ATT), bf16),
        compiler_params=_params(("parallel", "arbitrary")),
    )(qi_tab, ki_tab, qT, kb, v1T, bias, cfar, *lam_args, g_col)


def _attn_sample_kernel(q_ref, kn_ref, vn_ref, kc_ref, vc_ref, bc_ref, bn_ref,
                        lq1_ref, lk1_ref, lq2_ref, lk2_ref, g2_ref, o_ref, *, lam_init):
    s_len = q_ref.shape[0]
    lane = lax.broadcasted_iota(i32, (s_len, LANES), 1)
    lam = _lambda(lq1_ref, lk1_ref, lq2_ref, lk2_ref, lam_init)
    for hp in range(H_A // 2):
        sl = slice(hp * LANES, (hp + 1) * LANES)
        qt = q_ref[:, sl]
        kct = kc_ref[0, :, sl].astype(bf16)
        vct = vc_ref[0, :, sl].astype(bf16)
        knt = kn_ref[:, sl].astype(bf16)
        vnt = vn_ref[:, sl].astype(bf16)
        outs = []
        for hh in range(2):
            h = 2 * hp + hh
            oc = []
            for c in range(2):
                gi = 2 * hh + c
                qm = jnp.where((lane >= gi * DQK) & (lane < (gi + 1) * DQK), qt, jnp.zeros_like(qt))
                s1 = _dot_nt(qm, kct) + bc_ref[h]
                s2 = _dot_nt(qm, knt) + bn_ref[h]
                m = jnp.maximum(jnp.max(s1, axis=-1, keepdims=True), jnp.max(s2, axis=-1, keepdims=True))
                p1 = jnp.exp(s1 - m)
                p2 = jnp.exp(s2 - m)
                l = jnp.sum(p1, axis=-1, keepdims=True) + jnp.sum(p2, axis=-1, keepdims=True)
                pv = _dot(p1.astype(bf16), vct) + _dot(p2.astype(bf16), vnt)
                oc.append(pv / l)
            o = oc[0] - lam * oc[1]
            in_head = (lane >= hh * DV) & (lane < (hh + 1) * DV)
            ms = jnp.sum(jnp.where(in_head, o * o, 0.0), axis=-1, keepdims=True) * (1.0 / DV)
            outs.append(o * lax.rsqrt(ms + RMS_EPS))
        ot = jnp.where(lane < DV, outs[0], outs[1]) * g2_ref[...] * (1.0 - lam_init)
        o_ref[:, sl] = ot.astype(bf16)


def _attn_sample(q, k_new, v_new, cache_k, cache_v, rel_bias, lam_args, attn_g, lam_init):
    nb, past = cache_k.shape[0], cache_k.shape[1]
    T = q.shape[0]
    s = T // nb
    kc = cache_k.reshape(nb, past, D_QK_ALL)
    vc = cache_v.reshape(nb, past, D_ATT)
    bias = _sample_bias(rel_bias, past, s)
    bc, bn = bias[:, :, :past], bias[:, :, past:]
    g2 = jnp.tile(attn_g.astype(f32), 2).reshape(1, LANES)
    full = lambda a: pl.BlockSpec(a.shape, lambda b: (0,) * a.ndim)
    row = lambda n: pl.BlockSpec((s, n), lambda b: (b, 0))
    return pl.pallas_call(
        functools.partial(_attn_sample_kernel, lam_init=lam_init),
        grid=(nb,),
        in_specs=[row(D_QK_ALL), row(D_QK_ALL), row(D_ATT),
                  pl.BlockSpec((1, past, D_QK_ALL), lambda b: (b, 0, 0)),
                  pl.BlockSpec((1, past, D_ATT), lambda b: (b, 0, 0)),
                  full(bc), full(bn)] + [full(a) for a in lam_args] + [full(g2)],
        out_specs=row(D_ATT), out_shape=jax.ShapeDtypeStruct((T, D_ATT), bf16),
        compiler_params=_params(("parallel",)),
    )(q, k_new, v_new, kc, vc, bc, bn, *lam_args, g2)


def _ssd_kernel(z_ref, xbc_ref, dt_ref, dtT_ref, h0_ref, c0_ref, cw_ref, cb_ref, dtb_ref, dtbT_ref,
                alog_ref, alogT_ref, dsk_ref, g_ref, y_ref, hout_ref, cout_ref, xpad_sc, h_sc):
    q = xbc_ref.shape[0]
    c = pl.program_id(1)
    gw = HEADS_PER_GROUP * SSM_HEADDIM

    @pl.when(c == 0)
    def _():
        xpad_sc[0:SUBLANES] = c0_ref[0]
        h_sc[...] = h0_ref[0]

    xpad_sc[SUBLANES:SUBLANES + q] = xbc_ref[...]
    first = SUBLANES - (CONV_W - 1)
    conv = cb_ref[...]
    for i in range(CONV_W):
        conv = conv + xpad_sc[first + i:first + i + q] * cw_ref[i:i + 1, :]
    tail = xpad_sc[q:q + SUBLANES]
    xpad_sc[0:SUBLANES] = tail
    cout_ref[0] = tail
    act = conv * _sigmoid(conv)
    xs = act[:, :D_SSM]

    lane = lax.broadcasted_iota(i32, (1, LANES), 1)
    a_row = jnp.where(lane < H_S, -jnp.exp(alog_ref[...]), 0.0)
    dt = _softplus(dt_ref[...] + dtb_ref[...])
    ii = lax.broadcasted_iota(i32, (q, q), 0)
    jj = lax.broadcasted_iota(i32, (q, q), 1)
    causal = jj <= ii
    acs = _dot_f32_rhs(causal.astype(bf16), dt * a_row)
    sub = lax.broadcasted_iota(i32, (BF16_ROWS, 1), 0)
    a_col = jnp.where(sub < H_S, -jnp.exp(alogT_ref[...]), 0.0)
    dtT = _softplus(dtT_ref[0] + dtbT_ref[...])
    acsT = _dot_f32_lhs(dtT * a_col, (ii <= jj).astype(bf16))

    er = lax.broadcasted_iota(i32, (LANES, D_SSM), 0)
    ec = lax.broadcasted_iota(i32, (LANES, D_SSM), 1)
    expand = (ec // SSM_HEADDIM == er).astype(bf16)
    dt_x = _dot_f32_lhs(dt, expand)
    acs_x = _dot_f32_lhs(acs, expand)
    e_acs = jnp.exp(acs_x)
    acs_last = acs_x[q - 1:q, :]
    decay = jnp.exp(acs_last - acs_x) * dt_x
    dtx = xs * dt_x
    xd = xs * decay
    glane = lax.broadcasted_iota(i32, (1, gw), 1)

    ys = []
    for g in range(SSM_GROUPS):
        gs = slice(g * gw, (g + 1) * gw)
        bg = act[:, D_SSM + g * D_STATE:D_SSM + (g + 1) * D_STATE].astype(bf16)
        cg = act[:, D_SSM + (SSM_GROUPS + g) * D_STATE:D_SSM + (SSM_GROUPS + g + 1) * D_STATE].astype(bf16)
        cb = _dot_nt(cg, bg)
        h_old = h_sc[g]
        yg = _dot(cg, h_old.astype(bf16)) * e_acs[:, gs]
        dtx_g = dtx[:, gs]
        for e4 in range(HEADS_PER_GROUP):
            e = g * HEADS_PER_GROUP + e4
            seg = acs[:, e:e + 1] - acsT[e:e + 1, :]
            lmat = jnp.exp(jnp.where(causal, seg, -jnp.inf))
            rhs = jnp.where(glane // SSM_HEADDIM == e4, dtx_g, 0.0).astype(bf16)
            yg = yg + _dot((cb * lmat).astype(bf16), rhs)
        h_sc[g] = h_old * jnp.exp(acs_last[:, gs]) + _dot_tn(bg, xd[:, gs].astype(bf16))
        ys.append(yg)
    y = jnp.concatenate(ys, axis=1) + dsk_ref[...] * xs
    zz = z_ref[...]
    y = y * (zz * _sigmoid(zz))
    for g in range(SSM_GROUPS):
        gs = slice(g * gw, (g + 1) * gw)
        yg = y[:, gs]
        r = lax.rsqrt(jnp.mean(yg * yg, axis=-1, keepdims=True) + RMS_EPS)
        y_ref[:, gs] = (yg * r * g_ref[:, gs]).astype(bf16)
    hout_ref[0] = h_sc[...]


def _ssd(z, xbc, dt, dtT, h0, conv0, pw, B, L):
    q = min(SSD_TILE, L)
    nc = L // q
    gw = HEADS_PER_GROUP * SSM_HEADDIM
    dtT3 = dtT.reshape(BF16_ROWS, B, L).transpose(1, 0, 2)
    hT0 = h0.astype(f32).reshape(B, SSM_GROUPS, gw, D_STATE).transpose(0, 1, 3, 2)
    c0 = jnp.pad(conv0.astype(f32), ((0, 0), (SUBLANES - (CONV_W - 1), 0), (0, 0)))
    full = lambda a: pl.BlockSpec(a.shape, lambda b, c: (0,) * a.ndim)
    row = lambda n: pl.BlockSpec((q, n), lambda b, c: (b * nc + c, 0))
    params = [pw['conv_w'], pw['conv_b'], pw['dt_bias'], pw['dt_biasT'], pw['a_log'], pw['a_logT'],
              pw['d_skip'], pw['ssm_g']]
    y, hT, ctail = pl.pallas_call(
        _ssd_kernel, grid=(B, nc),
        in_specs=[row(D_SSM), row(CONV_DIM), row(LANES),
                  pl.BlockSpec((1, BF16_ROWS, q), lambda b, c: (b, 0, c)),
                  pl.BlockSpec((1, SSM_GROUPS, D_STATE, gw), lambda b, c: (b, 0, 0, 0)),
                  pl.BlockSpec((1, SUBLANES, CONV_DIM), lambda b, c: (b, 0, 0))] + [full(a) for a in params],
        out_specs=[row(D_SSM),
                   pl.BlockSpec((1, SSM_GROUPS, D_STATE, gw), lambda b, c: (b, 0, 0, 0)),
                   pl.BlockSpec((1, SUBLANES, CONV_DIM), lambda b, c: (b, 0, 0))],
        out_shape=[jax.ShapeDtypeStruct((B * L, D_SSM), bf16),
                   jax.ShapeDtypeStruct((B, SSM_GROUPS, D_STATE, gw), f32),
                   jax.ShapeDtypeStruct((B, SUBLANES, CONV_DIM), f32)],
        scratch_shapes=[pltpu.VMEM((q + SUBLANES, CONV_DIM), f32), pltpu.VMEM((SSM_GROUPS, D_STATE, gw), f32)],
        compiler_params=_params(("parallel", "arbitrary")),
    )(z, xbc, dt, dtT3, hT0, c0, *params)
    h_new = hT.transpose(0, 1, 3, 2).reshape(B, H_S, SSM_HEADDIM, D_STATE)
    return y, h_new, ctail[:, SUBLANES - (CONV_W - 1):, :]


def _mix_router_kernel(att_ref, ssm_ref, x_ref, woa_ref, wos_ref, g_ref, b_ref, wrh_ref, wrhl_ref, br_ref,
                       x1_ref, te_ref, gate_ref, cnt_ref, run_sc):
    @pl.when(pl.program_id(0) == 0)
    def _():
        run_sc[...] = jnp.zeros(run_sc.shape, f32)

    mix = _dot(att_ref[...], woa_ref[...]) + _dot(ssm_ref[...], wos_ref[...])
    x1 = _layer_norm(DEEPNORM_ALPHA * x_ref[...] + mix, g_ref[...], b_ref[...])
    x1_ref[...] = x1
    hi = x1.astype(bf16)
    lo = (x1 - hi.astype(f32)).astype(bf16)
    both = _dot(hi, wrhl_ref[...])
    logits = both[:, :LANES] + both[:, LANES:] + _dot(lo, wrh_ref[...]) + br_ref[...]
    lane =lax.broadcasted_iota(i32, logits.shape, 1)
    vals, idxs = [], []
    for _ in range(TOP_K):
        m = jnp.max(logits, axis=-1, keepdims=True)
        idx = jnp.min(jnp.where(logits == m, lane, LANES), axis=-1, keepdims=True)
        vals.append(m)
        idxs.append(idx)
        logits = jnp.where(lane == idx, -jnp.inf, logits)
    es = [jnp.exp(v - vals[0]) for v in vals]
    tot = es[0]
    for e in es[1:]:
        tot = tot + e
    tl = logits.shape[0]
    chosen = jnp.zeros(logits.shape, f32)
    for k in range(TOP_K):
        chosen = chosen + (lane == idxs[k]).astype(f32)
    ii = lax.broadcasted_iota(i32, (tl, tl), 0)
    jj = lax.broadcasted_iota(i32, (tl, tl), 1)
    before = _dot((jj < ii).astype(bf16), chosen.astype(bf16)) + run_sc[...]
    te = jnp.zeros(logits.shape, i32)
    gate = jnp.zeros(logits.shape, f32)
    for k in range(TOP_K):
        rank = jnp.sum(jnp.where(lane == idxs[k], before, 0.0), axis=-1, keepdims=True).astype(i32)
        te = jnp.where(lane == k, idxs[k], te)
        te = jnp.where(lane == TOP_K + k, rank, te)
        gate = jnp.where(lane == k, es[k] / tot, gate)
    te_ref[...] = te
    gate_ref[...] = gate
    run_sc[...] = run_sc[...] + jnp.sum(chosen, axis=0, keepdims=True)
    cnt_ref[...] = run_sc[...]


def _mix_router(att, ssm, x2d, w):
    T, D = x2d.shape
    tl = min(ROW_TILE, T)
    row = lambda n: pl.BlockSpec((tl, n), lambda i: (i, 0))
    full = lambda a: pl.BlockSpec(a.shape, lambda i: (0,) * a.ndim)
    ws = [w['wo_att'], w['wo_ssm'], w['ln1_g'], w['ln1_b'], w['wr_hi'], w['wr_hi_lo'], w['b_router']]
    return pl.pallas_call(
        _mix_router_kernel, grid=(T // tl,),
        in_specs=[row(D_ATT), row(D_SSM), row(D)] + [full(a) for a in ws],
        out_specs=[row(D), row(LANES), row(LANES), pl.BlockSpec((1, LANES), lambda i: (0, 0))],
        out_shape=[jax.ShapeDtypeStruct((T, D), f32), jax.ShapeDtypeStruct((T, LANES), i32),
                   jax.ShapeDtypeStruct((T, LANES), f32), jax.ShapeDtypeStruct((1, LANES), f32)],
        scratch_shapes=[pltpu.VMEM((1, LANES), f32)],
        compiler_params=_params(("arbitrary",)),
    )(att, ssm, x2d, *ws)


def _route(te, counts_f):
    T = te.shape[0]
    n_assign = T * TOP_K
    counts = counts_f[0, :N_EXPERTS].astype(i32)
    padded = (counts + MOE_BLOCK - 1) // MOE_BLOCK * MOE_BLOCK
    pad_end = jnp.cumsum(padded)
    pad_start = pad_end - padded
    top_e, rank = te[:, :TOP_K], te[:, TOP_K:2 * TOP_K]
    experts = jnp.arange(N_EXPERTS, dtype=i32)
    slot = rank + jnp.sum(jnp.where(top_e[..., None] == experts, pad_start, 0), axis=-1)
    n_blocks = -(-n_assign // MOE_BLOCK) + N_EXPERTS
    n_blocks += n_blocks % FFN_BLOCKS_PER_STEP
    block_start = jnp.arange(n_blocks, dtype=i32) * MOE_BLOCK
    block_e = jnp.minimum(jnp.sum(block_start[:, None] >= pad_end[None, :], axis=-1), N_EXPERTS - 1).astype(i32)
    n_used = (pad_end[-1] // MOE_BLOCK).astype(i32).reshape(1)
    row_end = jnp.sum(jnp.where(block_e[:, None] == experts, pad_start + counts, 0), axis=-1)
    rows_valid = jnp.clip(row_end - block_start, 0, MOE_BLOCK).astype(i32)
    return block_e, n_used, rows_valid, slot.astype(i32), n_blocks


SC_INDEX_WINDOW = 128
SC_ROWS = 32


def _sc_mesh():
    return plsc.VectorSubcoreMesh(core_axis_name="c", subcore_axis_name="s")


def _sc_move_rows(src, src_idx, dst_idx_list, n_out):
    M = src_idx.shape[0]
    D = src.shape[1]
    idx = [a.reshape(1, M) for a in [src_idx] + list(dst_idx_list)]

    @functools.partial(pl.kernel, out_type=jax.ShapeDtypeStruct((n_out, D), src.dtype), mesh=_sc_mesh(),
                       scratch_types=[pltpu.VMEM((SC_ROWS, D), src.dtype)])
    def move(s_hbm, *rest):
        i_hbm, o_hbm, buf = rest[:-2], rest[-2], rest[-1]

        def body(si_vmem, *di_vmem):
            for j in range(SC_INDEX_WINDOW // SC_ROWS):
                part = pl.ds(j * SC_ROWS, SC_ROWS)
                pltpu.sync_copy(s_hbm.at[si_vmem.at[0, part]], buf)
                for dv in di_vmem:
                    pltpu.sync_copy(buf, o_hbm.at[dv.at[0, part]])

        pltpu.emit_pipeline(
            body, grid=(M // SC_INDEX_WINDOW,),
            in_specs=[pl.BlockSpec((1, SC_INDEX_WINDOW), lambda i: (0, i))] * len(idx),
            out_specs=[], core_axis_name=("c", "s"), dimension_semantics=(pltpu.PARALLEL,),
        )(*i_hbm)

    return move(src, *idx)


def _sc_scatter_rows(x, idx_k, n_rows):
    return _sc_move_rows(x, jnp.arange(x.shape[0], dtype=i32), idx_k, n_rows)


def _sc_gather_rows(src, idx):
    M = idx.shape[0]
    return _sc_move_rows(src, idx, [jnp.arange(M, dtype=i32)], M)


def _slot_tiles(slot, tl):
    T = slot.shape[0]
    return slot.reshape(T // tl, tl, TOP_K).transpose(0, 2, 1).reshape(T // tl, 1, TOP_K * tl)


def _dispatch_kernel(last_ref, nu_ref, slot_ref, x_ref, xs_hbm, zero_sc, sem):
    n_copies = slot_ref.shape[-1]
    tl = n_copies // TOP_K
    n_blocks = xs_hbm.shape[0] // MOE_BLOCK

    @pl.when(pl.program_id(0) == 0)
    def _():
        zero_sc[...] = jnp.zeros(zero_sc.shape, f32)

        def zero_block(b):
            return pltpu.make_async_copy(zero_sc, xs_hbm.at[pl.ds(b * MOE_BLOCK, MOE_BLOCK)], sem.at[1])

        def each_zero_block(fn):
            def per_expert(e, carry):
                @pl.when(last_ref[e] >= 0)
                def _():
                    fn(zero_block(last_ref[e]))
                return carry
            lax.fori_loop(0, N_EXPERTS, per_expert, 0)

            def per_unused(b, carry):
                fn(zero_block(b))
                return carry
            lax.fori_loop(nu_ref[0], n_blocks, per_unused, 0)

        each_zero_block(lambda c: c.start())
        each_zero_block(lambda c: c.wait())

    def body(r, carry):
        src = x_ref.at[pl.ds(r, 1)]
        for k in range(TOP_K):
            pltpu.make_async_copy(src, xs_hbm.at[pl.ds(slot_ref[0, 0, k * tl + r], 1)], sem.at[0]).start()
        return carry
    lax.fori_loop(0, tl, body, 0, unroll=4)
    pltpu.make_async_copy(xs_hbm.at[pl.ds(0, n_copies)], xs_hbm.at[pl.ds(0, n_copies)], sem.at[0]).wait()


def _dispatch(x1, slot, last_block, n_used, n_blocks):
    T, D = x1.shape
    tl = min(ROW_TILE, T)
    tiles = _slot_tiles(slot, tl)
    grid_spec = pltpu.PrefetchScalarGridSpec(
        num_scalar_prefetch=2, grid=(T // tl,),
        in_specs=[pl.BlockSpec((1, 1, TOP_K * tl), lambda i, lb, nu: (i, 0, 0), memory_space=pltpu.SMEM),
                  pl.BlockSpec((tl, D), lambda i, lb, nu: (i, 0))],
        out_specs=pl.BlockSpec(memory_space=pl.ANY),
        scratch_shapes=[pltpu.VMEM((MOE_BLOCK, D), f32), pltpu.SemaphoreType.DMA((2,))])
    return pl.pallas_call(
        _dispatch_kernel, grid_spec=grid_spec,
        out_shape=jax.ShapeDtypeStruct((n_blocks * MOE_BLOCK, D), f32),
        compiler_params=_params(("arbitrary",)),
    )(last_block, n_used, tiles, x1)


def _deinterleave_kernel(w_ref, g_ref, u_ref):
    wb = w_ref[0].astype(bf16)
    tn = g_ref.shape[-1]
    r = lax.broadcasted_iota(i32, (2 * tn, tn), 0)
    c = lax.broadcasted_iota(i32, (2 * tn, tn), 1)
    g_ref[0] = _dot(wb, (r == 2 * c).astype(bf16)).astype(bf16)
    u_ref[0] = _dot(wb, (r == 2 * c + 1).astype(bf16)).astype(bf16)


def _deinterleave(w_gu):
    E, D, F2 = w_gu.shape
    tn = 2 * LANES
    out = jax.ShapeDtypeStruct((E, D, F2 // 2), bf16)
    return pl.pallas_call(
        _deinterleave_kernel, grid=(E, F2 // (2 * tn)),
        in_specs=[pl.BlockSpec((1, D, 2 * tn), lambda e, j: (e, 0, j))],
        out_specs=[pl.BlockSpec((1, D, tn), lambda e, j: (e, 0, j))] * 2,
        out_shape=[out, out],
        compiler_params=_params(("parallel", "parallel")),
    )(w_gu)


def _row_gather(idx_ref, n, src_hbm, dst, sem):
    def body(r, carry):
        pltpu.make_async_copy(src_hbm.at[pl.ds(idx_ref[0, 0, r], 1)], dst.at[pl.ds(r, 1)], sem).start()
        return carry
    lax.fori_loop(0, n, body, 0, unroll=8)


def _ffn_kernel(be_ref, nu_ref, rv_ref, xs_ref, *refs):
    del be_ref
    y_ref = refs[-1]
    wsets = [refs[6 * j:6 * j + 6] for j in range(FFN_BLOCKS_PER_STEP)]
    first = pl.program_id(0) * FFN_BLOCKS_PER_STEP
    n_live = jnp.clip(nu_ref[0] - first, 0, FFN_BLOCKS_PER_STEP)
    row_id = lax.broadcasted_iota(i32, (MOE_BLOCK, 1), 0)

    def block(j):
        wg_ref, wu_ref, bg_ref, bu_ref, wd_ref, bd_ref = wsets[j]
        rows = pl.ds(j * MOE_BLOCK, MOE_BLOCK)
        xb = jnp.where(row_id < rv_ref[first + j], xs_ref[rows, :], 0.0).astype(bf16)
        g = _dot(xb, wg_ref[0]) + bg_ref[0]
        u = _dot(xb, wu_ref[0]) + bu_ref[0]
        g = jnp.minimum(g, SWIGLU_LIMIT)
        u = jnp.clip(u, -SWIGLU_LIMIT, SWIGLU_LIMIT)
        act = (u + 1.0) * g * _sigmoid(SWIGLU_ALPHA * g)
        y_ref[rows, :] = _dot(act.astype(bf16), wd_ref[0]) + bd_ref[0]

    for live in range(FFN_BLOCKS_PER_STEP + 1):
        @pl.when(n_live == live)
        def _(live=live):
            for j in range(live):
                block(j)
            for j in range(live, FFN_BLOCKS_PER_STEP):
                y_ref[pl.ds(j * MOE_BLOCK, MOE_BLOCK), :] = jnp.zeros((MOE_BLOCK, y_ref.shape[1]), f32)


def _expert_ffn(xs, block_e, n_used, rows_valid, w):
    n_rows, D = xs.shape
    nps = FFN_BLOCKS_PER_STEP
    n_steps = n_rows // (MOE_BLOCK * nps)
    F = w['w_g'].shape[2]
    wspecs, wargs = [], []
    for j in range(nps):
        pick = lambda i, be, nu, rv, j=j: (be[i * nps + j], 0, 0)
        wspecs += [pl.BlockSpec((1, D, F), pick), pl.BlockSpec((1, D, F), pick), pl.BlockSpec((1, 1, F), pick),
                   pl.BlockSpec((1, 1, F), pick), pl.BlockSpec((1, F, D), pick), pl.BlockSpec((1, 1, D), pick)]
        wargs += [w['w_g'], w['w_u'], w['b_g'], w['b_u'], w['w_d'], w['b_d']]
    grid_spec = pltpu.PrefetchScalarGridSpec(
        num_scalar_prefetch=3, grid=(n_steps,),
        in_specs=[pl.BlockSpec((nps * MOE_BLOCK, D),
                               lambda i, be, nu, rv: (jnp.minimum(i, (nu[0] - 1) // nps), 0))] + wspecs,
        out_specs=pl.BlockSpec((nps * MOE_BLOCK, D), lambda i, be, nu, rv: (i, 0)))
    return pl.pallas_call(
        _ffn_kernel, grid_spec=grid_spec,
        out_shape=jax.ShapeDtypeStruct((n_rows, D), f32),
        compiler_params=_params(("arbitrary",)),
    )(block_e, n_used, rows_valid, xs, *wargs)


def _combine_dense_kernel(rows_ref, x1_ref, gate_ref, g_ref, b_ref, y_ref):
    gate = gate_ref[...]
    ff = gate[:, 0:1] * rows_ref[0]
    for k in range(1, TOP_K):
        ff = ff + gate[:, k:k + 1] * rows_ref[k]
    y_ref[...] = _layer_norm(DEEPNORM_ALPHA * x1_ref[...] + ff, g_ref[...], b_ref[...])


def _combine_dense(rows, x1, gate, ln_g, ln_b):
    T, D = x1.shape
    tl = min(ROW_TILE, T)
    row = lambda n: pl.BlockSpec((tl, n), lambda i: (i, 0))
    full = lambda a: pl.BlockSpec(a.shape, lambda i: (0,) * a.ndim)
    return pl.pallas_call(
        _combine_dense_kernel, grid=(T // tl,),
        in_specs=[pl.BlockSpec((TOP_K, tl, D), lambda i: (0, i, 0)), row(D), row(LANES), full(ln_g), full(ln_b)],
        out_specs=row(D), out_shape=jax.ShapeDtypeStruct((T, D), f32),
        compiler_params=_params(("parallel",)),
    )(rows, x1, gate, ln_g, ln_b)


def _combine_kernel(slot_ref, slotn_ref, ys_hbm, x1_ref, gate_ref, g_ref, b_ref, y_ref, buf, sem):
    i = pl.program_id(0)
    n = pl.num_programs(0)
    tl = x1_ref.shape[0]
    cur = i % 2

    @pl.when(i == 0)
    def _():
        _row_gather(slot_ref, TOP_K * tl, ys_hbm, buf.at[0], sem.at[0])

    @pl.when(i + 1 < n)
    def _():
        _row_gather(slotn_ref, TOP_K * tl, ys_hbm, buf.at[1 - cur], sem.at[1 - cur])

    pltpu.make_async_copy(buf.at[cur], buf.at[cur], sem.at[cur]).wait()
    gate = gate_ref[...]
    ff = gate[:, 0:1] * buf[cur, 0:tl]
    for k in range(1, TOP_K):
        ff = ff + gate[:, k:k + 1] * buf[cur, k * tl:(k + 1) * tl]
    y_ref[...] = _layer_norm(DEEPNORM_ALPHA * x1_ref[...] + ff, g_ref[...], b_ref[...])


def _combine(ys, slot, x1, gate, ln_g, ln_b):
    T, D = x1.shape
    tl = min(COMB_TILE, T)
    nt = T // tl
    slot_tiles = _slot_tiles(slot, tl)
    row = lambda n: pl.BlockSpec((tl, n), lambda i: (i, 0))
    full = lambda a: pl.BlockSpec(a.shape, lambda i: (0,) * a.ndim)
    return pl.pallas_call(
        _combine_kernel, grid=(nt,),
        in_specs=[pl.BlockSpec((1, 1, TOP_K * tl), lambda i: (i, 0, 0), memory_space=pltpu.SMEM),
                  pl.BlockSpec((1, 1, TOP_K * tl), lambda i: (jnp.minimum(i + 1, nt - 1), 0, 0),
                               memory_space=pltpu.SMEM),
                  pl.BlockSpec(memory_space=pl.ANY), row(D), row(LANES), full(ln_g), full(ln_b)],
        out_specs=row(D), out_shape=jax.ShapeDtypeStruct((T, D), f32),
        scratch_shapes=[pltpu.VMEM((2, TOP_K * tl, D), f32), pltpu.SemaphoreType.DMA((2,))],
        compiler_params=_params(("arbitrary",)),
    )(slot_tiles, slot_tiles, ys, x1, gate, ln_g, ln_b)


def _prep_weights(l, w_in, conv_w, conv_b, dt_bias, a_log, d_skip, ssm_norm_g, w_o, ln1_g, ln1_b,
                  w_router, b_router, w_gate_up, b_gate_up, w_down, b_down, ln2_g, ln2_b):
    wi = w_in[l]
    c0, c1, c2, c3 = D_QK_ALL, 2 * D_QK_ALL, 2 * D_QK_ALL + D_ATT, 2 * D_QK_ALL + D_ATT + D_SSM
    c4 = c3 + CONV_DIM
    wdt = wi[:, c4:c4 + H_S]
    pad_lane = lambda v, fill=0.0: jnp.pad(v.astype(f32).reshape(1, -1), ((0, 0), (0, LANES - v.shape[-1])),
                                            constant_values=fill)
    pad_col = lambda v: jnp.pad(v.astype(f32).reshape(-1, 1), ((0, BF16_ROWS - v.shape[-1]), (0, 0)))
    wr = jnp.pad(w_router[l].astype(f32), ((0, 0), (0, LANES - N_EXPERTS)))
    wr_hi = wr.astype(bf16)
    wgu = w_gate_up[l]
    return {
        'wq': wi[:, :c0].astype(bf16), 'wqT': wi[:, :c0].T.astype(bf16),
        'wk': wi[:, c0:c1].astype(bf16), 'wv': wi[:, c1:c2].astype(bf16), 'wvT': wi[:, c1:c2].T.astype(bf16),
        'wz': wi[:, c2:c3].astype(bf16), 'wx': wi[:, c3:c4].astype(bf16),
        'wdt': jnp.pad(wdt, ((0, 0), (0, LANES - H_S))).astype(bf16),
        'wdtT': jnp.pad(wdt.T, ((0, BF16_ROWS - H_S), (0, 0))).astype(bf16),
        'conv_w': conv_w[l].astype(f32), 'conv_b': conv_b[l].astype(f32).reshape(1, -1),
        'dt_bias': pad_lane(dt_bias[l]), 'dt_biasT': pad_col(dt_bias[l]),
        'a_log': pad_lane(a_log[l]), 'a_logT': pad_col(a_log[l]),
        'd_skip': jnp.repeat(d_skip[l].astype(f32), SSM_HEADDIM).reshape(1, -1),
        'ssm_g': ssm_norm_g[l].astype(f32).reshape(1, -1),
        'wo_att': w_o[l][:D_ATT].astype(bf16), 'wo_ssm': w_o[l][D_ATT:].astype(bf16),
        'ln1_g': ln1_g[l].astype(f32).reshape(1, -1), 'ln1_b': ln1_b[l].astype(f32).reshape(1, -1),
        'wr_hi': wr_hi, 'wr_hi_lo': jnp.concatenate([wr_hi, (wr - wr_hi.astype(f32)).astype(bf16)], axis=1),
        'b_router': pad_lane(b_router[l], NEG_INF),
        'w_gu': wgu,
        'b_g': b_gate_up[l][:, None, 0::2].astype(f32), 'b_u': b_gate_up[l][:, None, 1::2].astype(f32),
        'w_d': w_down[l].astype(bf16), 'b_d': b_down[l][:, None, :].astype(f32),
        'ln2_g': ln2_g[l].astype(f32).reshape(1, -1), 'ln2_b': ln2_b[l].astype(f32).reshape(1, -1),
    }


def _trunk_layer(x, l, w, rel_bias, lam_args, attn_g, k_past, v_past, h0, conv0):
    B, L, D = x.shape
    x2d = x.reshape(B * L, D)
    lam_init = 0.8 - 0.6 * math.exp(-0.3 * l)
    prompt = k_past is None
    proj = _in_proj(x2d, w, prompt)
    k, v, z, xbc, dt, dtT = proj[:6]
    if prompt:
        kb, qT, v1T = proj[6:]
        att = _attn_prompt(qT, kb, v1T, rel_bias, lam_args, attn_g, B, L, lam_init)
    else:
        att = _attn_sample(proj[6], k, v, k_past, v_past, rel_bias, lam_args, attn_g, lam_init)
    ssm, h_new, conv_new = _ssd(z, xbc, dt, dtT, h0, conv0, w, B, L)
    x1, te, gate, counts = _mix_router(att, ssm, x2d, w)
    block_e, n_used, rows_valid, slot, n_blocks = _route(te, counts)
    slot_k = slot.T
    xs = _sc_scatter_rows(x1, [slot_k[k] for k in range(TOP_K)], n_blocks * MOE_BLOCK)
    ys = _expert_ffn(xs, block_e, n_used, rows_valid, w)
    rows = _sc_gather_rows(ys, slot_k.reshape(-1)).reshape(TOP_K, B * L, D)
    y = _combine_dense(rows, x1, gate, w['ln2_g'], w['ln2_b'])
    return (y.reshape(B, L, D), k.reshape(B, L, H_A, 2 * DQK), v.reshape(B, L, H_A, DV), h_new, conv_new)


def kernel(x_prompt, x_sample, cache_k, cache_v, state_ssm, state_conv, rel_bias, w_in, lambda_q1, lambda_k1, lambda_q2, lambda_k2, attn_norm_g, conv_w, conv_b, dt_bias, a_log, d_skip, ssm_norm_g, w_o, ln1_g, ln1_b, w_router, b_router, w_gate_up, b_gate_up, w_down, b_down, ln2_g, ln2_b):
    yp, ys = x_prompt, x_sample
    bp = x_prompt.shape[0]
    depth = w_in.shape[0]
    outs = [[] for _ in range(8)]
    for l in range(depth):
        w = _prep_weights(l, w_in, conv_w, conv_b, dt_bias, a_log, d_skip, ssm_norm_g, w_o, ln1_g, ln1_b,
                          w_router, b_router, w_gate_up, b_gate_up, w_down, b_down, ln2_g, ln2_b)
        w['w_g'], w['w_u'] = _deinterleave(w.pop('w_gu'))
        lam_args = [a[l].astype(f32).reshape(1, -1) for a in (lambda_q1, lambda_k1, lambda_q2, lambda_k2)]
        h0 = jnp.zeros((bp, H_S, SSM_HEADDIM, D_STATE), f32)
        c0 = jnp.zeros((bp, CONV_W - 1, CONV_DIM), f32)
        yp, kp, vp, hp, cp = _trunk_layer(yp, l, w, rel_bias, lam_args, attn_norm_g[l], None, None, h0, c0)
        ys, ks, vs, hs, cs = _trunk_layer(ys, l, w, rel_bias, lam_args, attn_norm_g[l], cache_k[l], cache_v[l],
                                          state_ssm[l], state_conv[l])
        for lst, a in zip(outs, (kp, vp, hp, cp, ks, vs, hs, cs)):
            lst.append(a)
    return (yp, ys) + tuple(jnp.stack(o) for o in outs)
```

```python
import functools
import math

import numpy as np
import jax
import jax.numpy as jnp
from jax import lax
from jax.experimental import pallas as pl
from jax.experimental.pallas import tpu as pltpu
from jax.experimental.pallas import tpu_sc as plsc

f32 = jnp.float32
bf16 = jnp.bfloat16
i32 = jnp.int32

CHUNK = 64
H_A = 8
DQK = 32
DV = 2 * DQK
D_ATT = H_A * DV
SSM_HEADDIM = 64
H_S = 8
D_SSM = H_S * SSM_HEADDIM
SSM_GROUPS = 2
HEADS_PER_GROUP = H_S // SSM_GROUPS
D_STATE = 128
CONV_W = 4
CONV_DIM = D_SSM + 2 * SSM_GROUPS * D_STATE
D_QK_ALL = H_A * 2 * DQK
N_BUCKETS = 32
MAX_DISTANCE = 128
N_EXPERTS = 32
TOP_K = 4
SWIGLU_LIMIT = 7.0
SWIGLU_ALPHA = 1.702
MOE_BLOCK = 256
LN_EPS = 1e-5
RMS_EPS = 1e-5
NEG_INF = -1e30
DEPTH = 1
DEEPNORM_ALPHA = (2.0 * DEPTH) ** 0.25
LOG2E = math.log2(math.e)

LANES = 128
SUBLANES = 8
BF16_ROWS = 16
VMEM_LIMIT = 48 * 1024 * 1024

ROW_TILE = 512
ATT_TILE = 512
ATT_SUB = 256
SSD_TILE = 256
V_ROWS = DV + BF16_ROWS
FFN_BLOCKS_PER_STEP = 2


def _params(semantics):
    return pltpu.CompilerParams(dimension_semantics=semantics, vmem_limit_bytes=VMEM_LIMIT)


def _dot(a, b):
    return jnp.dot(a, b, preferred_element_type=f32)


def _dot_nt(a, b):
    return lax.dot_general(a, b, (((1,), (1,)), ((), ())), preferred_element_type=f32)


def _dot_tn(a, b):
    return lax.dot_general(a, b, (((0,), (0,)), ((), ())), preferred_element_type=f32)


def _split3(a):
    hi = a.astype(bf16)
    r1 = a - hi.astype(f32)
    mid = r1.astype(bf16)
    lo = (r1 - mid.astype(f32)).astype(bf16)
    return hi, mid, lo


def _dot_f32_lhs(a, b_exact):
    hi, mid, lo = _split3(a)
    return _dot(hi, b_exact) + _dot(mid, b_exact) + _dot(lo, b_exact)


def _dot_f32_rhs(a_exact, b):
    hi, mid, lo = _split3(b)
    return _dot(a_exact, hi) + _dot(a_exact, mid) + _dot(a_exact, lo)


def _softplus(x):
    return jnp.maximum(x, 0.0) + jnp.log1p(jnp.exp(-jnp.abs(x)))


def _sigmoid(x):
    return 1.0 / (1.0 + jnp.exp(-x))


def _layer_norm(y, g, b):
    mu = jnp.mean(y, axis=-1, keepdims=True)
    yc = y - mu
    var = jnp.mean(yc * yc, axis=-1, keepdims=True)
    return yc * lax.rsqrt(var + LN_EPS) * g + b


def _lambda(lq1_ref, lk1_ref, lq2_ref, lk2_ref, lam_init):
    s1 = jnp.sum(lq1_ref[...] * lk1_ref[...], axis=-1, keepdims=True)
    s2 = jnp.sum(lq2_ref[...] * lk2_ref[...], axis=-1, keepdims=True)
    return jnp.exp(s1) - jnp.exp(s2) + lam_init


def _in_proj_kernel(x_ref, wk_ref, wv_ref, wz_ref, wx_ref, wdt_ref, wdtT_ref, wq_ref, wvT_ref, *outs, prompt):
    xb = x_ref[...].astype(bf16)
    k_ref, v_ref, z_ref, xbc_ref, dt_ref, dtT_ref = outs[:6]
    k = _dot(xb, wk_ref[...])
    k_ref[...] = k
    v_ref[...] = _dot(xb, wv_ref[...])
    z_ref[...] = _dot(xb, wz_ref[...])
    xbc_ref[...] = _dot(xb, wx_ref[...])
    dt_ref[...] = _dot(xb, wdt_ref[...])
    dtT_ref[...] = _dot_nt(wdtT_ref[...], xb)
    scale = DQK ** -0.5
    if prompt:
        kb_ref, qT_ref, v1T_ref = outs[6:]
        kb_ref[...] = k.astype(bf16)
        qT_ref[...] = (_dot_nt(wq_ref[...], xb) * (scale * LOG2E)).astype(bf16)
        vT = _dot_nt(wvT_ref[...], xb).astype(bf16)
        ones = jnp.ones((BF16_ROWS, vT.shape[1]), bf16)
        for h in range(H_A):
            v1T_ref[h * V_ROWS:h * V_ROWS + DV, :] = vT[h * DV:(h + 1) * DV, :]
            v1T_ref[h * V_ROWS + DV:(h + 1) * V_ROWS, :] = ones
    else:
        (q_ref,) = outs[6:]
        q_ref[...] = (_dot(xb, wq_ref[...]) * scale).astype(bf16)


def _in_proj(x2d, w, prompt):
    T, D = x2d.shape
    tl = min(ROW_TILE, T)
    grid = (T // tl,)
    row = lambda n: pl.BlockSpec((tl, n), lambda i: (i, 0))
    col = lambda n: pl.BlockSpec((n, tl), lambda i: (0, i))
    full = lambda a: pl.BlockSpec(a.shape, lambda i: (0,) * a.ndim)
    wq = w['wqT'] if prompt else w['wq']
    ins = [x2d, w['wk'], w['wv'], w['wz'], w['wx'], w['wdt'], w['wdtT'], wq, w['wvT']]
    in_specs = [row(D)] + [full(a) for a in ins[1:]]
    out_shape = [jax.ShapeDtypeStruct((T, D_QK_ALL), f32), jax.ShapeDtypeStruct((T, D_ATT), f32),
                 jax.ShapeDtypeStruct((T, D_SSM), f32), jax.ShapeDtypeStruct((T, CONV_DIM), f32),
                 jax.ShapeDtypeStruct((T, LANES), f32), jax.ShapeDtypeStruct((BF16_ROWS, T), f32)]
    out_specs = [row(D_QK_ALL), row(D_ATT), row(D_SSM), row(CONV_DIM), row(LANES), col(BF16_ROWS)]
    if prompt:
        out_shape += [jax.ShapeDtypeStruct((T, D_QK_ALL), bf16), jax.ShapeDtypeStruct((D_QK_ALL, T), bf16),
                      jax.ShapeDtypeStruct((H_A * V_ROWS, T), bf16)]
        out_specs += [row(D_QK_ALL), col(D_QK_ALL), col(H_A * V_ROWS)]
    else:
        out_shape += [jax.ShapeDtypeStruct((T, D_QK_ALL), bf16)]
        out_specs += [row(D_QK_ALL)]
    return pl.pallas_call(
        functools.partial(_in_proj_kernel, prompt=prompt),
        grid=grid, in_specs=in_specs, out_specs=out_specs, out_shape=out_shape,
        compiler_params=_params(("parallel",)),
    )(*ins)


def _t5_bucket(rel):
    half = N_BUCKETS // 2
    max_exact = half // 2
    n = jnp.abs(rel)
    large = max_exact + (jnp.log(jnp.maximum(n, 1).astype(f32) / max_exact)
                         / math.log(MAX_DISTANCE / max_exact) * (half - max_exact)).astype(i32)
    large = jnp.minimum(large, half - 1)
    return jnp.where(rel > 0, half, 0) + jnp.where(n < max_exact, n, large)


def _far_bucket(min_dist):
    half = N_BUCKETS // 2
    max_exact = half // 2
    v = max_exact + int(math.log(min_dist / max_exact) / math.log(MAX_DISTANCE / max_exact) * (half - max_exact))
    return half - 1 if (min_dist >= max_exact and v - 1 >= half - 1) else None


def _bias_lookup(rel_bias, bucket):
    onehot = (bucket[..., None] == jnp.arange(N_BUCKETS, dtype=i32)).astype(f32)
    return jnp.dot(onehot, rel_bias.astype(f32), precision=lax.Precision.HIGHEST)


def _prompt_bias_tiles(rel_bias, ta):
    j = jnp.arange(ta, dtype=i32)[:, None]
    i = jnp.arange(ta, dtype=i32)[None, :]
    tiles = []
    for d in range(2):
        rel = j - i - d * ta
        b = _bias_lookup(rel_bias, _t5_bucket(rel))
        allowed = ((j // CHUNK) <= (i // CHUNK)) if d == 0 else jnp.ones((ta, ta), bool)
        tiles.append(jnp.where(allowed[..., None], b, NEG_INF))
    return jnp.transpose(jnp.stack(tiles), (3, 0, 1, 2))


def _sample_bias(rel_bias, past, s):
    q_pos = past + jnp.arange(s, dtype=i32)
    k_pos = jnp.arange(past + s, dtype=i32)
    rel = k_pos[None, :] - q_pos[:, None]
    b = _bias_lookup(rel_bias, _t5_bucket(rel))
    allowed = (k_pos[None, :] // CHUNK) <= (q_pos[:, None] // CHUNK)
    return jnp.transpose(jnp.where(allowed[..., None], b, NEG_INF), (2, 0, 1))


def _attn_prompt_kernel(qi_tab, ki_tab, qT_ref, k_ref, v1T_ref, bias_ref, cfar_ref,
                        lq1_ref, lk1_ref, lq2_ref, lk2_ref, g_ref, o_ref,
                        qm_sc, m_sc, acc_sc, oT_sc, *, lam_init):
    p = pl.program_id(1)
    qi = qi_tab[p]
    ki = ki_tab[p]
    d = qi - ki
    ta = k_ref.shape[0]
    n_hp = H_A // 2

    @pl.when(ki == 0)
    def _():
        m_sc[...] = jnp.full(m_sc.shape, NEG_INF, f32)
        acc_sc[...] = jnp.zeros(acc_sc.shape, f32)
        row = lax.broadcasted_iota(i32, (LANES, ta), 0)
        for hp in range(n_hp):
            qt = qT_ref[hp * LANES:(hp + 1) * LANES, :]
            for gi in range(4):
                keep = (row >= gi * DQK) & (row < (gi + 1) * DQK)
                qm_sc[hp, :, gi * ta:(gi + 1) * ta] = jnp.where(keep, qt, jnp.zeros_like(qt))

    def scores(hp):
        return _dot(k_ref[:, hp * LANES:(hp + 1) * LANES], qm_sc[hp])

    nb = ta // ATT_SUB

    def near_bias(hp, step_d):
        rows = []
        for jb in range(nb):
            cols = []
            for g in range(4):
                for ib in range(nb):
                    dd = step_d * nb + ib - jb
                    if dd < 0:
                        blk = jnp.full((ATT_SUB, ATT_SUB), NEG_INF * LOG2E, f32)
                    elif dd < 2:
                        blk = bias_ref[2 * hp + g // 2, dd]
                    else:
                        lo = g * ta + ib * ATT_SUB
                        blk = jnp.broadcast_to(cfar_ref[hp][:, lo:lo + ATT_SUB], (ATT_SUB, ATT_SUB))
                    cols.append(blk)
            rows.append(jnp.concatenate(cols, axis=1))
        return jnp.concatenate(rows, axis=0)

    def update(step_d):
        s_next = scores(0)
        for hp in range(n_hp):
            s = s_next
            if hp + 1 < n_hp:
                s_next = scores(hp + 1)
            m_old = m_sc[hp]
            if step_d is not None:
                s = s + near_bias(hp, step_d)
                m_new = jnp.maximum(m_old, jnp.max(s, axis=0, keepdims=True))
                shift = m_new
            else:
                c = cfar_ref[hp]
                m_new = jnp.maximum(m_old, jnp.max(s, axis=0, keepdims=True) + c)
                shift = m_new - c
            alpha = jnp.exp2(m_old - m_new)
            pT = jnp.exp2(s - shift).astype(bf16)
            for hh in range(2):
                h = 2 * hp + hh
                cols = slice(hh * 2 * ta, (hh + 1) * 2 * ta)
                pv = _dot(v1T_ref[h * V_ROWS:(h + 1) * V_ROWS, :], pT[:, cols])
                acc_sc[h] = alpha[:, cols] * acc_sc[h] + pv
            m_sc[hp] = m_new

    @pl.when(d == 0)
    def _():
        update(0)

    @pl.when(d == 1)
    def _():
        update(1)

    @pl.when(d >= 2)
    def _():
        update(None)

    @pl.when(d == 0)
    def _():
        lam = _lambda(lq1_ref, lk1_ref, lq2_ref, lk2_ref, lam_init)
        for h in range(H_A):
            a = acc_sc[h]
            a0 = a[:, :ta]
            a1 = a[:, ta:]
            o = a0[:DV] / a0[DV:DV + 1] - lam * (a1[:DV] / a1[DV:DV + 1])
            o = o * lax.rsqrt(jnp.mean(o * o, axis=0, keepdims=True) + RMS_EPS) * g_ref[...] * (1.0 - lam_init)
            oT_sc[h * DV:(h + 1) * DV, :] = o
        o_ref[...] = oT_sc[...].T.astype(bf16)


def _attn_prompt(qT, kb, v1T, rel_bias, lam_args, attn_g, B, L, lam_init):
    ta = min(ATT_TILE, L)
    assert L % ta == 0 and ta % ATT_SUB == 0 and ATT_SUB % CHUNK == 0
    nt = L // ta
    far = _far_bucket(ATT_SUB + 1)
    assert L <= 2 * ATT_SUB or far is not None, "key blocks two or more behind must share one bias bucket"
    far = far if far is not None else N_BUCKETS // 2 - 1
    pairs = [(q, k) for q in range(nt) for k in range(q + 1)]
    qi_tab = jnp.asarray([q for q, _ in pairs], i32)
    ki_tab = jnp.asarray([k for _, k in pairs], i32)
    bias = _prompt_bias_tiles(rel_bias, ATT_SUB) * LOG2E
    cfar = jnp.repeat((rel_bias.astype(f32)[far] * LOG2E).reshape(H_A // 2, 2), 2 * ta, axis=1)[:, None, :]
    g_col = attn_g.astype(f32).reshape(DV, 1)
    T = B * L
    n_hp = H_A // 2
    full = lambda a: pl.BlockSpec(a.shape, lambda b, p, qt, kt: (0,) * a.ndim)
    grid_spec = pltpu.PrefetchScalarGridSpec(
        num_scalar_prefetch=2, grid=(B, len(pairs)),
        in_specs=[
            pl.BlockSpec((D_QK_ALL, ta), lambda b, p, qt, kt: (0, b * nt + qt[p])),
            pl.BlockSpec((ta, D_QK_ALL), lambda b, p, qt, kt: (b * nt + kt[p], 0)),
            pl.BlockSpec((H_A * V_ROWS, ta), lambda b, p, qt, kt: (0, b * nt + kt[p])),
            full(bias), full(cfar)] + [full(a) for a in lam_args] + [full(g_col)],
        out_specs=pl.BlockSpec((ta, D_ATT), lambda b, p, qt, kt: (b * nt + qt[p], 0)),
        scratch_shapes=[pltpu.VMEM((n_hp, LANES, 4 * ta), bf16), pltpu.VMEM((n_hp, 1, 4 * ta), f32),
                        pltpu.VMEM((H_A, V_ROWS, 2 * ta), f32), pltpu.VMEM((D_ATT, ta), f32)])
    return pl.pallas_call(
        functools.partial(_attn_prompt_kernel, lam_init=lam_init),
        grid_spec=grid_spec, out_shape=jax.ShapeDtypeStruct((T, D_ATT), bf16),
        compiler_params=_params(("parallel", "arbitrary")),
    )(qi_tab, ki_tab, qT, kb, v1T, bias, cfar, *lam_args, g_col)


def _attn_sample_kernel(q_ref, kn_ref, vn_ref, kc_ref, vc_ref, bc_ref, bn_ref,
                        lq1_ref, lk1_ref, lq2_ref, lk2_ref, g2_ref, o_ref, *, lam_init):
    s_len = q_ref.shape[0]
    lane = lax.broadcasted_iota(i32, (s_len, LANES), 1)
    lam = _lambda(lq1_ref, lk1_ref, lq2_ref, lk2_ref, lam_init)
    for hp in range(H_A // 2):
        sl = slice(hp * LANES, (hp + 1) * LANES)
        qt = q_ref[:, sl]
        kct = kc_ref[0, :, sl].astype(bf16)
        vct = vc_ref[0, :, sl].astype(bf16)
        knt = kn_ref[:, sl].astype(bf16)
        vnt = vn_ref[:, sl].astype(bf16)
        outs = []
        for hh in range(2):
            h = 2 * hp + hh
            oc = []
            for c in range(2):
                gi = 2 * hh + c
                qm = jnp.where((lane >= gi * DQK) & (lane < (gi + 1) * DQK), qt, jnp.zeros_like(qt))
                s1 = _dot_nt(qm, kct) + bc_ref[h]
                s2 = _dot_nt(qm, knt) + bn_ref[h]
                m = jnp.maximum(jnp.max(s1, axis=-1, keepdims=True), jnp.max(s2, axis=-1, keepdims=True))
                p1 = jnp.exp(s1 - m)
                p2 = jnp.exp(s2 - m)
                l = jnp.sum(p1, axis=-1, keepdims=True) + jnp.sum(p2, axis=-1, keepdims=True)
                pv = _dot(p1.astype(bf16), vct) + _dot(p2.astype(bf16), vnt)
                oc.append(pv / l)
            o = oc[0] - lam * oc[1]
            in_head = (lane >= hh * DV) & (lane < (hh + 1) * DV)
            ms = jnp.sum(jnp.where(in_head, o * o, 0.0), axis=-1, keepdims=True) * (1.0 / DV)
            outs.append(o * lax.rsqrt(ms + RMS_EPS))
        ot = jnp.where(lane < DV, outs[0], outs[1]) * g2_ref[...] * (1.0 - lam_init)
        o_ref[:, sl] = ot.astype(bf16)


def _attn_sample(q, k_new, v_new, cache_k, cache_v, rel_bias, lam_args, attn_g, lam_init):
    nb, past = cache_k.shape[0], cache_k.shape[1]
    T = q.shape[0]
    s = T // nb
    kc = cache_k.reshape(nb, past, D_QK_ALL)
    vc = cache_v.reshape(nb, past, D_ATT)
    bias = _sample_bias(rel_bias, past, s)
    bc, bn = bias[:, :, :past], bias[:, :, past:]
    g2 = jnp.tile(attn_g.astype(f32), 2).reshape(1, LANES)
    full = lambda a: pl.BlockSpec(a.shape, lambda b: (0,) * a.ndim)
    row = lambda n: pl.BlockSpec((s, n), lambda b: (b, 0))
    return pl.pallas_call(
        functools.partial(_attn_sample_kernel, lam_init=lam_init),
        grid=(nb,),
        in_specs=[row(D_QK_ALL), row(D_QK_ALL), row(D_ATT),
                  pl.BlockSpec((1, past, D_QK_ALL), lambda b: (b, 0, 0)),
                  pl.BlockSpec((1, past, D_ATT), lambda b: (b, 0, 0)),
                  full(bc), full(bn)] + [full(a) for a in lam_args] + [full(g2)],
        out_specs=row(D_ATT), out_shape=jax.ShapeDtypeStruct((T, D_ATT), bf16),
        compiler_params=_params(("parallel",)),
    )(q, k_new, v_new, kc, vc, bc, bn, *lam_args, g2)


def _ssd_kernel(z_ref, xbc_ref, dt_ref, dtT_ref, h0_ref, c0_ref, cw_ref, cb_ref, dtb_ref, dtbT_ref,
                alog_ref, alogT_ref, dsk_ref, g_ref, y_ref, hout_ref, cout_ref, xpad_sc, h_sc):
    q = xbc_ref.shape[0]
    c = pl.program_id(1)
    gw = HEADS_PER_GROUP * SSM_HEADDIM

    @pl.when(c == 0)
    def _():
        xpad_sc[0:SUBLANES] = c0_ref[0]
        h_sc[...] = h0_ref[0]

    xpad_sc[SUBLANES:SUBLANES + q] = xbc_ref[...]
    first = SUBLANES - (CONV_W - 1)
    conv = cb_ref[...]
    for i in range(CONV_W):
        conv = conv + xpad_sc[first + i:first + i + q] * cw_ref[i:i + 1, :]
    tail = xpad_sc[q:q + SUBLANES]
    xpad_sc[0:SUBLANES] = tail
    cout_ref[0] = tail
    act = conv * _sigmoid(conv)
    xs = act[:, :D_SSM]

    lane = lax.broadcasted_iota(i32, (1, LANES), 1)
    a_row = jnp.where(lane < H_S, -jnp.exp(alog_ref[...]), 0.0)
    dt = _softplus(dt_ref[...] + dtb_ref[...])
    ii = lax.broadcasted_iota(i32, (q, q), 0)
    jj = lax.broadcasted_iota(i32, (q, q), 1)
    causal = jj <= ii
    acs = _dot_f32_rhs(causal.astype(bf16), dt * a_row)
    sub = lax.broadcasted_iota(i32, (BF16_ROWS, 1), 0)
    a_col = jnp.where(sub < H_S, -jnp.exp(alogT_ref[...]), 0.0)
    dtT = _softplus(dtT_ref[0] + dtbT_ref[...])
    acsT = _dot_f32_lhs(dtT * a_col, (ii <= jj).astype(bf16))

    er = lax.broadcasted_iota(i32, (LANES, D_SSM), 0)
    ec = lax.broadcasted_iota(i32, (LANES, D_SSM), 1)
    expand = (ec // SSM_HEADDIM == er).astype(bf16)
    dt_x = _dot_f32_lhs(dt, expand)
    acs_x = _dot_f32_lhs(acs, expand)
    e_acs = jnp.exp(acs_x)
    acs_last = acs_x[q - 1:q, :]
    decay = jnp.exp(acs_last - acs_x) * dt_x
    dtx = xs * dt_x
    xd = xs * decay
    glane = lax.broadcasted_iota(i32, (1, gw), 1)

    ys = []
    for g in range(SSM_GROUPS):
        gs = slice(g * gw, (g + 1) * gw)
        bg = act[:, D_SSM + g * D_STATE:D_SSM + (g + 1) * D_STATE].astype(bf16)
        cg = act[:, D_SSM + (SSM_GROUPS + g) * D_STATE:D_SSM + (SSM_GROUPS + g + 1) * D_STATE].astype(bf16)
        cb = _dot_nt(cg, bg)
        h_old = h_sc[g]
        yg = _dot(cg, h_old.astype(bf16)) * e_acs[:, gs]
        dtx_g = dtx[:, gs]
        for e4 in range(HEADS_PER_GROUP):
            e = g * HEADS_PER_GROUP + e4
            seg = acs[:, e:e + 1] - acsT[e:e + 1, :]
            lmat = jnp.exp(jnp.where(causal, seg, -jnp.inf))
            rhs = jnp.where(glane // SSM_HEADDIM == e4, dtx_g, 0.0).astype(bf16)
            yg = yg + _dot((cb * lmat).astype(bf16), rhs)
        h_sc[g] = h_old * jnp.exp(acs_last[:, gs]) + _dot_tn(bg, xd[:, gs].astype(bf16))
        ys.append(yg)
    y = jnp.concatenate(ys, axis=1) + dsk_ref[...] * xs
    zz = z_ref[...]
    y = y * (zz * _sigmoid(zz))
    for g in range(SSM_GROUPS):
        gs = slice(g * gw, (g + 1) * gw)
        yg = y[:, gs]
        r = lax.rsqrt(jnp.mean(yg * yg, axis=-1, keepdims=True) + RMS_EPS)
        y_ref[:, gs] = (yg * r * g_ref[:, gs]).astype(bf16)
    hout_ref[0] = h_sc[...]


def _ssd(z, xbc, dt, dtT, h0, conv0, pw, B, L):
    q = min(SSD_TILE, L)
    nc = L // q
    gw = HEADS_PER_GROUP * SSM_HEADDIM
    dtT3 = dtT.reshape(BF16_ROWS, B, L).transpose(1, 0, 2)
    hT0 = h0.astype(f32).reshape(B, SSM_GROUPS, gw, D_STATE).transpose(0, 1, 3, 2)
    c0 = jnp.pad(conv0.astype(f32), ((0, 0), (SUBLANES - (CONV_W - 1), 0), (0, 0)))
    full = lambda a: pl.BlockSpec(a.shape, lambda b, c: (0,) * a.ndim)
    row = lambda n: pl.BlockSpec((q, n), lambda b, c: (b * nc + c, 0))
    params = [pw['conv_w'], pw['conv_b'], pw['dt_bias'], pw['dt_biasT'], pw['a_log'], pw['a_logT'],
              pw['d_skip'], pw['ssm_g']]
    y, hT, ctail = pl.pallas_call(
        _ssd_kernel, grid=(B, nc),
        in_specs=[row(D_SSM), row(CONV_DIM), row(LANES),
                  pl.BlockSpec((1, BF16_ROWS, q), lambda b, c: (b, 0, c)),
                  pl.BlockSpec((1, SSM_GROUPS, D_STATE, gw), lambda b, c: (b, 0, 0, 0)),
                  pl.BlockSpec((1, SUBLANES, CONV_DIM), lambda b, c: (b, 0, 0))] + [full(a) for a in params],
        out_specs=[row(D_SSM),
                   pl.BlockSpec((1, SSM_GROUPS, D_STATE, gw), lambda b, c: (b, 0, 0, 0)),
                   pl.BlockSpec((1, SUBLANES, CONV_DIM), lambda b, c: (b, 0, 0))],
        out_shape=[jax.ShapeDtypeStruct((B * L, D_SSM), bf16),
                   jax.ShapeDtypeStruct((B, SSM_GROUPS, D_STATE, gw), f32),
                   jax.ShapeDtypeStruct((B, SUBLANES, CONV_DIM), f32)],
        scratch_shapes=[pltpu.VMEM((q + SUBLANES, CONV_DIM), f32), pltpu.VMEM((SSM_GROUPS, D_STATE, gw), f32)],
        compiler_params=_params(("parallel", "arbitrary")),
    )(z, xbc, dt, dtT3, hT0, c0, *params)
    h_new = hT.transpose(0, 1, 3, 2).reshape(B, H_S, SSM_HEADDIM, D_STATE)
    return y, h_new, ctail[:, SUBLANES - (CONV_W - 1):, :]


def _pack_bf16_pairs(x):
    half = x.shape[1] // 2
    bits = lambda a: lax.bitcast_convert_type(a.astype(bf16).astype(f32), jnp.uint32)
    return (bits(x[:, half:]) & jnp.uint32(0xFFFF0000)) | (bits(x[:, :half]) >> 16)


def _unpack_bf16_pairs(word):
    lo = lax.bitcast_convert_type(word << 16, f32)
    hi = lax.bitcast_convert_type(word & jnp.uint32(0xFFFF0000), f32)
    return jnp.concatenate([lo, hi], axis=1)


def _mix_router_kernel(att_ref, ssm_ref, x_ref, woa_ref, wos_ref, g_ref, b_ref, wrh_ref, wrhl_ref, br_ref,
                       x1_ref, x1p_ref, te_ref, gate_ref, cnt_ref, run_sc):
    @pl.when(pl.program_id(0) == 0)
    def _():
        run_sc[...] = jnp.zeros(run_sc.shape, f32)

    mix = _dot(att_ref[...], woa_ref[...]) + _dot(ssm_ref[...], wos_ref[...])
    x1 = _layer_norm(DEEPNORM_ALPHA * x_ref[...] + mix, g_ref[...], b_ref[...])
    x1_ref[...] = x1
    x1p_ref[...] = _pack_bf16_pairs(x1)
    hi = x1.astype(bf16)
    lo = (x1 - hi.astype(f32)).astype(bf16)
    both = _dot(hi, wrhl_ref[...])
    logits = both[:, :LANES] + both[:, LANES:] + _dot(lo, wrh_ref[...]) + br_ref[...]
    lane =lax.broadcasted_iota(i32, logits.shape, 1)
    vals, idxs = [], []
    for _ in range(TOP_K):
        m = jnp.max(logits, axis=-1, keepdims=True)
        idx = jnp.min(jnp.where(logits == m, lane, LANES), axis=-1, keepdims=True)
        vals.append(m)
        idxs.append(idx)
        logits = jnp.where(lane == idx, -jnp.inf, logits)
    es = [jnp.exp(v - vals[0]) for v in vals]
    tot = es[0]
    for e in es[1:]:
        tot = tot + e
    tl = logits.shape[0]
    chosen = jnp.zeros(logits.shape, f32)
    for k in range(TOP_K):
        chosen = chosen + (lane == idxs[k]).astype(f32)
    ii = lax.broadcasted_iota(i32, (tl, tl), 0)
    jj = lax.broadcasted_iota(i32, (tl, tl), 1)
    before = _dot((jj < ii).astype(bf16), chosen.astype(bf16)) + run_sc[...]
    te = jnp.zeros(logits.shape, i32)
    gate = jnp.zeros(logits.shape, f32)
    for k in range(TOP_K):
        rank = jnp.sum(jnp.where(lane == idxs[k], before, 0.0), axis=-1, keepdims=True).astype(i32)
        te = jnp.where(lane == k, idxs[k], te)
        te = jnp.where(lane == TOP_K + k, rank, te)
        gate = jnp.where(lane == k, es[k] / tot, gate)
    te_ref[...] = te
    gate_ref[...] = gate
    run_sc[...] = run_sc[...] + jnp.sum(chosen, axis=0, keepdims=True)
    cnt_ref[...] = run_sc[...]


def _mix_router(att, ssm, x2d, w):
    T, D = x2d.shape
    tl = min(ROW_TILE, T)
    row = lambda n: pl.BlockSpec((tl, n), lambda i: (i, 0))
    full = lambda a: pl.BlockSpec(a.shape, lambda i: (0,) * a.ndim)
    ws = [w['wo_att'], w['wo_ssm'], w['ln1_g'], w['ln1_b'], w['wr_hi'], w['wr_hi_lo'], w['b_router']]
    return pl.pallas_call(
        _mix_router_kernel, grid=(T // tl,),
        in_specs=[row(D_ATT), row(D_SSM), row(D)] + [full(a) for a in ws],
        out_specs=[row(D), row(D // 2), row(LANES), row(LANES), pl.BlockSpec((1, LANES), lambda i: (0, 0))],
        out_shape=[jax.ShapeDtypeStruct((T, D), f32), jax.ShapeDtypeStruct((T, D // 2), jnp.uint32),
                   jax.ShapeDtypeStruct((T, LANES), i32),
                   jax.ShapeDtypeStruct((T, LANES), f32), jax.ShapeDtypeStruct((1, LANES), f32)],
        scratch_shapes=[pltpu.VMEM((1, LANES), f32)],
        compiler_params=_params(("arbitrary",)),
    )(att, ssm, x2d, *ws)


def _route(te, counts_f):
    T = te.shape[0]
    n_assign = T * TOP_K
    counts = counts_f[0, :N_EXPERTS].astype(i32)
    padded = (counts + MOE_BLOCK - 1) // MOE_BLOCK * MOE_BLOCK
    pad_end = jnp.cumsum(padded)
    pad_start = pad_end - padded
    top_e, rank = te[:, :TOP_K], te[:, TOP_K:2 * TOP_K]
    experts = jnp.arange(N_EXPERTS, dtype=i32)
    slot = rank + jnp.sum(jnp.where(top_e[..., None] == experts, pad_start, 0), axis=-1)
    n_blocks = -(-n_assign // MOE_BLOCK) + N_EXPERTS
    n_blocks += n_blocks % FFN_BLOCKS_PER_STEP
    block_start = jnp.arange(n_blocks, dtype=i32) * MOE_BLOCK
    block_e = jnp.minimum(jnp.sum(block_start[:, None] >= pad_end[None, :], axis=-1), N_EXPERTS - 1).astype(i32)
    n_used = (pad_end[-1] // MOE_BLOCK).astype(i32).reshape(1)
    row_end = jnp.sum(jnp.where(block_e[:, None] == experts, pad_start + counts, 0), axis=-1)
    rows_valid = jnp.clip(row_end - block_start, 0, MOE_BLOCK).astype(i32)
    return block_e, n_used, rows_valid, slot.astype(i32), n_blocks


SC_INDEX_WINDOW = 128
SC_ROWS = 32


def _sc_mesh():
    return plsc.VectorSubcoreMesh(core_axis_name="c", subcore_axis_name="s")


def _sc_move_rows(src, src_idx, dst_idx_list, n_out):
    M = src_idx.shape[0]
    D = src.shape[1]
    idx = [a.reshape(1, M) for a in [src_idx] + list(dst_idx_list)]

    @functools.partial(pl.kernel, out_type=jax.ShapeDtypeStruct((n_out, D), src.dtype), mesh=_sc_mesh(),
                       scratch_types=[pltpu.VMEM((2, SC_ROWS, D), src.dtype), pltpu.SemaphoreType.DMA((2,))])
    def move(s_hbm, *rest):
        i_hbm, o_hbm, buf, sem = rest[:-3], rest[-3], rest[-2], rest[-1]
        n_parts = SC_INDEX_WINDOW // SC_ROWS

        def body(si_vmem, *di_vmem):
            def fetch(j):
                part = pl.ds(j * SC_ROWS, SC_ROWS)
                return pltpu.async_copy(s_hbm.at[si_vmem.at[0, part]], buf.at[j % 2], sem.at[j % 2])

            pending = fetch(0)
            for j in range(n_parts):
                nxt = fetch(j + 1) if j + 1 < n_parts else None
                pending.wait()
                part = pl.ds(j * SC_ROWS, SC_ROWS)
                for dv in di_vmem:
                    pltpu.sync_copy(buf.at[j % 2], o_hbm.at[dv.at[0, part]])
                pending = nxt

        pltpu.emit_pipeline(
            body, grid=(M // SC_INDEX_WINDOW,),
            in_specs=[pl.BlockSpec((1, SC_INDEX_WINDOW), lambda i: (0, i))] * len(idx),
            out_specs=[], core_axis_name=("c", "s"), dimension_semantics=(pltpu.PARALLEL,),
        )(*i_hbm)

    return move(src, *idx)


def _sc_scatter_rows(x, idx_k, n_rows):
    return _sc_move_rows(x, jnp.arange(x.shape[0], dtype=i32), idx_k, n_rows)


def _sc_gather_rows(src, idx):
    M = idx.shape[0]
    return _sc_move_rows(src, idx, [jnp.arange(M, dtype=i32)], M)


def _deinterleave_kernel(w_ref, g_ref, u_ref):
    wb = w_ref[0].astype(bf16)
    tn = g_ref.shape[-1]
    r = lax.broadcasted_iota(i32, (2 * tn, tn), 0)
    c = lax.broadcasted_iota(i32, (2 * tn, tn), 1)
    g_ref[0] = _dot(wb, (r == 2 * c).astype(bf16)).astype(bf16)
    u_ref[0] = _dot(wb, (r == 2 * c + 1).astype(bf16)).astype(bf16)


def _deinterleave(w_gu):
    E, D, F2 = w_gu.shape
    tn = 2 * LANES
    out = jax.ShapeDtypeStruct((E, D, F2 // 2), bf16)
    return pl.pallas_call(
        _deinterleave_kernel, grid=(E, F2 // (2 * tn)),
        in_specs=[pl.BlockSpec((1, D, 2 * tn), lambda e, j: (e, 0, j))],
        out_specs=[pl.BlockSpec((1, D, tn), lambda e, j: (e, 0, j))] * 2,
        out_shape=[out, out],
        compiler_params=_params(("parallel", "parallel")),
    )(w_gu)


def _ffn_kernel(be_ref, nu_ref, rv_ref, xs_ref, *refs):
    del be_ref
    y_ref = refs[-1]
    wsets = [refs[6 * j:6 * j + 6] for j in range(FFN_BLOCKS_PER_STEP)]
    first = pl.program_id(0) * FFN_BLOCKS_PER_STEP
    n_live = jnp.clip(nu_ref[0] - first, 0, FFN_BLOCKS_PER_STEP)
    row_id = lax.broadcasted_iota(i32, (MOE_BLOCK, 1), 0)

    def block(j):
        wg_ref, wu_ref, bg_ref, bu_ref, wd_ref, bd_ref = wsets[j]
        rows = pl.ds(j * MOE_BLOCK, MOE_BLOCK)
        words = jnp.where(row_id < rv_ref[first + j], xs_ref[rows, :], jnp.uint32(0))
        xb = _unpack_bf16_pairs(words).astype(bf16)
        g = _dot(xb, wg_ref[0]) + bg_ref[0]
        u = _dot(xb, wu_ref[0]) + bu_ref[0]
        g = jnp.minimum(g, SWIGLU_LIMIT)
        u = jnp.clip(u, -SWIGLU_LIMIT, SWIGLU_LIMIT)
        act = (u + 1.0) * g * _sigmoid(SWIGLU_ALPHA * g)
        y_ref[rows, :] = _pack_bf16_pairs(_dot(act.astype(bf16), wd_ref[0]) + bd_ref[0])

    for live in range(FFN_BLOCKS_PER_STEP + 1):
        @pl.when(n_live == live)
        def _(live=live):
            for j in range(live):
                block(j)
            for j in range(live, FFN_BLOCKS_PER_STEP):
                y_ref[pl.ds(j * MOE_BLOCK, MOE_BLOCK), :] = jnp.zeros((MOE_BLOCK, y_ref.shape[1]), jnp.uint32)


def _expert_ffn(xs, block_e, n_used, rows_valid, w):
    n_rows, half = xs.shape
    D = 2 * half
    nps = FFN_BLOCKS_PER_STEP
    n_steps = n_rows // (MOE_BLOCK * nps)
    F = w['w_g'].shape[2]
    wspecs, wargs = [], []
    for j in range(nps):
        pick = lambda i, be, nu, rv, j=j: (be[i * nps + j], 0, 0)
        wspecs += [pl.BlockSpec((1, D, F), pick), pl.BlockSpec((1, D, F), pick), pl.BlockSpec((1, 1, F), pick),
                   pl.BlockSpec((1, 1, F), pick), pl.BlockSpec((1, F, D), pick), pl.BlockSpec((1, 1, D), pick)]
        wargs += [w['w_g'], w['w_u'], w['b_g'], w['b_u'], w['w_d'], w['b_d']]
    grid_spec = pltpu.PrefetchScalarGridSpec(
        num_scalar_prefetch=3, grid=(n_steps,),
        in_specs=[pl.BlockSpec((nps * MOE_BLOCK, half),
                               lambda i, be, nu, rv: (jnp.minimum(i, (nu[0] - 1) // nps), 0))] + wspecs,
        out_specs=pl.BlockSpec((nps * MOE_BLOCK, half), lambda i, be, nu, rv: (i, 0)))
    return pl.pallas_call(
        _ffn_kernel, grid_spec=grid_spec,
        out_shape=jax.ShapeDtypeStruct((n_rows, half), jnp.uint32),
        compiler_params=_params(("arbitrary",)),
    )(block_e, n_used, rows_valid, xs, *wargs)


def _combine_dense_kernel(rows_ref, x1_ref, gate_ref, g_ref, b_ref, y_ref):
    gate = gate_ref[...]
    ff = gate[:, 0:1] * _unpack_bf16_pairs(rows_ref[0])
    for k in range(1, TOP_K):
        ff = ff + gate[:, k:k + 1] * _unpack_bf16_pairs(rows_ref[k])
    y_ref[...] = _layer_norm(DEEPNORM_ALPHA * x1_ref[...] + ff, g_ref[...], b_ref[...])


def _combine_dense(rows, x1, gate, ln_g, ln_b):
    T, D = x1.shape
    tl = min(ROW_TILE, T)
    row = lambda n: pl.BlockSpec((tl, n), lambda i: (i, 0))
    full = lambda a: pl.BlockSpec(a.shape, lambda i: (0,) * a.ndim)
    return pl.pallas_call(
        _combine_dense_kernel, grid=(T // tl,),
        in_specs=[pl.BlockSpec((TOP_K, tl, D // 2), lambda i: (0, i, 0)), row(D), row(LANES), full(ln_g),
                  full(ln_b)],
        out_specs=row(D), out_shape=jax.ShapeDtypeStruct((T, D), f32),
        compiler_params=_params(("parallel",)),
    )(rows, x1, gate, ln_g, ln_b)


def _prep_weights(l, w_in, conv_w, conv_b, dt_bias, a_log, d_skip, ssm_norm_g, w_o, ln1_g, ln1_b,
                  w_router, b_router, w_gate_up, b_gate_up, w_down, b_down, ln2_g, ln2_b):
    wi = w_in[l]
    c0, c1, c2, c3 = D_QK_ALL, 2 * D_QK_ALL, 2 * D_QK_ALL + D_ATT, 2 * D_QK_ALL + D_ATT + D_SSM
    c4 = c3 + CONV_DIM
    wdt = wi[:, c4:c4 + H_S]
    pad_lane = lambda v, fill=0.0: jnp.pad(v.astype(f32).reshape(1, -1), ((0, 0), (0, LANES - v.shape[-1])),
                                            constant_values=fill)
    pad_col = lambda v: jnp.pad(v.astype(f32).reshape(-1, 1), ((0, BF16_ROWS - v.shape[-1]), (0, 0)))
    wr = jnp.pad(w_router[l].astype(f32), ((0, 0), (0, LANES - N_EXPERTS)))
    wr_hi = wr.astype(bf16)
    wgu = w_gate_up[l]
    return {
        'wq': wi[:, :c0].astype(bf16), 'wqT': wi[:, :c0].T.astype(bf16),
        'wk': wi[:, c0:c1].astype(bf16), 'wv': wi[:, c1:c2].astype(bf16), 'wvT': wi[:, c1:c2].T.astype(bf16),
        'wz': wi[:, c2:c3].astype(bf16), 'wx': wi[:, c3:c4].astype(bf16),
        'wdt': jnp.pad(wdt, ((0, 0), (0, LANES - H_S))).astype(bf16),
        'wdtT': jnp.pad(wdt.T, ((0, BF16_ROWS - H_S), (0, 0))).astype(bf16),
        'conv_w': conv_w[l].astype(f32), 'conv_b': conv_b[l].astype(f32).reshape(1, -1),
        'dt_bias': pad_lane(dt_bias[l]), 'dt_biasT': pad_col(dt_bias[l]),
        'a_log': pad_lane(a_log[l]), 'a_logT': pad_col(a_log[l]),
        'd_skip': jnp.repeat(d_skip[l].astype(f32), SSM_HEADDIM).reshape(1, -1),
        'ssm_g': ssm_norm_g[l].astype(f32).reshape(1, -1),
        'wo_att': w_o[l][:D_ATT].astype(bf16), 'wo_ssm': w_o[l][D_ATT:].astype(bf16),
        'ln1_g': ln1_g[l].astype(f32).reshape(1, -1), 'ln1_b': ln1_b[l].astype(f32).reshape(1, -1),
        'wr_hi': wr_hi, 'wr_hi_lo': jnp.concatenate([wr_hi, (wr - wr_hi.astype(f32)).astype(bf16)], axis=1),
        'b_router': pad_lane(b_router[l], NEG_INF),
        'w_gu': wgu,
        'b_g': b_gate_up[l][:, None, 0::2].astype(f32), 'b_u': b_gate_up[l][:, None, 1::2].astype(f32),
        'w_d': w_down[l].astype(bf16), 'b_d': b_down[l][:, None, :].astype(f32),
        'ln2_g': ln2_g[l].astype(f32).reshape(1, -1), 'ln2_b': ln2_b[l].astype(f32).reshape(1, -1),
    }


def _trunk_layer(x, l, w, rel_bias, lam_args, attn_g, k_past, v_past, h0, conv0):
    B, L, D = x.shape
    x2d = x.reshape(B * L, D)
    lam_init = 0.8 - 0.6 * math.exp(-0.3 * l)
    prompt = k_past is None
    proj = _in_proj(x2d, w, prompt)
    k, v, z, xbc, dt, dtT = proj[:6]
    if prompt:
        kb, qT, v1T = proj[6:]
        att = _attn_prompt(qT, kb, v1T, rel_bias, lam_args, attn_g, B, L, lam_init)
    else:
        att = _attn_sample(proj[6], k, v, k_past, v_past, rel_bias, lam_args, attn_g, lam_init)
    ssm, h_new, conv_new = _ssd(z, xbc, dt, dtT, h0, conv0, w, B, L)
    x1, x1p, te, gate, counts = _mix_router(att, ssm, x2d, w)
    block_e, n_used, rows_valid, slot, n_blocks = _route(te, counts)
    slot_k = slot.T
    xs = _sc_scatter_rows(x1p, [slot_k[k] for k in range(TOP_K)], n_blocks * MOE_BLOCK)
    ys = _expert_ffn(xs, block_e, n_used, rows_valid, w)
    rows = _sc_gather_rows(ys, slot_k.reshape(-1)).reshape(TOP_K, B * L, D // 2)
    y = _combine_dense(rows, x1, gate, w['ln2_g'], w['ln2_b'])
    return (y.reshape(B, L, D), k.reshape(B, L, H_A, 2 * DQK), v.reshape(B, L, H_A, DV), h_new, conv_new)


def kernel(x_prompt, x_sample, cache_k, cache_v, state_ssm, state_conv, rel_bias, w_in, lambda_q1, lambda_k1, lambda_q2, lambda_k2, attn_norm_g, conv_w, conv_b, dt_bias, a_log, d_skip, ssm_norm_g, w_o, ln1_g, ln1_b, w_router, b_router, w_gate_up, b_gate_up, w_down, b_down, ln2_g, ln2_b):
    yp, ys = x_prompt, x_sample
    bp = x_prompt.shape[0]
    depth = w_in.shape[0]
    outs = [[] for _ in range(8)]
    for l in range(depth):
        w = _prep_weights(l, w_in, conv_w, conv_b, dt_bias, a_log, d_skip, ssm_norm_g, w_o, ln1_g, ln1_b,
                          w_router, b_router, w_gate_up, b_gate_up, w_down, b_down, ln2_g, ln2_b)
        w['w_g'], w['w_u'] = _deinterleave(w.pop('w_gu'))
        lam_args = [a[l].astype(f32).reshape(1, -1) for a in (lambda_q1, lambda_k1, lambda_q2, lambda_k2)]
        h0 = jnp.zeros((bp, H_S, SSM_HEADDIM, D_STATE), f32)
        c0 = jnp.zeros((bp, CONV_W - 1, CONV_DIM), f32)
        yp, kp, vp, hp, cp = _trunk_layer(yp, l, w, rel_bias, lam_args, attn_norm_g[l], None, None, h0, c0)
        ys, ks, vs, hs, cs = _trunk_layer(ys, l, w, rel_bias, lam_args, attn_norm_g[l], cache_k[l], cache_v[l],
                                          state_ssm[l], state_conv[l])
        for lst, a in zip(outs, (kp, vp, hp, cp, ks, vs, hs, cs)):
            lst.append(a)
    return (yp, ys) + tuple(jnp.stack(o) for o in outs)
```

```python
import functools
import math

import numpy as np
import jax
import jax.numpy as jnp
from jax import lax
from jax.experimental import pallas as pl
from jax.experimental.pallas import tpu as pltpu
from jax.experimental.pallas import tpu_sc as plsc

f32 = jnp.float32
bf16 = jnp.bfloat16
i32 = jnp.int32

CHUNK = 64
H_A = 8
DQK = 32
DV = 2 * DQK
D_ATT = H_A * DV
SSM_HEADDIM = 64
H_S = 8
D_SSM = H_S * SSM_HEADDIM
SSM_GROUPS = 2
HEADS_PER_GROUP = H_S // SSM_GROUPS
D_STATE = 128
CONV_W = 4
CONV_DIM = D_SSM + 2 * SSM_GROUPS * D_STATE
D_QK_ALL = H_A * 2 * DQK
N_BUCKETS = 32
MAX_DISTANCE = 128
N_EXPERTS = 32
TOP_K = 4
SWIGLU_LIMIT = 7.0
SWIGLU_ALPHA = 1.702
MOE_BLOCK = 256
LN_EPS = 1e-5
RMS_EPS = 1e-5
NEG_INF = -1e30
DEPTH = 1
DEEPNORM_ALPHA = (2.0 * DEPTH) ** 0.25
LOG2E = math.log2(math.e)

LANES = 128
SUBLANES = 8
BF16_ROWS = 16
VMEM_LIMIT = 48 * 1024 * 1024

ROW_TILE = 512
ATT_TILE = 512
ATT_SUB = 256
SSD_TILE = 256
V_ROWS = DV + BF16_ROWS
FFN_BLOCKS_PER_STEP = 2


def _params(semantics):
    return pltpu.CompilerParams(dimension_semantics=semantics, vmem_limit_bytes=VMEM_LIMIT)


def _dot(a, b):
    return jnp.dot(a, b, preferred_element_type=f32)


def _dot_nt(a, b):
    return lax.dot_general(a, b, (((1,), (1,)), ((), ())), preferred_element_type=f32)


def _dot_tn(a, b):
    return lax.dot_general(a, b, (((0,), (0,)), ((), ())), preferred_element_type=f32)


def _split3(a):
    hi = a.astype(bf16)
    r1 = a - hi.astype(f32)
    mid = r1.astype(bf16)
    lo = (r1 - mid.astype(f32)).astype(bf16)
    return hi, mid, lo


def _dot_f32_lhs(a, b_exact):
    hi, mid, lo = _split3(a)
    return _dot(hi, b_exact) + _dot(mid, b_exact) + _dot(lo, b_exact)


def _dot_f32_rhs(a_exact, b):
    hi, mid, lo = _split3(b)
    return _dot(a_exact, hi) + _dot(a_exact, mid) + _dot(a_exact, lo)


def _softplus(x):
    return jnp.maximum(x, 0.0) + jnp.log1p(jnp.exp(-jnp.abs(x)))


def _sigmoid(x):
    return 1.0 / (1.0 + jnp.exp(-x))


def _layer_norm(y, g, b):
    mu = jnp.mean(y, axis=-1, keepdims=True)
    yc = y - mu
    var = jnp.mean(yc * yc, axis=-1, keepdims=True)
    return yc * lax.rsqrt(var + LN_EPS) * g + b


def _lambda(lq1_ref, lk1_ref, lq2_ref, lk2_ref, lam_init):
    s1 = jnp.sum(lq1_ref[...] * lk1_ref[...], axis=-1, keepdims=True)
    s2 = jnp.sum(lq2_ref[...] * lk2_ref[...], axis=-1, keepdims=True)
    return jnp.exp(s1) - jnp.exp(s2) + lam_init


def _in_proj_kernel(x_ref, wk_ref, wv_ref, wz_ref, wx_ref, wdt_ref, wdtT_ref, wq_ref, wkT_ref, wvT_ref, *outs,
                    prompt):
    xb = x_ref[...].astype(bf16)
    z_ref, xbc_ref, dt_ref, dtT_ref = outs[:4]
    k = _dot(xb, wk_ref[...])
    z_ref[...] = _dot(xb, wz_ref[...])
    xbc_ref[...] = _dot(xb, wx_ref[...])
    dt_ref[...] = _dot(xb, wdt_ref[...])
    dtT_ref[...] = _dot_nt(wdtT_ref[...], xb)
    scale = DQK ** -0.5
    if prompt:
        kT_ref, vT_ref, kb_ref, qT_ref, v1T_ref = outs[4:]
        kb_ref[...] = k.astype(bf16)
        kT_ref[0] = _dot_nt(wkT_ref[...], xb)
        qT_ref[...] = (_dot_nt(wq_ref[...], xb) * (scale * LOG2E)).astype(bf16)
        vT32 = _dot_nt(wvT_ref[...], xb)
        vT_ref[0] = vT32
        vT = vT32.astype(bf16)
        ones = jnp.ones((BF16_ROWS, vT.shape[1]), bf16)
        for h in range(H_A):
            v1T_ref[h * V_ROWS:h * V_ROWS + DV, :] = vT[h * DV:(h + 1) * DV, :]
            v1T_ref[h * V_ROWS + DV:(h + 1) * V_ROWS, :] = ones
    else:
        k_ref, v_ref, q_ref = outs[4:]
        k_ref[...] = k
        v_ref[...] = _dot(xb, wv_ref[...])
        q_ref[...] = (_dot(xb, wq_ref[...]) * scale).astype(bf16)


def _in_proj(x2d, w, prompt, B):
    T, D = x2d.shape
    tl = min(ROW_TILE, T)
    L = T // B
    n_l = L // tl
    grid = (T // tl,)
    row = lambda n: pl.BlockSpec((tl, n), lambda i: (i, 0))
    col = lambda n: pl.BlockSpec((n, tl), lambda i: (0, i))
    per_stream = lambda n: pl.BlockSpec((1, n, tl), lambda i: (i // n_l, 0, i % n_l))
    full = lambda a: pl.BlockSpec(a.shape, lambda i: (0,) * a.ndim)
    wq = w['wqT'] if prompt else w['wq']
    ins = [x2d, w['wk'], w['wv'], w['wz'], w['wx'], w['wdt'], w['wdtT'], wq, w['wkT'], w['wvT']]
    in_specs = [row(D)] + [full(a) for a in ins[1:]]
    out_shape = [jax.ShapeDtypeStruct((T, D_SSM), f32), jax.ShapeDtypeStruct((T, CONV_DIM), f32),
                 jax.ShapeDtypeStruct((T, LANES), f32), jax.ShapeDtypeStruct((BF16_ROWS, T), f32)]
    out_specs = [row(D_SSM), row(CONV_DIM), row(LANES), col(BF16_ROWS)]
    if prompt:
        assert L % tl == 0
        out_shape += [jax.ShapeDtypeStruct((B, D_QK_ALL, L), f32), jax.ShapeDtypeStruct((B, D_ATT, L), f32),
                      jax.ShapeDtypeStruct((T, D_QK_ALL), bf16), jax.ShapeDtypeStruct((D_QK_ALL, T), bf16),
                      jax.ShapeDtypeStruct((H_A * V_ROWS, T), bf16)]
        out_specs += [per_stream(D_QK_ALL), per_stream(D_ATT), row(D_QK_ALL), col(D_QK_ALL), col(H_A * V_ROWS)]
    else:
        out_shape += [jax.ShapeDtypeStruct((T, D_QK_ALL), f32), jax.ShapeDtypeStruct((T, D_ATT), f32),
                      jax.ShapeDtypeStruct((T, D_QK_ALL), bf16)]
        out_specs += [row(D_QK_ALL), row(D_ATT), row(D_QK_ALL)]
    return pl.pallas_call(
        functools.partial(_in_proj_kernel, prompt=prompt),
        grid=grid, in_specs=in_specs, out_specs=out_specs, out_shape=out_shape,
        compiler_params=_params(("parallel",)),
    )(*ins)


def _t5_bucket(rel):
    half = N_BUCKETS // 2
    max_exact = half // 2
    n = jnp.abs(rel)
    large = max_exact + (jnp.log(jnp.maximum(n, 1).astype(f32) / max_exact)
                         / math.log(MAX_DISTANCE / max_exact) * (half - max_exact)).astype(i32)
    large = jnp.minimum(large, half - 1)
    return jnp.where(rel > 0, half, 0) + jnp.where(n < max_exact, n, large)


def _far_bucket(min_dist):
    half = N_BUCKETS // 2
    max_exact = half // 2
    v = max_exact + int(math.log(min_dist / max_exact) / math.log(MAX_DISTANCE / max_exact) * (half - max_exact))
    return half - 1 if (min_dist >= max_exact and v - 1 >= half - 1) else None


def _bias_lookup(rel_bias, bucket):
    onehot = (bucket[..., None] == jnp.arange(N_BUCKETS, dtype=i32)).astype(f32)
    return jnp.dot(onehot, rel_bias.astype(f32), precision=lax.Precision.HIGHEST)


def _prompt_bias_tiles(rel_bias, ta):
    j = jnp.arange(ta, dtype=i32)[:, None]
    i = jnp.arange(ta, dtype=i32)[None, :]
    tiles = []
    for d in range(2):
        rel = j - i - d * ta
        b = _bias_lookup(rel_bias, _t5_bucket(rel))
        allowed = ((j // CHUNK) <= (i // CHUNK)) if d == 0 else jnp.ones((ta, ta), bool)
        tiles.append(jnp.where(allowed[..., None], b, NEG_INF))
    return jnp.transpose(jnp.stack(tiles), (3, 0, 1, 2))


def _sample_bias(rel_bias, past, s):
    q_pos = past + jnp.arange(s, dtype=i32)
    k_pos = jnp.arange(past + s, dtype=i32)
    rel = k_pos[None, :] - q_pos[:, None]
    b = _bias_lookup(rel_bias, _t5_bucket(rel))
    allowed = (k_pos[None, :] // CHUNK) <= (q_pos[:, None] // CHUNK)
    return jnp.transpose(jnp.where(allowed[..., None], b, NEG_INF), (2, 0, 1))


def _attn_prompt_kernel(qi_tab, ki_tab, qT_ref, k_ref, v1T_ref, bias_ref, cfar_ref,
                        lq1_ref, lk1_ref, lq2_ref, lk2_ref, g_ref, o_ref,
                        qm_sc, m_sc, acc_sc, oT_sc, *, lam_init):
    p = pl.program_id(1)
    qi = qi_tab[p]
    ki = ki_tab[p]
    d = qi - ki
    ta = k_ref.shape[0]
    n_hp = H_A // 2

    @pl.when(ki == 0)
    def _():
        m_sc[...] = jnp.full(m_sc.shape, NEG_INF, f32)
        acc_sc[...] = jnp.zeros(acc_sc.shape, f32)
        row = lax.broadcasted_iota(i32, (LANES, ta), 0)
        for hp in range(n_hp):
            qt = qT_ref[hp * LANES:(hp + 1) * LANES, :]
            for gi in range(4):
                keep = (row >= gi * DQK) & (row < (gi + 1) * DQK)
                qm_sc[hp, :, gi * ta:(gi + 1) * ta] = jnp.where(keep, qt, jnp.zeros_like(qt))

    def scores(hp):
        return _dot(k_ref[:, hp * LANES:(hp + 1) * LANES], qm_sc[hp])

    nb = ta // ATT_SUB

    def near_bias(hp, step_d):
        rows = []
        for jb in range(nb):
            cols = []
            for g in range(4):
                for ib in range(nb):
                    dd = step_d * nb + ib - jb
                    if dd < 0:
                        blk = jnp.full((ATT_SUB, ATT_SUB), NEG_INF * LOG2E, f32)
                    elif dd < 2:
                        blk = bias_ref[2 * hp + g // 2, dd]
                    else:
                        lo = g * ta + ib * ATT_SUB
                        blk = jnp.broadcast_to(cfar_ref[hp][:, lo:lo + ATT_SUB], (ATT_SUB, ATT_SUB))
                    cols.append(blk)
            rows.append(jnp.concatenate(cols, axis=1))
        return jnp.concatenate(rows, axis=0)

    def update(step_d):
        s_next = scores(0)
        for hp in range(n_hp):
            s = s_next
            if hp + 1 < n_hp:
                s_next = scores(hp + 1)
            m_old = m_sc[hp]
            if step_d is not None:
                s = s + near_bias(hp, step_d)
                m_new = jnp.maximum(m_old, jnp.max(s, axis=0, keepdims=True))
                shift = m_new
            else:
                c = cfar_ref[hp]
                m_new = jnp.maximum(m_old, jnp.max(s, axis=0, keepdims=True) + c)
                shift = m_new - c
            alpha = jnp.exp2(m_old - m_new)
            pT = jnp.exp2(s - shift).astype(bf16)
            for hh in range(2):
                h = 2 * hp + hh
                cols = slice(hh * 2 * ta, (hh + 1) * 2 * ta)
                pv = _dot(v1T_ref[h * V_ROWS:(h + 1) * V_ROWS, :], pT[:, cols])
                acc_sc[h] = alpha[:, cols] * acc_sc[h] + pv
            m_sc[hp] = m_new

    @pl.when(d == 0)
    def _():
        update(0)

    @pl.when(d == 1)
    def _():
        update(1)

    @pl.when(d >= 2)
    def _():
        update(None)

    @pl.when(d == 0)
    def _():
        lam = _lambda(lq1_ref, lk1_ref, lq2_ref, lk2_ref, lam_init)
        for h in range(H_A):
            a = acc_sc[h]
            a0 = a[:, :ta]
            a1 = a[:, ta:]
            o = a0[:DV] / a0[DV:DV + 1] - lam * (a1[:DV] / a1[DV:DV + 1])
            o = o * lax.rsqrt(jnp.mean(o * o, axis=0, keepdims=True) + RMS_EPS) * g_ref[...] * (1.0 - lam_init)
            oT_sc[h * DV:(h + 1) * DV, :] = o
        o_ref[...] = oT_sc[...].T.astype(bf16)


def _attn_prompt(qT, kb, v1T, rel_bias, lam_args, attn_g, B, L, lam_init):
    ta = min(ATT_TILE, L)
    assert L % ta == 0 and ta % ATT_SUB == 0 and ATT_SUB % CHUNK == 0
    nt = L // ta
    far = _far_bucket(ATT_SUB + 1)
    assert L <= 2 * ATT_SUB or far is not None, "key blocks two or more behind must share one bias bucket"
    far = far if far is not None else N_BUCKETS // 2 - 1
    pairs = [(q, k) for q in range(nt) for k in range(q + 1)]
    qi_tab = jnp.asarray([q for q, _ in pairs], i32)
    ki_tab = jnp.asarray([k for _, k in pairs], i32)
    bias = _prompt_bias_tiles(rel_bias, ATT_SUB) * LOG2E
    cfar = jnp.repeat((rel_bias.astype(f32)[far] * LOG2E).reshape(H_A // 2, 2), 2 * ta, axis=1)[:, None, :]
    g_col = attn_g.astype(f32).reshape(DV, 1)
    T = B * L
    n_hp = H_A // 2
    full = lambda a: pl.BlockSpec(a.shape, lambda b, p, qt, kt: (0,) * a.ndim)
    grid_spec = pltpu.PrefetchScalarGridSpec(
        num_scalar_prefetch=2, grid=(B, len(pairs)),
        in_specs=[
            pl.BlockSpec((D_QK_ALL, ta), lambda b, p, qt, kt: (0, b * nt + qt[p])),
            pl.BlockSpec((ta, D_QK_ALL), lambda b, p, qt, kt: (b * nt + kt[p], 0)),
            pl.BlockSpec((H_A * V_ROWS, ta), lambda b, p, qt, kt: (0, b * nt + kt[p])),
            full(bias), full(cfar)] + [full(a) for a in lam_args] + [full(g_col)],
        out_specs=pl.BlockSpec((ta, D_ATT), lambda b, p, qt, kt: (b * nt + qt[p], 0)),
        scratch_shapes=[pltpu.VMEM((n_hp, LANES, 4 * ta), bf16), pltpu.VMEM((n_hp, 1, 4 * ta), f32),
                        pltpu.VMEM((H_A, V_ROWS, 2 * ta), f32), pltpu.VMEM((D_ATT, ta), f32)])
    return pl.pallas_call(
        functools.partial(_attn_prompt_kernel, lam_init=lam_init),
        grid_spec=grid_spec, out_shape=jax.ShapeDtypeStruct((T, D_ATT), bf16),
        compiler_params=_params(("parallel", "arbitrary")),
    )(qi_tab, ki_tab, qT, kb, v1T, bias, cfar, *lam_args, g_col)


def _attn_sample_kernel(q_ref, kn_ref, vn_ref, kc_ref, vc_ref, bc_ref, bn_ref,
                        lq1_ref, lk1_ref, lq2_ref, lk2_ref, g2_ref, o_ref, *, lam_init):
    s_len = q_ref.shape[0]
    lane = lax.broadcasted_iota(i32, (s_len, LANES), 1)
    lam = _lambda(lq1_ref, lk1_ref, lq2_ref, lk2_ref, lam_init)
    for hp in range(H_A // 2):
        sl = slice(hp * LANES, (hp + 1) * LANES)
        qt = q_ref[:, sl]
        kct = kc_ref[0, :, sl].astype(bf16)
        vct = vc_ref[0, :, sl].astype(bf16)
        knt = kn_ref[:, sl].astype(bf16)
        vnt = vn_ref[:, sl].astype(bf16)
        outs = []
        for hh in range(2):
            h = 2 * hp + hh
            oc = []
            for c in range(2):
                gi = 2 * hh + c
                qm = jnp.where((lane >= gi * DQK) & (lane < (gi + 1) * DQK), qt, jnp.zeros_like(qt))
                s1 = _dot_nt(qm, kct) + bc_ref[h]
                s2 = _dot_nt(qm, knt) + bn_ref[h]
                m = jnp.maximum(jnp.max(s1, axis=-1, keepdims=True), jnp.max(s2, axis=-1, keepdims=True))
                p1 = jnp.exp(s1 - m)
                p2 = jnp.exp(s2 - m)
                l = jnp.sum(p1, axis=-1, keepdims=True) + jnp.sum(p2, axis=-1, keepdims=True)
                pv = _dot(p1.astype(bf16), vct) + _dot(p2.astype(bf16), vnt)
                oc.append(pv / l)
            o = oc[0] - lam * oc[1]
            in_head = (lane >= hh * DV) & (lane < (hh + 1) * DV)
            ms = jnp.sum(jnp.where(in_head, o * o, 0.0), axis=-1, keepdims=True) * (1.0 / DV)
            outs.append(o * lax.rsqrt(ms + RMS_EPS))
        ot = jnp.where(lane < DV, outs[0], outs[1]) * g2_ref[...] * (1.0 - lam_init)
        o_ref[:, sl] = ot.astype(bf16)


def _attn_sample(q, k_new, v_new, cache_k, cache_v, rel_bias, lam_args, attn_g, lam_init):
    nb, past = cache_k.shape[0], cache_k.shape[1]
    T = q.shape[0]
    s = T // nb
    kc = cache_k.reshape(nb, past, D_QK_ALL)
    vc = cache_v.reshape(nb, past, D_ATT)
    bias = _sample_bias(rel_bias, past, s)
    bc, bn = bias[:, :, :past], bias[:, :, past:]
    g2 = jnp.tile(attn_g.astype(f32), 2).reshape(1, LANES)
    full = lambda a: pl.BlockSpec(a.shape, lambda b: (0,) * a.ndim)
    row = lambda n: pl.BlockSpec((s, n), lambda b: (b, 0))
    return pl.pallas_call(
        functools.partial(_attn_sample_kernel, lam_init=lam_init),
        grid=(nb,),
        in_specs=[row(D_QK_ALL), row(D_QK_ALL), row(D_ATT),
                  pl.BlockSpec((1, past, D_QK_ALL), lambda b: (b, 0, 0)),
                  pl.BlockSpec((1, past, D_ATT), lambda b: (b, 0, 0)),
                  full(bc), full(bn)] + [full(a) for a in lam_args] + [full(g2)],
        out_specs=row(D_ATT), out_shape=jax.ShapeDtypeStruct((T, D_ATT), bf16),
        compiler_params=_params(("parallel",)),
    )(q, k_new, v_new, kc, vc, bc, bn, *lam_args, g2)


def _ssd_kernel(z_ref, xbc_ref, dt_ref, dtT_ref, h0_ref, c0_ref, cw_ref, cb_ref, dtb_ref, dtbT_ref,
                alog_ref, alogT_ref, dsk_ref, g_ref, y_ref, hout_ref, cout_ref, xpad_sc, h_sc):
    q = xbc_ref.shape[0]
    c = pl.program_id(1)
    gw = HEADS_PER_GROUP * SSM_HEADDIM

    @pl.when(c == 0)
    def _():
        xpad_sc[0:SUBLANES] = c0_ref[0]
        h_sc[...] = h0_ref[0]

    xpad_sc[SUBLANES:SUBLANES + q] = xbc_ref[...]
    first = SUBLANES - (CONV_W - 1)
    conv = cb_ref[...]
    for i in range(CONV_W):
        conv = conv + xpad_sc[first + i:first + i + q] * cw_ref[i:i + 1, :]
    tail = xpad_sc[q:q + SUBLANES]
    xpad_sc[0:SUBLANES] = tail
    cout_ref[0] = tail
    act = conv * _sigmoid(conv)
    xs = act[:, :D_SSM]

    lane = lax.broadcasted_iota(i32, (1, LANES), 1)
    a_row = jnp.where(lane < H_S, -jnp.exp(alog_ref[...]), 0.0)
    dt = _softplus(dt_ref[...] + dtb_ref[...])
    ii = lax.broadcasted_iota(i32, (q, q), 0)
    jj = lax.broadcasted_iota(i32, (q, q), 1)
    causal = jj <= ii
    acs = _dot_f32_rhs(causal.astype(bf16), dt * a_row)
    sub = lax.broadcasted_iota(i32, (BF16_ROWS, 1), 0)
    a_col = jnp.where(sub < H_S, -jnp.exp(alogT_ref[...]), 0.0)
    dtT = _softplus(dtT_ref[0] + dtbT_ref[...])
    acsT = _dot_f32_lhs(dtT * a_col, (ii <= jj).astype(bf16))

    er = lax.broadcasted_iota(i32, (LANES, D_SSM), 0)
    ec = lax.broadcasted_iota(i32, (LANES, D_SSM), 1)
    expand = (ec // SSM_HEADDIM == er).astype(bf16)
    dt_x = _dot_f32_lhs(dt, expand)
    acs_x = _dot_f32_lhs(acs, expand)
    e_acs = jnp.exp(acs_x)
    acs_last = acs_x[q - 1:q, :]
    decay = jnp.exp(acs_last - acs_x) * dt_x
    dtx = xs * dt_x
    xd = xs * decay
    glane = lax.broadcasted_iota(i32, (1, gw), 1)

    ys = []
    for g in range(SSM_GROUPS):
        gs = slice(g * gw, (g + 1) * gw)
        bg = act[:, D_SSM + g * D_STATE:D_SSM + (g + 1) * D_STATE].astype(bf16)
        cg = act[:, D_SSM + (SSM_GROUPS + g) * D_STATE:D_SSM + (SSM_GROUPS + g + 1) * D_STATE].astype(bf16)
        cb = _dot_nt(cg, bg)
        h_old = h_sc[g]
        yg = _dot(cg, h_old.astype(bf16)) * e_acs[:, gs]
        dtx_g = dtx[:, gs]
        for e4 in range(HEADS_PER_GROUP):
            e = g * HEADS_PER_GROUP + e4
            seg = acs[:, e:e + 1] - acsT[e:e + 1, :]
            lmat = jnp.exp(jnp.where(causal, seg, -jnp.inf))
            rhs = jnp.where(glane // SSM_HEADDIM == e4, dtx_g, 0.0).astype(bf16)
            yg = yg + _dot((cb * lmat).astype(bf16), rhs)
        h_sc[g] = h_old * jnp.exp(acs_last[:, gs]) + _dot_tn(bg, xd[:, gs].astype(bf16))
        ys.append(yg)
    y = jnp.concatenate(ys, axis=1) + dsk_ref[...] * xs
    zz = z_ref[...]
    y = y * (zz * _sigmoid(zz))
    for g in range(SSM_GROUPS):
        gs = slice(g * gw, (g + 1) * gw)
        yg = y[:, gs]
        r = lax.rsqrt(jnp.mean(yg * yg, axis=-1, keepdims=True) + RMS_EPS)
        y_ref[:, gs] = (yg * r * g_ref[:, gs]).astype(bf16)
    hout_ref[0] = h_sc[...]


def _ssd(z, xbc, dt, dtT, h0, conv0, pw, B, L):
    q = min(SSD_TILE, L)
    nc = L // q
    gw = HEADS_PER_GROUP * SSM_HEADDIM
    dtT3 = dtT.reshape(BF16_ROWS, B, L).transpose(1, 0, 2)
    hT0 = h0.astype(f32).reshape(B, SSM_GROUPS, gw, D_STATE).transpose(0, 1, 3, 2)
    c0 = jnp.pad(conv0.astype(f32), ((0, 0), (SUBLANES - (CONV_W - 1), 0), (0, 0)))
    full = lambda a: pl.BlockSpec(a.shape, lambda b, c: (0,) * a.ndim)
    row = lambda n: pl.BlockSpec((q, n), lambda b, c: (b * nc + c, 0))
    params = [pw['conv_w'], pw['conv_b'], pw['dt_bias'], pw['dt_biasT'], pw['a_log'], pw['a_logT'],
              pw['d_skip'], pw['ssm_g']]
    y, hT, ctail = pl.pallas_call(
        _ssd_kernel, grid=(B, nc),
        in_specs=[row(D_SSM), row(CONV_DIM), row(LANES),
                  pl.BlockSpec((1, BF16_ROWS, q), lambda b, c: (b, 0, c)),
                  pl.BlockSpec((1, SSM_GROUPS, D_STATE, gw), lambda b, c: (b, 0, 0, 0)),
                  pl.BlockSpec((1, SUBLANES, CONV_DIM), lambda b, c: (b, 0, 0))] + [full(a) for a in params],
        out_specs=[row(D_SSM),
                   pl.BlockSpec((1, SSM_GROUPS, D_STATE, gw), lambda b, c: (b, 0, 0, 0)),
                   pl.BlockSpec((1, SUBLANES, CONV_DIM), lambda b, c: (b, 0, 0))],
        out_shape=[jax.ShapeDtypeStruct((B * L, D_SSM), bf16),
                   jax.ShapeDtypeStruct((B, SSM_GROUPS, D_STATE, gw), f32),
                   jax.ShapeDtypeStruct((B, SUBLANES, CONV_DIM), f32)],
        scratch_shapes=[pltpu.VMEM((q + SUBLANES, CONV_DIM), f32), pltpu.VMEM((SSM_GROUPS, D_STATE, gw), f32)],
        compiler_params=_params(("parallel", "arbitrary")),
    )(z, xbc, dt, dtT3, hT0, c0, *params)
    h_new = hT.transpose(0, 1, 3, 2).reshape(B, H_S, SSM_HEADDIM, D_STATE)
    return y, h_new, ctail[:, SUBLANES - (CONV_W - 1):, :]


def _pack_bf16_pairs(x):
    half = x.shape[1] // 2
    bits = lambda a: lax.bitcast_convert_type(a.astype(bf16).astype(f32), jnp.uint32)
    return (bits(x[:, half:]) & jnp.uint32(0xFFFF0000)) | (bits(x[:, :half]) >> 16)


def _unpack_bf16_pairs(word):
    lo = lax.bitcast_convert_type(word << 16, f32)
    hi = lax.bitcast_convert_type(word & jnp.uint32(0xFFFF0000), f32)
    return jnp.concatenate([lo, hi], axis=1)


def _mix_router_kernel(att_ref, ssm_ref, x_ref, woa_ref, wos_ref, g_ref, b_ref, wrh_ref, wrhl_ref, br_ref,
                       x1_ref, x1p_ref, te_ref, gate_ref, cnt_ref, run_sc):
    @pl.when(pl.program_id(0) == 0)
    def _():
        run_sc[...] = jnp.zeros(run_sc.shape, f32)

    mix = _dot(att_ref[...], woa_ref[...]) + _dot(ssm_ref[...], wos_ref[...])
    x1 = _layer_norm(DEEPNORM_ALPHA * x_ref[...] + mix, g_ref[...], b_ref[...])
    x1_ref[...] = x1
    x1p_ref[...] = _pack_bf16_pairs(x1)
    hi = x1.astype(bf16)
    lo = (x1 - hi.astype(f32)).astype(bf16)
    both = _dot(hi, wrhl_ref[...])
    logits = both[:, :LANES] + both[:, LANES:] + _dot(lo, wrh_ref[...]) + br_ref[...]
    lane =lax.broadcasted_iota(i32, logits.shape, 1)
    vals, idxs = [], []
    for _ in range(TOP_K):
        m = jnp.max(logits, axis=-1, keepdims=True)
        idx = jnp.min(jnp.where(logits == m, lane, LANES), axis=-1, keepdims=True)
        vals.append(m)
        idxs.append(idx)
        logits = jnp.where(lane == idx, -jnp.inf, logits)
    es = [jnp.exp(v - vals[0]) for v in vals]
    tot = es[0]
    for e in es[1:]:
        tot = tot + e
    tl = logits.shape[0]
    chosen = jnp.zeros(logits.shape, f32)
    for k in range(TOP_K):
        chosen = chosen + (lane == idxs[k]).astype(f32)
    ii = lax.broadcasted_iota(i32, (tl, tl), 0)
    jj = lax.broadcasted_iota(i32, (tl, tl), 1)
    before = _dot((jj < ii).astype(bf16), chosen.astype(bf16)) + run_sc[...]
    te = jnp.zeros(logits.shape, i32)
    gate = jnp.zeros(logits.shape, f32)
    for k in range(TOP_K):
        rank = jnp.sum(jnp.where(lane == idxs[k], before, 0.0), axis=-1, keepdims=True).astype(i32)
        te = jnp.where(lane == k, idxs[k], te)
        te = jnp.where(lane == TOP_K + k, rank, te)
        gate = jnp.where(lane == k, es[k] / tot, gate)
    te_ref[...] = te
    gate_ref[...] = gate
    run_sc[...] = run_sc[...] + jnp.sum(chosen, axis=0, keepdims=True)
    cnt_ref[...] = run_sc[...]


def _mix_router(att, ssm, x2d, w):
    T, D = x2d.shape
    tl = min(ROW_TILE, T)
    row = lambda n: pl.BlockSpec((tl, n), lambda i: (i, 0))
    full = lambda a: pl.BlockSpec(a.shape, lambda i: (0,) * a.ndim)
    ws = [w['wo_att'], w['wo_ssm'], w['ln1_g'], w['ln1_b'], w['wr_hi'], w['wr_hi_lo'], w['b_router']]
    return pl.pallas_call(
        _mix_router_kernel, grid=(T // tl,),
        in_specs=[row(D_ATT), row(D_SSM), row(D)] + [full(a) for a in ws],
        out_specs=[row(D), row(D // 2), row(LANES), row(LANES), pl.BlockSpec((1, LANES), lambda i: (0, 0))],
        out_shape=[jax.ShapeDtypeStruct((T, D), f32), jax.ShapeDtypeStruct((T, D // 2), jnp.uint32),
                   jax.ShapeDtypeStruct((T, LANES), i32),
                   jax.ShapeDtypeStruct((T, LANES), f32), jax.ShapeDtypeStruct((1, LANES), f32)],
        scratch_shapes=[pltpu.VMEM((1, LANES), f32)],
        compiler_params=_params(("arbitrary",)),
    )(att, ssm, x2d, *ws)


def _route(te, counts_f):
    T = te.shape[0]
    n_assign = T * TOP_K
    counts = counts_f[0, :N_EXPERTS].astype(i32)
    padded = (counts + MOE_BLOCK - 1) // MOE_BLOCK * MOE_BLOCK
    pad_end = jnp.cumsum(padded)
    pad_start = pad_end - padded
    top_e, rank = te[:, :TOP_K], te[:, TOP_K:2 * TOP_K]
    experts = jnp.arange(N_EXPERTS, dtype=i32)
    slot = rank + jnp.sum(jnp.where(top_e[..., None] == experts, pad_start, 0), axis=-1)
    n_blocks = -(-n_assign // MOE_BLOCK) + N_EXPERTS
    n_blocks += n_blocks % FFN_BLOCKS_PER_STEP
    block_start = jnp.arange(n_blocks, dtype=i32) * MOE_BLOCK
    block_e = jnp.minimum(jnp.sum(block_start[:, None] >= pad_end[None, :], axis=-1), N_EXPERTS - 1).astype(i32)
    n_used = (pad_end[-1] // MOE_BLOCK).astype(i32).reshape(1)
    row_end = jnp.sum(jnp.where(block_e[:, None] == experts, pad_start + counts, 0), axis=-1)
    rows_valid = jnp.clip(row_end - block_start, 0, MOE_BLOCK).astype(i32)
    return block_e, n_used, rows_valid, slot.astype(i32), n_blocks


SC_INDEX_WINDOW = 128
SC_ROWS = 32


def _sc_mesh():
    return plsc.VectorSubcoreMesh(core_axis_name="c", subcore_axis_name="s")


def _sc_move_rows(src, src_idx, dst_idx_list, n_out):
    M = src_idx.shape[0]
    D = src.shape[1]
    idx = [a.reshape(1, M) for a in [src_idx] + list(dst_idx_list)]

    @functools.partial(pl.kernel, out_type=jax.ShapeDtypeStruct((n_out, D), src.dtype), mesh=_sc_mesh(),
                       scratch_types=[pltpu.VMEM((2, SC_ROWS, D), src.dtype), pltpu.SemaphoreType.DMA((2,))])
    def move(s_hbm, *rest):
        i_hbm, o_hbm, buf, sem = rest[:-3], rest[-3], rest[-2], rest[-1]
        n_parts = SC_INDEX_WINDOW // SC_ROWS

        def body(si_vmem, *di_vmem):
            def fetch(j):
                part = pl.ds(j * SC_ROWS, SC_ROWS)
                return pltpu.async_copy(s_hbm.at[si_vmem.at[0, part]], buf.at[j % 2], sem.at[j % 2])

            pending = fetch(0)
            for j in range(n_parts):
                nxt = fetch(j + 1) if j + 1 < n_parts else None
                pending.wait()
                part = pl.ds(j * SC_ROWS, SC_ROWS)
                for dv in di_vmem:
                    pltpu.sync_copy(buf.at[j % 2], o_hbm.at[dv.at[0, part]])
                pending = nxt

        pltpu.emit_pipeline(
            body, grid=(M // SC_INDEX_WINDOW,),
            in_specs=[pl.BlockSpec((1, SC_INDEX_WINDOW), lambda i: (0, i))] * len(idx),
            out_specs=[], core_axis_name=("c", "s"), dimension_semantics=(pltpu.PARALLEL,),
        )(*i_hbm)

    return move(src, *idx)


def _sc_scatter_rows(x, idx_k, n_rows):
    return _sc_move_rows(x, jnp.arange(x.shape[0], dtype=i32), idx_k, n_rows)


def _sc_gather_rows(src, idx):
    M = idx.shape[0]
    return _sc_move_rows(src, idx, [jnp.arange(M, dtype=i32)], M)


def _deinterleave_kernel(w_ref, g_ref, u_ref):
    wb = w_ref[0].astype(bf16)
    tn = g_ref.shape[-1]
    r = lax.broadcasted_iota(i32, (2 * tn, tn), 0)
    c = lax.broadcasted_iota(i32, (2 * tn, tn), 1)
    g_ref[0] = _dot(wb, (r == 2 * c).astype(bf16)).astype(bf16)
    u_ref[0] = _dot(wb, (r == 2 * c + 1).astype(bf16)).astype(bf16)


def _deinterleave(w_gu):
    E, D, F2 = w_gu.shape
    tn = 2 * LANES
    out = jax.ShapeDtypeStruct((E, D, F2 // 2), bf16)
    return pl.pallas_call(
        _deinterleave_kernel, grid=(E, F2 // (2 * tn)),
        in_specs=[pl.BlockSpec((1, D, 2 * tn), lambda e, j: (e, 0, j))],
        out_specs=[pl.BlockSpec((1, D, tn), lambda e, j: (e, 0, j))] * 2,
        out_shape=[out, out],
        compiler_params=_params(("parallel", "parallel")),
    )(w_gu)


def _ffn_kernel(be_ref, nu_ref, rv_ref, xs_ref, *refs):
    del be_ref
    y_ref = refs[-1]
    wsets = [refs[6 * j:6 * j + 6] for j in range(FFN_BLOCKS_PER_STEP)]
    first = pl.program_id(0) * FFN_BLOCKS_PER_STEP
    n_live = jnp.clip(nu_ref[0] - first, 0, FFN_BLOCKS_PER_STEP)
    row_id = lax.broadcasted_iota(i32, (MOE_BLOCK, 1), 0)

    def block(j):
        wg_ref, wu_ref, bg_ref, bu_ref, wd_ref, bd_ref = wsets[j]
        rows = pl.ds(j * MOE_BLOCK, MOE_BLOCK)
        words = jnp.where(row_id < rv_ref[first + j], xs_ref[rows, :], jnp.uint32(0))
        xb = _unpack_bf16_pairs(words).astype(bf16)
        g = _dot(xb, wg_ref[0]) + bg_ref[0]
        u = _dot(xb, wu_ref[0]) + bu_ref[0]
        g = jnp.minimum(g, SWIGLU_LIMIT)
        u = jnp.clip(u, -SWIGLU_LIMIT, SWIGLU_LIMIT)
        act = (u + 1.0) * g * _sigmoid(SWIGLU_ALPHA * g)
        y_ref[rows, :] = _pack_bf16_pairs(_dot(act.astype(bf16), wd_ref[0]) + bd_ref[0])

    for live in range(FFN_BLOCKS_PER_STEP + 1):
        @pl.when(n_live == live)
        def _(live=live):
            for j in range(live):
                block(j)
            for j in range(live, FFN_BLOCKS_PER_STEP):
                y_ref[pl.ds(j * MOE_BLOCK, MOE_BLOCK), :] = jnp.zeros((MOE_BLOCK, y_ref.shape[1]), jnp.uint32)


def _expert_ffn(xs, block_e, n_used, rows_valid, w):
    n_rows, half = xs.shape
    D = 2 * half
    nps = FFN_BLOCKS_PER_STEP
    n_steps = n_rows // (MOE_BLOCK * nps)
    F = w['w_g'].shape[2]
    wspecs, wargs = [], []
    for j in range(nps):
        pick = lambda i, be, nu, rv, j=j: (be[i * nps + j], 0, 0)
        wspecs += [pl.BlockSpec((1, D, F), pick), pl.BlockSpec((1, D, F), pick), pl.BlockSpec((1, 1, F), pick),
                   pl.BlockSpec((1, 1, F), pick), pl.BlockSpec((1, F, D), pick), pl.BlockSpec((1, 1, D), pick)]
        wargs += [w['w_g'], w['w_u'], w['b_g'], w['b_u'], w['w_d'], w['b_d']]
    grid_spec = pltpu.PrefetchScalarGridSpec(
        num_scalar_prefetch=3, grid=(n_steps,),
        in_specs=[pl.BlockSpec((nps * MOE_BLOCK, half),
                               lambda i, be, nu, rv: (jnp.minimum(i, (nu[0] - 1) // nps), 0))] + wspecs,
        out_specs=pl.BlockSpec((nps * MOE_BLOCK, half), lambda i, be, nu, rv: (i, 0)))
    return pl.pallas_call(
        _ffn_kernel, grid_spec=grid_spec,
        out_shape=jax.ShapeDtypeStruct((n_rows, half), jnp.uint32),
        compiler_params=_params(("arbitrary",)),
    )(block_e, n_used, rows_valid, xs, *wargs)


def _combine_dense_kernel(rows_ref, x1_ref, gate_ref, g_ref, b_ref, y_ref):
    gate = gate_ref[...]
    ff = gate[:, 0:1] * _unpack_bf16_pairs(rows_ref[0])
    for k in range(1, TOP_K):
        ff = ff + gate[:, k:k + 1] * _unpack_bf16_pairs(rows_ref[k])
    y_ref[...] = _layer_norm(DEEPNORM_ALPHA * x1_ref[...] + ff, g_ref[...], b_ref[...])


def _combine_dense(rows, x1, gate, ln_g, ln_b):
    T, D = x1.shape
    tl = min(ROW_TILE, T)
    row = lambda n: pl.BlockSpec((tl, n), lambda i: (i, 0))
    full = lambda a: pl.BlockSpec(a.shape, lambda i: (0,) * a.ndim)
    return pl.pallas_call(
        _combine_dense_kernel, grid=(T // tl,),
        in_specs=[pl.BlockSpec((TOP_K, tl, D // 2), lambda i: (0, i, 0)), row(D), row(LANES), full(ln_g),
                  full(ln_b)],
        out_specs=row(D), out_shape=jax.ShapeDtypeStruct((T, D), f32),
        compiler_params=_params(("parallel",)),
    )(rows, x1, gate, ln_g, ln_b)


def _prep_weights(l, w_in, conv_w, conv_b, dt_bias, a_log, d_skip, ssm_norm_g, w_o, ln1_g, ln1_b,
                  w_router, b_router, w_gate_up, b_gate_up, w_down, b_down, ln2_g, ln2_b):
    wi = w_in[l]
    c0, c1, c2, c3 = D_QK_ALL, 2 * D_QK_ALL, 2 * D_QK_ALL + D_ATT, 2 * D_QK_ALL + D_ATT + D_SSM
    c4 = c3 + CONV_DIM
    wdt = wi[:, c4:c4 + H_S]
    pad_lane = lambda v, fill=0.0: jnp.pad(v.astype(f32).reshape(1, -1), ((0, 0), (0, LANES - v.shape[-1])),
                                            constant_values=fill)
    pad_col = lambda v: jnp.pad(v.astype(f32).reshape(-1, 1), ((0, BF16_ROWS - v.shape[-1]), (0, 0)))
    wr = jnp.pad(w_router[l].astype(f32), ((0, 0), (0, LANES - N_EXPERTS)))
    wr_hi = wr.astype(bf16)
    wgu = w_gate_up[l]
    return {
        'wq': wi[:, :c0].astype(bf16), 'wqT': wi[:, :c0].T.astype(bf16),
        'wk': wi[:, c0:c1].astype(bf16), 'wkT': wi[:, c0:c1].T.astype(bf16), 'wv': wi[:, c1:c2].astype(bf16), 'wvT': wi[:, c1:c2].T.astype(bf16),
        'wz': wi[:, c2:c3].astype(bf16), 'wx': wi[:, c3:c4].astype(bf16),
        'wdt': jnp.pad(wdt, ((0, 0), (0, LANES - H_S))).astype(bf16),
        'wdtT': jnp.pad(wdt.T, ((0, BF16_ROWS - H_S), (0, 0))).astype(bf16),
        'conv_w': conv_w[l].astype(f32), 'conv_b': conv_b[l].astype(f32).reshape(1, -1),
        'dt_bias': pad_lane(dt_bias[l]), 'dt_biasT': pad_col(dt_bias[l]),
        'a_log': pad_lane(a_log[l]), 'a_logT': pad_col(a_log[l]),
        'd_skip': jnp.repeat(d_skip[l].astype(f32), SSM_HEADDIM).reshape(1, -1),
        'ssm_g': ssm_norm_g[l].astype(f32).reshape(1, -1),
        'wo_att': w_o[l][:D_ATT].astype(bf16), 'wo_ssm': w_o[l][D_ATT:].astype(bf16),
        'ln1_g': ln1_g[l].astype(f32).reshape(1, -1), 'ln1_b': ln1_b[l].astype(f32).reshape(1, -1),
        'wr_hi': wr_hi, 'wr_hi_lo': jnp.concatenate([wr_hi, (wr - wr_hi.astype(f32)).astype(bf16)], axis=1),
        'b_router': pad_lane(b_router[l], NEG_INF),
        'w_gu': wgu,
        'b_g': b_gate_up[l][:, None, 0::2].astype(f32), 'b_u': b_gate_up[l][:, None, 1::2].astype(f32),
        'w_d': w_down[l].astype(bf16), 'b_d': b_down[l][:, None, :].astype(f32),
        'ln2_g': ln2_g[l].astype(f32).reshape(1, -1), 'ln2_b': ln2_b[l].astype(f32).reshape(1, -1),
    }


def _trunk_layer(x, l, w, rel_bias, lam_args, attn_g, k_past, v_past, h0, conv0):
    B, L, D = x.shape
    x2d = x.reshape(B * L, D)
    lam_init = 0.8 - 0.6 * math.exp(-0.3 * l)
    prompt = k_past is None
    proj = _in_proj(x2d, w, prompt, B)
    z, xbc, dt, dtT = proj[:4]
    if prompt:
        kT, vT, kb, qT, v1T = proj[4:]
        att = _attn_prompt(qT, kb, v1T, rel_bias, lam_args, attn_g, B, L, lam_init)
        k_rows = kT.reshape(B, H_A, 2 * DQK, L).transpose(0, 3, 1, 2)
        v_rows = vT.reshape(B, H_A, DV, L).transpose(0, 3, 1, 2)
    else:
        k, v, q = proj[4:]
        att = _attn_sample(q, k, v, k_past, v_past, rel_bias, lam_args, attn_g, lam_init)
        k_rows = k.reshape(B, L, H_A, 2 * DQK)
        v_rows = v.reshape(B, L, H_A, DV)
    ssm, h_new, conv_new = _ssd(z, xbc, dt, dtT, h0, conv0, w, B, L)
    x1, x1p, te, gate, counts = _mix_router(att, ssm, x2d, w)
    block_e, n_used, rows_valid, slot, n_blocks = _route(te, counts)
    slot_k = slot.T
    xs = _sc_scatter_rows(x1p, [slot_k[k] for k in range(TOP_K)], n_blocks * MOE_BLOCK)
    ys = _expert_ffn(xs, block_e, n_used, rows_valid, w)
    rows = _sc_gather_rows(ys, slot_k.reshape(-1)).reshape(TOP_K, B * L, D // 2)
    y = _combine_dense(rows, x1, gate, w['ln2_g'], w['ln2_b'])
    return (y.reshape(B, L, D), k_rows, v_rows, h_new, conv_new)


def kernel(x_prompt, x_sample, cache_k, cache_v, state_ssm, state_conv, rel_bias, w_in, lambda_q1, lambda_k1, lambda_q2, lambda_k2, attn_norm_g, conv_w, conv_b, dt_bias, a_log, d_skip, ssm_norm_g, w_o, ln1_g, ln1_b, w_router, b_router, w_gate_up, b_gate_up, w_down, b_down, ln2_g, ln2_b):
    yp, ys = x_prompt, x_sample
    bp = x_prompt.shape[0]
    depth = w_in.shape[0]
    outs = [[] for _ in range(8)]
    for l in range(depth):
        w = _prep_weights(l, w_in, conv_w, conv_b, dt_bias, a_log, d_skip, ssm_norm_g, w_o, ln1_g, ln1_b,
                          w_router, b_router, w_gate_up, b_gate_up, w_down, b_down, ln2_g, ln2_b)
        w['w_g'], w['w_u'] = _deinterleave(w.pop('w_gu'))
        lam_args = [a[l].astype(f32).reshape(1, -1) for a in (lambda_q1, lambda_k1, lambda_q2, lambda_k2)]
        h0 = jnp.zeros((bp, H_S, SSM_HEADDIM, D_STATE), f32)
        c0 = jnp.zeros((bp, CONV_W - 1, CONV_DIM), f32)
        yp, kp, vp, hp, cp = _trunk_layer(yp, l, w, rel_bias, lam_args, attn_norm_g[l], None, None, h0, c0)
        ys, ks, vs, hs, cs = _trunk_layer(ys, l, w, rel_bias, lam_args, attn_norm_g[l], cache_k[l], cache_v[l],
                                          state_ssm[l], state_conv[l])
        for lst, a in zip(outs, (kp, vp, hp, cp, ks, vs, hs, cs)):
            lst.append(a)
    return (yp, ys) + tuple(jnp.stack(o) for o in outs)
```

```python
import functools
import math

import numpy as np
import jax
import jax.numpy as jnp
from jax import lax
from jax.experimental import pallas as pl
from jax.experimental.pallas import tpu as pltpu
from jax.experimental.pallas import tpu_sc as plsc

f32 = jnp.float32
bf16 = jnp.bfloat16
i32 = jnp.int32

CHUNK = 64
H_A = 8
DQK = 32
DV = 2 * DQK
D_ATT = H_A * DV
SSM_HEADDIM = 64
H_S = 8
D_SSM = H_S * SSM_HEADDIM
SSM_GROUPS = 2
HEADS_PER_GROUP = H_S // SSM_GROUPS
D_STATE = 128
CONV_W = 4
CONV_DIM = D_SSM + 2 * SSM_GROUPS * D_STATE
D_QK_ALL = H_A * 2 * DQK
N_BUCKETS = 32
MAX_DISTANCE = 128
N_EXPERTS = 32
TOP_K = 4
SWIGLU_LIMIT = 7.0
SWIGLU_ALPHA = 1.702
MOE_BLOCK = 256
LN_EPS = 1e-5
RMS_EPS = 1e-5
NEG_INF = -1e30
DEPTH = 1
DEEPNORM_ALPHA = (2.0 * DEPTH) ** 0.25
LOG2E = math.log2(math.e)

LANES = 128
SUBLANES = 8
BF16_ROWS = 16
VMEM_LIMIT = 48 * 1024 * 1024

ROW_TILE = 512
ATT_TILE = 512
ATT_SUB = 256
SSD_TILE = 256
V_ROWS = DV + BF16_ROWS
FFN_BLOCKS_PER_STEP = 2
COMBINE_GROUPS = 4


def _params(semantics):
    return pltpu.CompilerParams(dimension_semantics=semantics, vmem_limit_bytes=VMEM_LIMIT)


def _dot(a, b):
    return jnp.dot(a, b, preferred_element_type=f32)


def _dot_nt(a, b):
    return lax.dot_general(a, b, (((1,), (1,)), ((), ())), preferred_element_type=f32)


def _dot_tn(a, b):
    return lax.dot_general(a, b, (((0,), (0,)), ((), ())), preferred_element_type=f32)


def _split3(a):
    hi = a.astype(bf16)
    r1 = a - hi.astype(f32)
    mid = r1.astype(bf16)
    lo = (r1 - mid.astype(f32)).astype(bf16)
    return hi, mid, lo


def _dot_f32_lhs(a, b_exact):
    hi, mid, lo = _split3(a)
    return _dot(hi, b_exact) + _dot(mid, b_exact) + _dot(lo, b_exact)


def _dot_f32_rhs(a_exact, b):
    hi, mid, lo = _split3(b)
    return _dot(a_exact, hi) + _dot(a_exact, mid) + _dot(a_exact, lo)


def _softplus(x):
    return jnp.maximum(x, 0.0) + jnp.log1p(jnp.exp(-jnp.abs(x)))


def _sigmoid(x):
    return 1.0 / (1.0 + jnp.exp(-x))


def _layer_norm(y, g, b):
    mu = jnp.mean(y, axis=-1, keepdims=True)
    yc = y - mu
    var = jnp.mean(yc * yc, axis=-1, keepdims=True)
    return yc * lax.rsqrt(var + LN_EPS) * g + b


def _lambda(lq1_ref, lk1_ref, lq2_ref, lk2_ref, lam_init):
    s1 = jnp.sum(lq1_ref[...] * lk1_ref[...], axis=-1, keepdims=True)
    s2 = jnp.sum(lq2_ref[...] * lk2_ref[...], axis=-1, keepdims=True)
    return jnp.exp(s1) - jnp.exp(s2) + lam_init


def _in_proj_kernel(x_ref, wk_ref, wv_ref, wz_ref, wx_ref, wdt_ref, wdtT_ref, wq_ref, wkT_ref, wvT_ref, *outs,
                    prompt):
    xb = x_ref[...].astype(bf16)
    z_ref, xbc_ref, dt_ref, dtT_ref = outs[:4]
    k = _dot(xb, wk_ref[...])
    z_ref[...] = _dot(xb, wz_ref[...])
    xbc_ref[...] = _dot(xb, wx_ref[...])
    dt_ref[...] = _dot(xb, wdt_ref[...])
    dtT_ref[...] = _dot_nt(wdtT_ref[...], xb)
    scale = DQK ** -0.5
    if prompt:
        kT_ref, vT_ref, kb_ref, qT_ref, v1T_ref = outs[4:]
        kb_ref[...] = k.astype(bf16)
        kT_ref[0] = _dot_nt(wkT_ref[...], xb)
        qT_ref[...] = (_dot_nt(wq_ref[...], xb) * (scale * LOG2E)).astype(bf16)
        vT32 = _dot_nt(wvT_ref[...], xb)
        vT_ref[0] = vT32
        vT = vT32.astype(bf16)
        ones = jnp.ones((BF16_ROWS, vT.shape[1]), bf16)
        for h in range(H_A):
            v1T_ref[h * V_ROWS:h * V_ROWS + DV, :] = vT[h * DV:(h + 1) * DV, :]
            v1T_ref[h * V_ROWS + DV:(h + 1) * V_ROWS, :] = ones
    else:
        k_ref, v_ref, q_ref = outs[4:]
        k_ref[...] = k
        v_ref[...] = _dot(xb, wv_ref[...])
        q_ref[...] = (_dot(xb, wq_ref[...]) * scale).astype(bf16)


def _in_proj(x2d, w, prompt, B):
    T, D = x2d.shape
    tl = min(ROW_TILE, T)
    L = T // B
    n_l = L // tl
    grid = (T // tl,)
    row = lambda n: pl.BlockSpec((tl, n), lambda i: (i, 0))
    col = lambda n: pl.BlockSpec((n, tl), lambda i: (0, i))
    per_stream = lambda n: pl.BlockSpec((1, n, tl), lambda i: (i // n_l, 0, i % n_l))
    full = lambda a: pl.BlockSpec(a.shape, lambda i: (0,) * a.ndim)
    wq = w['wqT'] if prompt else w['wq']
    ins = [x2d, w['wk'], w['wv'], w['wz'], w['wx'], w['wdt'], w['wdtT'], wq, w['wkT'], w['wvT']]
    in_specs = [row(D)] + [full(a) for a in ins[1:]]
    out_shape = [jax.ShapeDtypeStruct((T, D_SSM), f32), jax.ShapeDtypeStruct((T, CONV_DIM), f32),
                 jax.ShapeDtypeStruct((T, LANES), f32), jax.ShapeDtypeStruct((BF16_ROWS, T), f32)]
    out_specs = [row(D_SSM), row(CONV_DIM), row(LANES), col(BF16_ROWS)]
    if prompt:
        assert L % tl == 0
        out_shape += [jax.ShapeDtypeStruct((B, D_QK_ALL, L), f32), jax.ShapeDtypeStruct((B, D_ATT, L), f32),
                      jax.ShapeDtypeStruct((T, D_QK_ALL), bf16), jax.ShapeDtypeStruct((D_QK_ALL, T), bf16),
                      jax.ShapeDtypeStruct((H_A * V_ROWS, T), bf16)]
        out_specs += [per_stream(D_QK_ALL), per_stream(D_ATT), row(D_QK_ALL), col(D_QK_ALL), col(H_A * V_ROWS)]
    else:
        out_shape += [jax.ShapeDtypeStruct((T, D_QK_ALL), f32), jax.ShapeDtypeStruct((T, D_ATT), f32),
                      jax.ShapeDtypeStruct((T, D_QK_ALL), bf16)]
        out_specs += [row(D_QK_ALL), row(D_ATT), row(D_QK_ALL)]
    return pl.pallas_call(
        functools.partial(_in_proj_kernel, prompt=prompt),
        grid=grid, in_specs=in_specs, out_specs=out_specs, out_shape=out_shape,
        compiler_params=_params(("parallel",)),
    )(*ins)


def _t5_bucket(rel):
    half = N_BUCKETS // 2
    max_exact = half // 2
    n = jnp.abs(rel)
    large = max_exact + (jnp.log(jnp.maximum(n, 1).astype(f32) / max_exact)
                         / math.log(MAX_DISTANCE / max_exact) * (half - max_exact)).astype(i32)
    large = jnp.minimum(large, half - 1)
    return jnp.where(rel > 0, half, 0) + jnp.where(n < max_exact, n, large)


def _far_bucket(min_dist):
    half = N_BUCKETS // 2
    max_exact = half // 2
    v = max_exact + int(math.log(min_dist / max_exact) / math.log(MAX_DISTANCE / max_exact) * (half - max_exact))
    return half - 1 if (min_dist >= max_exact and v - 1 >= half - 1) else None


def _bias_lookup(rel_bias, bucket):
    onehot = (bucket[..., None] == jnp.arange(N_BUCKETS, dtype=i32)).astype(f32)
    return jnp.dot(onehot, rel_bias.astype(f32), precision=lax.Precision.HIGHEST)


def _prompt_bias_tiles(rel_bias, ta):
    j = jnp.arange(ta, dtype=i32)[:, None]
    i = jnp.arange(ta, dtype=i32)[None, :]
    tiles = []
    for d in range(2):
        rel = j - i - d * ta
        b = _bias_lookup(rel_bias, _t5_bucket(rel))
        allowed = ((j // CHUNK) <= (i // CHUNK)) if d == 0 else jnp.ones((ta, ta), bool)
        tiles.append(jnp.where(allowed[..., None], b, NEG_INF))
    return jnp.transpose(jnp.stack(tiles), (3, 0, 1, 2))


def _sample_bias(rel_bias, past, s):
    q_pos = past + jnp.arange(s, dtype=i32)
    k_pos = jnp.arange(past + s, dtype=i32)
    rel = k_pos[None, :] - q_pos[:, None]
    b = _bias_lookup(rel_bias, _t5_bucket(rel))
    allowed = (k_pos[None, :] // CHUNK) <= (q_pos[:, None] // CHUNK)
    return jnp.transpose(jnp.where(allowed[..., None], b, NEG_INF), (2, 0, 1))


def _attn_prompt_kernel(qi_tab, ki_tab, qT_ref, k_ref, v1T_ref, bias_ref, cfar_ref,
                        lq1_ref, lk1_ref, lq2_ref, lk2_ref, g_ref, o_ref,
                        qm_sc, m_sc, acc_sc, oT_sc, *, lam_init):
    p = pl.program_id(1)
    qi = qi_tab[p]
    ki = ki_tab[p]
    d = qi - ki
    ta = k_ref.shape[0]
    n_hp = H_A // 2

    @pl.when(ki == 0)
    def _():
        m_sc[...] = jnp.full(m_sc.shape, NEG_INF, f32)
        acc_sc[...] = jnp.zeros(acc_sc.shape, f32)
        row = lax.broadcasted_iota(i32, (LANES, ta), 0)
        for hp in range(n_hp):
            qt = qT_ref[hp * LANES:(hp + 1) * LANES, :]
            for gi in range(4):
                keep = (row >= gi * DQK) & (row < (gi + 1) * DQK)
                qm_sc[hp, :, gi * ta:(gi + 1) * ta] = jnp.where(keep, qt, jnp.zeros_like(qt))

    def scores(hp):
        return _dot(k_ref[:, hp * LANES:(hp + 1) * LANES], qm_sc[hp])

    nb = ta // ATT_SUB

    def near_bias(hp, step_d):
        rows = []
        for jb in range(nb):
            cols = []
            for g in range(4):
                for ib in range(nb):
                    dd = step_d * nb + ib - jb
                    if dd < 0:
                        blk = jnp.full((ATT_SUB, ATT_SUB), NEG_INF * LOG2E, f32)
                    elif dd < 2:
                        blk = bias_ref[2 * hp + g // 2, dd]
                    else:
                        lo = g * ta + ib * ATT_SUB
                        blk = jnp.broadcast_to(cfar_ref[hp][:, lo:lo + ATT_SUB], (ATT_SUB, ATT_SUB))
                    cols.append(blk)
            rows.append(jnp.concatenate(cols, axis=1))
        return jnp.concatenate(rows, axis=0)

    def update(step_d):
        s_next = scores(0)
        for hp in range(n_hp):
            s = s_next
            if hp + 1 < n_hp:
                s_next = scores(hp + 1)
            m_old = m_sc[hp]
            if step_d is not None:
                s = s + near_bias(hp, step_d)
                m_new = jnp.maximum(m_old, jnp.max(s, axis=0, keepdims=True))
                shift = m_new
            else:
                c = cfar_ref[hp]
                m_new = jnp.maximum(m_old, jnp.max(s, axis=0, keepdims=True) + c)
                shift = m_new - c
            alpha = jnp.exp2(m_old - m_new)
            pT = jnp.exp2(s - shift).astype(bf16)
            for hh in range(2):
                h = 2 * hp + hh
                cols = slice(hh * 2 * ta, (hh + 1) * 2 * ta)
                pv = _dot(v1T_ref[h * V_ROWS:(h + 1) * V_ROWS, :], pT[:, cols])
                acc_sc[h] = alpha[:, cols] * acc_sc[h] + pv
            m_sc[hp] = m_new

    @pl.when(d == 0)
    def _():
        update(0)

    @pl.when(d == 1)
    def _():
        update(1)

    @pl.when(d >= 2)
    def _():
        update(None)

    @pl.when(d == 0)
    def _():
        lam = _lambda(lq1_ref, lk1_ref, lq2_ref, lk2_ref, lam_init)
        for h in range(H_A):
            a = acc_sc[h]
            a0 = a[:, :ta]
            a1 = a[:, ta:]
            o = a0[:DV] / a0[DV:DV + 1] - lam * (a1[:DV] / a1[DV:DV + 1])
            o = o * lax.rsqrt(jnp.mean(o * o, axis=0, keepdims=True) + RMS_EPS) * g_ref[...] * (1.0 - lam_init)
            oT_sc[h * DV:(h + 1) * DV, :] = o
        o_ref[...] = oT_sc[...].T.astype(bf16)


def _attn_prompt(qT, kb, v1T, rel_bias, lam_args, attn_g, B, L, lam_init):
    ta = min(ATT_TILE, L)
    assert L % ta == 0 and ta % ATT_SUB == 0 and ATT_SUB % CHUNK == 0
    nt = L // ta
    far = _far_bucket(ATT_SUB + 1)
    assert L <= 2 * ATT_SUB or far is not None, "key blocks two or more behind must share one bias bucket"
    far = far if far is not None else N_BUCKETS // 2 - 1
    pairs = [(q, k) for q in range(nt) for k in range(q + 1)]
    qi_tab = jnp.asarray([q for q, _ in pairs], i32)
    ki_tab = jnp.asarray([k for _, k in pairs], i32)
    bias = _prompt_bias_tiles(rel_bias, ATT_SUB) * LOG2E
    cfar = jnp.repeat((rel_bias.astype(f32)[far] * LOG2E).reshape(H_A // 2, 2), 2 * ta, axis=1)[:, None, :]
    g_col = attn_g.astype(f32).reshape(DV, 1)
    T = B * L
    n_hp = H_A // 2
    full = lambda a: pl.BlockSpec(a.shape, lambda b, p, qt, kt: (0,) * a.ndim)
    grid_spec = pltpu.PrefetchScalarGridSpec(
        num_scalar_prefetch=2, grid=(B, len(pairs)),
        in_specs=[
            pl.BlockSpec((D_QK_ALL, ta), lambda b, p, qt, kt: (0, b * nt + qt[p])),
            pl.BlockSpec((ta, D_QK_ALL), lambda b, p, qt, kt: (b * nt + kt[p], 0)),
            pl.BlockSpec((H_A * V_ROWS, ta), lambda b, p, qt, kt: (0, b * nt + kt[p])),
            full(bias), full(cfar)] + [full(a) for a in lam_args] + [full(g_col)],
        out_specs=pl.BlockSpec((ta, D_ATT), lambda b, p, qt, kt: (b * nt + qt[p], 0)),
        scratch_shapes=[pltpu.VMEM((n_hp, LANES, 4 * ta), bf16), pltpu.VMEM((n_hp, 1, 4 * ta), f32),
                        pltpu.VMEM((H_A, V_ROWS, 2 * ta), f32), pltpu.VMEM((D_ATT, ta), f32)])
    return pl.pallas_call(
        functools.partial(_attn_prompt_kernel, lam_init=lam_init),
        grid_spec=grid_spec, out_shape=jax.ShapeDtypeStruct((T, D_ATT), bf16),
        compiler_params=_params(("parallel", "arbitrary")),
    )(qi_tab, ki_tab, qT, kb, v1T, bias, cfar, *lam_args, g_col)


def _attn_sample_kernel(q_ref, kn_ref, vn_ref, kc_ref, vc_ref, bc_ref, bn_ref,
                        lq1_ref, lk1_ref, lq2_ref, lk2_ref, g2_ref, o_ref, *, lam_init):
    s_len = q_ref.shape[0]
    lane = lax.broadcasted_iota(i32, (s_len, LANES), 1)
    lam = _lambda(lq1_ref, lk1_ref, lq2_ref, lk2_ref, lam_init)
    for hp in range(H_A // 2):
        sl = slice(hp * LANES, (hp + 1) * LANES)
        qt = q_ref[:, sl]
        kct = kc_ref[0, sl, :].astype(bf16)
        vct = vc_ref[0, sl, :].astype(bf16)
        knt = kn_ref[:, sl].astype(bf16)
        vnt = vn_ref[:, sl].astype(bf16)
        outs = []
        for hh in range(2):
            h = 2 * hp + hh
            oc = []
            for c in range(2):
                gi = 2 * hh + c
                qm = jnp.where((lane >= gi * DQK) & (lane < (gi + 1) * DQK), qt, jnp.zeros_like(qt))
                s1 = _dot(qm, kct) + bc_ref[h]
                s2 = _dot_nt(qm, knt) + bn_ref[h]
                m = jnp.maximum(jnp.max(s1, axis=-1, keepdims=True), jnp.max(s2, axis=-1, keepdims=True))
                p1 = jnp.exp(s1 - m)
                p2 = jnp.exp(s2 - m)
                l = jnp.sum(p1, axis=-1, keepdims=True) + jnp.sum(p2, axis=-1, keepdims=True)
                pv = _dot_nt(p1.astype(bf16), vct) + _dot(p2.astype(bf16), vnt)
                oc.append(pv / l)
            o = oc[0] - lam * oc[1]
            in_head = (lane >= hh * DV) & (lane < (hh + 1) * DV)
            ms = jnp.sum(jnp.where(in_head, o * o, 0.0), axis=-1, keepdims=True) * (1.0 / DV)
            outs.append(o * lax.rsqrt(ms + RMS_EPS))
        ot = jnp.where(lane < DV, outs[0], outs[1]) * g2_ref[...] * (1.0 - lam_init)
        o_ref[:, sl] = ot.astype(bf16)


def _attn_sample(q, k_new, v_new, cache_k, cache_v, rel_bias, lam_args, attn_g, lam_init):
    nb, past = cache_k.shape[0], cache_k.shape[1]
    T = q.shape[0]
    s = T // nb
    kc = cache_k.transpose(0, 2, 3, 1).reshape(nb, D_QK_ALL, past)
    vc = cache_v.transpose(0, 2, 3, 1).reshape(nb, D_ATT, past)
    bias = _sample_bias(rel_bias, past, s)
    bc, bn = bias[:, :, :past], bias[:, :, past:]
    g2 = jnp.tile(attn_g.astype(f32), 2).reshape(1, LANES)
    full = lambda a: pl.BlockSpec(a.shape, lambda b: (0,) * a.ndim)
    row = lambda n: pl.BlockSpec((s, n), lambda b: (b, 0))
    return pl.pallas_call(
        functools.partial(_attn_sample_kernel, lam_init=lam_init),
        grid=(nb,),
        in_specs=[row(D_QK_ALL), row(D_QK_ALL), row(D_ATT),
                  pl.BlockSpec((1, D_QK_ALL, past), lambda b: (b, 0, 0)),
                  pl.BlockSpec((1, D_ATT, past), lambda b: (b, 0, 0)),
                  full(bc), full(bn)] + [full(a) for a in lam_args] + [full(g2)],
        out_specs=row(D_ATT), out_shape=jax.ShapeDtypeStruct((T, D_ATT), bf16),
        compiler_params=_params(("parallel",)),
    )(q, k_new, v_new, kc, vc, bc, bn, *lam_args, g2)


def _ssd_kernel(z_ref, xbc_ref, dt_ref, dtT_ref, h0_ref, c0_ref, cw_ref, cb_ref, dtb_ref, dtbT_ref,
                alog_ref, alogT_ref, dsk_ref, g_ref, y_ref, hout_ref, cout_ref, xpad_sc, h_sc):
    q = xbc_ref.shape[0]
    c = pl.program_id(1)
    gw = HEADS_PER_GROUP * SSM_HEADDIM

    @pl.when(c == 0)
    def _():
        xpad_sc[0:SUBLANES] = c0_ref[0]
        h_sc[...] = h0_ref[0]

    xpad_sc[SUBLANES:SUBLANES + q] = xbc_ref[...]
    first = SUBLANES - (CONV_W - 1)
    conv = cb_ref[...]
    for i in range(CONV_W):
        conv = conv + xpad_sc[first + i:first + i + q] * cw_ref[i:i + 1, :]
    tail = xpad_sc[q:q + SUBLANES]
    xpad_sc[0:SUBLANES] = tail
    cout_ref[0] = tail
    act = conv * _sigmoid(conv)
    xs = act[:, :D_SSM]

    lane = lax.broadcasted_iota(i32, (1, LANES), 1)
    a_row = jnp.where(lane < H_S, -jnp.exp(alog_ref[...]), 0.0)
    dt = _softplus(dt_ref[...] + dtb_ref[...])
    ii = lax.broadcasted_iota(i32, (q, q), 0)
    jj = lax.broadcasted_iota(i32, (q, q), 1)
    causal = jj <= ii
    acs = _dot_f32_rhs(causal.astype(bf16), dt * a_row)
    sub = lax.broadcasted_iota(i32, (BF16_ROWS, 1), 0)
    a_col = jnp.where(sub < H_S, -jnp.exp(alogT_ref[...]), 0.0)
    dtT = _softplus(dtT_ref[0] + dtbT_ref[...])
    acsT = _dot_f32_lhs(dtT * a_col, (ii <= jj).astype(bf16))

    er = lax.broadcasted_iota(i32, (LANES, D_SSM), 0)
    ec = lax.broadcasted_iota(i32, (LANES, D_SSM), 1)
    expand = (ec // SSM_HEADDIM == er).astype(bf16)
    dt_x = _dot_f32_lhs(dt, expand)
    acs_x = _dot_f32_lhs(acs, expand)
    e_acs = jnp.exp(acs_x)
    acs_last = acs_x[q - 1:q, :]
    decay = jnp.exp(acs_last - acs_x) * dt_x
    dtx = xs * dt_x
    xd = xs * decay
    glane = lax.broadcasted_iota(i32, (1, gw), 1)

    ys = []
    for g in range(SSM_GROUPS):
        gs = slice(g * gw, (g + 1) * gw)
        bg = act[:, D_SSM + g * D_STATE:D_SSM + (g + 1) * D_STATE].astype(bf16)
        cg = act[:, D_SSM + (SSM_GROUPS + g) * D_STATE:D_SSM + (SSM_GROUPS + g + 1) * D_STATE].astype(bf16)
        cb = _dot_nt(cg, bg)
        h_old = h_sc[g]
        yg = _dot(cg, h_old.astype(bf16)) * e_acs[:, gs]
        dtx_g = dtx[:, gs]
        for e4 in range(HEADS_PER_GROUP):
            e = g * HEADS_PER_GROUP + e4
            seg = acs[:, e:e + 1] - acsT[e:e + 1, :]
            lmat = jnp.exp(jnp.where(causal, seg, -jnp.inf))
            rhs = jnp.where(glane // SSM_HEADDIM == e4, dtx_g, 0.0).astype(bf16)
            yg = yg + _dot((cb * lmat).astype(bf16), rhs)
        h_sc[g] = h_old * jnp.exp(acs_last[:, gs]) + _dot_tn(bg, xd[:, gs].astype(bf16))
        ys.append(yg)
    y = jnp.concatenate(ys, axis=1) + dsk_ref[...] * xs
    zz = z_ref[...]
    y = y * (zz * _sigmoid(zz))
    for g in range(SSM_GROUPS):
        gs = slice(g * gw, (g + 1) * gw)
        yg = y[:, gs]
        r = lax.rsqrt(jnp.mean(yg * yg, axis=-1, keepdims=True) + RMS_EPS)
        y_ref[:, gs] = (yg * r * g_ref[:, gs]).astype(bf16)
    hout_ref[0] = h_sc[...]


def _ssd(z, xbc, dt, dtT, h0, conv0, pw, B, L):
    q = min(SSD_TILE, L)
    nc = L // q
    gw = HEADS_PER_GROUP * SSM_HEADDIM
    dtT3 = dtT.reshape(BF16_ROWS, B, L).transpose(1, 0, 2)
    hT0 = h0.astype(f32).reshape(B, SSM_GROUPS, gw, D_STATE).transpose(0, 1, 3, 2)
    c0 = jnp.pad(conv0.astype(f32), ((0, 0), (SUBLANES - (CONV_W - 1), 0), (0, 0)))
    full = lambda a: pl.BlockSpec(a.shape, lambda b, c: (0,) * a.ndim)
    row = lambda n: pl.BlockSpec((q, n), lambda b, c: (b * nc + c, 0))
    params = [pw['conv_w'], pw['conv_b'], pw['dt_bias'], pw['dt_biasT'], pw['a_log'], pw['a_logT'],
              pw['d_skip'], pw['ssm_g']]
    y, hT, ctail = pl.pallas_call(
        _ssd_kernel, grid=(B, nc),
        in_specs=[row(D_SSM), row(CONV_DIM), row(LANES),
                  pl.BlockSpec((1, BF16_ROWS, q), lambda b, c: (b, 0, c)),
                  pl.BlockSpec((1, SSM_GROUPS, D_STATE, gw), lambda b, c: (b, 0, 0, 0)),
                  pl.BlockSpec((1, SUBLANES, CONV_DIM), lambda b, c: (b, 0, 0))] + [full(a) for a in params],
        out_specs=[row(D_SSM),
                   pl.BlockSpec((1, SSM_GROUPS, D_STATE, gw), lambda b, c: (b, 0, 0, 0)),
                   pl.BlockSpec((1, SUBLANES, CONV_DIM), lambda b, c: (b, 0, 0))],
        out_shape=[jax.ShapeDtypeStruct((B * L, D_SSM), bf16),
                   jax.ShapeDtypeStruct((B, SSM_GROUPS, D_STATE, gw), f32),
                   jax.ShapeDtypeStruct((B, SUBLANES, CONV_DIM), f32)],
        scratch_shapes=[pltpu.VMEM((q + SUBLANES, CONV_DIM), f32), pltpu.VMEM((SSM_GROUPS, D_STATE, gw), f32)],
        compiler_params=_params(("parallel", "arbitrary")),
    )(z, xbc, dt, dtT3, hT0, c0, *params)
    h_new = hT.transpose(0, 1, 3, 2).reshape(B, H_S, SSM_HEADDIM, D_STATE)
    return y, h_new, ctail[:, SUBLANES - (CONV_W - 1):, :]


def _pack_bf16_pairs(x):
    half = x.shape[1] // 2
    bits = lambda a: lax.bitcast_convert_type(a.astype(bf16).astype(f32), jnp.uint32)
    return (bits(x[:, half:]) & jnp.uint32(0xFFFF0000)) | (bits(x[:, :half]) >> 16)


def _unpack_bf16_pairs(word):
    lo = lax.bitcast_convert_type(word << 16, f32)
    hi = lax.bitcast_convert_type(word & jnp.uint32(0xFFFF0000), f32)
    return jnp.concatenate([lo, hi], axis=1)


def _mix_router_kernel(att_ref, ssm_ref, x_ref, woa_ref, wos_ref, g_ref, b_ref, wrh_ref, wrhl_ref, br_ref,
                       x1_ref, x1p_ref, te_ref, gate_ref, cnt_ref, run_sc):
    @pl.when(pl.program_id(0) == 0)
    def _():
        run_sc[...] = jnp.zeros(run_sc.shape, f32)

    mix = _dot(att_ref[...], woa_ref[...]) + _dot(ssm_ref[...], wos_ref[...])
    x1 = _layer_norm(DEEPNORM_ALPHA * x_ref[...] + mix, g_ref[...], b_ref[...])
    x1_ref[...] = x1
    x1p_ref[...] = _pack_bf16_pairs(x1)
    hi = x1.astype(bf16)
    lo = (x1 - hi.astype(f32)).astype(bf16)
    both = _dot(hi, wrhl_ref[...])
    logits = both[:, :LANES] + both[:, LANES:] + _dot(lo, wrh_ref[...]) + br_ref[...]
    lane =lax.broadcasted_iota(i32, logits.shape, 1)
    vals, idxs = [], []
    for _ in range(TOP_K):
        m = jnp.max(logits, axis=-1, keepdims=True)
        idx = jnp.min(jnp.where(logits == m, lane, LANES), axis=-1, keepdims=True)
        vals.append(m)
        idxs.append(idx)
        logits = jnp.where(lane == idx, -jnp.inf, logits)
    es = [jnp.exp(v - vals[0]) for v in vals]
    tot = es[0]
    for e in es[1:]:
        tot = tot + e
    tl = logits.shape[0]
    chosen = jnp.zeros(logits.shape, f32)
    for k in range(TOP_K):
        chosen = chosen + (lane == idxs[k]).astype(f32)
    ii = lax.broadcasted_iota(i32, (tl, tl), 0)
    jj = lax.broadcasted_iota(i32, (tl, tl), 1)
    before = _dot((jj < ii).astype(bf16), chosen.astype(bf16)) + run_sc[...]
    te = jnp.zeros(logits.shape, i32)
    gate = jnp.zeros(logits.shape, f32)
    for k in range(TOP_K):
        rank = jnp.sum(jnp.where(lane == idxs[k], before, 0.0), axis=-1, keepdims=True).astype(i32)
        te = jnp.where(lane == k, idxs[k], te)
        te = jnp.where(lane == TOP_K + k, rank, te)
        gate = jnp.where(lane == k, es[k] / tot, gate)
    te_ref[...] = te
    gate_ref[...] = gate
    run_sc[...] = run_sc[...] + jnp.sum(chosen, axis=0, keepdims=True)
    cnt_ref[...] = run_sc[...]


def _mix_router(att, ssm, x2d, w):
    T, D = x2d.shape
    tl = min(ROW_TILE, T)
    row = lambda n: pl.BlockSpec((tl, n), lambda i: (i, 0))
    full = lambda a: pl.BlockSpec(a.shape, lambda i: (0,) * a.ndim)
    ws = [w['wo_att'], w['wo_ssm'], w['ln1_g'], w['ln1_b'], w['wr_hi'], w['wr_hi_lo'], w['b_router']]
    return pl.pallas_call(
        _mix_router_kernel, grid=(T // tl,),
        in_specs=[row(D_ATT), row(D_SSM), row(D)] + [full(a) for a in ws],
        out_specs=[row(D), row(D // 2), row(LANES), row(LANES), pl.BlockSpec((1, LANES), lambda i: (0, 0))],
        out_shape=[jax.ShapeDtypeStruct((T, D), f32), jax.ShapeDtypeStruct((T, D // 2), jnp.uint32),
                   jax.ShapeDtypeStruct((T, LANES), i32),
                   jax.ShapeDtypeStruct((T, LANES), f32), jax.ShapeDtypeStruct((1, LANES), f32)],
        scratch_shapes=[pltpu.VMEM((1, LANES), f32)],
        compiler_params=_params(("arbitrary",)),
    )(att, ssm, x2d, *ws)


def _route(te, counts_f):
    T = te.shape[0]
    n_assign = T * TOP_K
    counts = counts_f[0, :N_EXPERTS].astype(i32)
    padded = (counts + MOE_BLOCK - 1) // MOE_BLOCK * MOE_BLOCK
    pad_end = jnp.cumsum(padded)
    pad_start = pad_end - padded
    top_e, rank = te[:, :TOP_K], te[:, TOP_K:2 * TOP_K]
    experts = jnp.arange(N_EXPERTS, dtype=i32)
    slot = rank + jnp.sum(jnp.where(top_e[..., None] == experts, pad_start, 0), axis=-1)
    n_blocks = -(-n_assign // MOE_BLOCK) + N_EXPERTS
    n_blocks += n_blocks % FFN_BLOCKS_PER_STEP
    block_start = jnp.arange(n_blocks, dtype=i32) * MOE_BLOCK
    block_e = jnp.minimum(jnp.sum(block_start[:, None] >= pad_end[None, :], axis=-1), N_EXPERTS - 1).astype(i32)
    n_used = (pad_end[-1] // MOE_BLOCK).astype(i32).reshape(1)
    row_end = jnp.sum(jnp.where(block_e[:, None] == experts, pad_start + counts, 0), axis=-1)
    rows_valid = jnp.clip(row_end - block_start, 0, MOE_BLOCK).astype(i32)
    return block_e, n_used, rows_valid, slot.astype(i32), n_blocks


SC_INDEX_WINDOW = 128
SC_ROWS = 32


def _sc_mesh():
    return plsc.VectorSubcoreMesh(core_axis_name="c", subcore_axis_name="s")


def _sc_move_rows(src, src_idx, dst_idx_list, n_out):
    M = src_idx.shape[0]
    D = src.shape[1]
    idx = [a.reshape(1, M) for a in [src_idx] + list(dst_idx_list)]

    @functools.partial(pl.kernel, out_type=jax.ShapeDtypeStruct((n_out, D), src.dtype), mesh=_sc_mesh(),
                       scratch_types=[pltpu.VMEM((2, SC_ROWS, D), src.dtype), pltpu.SemaphoreType.DMA((2,))])
    def move(s_hbm, *rest):
        i_hbm, o_hbm, buf, sem = rest[:-3], rest[-3], rest[-2], rest[-1]
        n_parts = SC_INDEX_WINDOW // SC_ROWS

        def body(si_vmem, *di_vmem):
            def fetch(j):
                part = pl.ds(j * SC_ROWS, SC_ROWS)
                return pltpu.async_copy(s_hbm.at[si_vmem.at[0, part]], buf.at[j % 2], sem.at[j % 2])

            pending = fetch(0)
            for j in range(n_parts):
                nxt = fetch(j + 1) if j + 1 < n_parts else None
                pending.wait()
                part = pl.ds(j * SC_ROWS, SC_ROWS)
                for dv in di_vmem:
                    pltpu.sync_copy(buf.at[j % 2], o_hbm.at[dv.at[0, part]])
                pending = nxt

        pltpu.emit_pipeline(
            body, grid=(M // SC_INDEX_WINDOW,),
            in_specs=[pl.BlockSpec((1, SC_INDEX_WINDOW), lambda i: (0, i))] * len(idx),
            out_specs=[], core_axis_name=("c", "s"), dimension_semantics=(pltpu.PARALLEL,),
        )(*i_hbm)

    return move(src, *idx)


def _sc_scatter_rows(x, idx_k, n_rows):
    return _sc_move_rows(x, jnp.arange(x.shape[0], dtype=i32), idx_k, n_rows)


def _sc_gather_rows(src, idx):
    M = idx.shape[0]
    return _sc_move_rows(src, idx, [jnp.arange(M, dtype=i32)], M)


def _deinterleave_kernel(w_ref, g_ref, u_ref):
    wb = w_ref[0].astype(bf16)
    tn = g_ref.shape[-1]
    r = lax.broadcasted_iota(i32, (2 * tn, tn), 0)
    c = lax.broadcasted_iota(i32, (2 * tn, tn), 1)
    g_ref[0] = _dot(wb, (r == 2 * c).astype(bf16)).astype(bf16)
    u_ref[0] = _dot(wb, (r == 2 * c + 1).astype(bf16)).astype(bf16)


def _deinterleave(w_gu):
    E, D, F2 = w_gu.shape
    tn = 2 * LANES
    out = jax.ShapeDtypeStruct((E, D, F2 // 2), bf16)
    return pl.pallas_call(
        _deinterleave_kernel, grid=(E, F2 // (2 * tn)),
        in_specs=[pl.BlockSpec((1, D, 2 * tn), lambda e, j: (e, 0, j))],
        out_specs=[pl.BlockSpec((1, D, tn), lambda e, j: (e, 0, j))] * 2,
        out_shape=[out, out],
        compiler_params=_params(("parallel", "parallel")),
    )(w_gu)


def _ffn_kernel(be_ref, nu_ref, rv_ref, xs_ref, *refs):
    del be_ref
    y_ref = refs[-1]
    wsets = [refs[6 * j:6 * j + 6] for j in range(FFN_BLOCKS_PER_STEP)]
    first = pl.program_id(0) * FFN_BLOCKS_PER_STEP
    n_live = jnp.clip(nu_ref[0] - first, 0, FFN_BLOCKS_PER_STEP)
    row_id = lax.broadcasted_iota(i32, (MOE_BLOCK, 1), 0)

    def block(j):
        wg_ref, wu_ref, bg_ref, bu_ref, wd_ref, bd_ref = wsets[j]
        rows = pl.ds(j * MOE_BLOCK, MOE_BLOCK)
        words = jnp.where(row_id < rv_ref[first + j], xs_ref[rows, :], jnp.uint32(0))
        xb = _unpack_bf16_pairs(words).astype(bf16)
        g = _dot(xb, wg_ref[0]) + bg_ref[0]
        u = _dot(xb, wu_ref[0]) + bu_ref[0]
        g = jnp.minimum(g, SWIGLU_LIMIT)
        u = jnp.clip(u, -SWIGLU_LIMIT, SWIGLU_LIMIT)
        act = (u + 1.0) * g * _sigmoid(SWIGLU_ALPHA * g)
        y_ref[rows, :] = _pack_bf16_pairs(_dot(act.astype(bf16), wd_ref[0]) + bd_ref[0])

    for live in range(FFN_BLOCKS_PER_STEP + 1):
        @pl.when(n_live == live)
        def _(live=live):
            for j in range(live):
                block(j)
            for j in range(live, FFN_BLOCKS_PER_STEP):
                y_ref[pl.ds(j * MOE_BLOCK, MOE_BLOCK), :] = jnp.zeros((MOE_BLOCK, y_ref.shape[1]), jnp.uint32)


def _expert_ffn(xs, block_e, n_used, rows_valid, w):
    n_rows, half = xs.shape
    D = 2 * half
    nps = FFN_BLOCKS_PER_STEP
    n_steps = n_rows // (MOE_BLOCK * nps)
    F = w['w_g'].shape[2]
    wspecs, wargs = [], []
    for j in range(nps):
        pick = lambda i, be, nu, rv, j=j: (be[i * nps + j], 0, 0)
        wspecs += [pl.BlockSpec((1, D, F), pick), pl.BlockSpec((1, D, F), pick), pl.BlockSpec((1, 1, F), pick),
                   pl.BlockSpec((1, 1, F), pick), pl.BlockSpec((1, F, D), pick), pl.BlockSpec((1, 1, D), pick)]
        wargs += [w['w_g'], w['w_u'], w['b_g'], w['b_u'], w['w_d'], w['b_d']]
    grid_spec = pltpu.PrefetchScalarGridSpec(
        num_scalar_prefetch=3, grid=(n_steps,),
        in_specs=[pl.BlockSpec((nps * MOE_BLOCK, half),
                               lambda i, be, nu, rv: (jnp.minimum(i, (nu[0] - 1) // nps), 0))] + wspecs,
        out_specs=pl.BlockSpec((nps * MOE_BLOCK, half), lambda i, be, nu, rv: (i, 0)))
    return pl.pallas_call(
        _ffn_kernel, grid_spec=grid_spec,
        out_shape=jax.ShapeDtypeStruct((n_rows, half), jnp.uint32),
        compiler_params=_params(("arbitrary",)),
    )(block_e, n_used, rows_valid, xs, *wargs)


def _combine_dense_kernel(rows_ref, x1_ref, gate_ref, g_ref, b_ref, *rest):
    y_ref = rest[-1]
    gate = gate_ref[...]
    ff = gate[:, 0:1] * _unpack_bf16_pairs(rows_ref[0])
    for k in range(1, TOP_K):
        ff = ff + gate[:, k:k + 1] * _unpack_bf16_pairs(rows_ref[k])
    y_ref[...] = _layer_norm(DEEPNORM_ALPHA * x1_ref[...] + ff, g_ref[...], b_ref[...])


def _combine_dense(rows, x1, gate, ln_g, ln_b, chunk, y_partial):
    T, D = x1.shape
    tc = rows.shape[1]
    tl = min(ROW_TILE, tc)
    first = chunk * (tc // tl)
    row = lambda n: pl.BlockSpec((tl, n), lambda i: (first + i, 0))
    full = lambda a: pl.BlockSpec(a.shape, lambda i: (0,) * a.ndim)
    ins = [rows, x1, gate, ln_g, ln_b]
    in_specs = [pl.BlockSpec((TOP_K, tl, D // 2), lambda i: (0, i, 0)), row(D), row(LANES), full(ln_g), full(ln_b)]
    aliases = {}
    if y_partial is not None:
        ins.append(y_partial)
        in_specs.append(pl.BlockSpec(memory_space=pl.ANY))
        aliases = {len(ins) - 1: 0}
    return pl.pallas_call(
        _combine_dense_kernel, grid=(tc // tl,), in_specs=in_specs,
        out_specs=row(D), out_shape=jax.ShapeDtypeStruct((T, D), f32), input_output_aliases=aliases,
        compiler_params=_params(("parallel",)),
    )(*ins)


def _prep_weights(l, w_in, conv_w, conv_b, dt_bias, a_log, d_skip, ssm_norm_g, w_o, ln1_g, ln1_b,
                  w_router, b_router, w_gate_up, b_gate_up, w_down, b_down, ln2_g, ln2_b):
    wi = w_in[l]
    c0, c1, c2, c3 = D_QK_ALL, 2 * D_QK_ALL, 2 * D_QK_ALL + D_ATT, 2 * D_QK_ALL + D_ATT + D_SSM
    c4 = c3 + CONV_DIM
    wdt = wi[:, c4:c4 + H_S]
    pad_lane = lambda v, fill=0.0: jnp.pad(v.astype(f32).reshape(1, -1), ((0, 0), (0, LANES - v.shape[-1])),
                                            constant_values=fill)
    pad_col = lambda v: jnp.pad(v.astype(f32).reshape(-1, 1), ((0, BF16_ROWS - v.shape[-1]), (0, 0)))
    wr = jnp.pad(w_router[l].astype(f32), ((0, 0), (0, LANES - N_EXPERTS)))
    wr_hi = wr.astype(bf16)
    wgu = w_gate_up[l]
    return {
        'wq': wi[:, :c0].astype(bf16), 'wqT': wi[:, :c0].T.astype(bf16),
        'wk': wi[:, c0:c1].astype(bf16), 'wkT': wi[:, c0:c1].T.astype(bf16), 'wv': wi[:, c1:c2].astype(bf16), 'wvT': wi[:, c1:c2].T.astype(bf16),
        'wz': wi[:, c2:c3].astype(bf16), 'wx': wi[:, c3:c4].astype(bf16),
        'wdt': jnp.pad(wdt, ((0, 0), (0, LANES - H_S))).astype(bf16),
        'wdtT': jnp.pad(wdt.T, ((0, BF16_ROWS - H_S), (0, 0))).astype(bf16),
        'conv_w': conv_w[l].astype(f32), 'conv_b': conv_b[l].astype(f32).reshape(1, -1),
        'dt_bias': pad_lane(dt_bias[l]), 'dt_biasT': pad_col(dt_bias[l]),
        'a_log': pad_lane(a_log[l]), 'a_logT': pad_col(a_log[l]),
        'd_skip': jnp.repeat(d_skip[l].astype(f32), SSM_HEADDIM).reshape(1, -1),
        'ssm_g': ssm_norm_g[l].astype(f32).reshape(1, -1),
        'wo_att': w_o[l][:D_ATT].astype(bf16), 'wo_ssm': w_o[l][D_ATT:].astype(bf16),
        'ln1_g': ln1_g[l].astype(f32).reshape(1, -1), 'ln1_b': ln1_b[l].astype(f32).reshape(1, -1),
        'wr_hi': wr_hi, 'wr_hi_lo': jnp.concatenate([wr_hi, (wr - wr_hi.astype(f32)).astype(bf16)], axis=1),
        'b_router': pad_lane(b_router[l], NEG_INF),
        'w_gu': wgu,
        'b_g': b_gate_up[l][:, None, 0::2].astype(f32), 'b_u': b_gate_up[l][:, None, 1::2].astype(f32),
        'w_d': w_down[l].astype(bf16), 'b_d': b_down[l][:, None, :].astype(f32),
        'ln2_g': ln2_g[l].astype(f32).reshape(1, -1), 'ln2_b': ln2_b[l].astype(f32).reshape(1, -1),
    }


def _trunk_layer(x, l, w, rel_bias, lam_args, attn_g, k_past, v_past, h0, conv0):
    B, L, D = x.shape
    x2d = x.reshape(B * L, D)
    lam_init = 0.8 - 0.6 * math.exp(-0.3 * l)
    prompt = k_past is None
    proj = _in_proj(x2d, w, prompt, B)
    z, xbc, dt, dtT = proj[:4]
    if prompt:
        kT, vT, kb, qT, v1T = proj[4:]
        att = _attn_prompt(qT, kb, v1T, rel_bias, lam_args, attn_g, B, L, lam_init)
        k_rows = kT.reshape(B, H_A, 2 * DQK, L).transpose(0, 3, 1, 2)
        v_rows = vT.reshape(B, H_A, DV, L).transpose(0, 3, 1, 2)
    else:
        k, v, q = proj[4:]
        att = _attn_sample(q, k, v, k_past, v_past, rel_bias, lam_args, attn_g, lam_init)
        k_rows = k.reshape(B, L, H_A, 2 * DQK)
        v_rows = v.reshape(B, L, H_A, DV)
    ssm, h_new, conv_new = _ssd(z, xbc, dt, dtT, h0, conv0, w, B, L)
    x1, x1p, te, gate, counts = _mix_router(att, ssm, x2d, w)
    block_e, n_used, rows_valid, slot, n_blocks = _route(te, counts)
    slot_k = slot.T
    xs = _sc_scatter_rows(x1p, [slot_k[k] for k in range(TOP_K)], n_blocks * MOE_BLOCK)
    ys = _expert_ffn(xs, block_e, n_used, rows_valid, w)
    T = B * L
    n_groups = COMBINE_GROUPS if T % (COMBINE_GROUPS * ROW_TILE) == 0 else 1
    tc = T // n_groups
    y = None
    for c in range(n_groups):
        idx = slot_k[:, c * tc:(c + 1) * tc].reshape(-1)
        rows = _sc_gather_rows(ys, idx).reshape(TOP_K, tc, D // 2)
        y = _combine_dense(rows, x1, gate, w['ln2_g'], w['ln2_b'], c, y)
    return (y.reshape(B, L, D), k_rows, v_rows, h_new, conv_new)


def kernel(x_prompt, x_sample, cache_k, cache_v, state_ssm, state_conv, rel_bias, w_in, lambda_q1, lambda_k1, lambda_q2, lambda_k2, attn_norm_g, conv_w, conv_b, dt_bias, a_log, d_skip, ssm_norm_g, w_o, ln1_g, ln1_b, w_router, b_router, w_gate_up, b_gate_up, w_down, b_down, ln2_g, ln2_b):
    yp, ys = x_prompt, x_sample
    bp = x_prompt.shape[0]
    depth = w_in.shape[0]
    outs = [[] for _ in range(8)]
    for l in range(depth):
        w = _prep_weights(l, w_in, conv_w, conv_b, dt_bias, a_log, d_skip, ssm_norm_g, w_o, ln1_g, ln1_b,
                          w_router, b_router, w_gate_up, b_gate_up, w_down, b_down, ln2_g, ln2_b)
        w['w_g'], w['w_u'] = _deinterleave(w.pop('w_gu'))
        lam_args = [a[l].astype(f32).reshape(1, -1) for a in (lambda_q1, lambda_k1, lambda_q2, lambda_k2)]
        h0 = jnp.zeros((bp, H_S, SSM_HEADDIM, D_STATE), f32)
        c0 = jnp.zeros((bp, CONV_W - 1, CONV_DIM), f32)
        yp, kp, vp, hp, cp = _trunk_layer(yp, l, w, rel_bias, lam_args, attn_norm_g[l], None, None, h0, c0)
        ys, ks, vs, hs, cs = _trunk_layer(ys, l, w, rel_bias, lam_args, attn_norm_g[l], cache_k[l], cache_v[l],
                                          state_ssm[l], state_conv[l])
        for lst, a in zip(outs, (kp, vp, hp, cp, ks, vs, hs, cs)):
            lst.append(a)
    return (yp, ys) + tuple(jnp.stack(o) for o in outs)
```

```python
import functools
import math

import numpy as np
import jax
import jax.numpy as jnp
from jax import lax
from jax.experimental import pallas as pl
from jax.experimental.pallas import tpu as pltpu
from jax.experimental.pallas import tpu_sc as plsc

f32 = jnp.float32
bf16 = jnp.bfloat16
i32 = jnp.int32

CHUNK = 64
H_A = 8
DQK = 32
DV = 2 * DQK
D_ATT = H_A * DV
SSM_HEADDIM = 64
H_S = 8
D_SSM = H_S * SSM_HEADDIM
SSM_GROUPS = 2
HEADS_PER_GROUP = H_S // SSM_GROUPS
D_STATE = 128
CONV_W = 4
CONV_DIM = D_SSM + 2 * SSM_GROUPS * D_STATE
D_QK_ALL = H_A * 2 * DQK
N_BUCKETS = 32
MAX_DISTANCE = 128
N_EXPERTS = 32
TOP_K = 4
SWIGLU_LIMIT = 7.0
SWIGLU_ALPHA = 1.702
MOE_BLOCK = 256
LN_EPS = 1e-5
RMS_EPS = 1e-5
NEG_INF = -1e30
DEPTH = 1
DEEPNORM_ALPHA = (2.0 * DEPTH) ** 0.25
LOG2E = math.log2(math.e)

LANES = 128
SUBLANES = 8
BF16_ROWS = 16
VMEM_LIMIT = 48 * 1024 * 1024

ROW_TILE = 512
ATT_TILE = 512
ATT_SUB = 256
SSD_TILE = 256
V_ROWS = DV + BF16_ROWS
FFN_BLOCKS_PER_STEP = 2
COMBINE_GROUPS = 4


def _params(semantics):
    return pltpu.CompilerParams(dimension_semantics=semantics, vmem_limit_bytes=VMEM_LIMIT)


def _dot(a, b):
    return jnp.dot(a, b, preferred_element_type=f32)


def _dot_nt(a, b):
    return lax.dot_general(a, b, (((1,), (1,)), ((), ())), preferred_element_type=f32)


def _dot_tn(a, b):
    return lax.dot_general(a, b, (((0,), (0,)), ((), ())), preferred_element_type=f32)


def _split3(a):
    hi = a.astype(bf16)
    r1 = a - hi.astype(f32)
    mid = r1.astype(bf16)
    lo = (r1 - mid.astype(f32)).astype(bf16)
    return hi, mid, lo


def _dot_f32_lhs(a, b_exact):
    hi, mid, lo = _split3(a)
    return _dot(hi, b_exact) + _dot(mid, b_exact) + _dot(lo, b_exact)


def _dot_f32_rhs(a_exact, b):
    hi, mid, lo = _split3(b)
    return _dot(a_exact, hi) + _dot(a_exact, mid) + _dot(a_exact, lo)


def _softplus(x):
    return jnp.maximum(x, 0.0) + jnp.log1p(jnp.exp(-jnp.abs(x)))


def _sigmoid(x):
    return 1.0 / (1.0 + jnp.exp(-x))


def _layer_norm(y, g, b):
    mu = jnp.mean(y, axis=-1, keepdims=True)
    yc = y - mu
    var = jnp.mean(yc * yc, axis=-1, keepdims=True)
    return yc * lax.rsqrt(var + LN_EPS) * g + b


def _lambda(lq1_ref, lk1_ref, lq2_ref, lk2_ref, lam_init):
    s1 = jnp.sum(lq1_ref[...] * lk1_ref[...], axis=-1, keepdims=True)
    s2 = jnp.sum(lq2_ref[...] * lk2_ref[...], axis=-1, keepdims=True)
    return jnp.exp(s1) - jnp.exp(s2) + lam_init


def _in_proj_kernel(x_ref, wk_ref, wv_ref, wz_ref, wx_ref, wdt_ref, wdtT_ref, wq_ref, wkT_ref, wvT_ref, *outs,
                    prompt):
    xb = x_ref[...].astype(bf16)
    z_ref, xbc_ref, dt_ref, dtT_ref = outs[:4]
    k = _dot(xb, wk_ref[...])
    z_ref[...] = _dot(xb, wz_ref[...])
    xbc_ref[...] = _dot(xb, wx_ref[...])
    dt_ref[...] = _dot(xb, wdt_ref[...])
    dtT_ref[...] = _dot_nt(wdtT_ref[...], xb)
    scale = DQK ** -0.5
    if prompt:
        kT_ref, vT_ref, kb_ref, qT_ref, v1T_ref = outs[4:]
        kb_ref[...] = k.astype(bf16)
        kT_ref[0] = _dot_nt(wkT_ref[...], xb)
        qT_ref[...] = (_dot_nt(wq_ref[...], xb) * (scale * LOG2E)).astype(bf16)
        vT32 = _dot_nt(wvT_ref[...], xb)
        vT_ref[0] = vT32
        vT = vT32.astype(bf16)
        ones = jnp.ones((BF16_ROWS, vT.shape[1]), bf16)
        for h in range(H_A):
            v1T_ref[h * V_ROWS:h * V_ROWS + DV, :] = vT[h * DV:(h + 1) * DV, :]
            v1T_ref[h * V_ROWS + DV:(h + 1) * V_ROWS, :] = ones
    else:
        k_ref, v_ref, q_ref = outs[4:]
        k_ref[...] = k
        v_ref[...] = _dot(xb, wv_ref[...])
        q_ref[...] = (_dot(xb, wq_ref[...]) * scale).astype(bf16)


def _in_proj(x2d, w, prompt, B):
    T, D = x2d.shape
    tl = min(ROW_TILE, T)
    L = T // B
    n_l = L // tl
    grid = (T // tl,)
    row = lambda n: pl.BlockSpec((tl, n), lambda i: (i, 0))
    col = lambda n: pl.BlockSpec((n, tl), lambda i: (0, i))
    per_stream = lambda n: pl.BlockSpec((1, n, tl), lambda i: (i // n_l, 0, i % n_l))
    full = lambda a: pl.BlockSpec(a.shape, lambda i: (0,) * a.ndim)
    wq = w['wqT'] if prompt else w['wq']
    ins = [x2d, w['wk'], w['wv'], w['wz'], w['wx'], w['wdt'], w['wdtT'], wq, w['wkT'], w['wvT']]
    in_specs = [row(D)] + [full(a) for a in ins[1:]]
    out_shape = [jax.ShapeDtypeStruct((T, D_SSM), f32), jax.ShapeDtypeStruct((T, CONV_DIM), f32),
                 jax.ShapeDtypeStruct((T, LANES), f32), jax.ShapeDtypeStruct((BF16_ROWS, T), f32)]
    out_specs = [row(D_SSM), row(CONV_DIM), row(LANES), col(BF16_ROWS)]
    if prompt:
        assert L % tl == 0
        out_shape += [jax.ShapeDtypeStruct((B, D_QK_ALL, L), f32), jax.ShapeDtypeStruct((B, D_ATT, L), f32),
                      jax.ShapeDtypeStruct((T, D_QK_ALL), bf16), jax.ShapeDtypeStruct((D_QK_ALL, T), bf16),
                      jax.ShapeDtypeStruct((H_A * V_ROWS, T), bf16)]
        out_specs += [per_stream(D_QK_ALL), per_stream(D_ATT), row(D_QK_ALL), col(D_QK_ALL), col(H_A * V_ROWS)]
    else:
        out_shape += [jax.ShapeDtypeStruct((T, D_QK_ALL), f32), jax.ShapeDtypeStruct((T, D_ATT), f32),
                      jax.ShapeDtypeStruct((T, D_QK_ALL), bf16)]
        out_specs += [row(D_QK_ALL), row(D_ATT), row(D_QK_ALL)]
    return pl.pallas_call(
        functools.partial(_in_proj_kernel, prompt=prompt),
        grid=grid, in_specs=in_specs, out_specs=out_specs, out_shape=out_shape,
        compiler_params=_params(("parallel",)),
    )(*ins)


def _t5_bucket(rel):
    half = N_BUCKETS // 2
    max_exact = half // 2
    n = jnp.abs(rel)
    large = max_exact + (jnp.log(jnp.maximum(n, 1).astype(f32) / max_exact)
                         / math.log(MAX_DISTANCE / max_exact) * (half - max_exact)).astype(i32)
    large = jnp.minimum(large, half - 1)
    return jnp.where(rel > 0, half, 0) + jnp.where(n < max_exact, n, large)


def _far_bucket(min_dist):
    half = N_BUCKETS // 2
    max_exact = half // 2
    v = max_exact + int(math.log(min_dist / max_exact) / math.log(MAX_DISTANCE / max_exact) * (half - max_exact))
    return half - 1 if (min_dist >= max_exact and v - 1 >= half - 1) else None


def _bias_lookup(rel_bias, bucket):
    onehot = (bucket[..., None] == jnp.arange(N_BUCKETS, dtype=i32)).astype(f32)
    return jnp.dot(onehot, rel_bias.astype(f32), precision=lax.Precision.HIGHEST)


def _prompt_bias_tiles(rel_bias, ta):
    j = jnp.arange(ta, dtype=i32)[:, None]
    i = jnp.arange(ta, dtype=i32)[None, :]
    tiles = []
    for d in range(2):
        rel = j - i - d * ta
        b = _bias_lookup(rel_bias, _t5_bucket(rel))
        allowed = ((j // CHUNK) <= (i // CHUNK)) if d == 0 else jnp.ones((ta, ta), bool)
        tiles.append(jnp.where(allowed[..., None], b, NEG_INF))
    return jnp.transpose(jnp.stack(tiles), (3, 0, 1, 2))


def _sample_bias(rel_bias, past, s):
    q_pos = past + jnp.arange(s, dtype=i32)
    k_pos = jnp.arange(past + s, dtype=i32)
    rel = k_pos[None, :] - q_pos[:, None]
    b = _bias_lookup(rel_bias, _t5_bucket(rel))
    allowed = (k_pos[None, :] // CHUNK) <= (q_pos[:, None] // CHUNK)
    return jnp.transpose(jnp.where(allowed[..., None], b, NEG_INF), (2, 0, 1))


def _attn_prompt_kernel(qi_tab, ki_tab, qT_ref, k_ref, v1T_ref, bias_ref, cfar_ref,
                        lq1_ref, lk1_ref, lq2_ref, lk2_ref, g_ref, o_ref,
                        qm_sc, m_sc, acc_sc, oT_sc, *, lam_init):
    p = pl.program_id(1)
    qi = qi_tab[p]
    ki = ki_tab[p]
    d = qi - ki
    ta = k_ref.shape[0]
    n_hp = H_A // 2

    @pl.when(ki == 0)
    def _():
        m_sc[...] = jnp.full(m_sc.shape, NEG_INF, f32)
        acc_sc[...] = jnp.zeros(acc_sc.shape, f32)
        row = lax.broadcasted_iota(i32, (LANES, ta), 0)
        for hp in range(n_hp):
            qt = qT_ref[hp * LANES:(hp + 1) * LANES, :]
            for gi in range(4):
                keep = (row >= gi * DQK) & (row < (gi + 1) * DQK)
                qm_sc[hp, :, gi * ta:(gi + 1) * ta] = jnp.where(keep, qt, jnp.zeros_like(qt))

    def scores(hp):
        return _dot(k_ref[:, hp * LANES:(hp + 1) * LANES], qm_sc[hp])

    nb = ta // ATT_SUB

    def near_bias(hp, step_d):
        rows = []
        for jb in range(nb):
            cols = []
            for g in range(4):
                for ib in range(nb):
                    dd = step_d * nb + ib - jb
                    if dd < 0:
                        blk = jnp.full((ATT_SUB, ATT_SUB), NEG_INF * LOG2E, f32)
                    elif dd < 2:
                        blk = bias_ref[2 * hp + g // 2, dd]
                    else:
                        lo = g * ta + ib * ATT_SUB
                        blk = jnp.broadcast_to(cfar_ref[hp][:, lo:lo + ATT_SUB], (ATT_SUB, ATT_SUB))
                    cols.append(blk)
            rows.append(jnp.concatenate(cols, axis=1))
        return jnp.concatenate(rows, axis=0)

    def update(step_d):
        s_next = scores(0)
        for hp in range(n_hp):
            s = s_next
            if hp + 1 < n_hp:
                s_next = scores(hp + 1)
            m_old = m_sc[hp]
            if step_d is not None:
                s = s + near_bias(hp, step_d)
                m_new = jnp.maximum(m_old, jnp.max(s, axis=0, keepdims=True))
                shift = m_new
            else:
                c = cfar_ref[hp]
                m_new = jnp.maximum(m_old, jnp.max(s, axis=0, keepdims=True) + c)
                shift = m_new - c
            alpha = jnp.exp2(m_old - m_new)
            pT = jnp.exp2(s - shift).astype(bf16)
            for hh in range(2):
                h = 2 * hp + hh
                cols = slice(hh * 2 * ta, (hh + 1) * 2 * ta)
                pv = _dot(v1T_ref[h * V_ROWS:(h + 1) * V_ROWS, :], pT[:, cols])
                acc_sc[h] = alpha[:, cols] * acc_sc[h] + pv
            m_sc[hp] = m_new

    @pl.when(d == 0)
    def _():
        update(0)

    @pl.when(d == 1)
    def _():
        update(1)

    @pl.when(d >= 2)
    def _():
        update(None)

    @pl.when(d == 0)
    def _():
        lam = _lambda(lq1_ref, lk1_ref, lq2_ref, lk2_ref, lam_init)
        for h in range(H_A):
            a = acc_sc[h]
            a0 = a[:, :ta]
            a1 = a[:, ta:]
            o = a0[:DV] / a0[DV:DV + 1] - lam * (a1[:DV] / a1[DV:DV + 1])
            o = o * lax.rsqrt(jnp.mean(o * o, axis=0, keepdims=True) + RMS_EPS) * g_ref[...] * (1.0 - lam_init)
            oT_sc[h * DV:(h + 1) * DV, :] = o
        o_ref[...] = oT_sc[...].T.astype(bf16)


def _attn_prompt(qT, kb, v1T, rel_bias, lam_args, attn_g, B, L, lam_init):
    ta = min(ATT_TILE, L)
    assert L % ta == 0 and ta % ATT_SUB == 0 and ATT_SUB % CHUNK == 0
    nt = L // ta
    far = _far_bucket(ATT_SUB + 1)
    assert L <= 2 * ATT_SUB or far is not None, "key blocks two or more behind must share one bias bucket"
    far = far if far is not None else N_BUCKETS // 2 - 1
    pairs = [(q, k) for q in range(nt) for k in range(q + 1)]
    qi_tab = jnp.asarray([q for q, _ in pairs], i32)
    ki_tab = jnp.asarray([k for _, k in pairs], i32)
    bias = _prompt_bias_tiles(rel_bias, ATT_SUB) * LOG2E
    cfar = jnp.repeat((rel_bias.astype(f32)[far] * LOG2E).reshape(H_A // 2, 2), 2 * ta, axis=1)[:, None, :]
    g_col = attn_g.astype(f32).reshape(DV, 1)
    T = B * L
    n_hp = H_A // 2
    full = lambda a: pl.BlockSpec(a.shape, lambda b, p, qt, kt: (0,) * a.ndim)
    grid_spec = pltpu.PrefetchScalarGridSpec(
        num_scalar_prefetch=2, grid=(B, len(pairs)),
        in_specs=[
            pl.BlockSpec((D_QK_ALL, ta), lambda b, p, qt, kt: (0, b * nt + qt[p])),
            pl.BlockSpec((ta, D_QK_ALL), lambda b, p, qt, kt: (b * nt + kt[p], 0)),
            pl.BlockSpec((H_A * V_ROWS, ta), lambda b, p, qt, kt: (0, b * nt + kt[p])),
            full(bias), full(cfar)] + [full(a) for a in lam_args] + [full(g_col)],
        out_specs=pl.BlockSpec((ta, D_ATT), lambda b, p, qt, kt: (b * nt + qt[p], 0)),
        scratch_shapes=[pltpu.VMEM((n_hp, LANES, 4 * ta), bf16), pltpu.VMEM((n_hp, 1, 4 * ta), f32),
                        pltpu.VMEM((H_A, V_ROWS, 2 * ta), f32), pltpu.VMEM((D_ATT, ta), f32)])
    return pl.pallas_call(
        functools.partial(_attn_prompt_kernel, lam_init=lam_init),
        grid_spec=grid_spec, out_shape=jax.ShapeDtypeStruct((T, D_ATT), bf16),
        compiler_params=_params(("parallel", "arbitrary")),
    )(qi_tab, ki_tab, qT, kb, v1T, bias, cfar, *lam_args, g_col)


def _attn_sample_kernel(q_ref, kn_ref, vn_ref, kc_ref, vc_ref, bc_ref, bn_ref,
                        lq1_ref, lk1_ref, lq2_ref, lk2_ref, g2_ref, o_ref, *, lam_init):
    s_len = q_ref.shape[0]
    lane = lax.broadcasted_iota(i32, (s_len, LANES), 1)
    lam = _lambda(lq1_ref, lk1_ref, lq2_ref, lk2_ref, lam_init)
    for hp in range(H_A // 2):
        sl = slice(hp * LANES, (hp + 1) * LANES)
        qt = q_ref[:, sl]
        kct = kc_ref[0, sl, :].astype(bf16)
        vct = vc_ref[0, sl, :].astype(bf16)
        knt = kn_ref[:, sl].astype(bf16)
        vnt = vn_ref[:, sl].astype(bf16)
        outs = []
        for hh in range(2):
            h = 2 * hp + hh
            oc = []
            for c in range(2):
                gi = 2 * hh + c
                qm = jnp.where((lane >= gi * DQK) & (lane < (gi + 1) * DQK), qt, jnp.zeros_like(qt))
                s1 = _dot(qm, kct) + bc_ref[h]
                s2 = _dot_nt(qm, knt) + bn_ref[h]
                m = jnp.maximum(jnp.max(s1, axis=-1, keepdims=True), jnp.max(s2, axis=-1, keepdims=True))
                p1 = jnp.exp(s1 - m)
                p2 = jnp.exp(s2 - m)
                l = jnp.sum(p1, axis=-1, keepdims=True) + jnp.sum(p2, axis=-1, keepdims=True)
                pv = _dot_nt(p1.astype(bf16), vct) + _dot(p2.astype(bf16), vnt)
                oc.append(pv / l)
            o = oc[0] - lam * oc[1]
            in_head = (lane >= hh * DV) & (lane < (hh + 1) * DV)
            ms = jnp.sum(jnp.where(in_head, o * o, 0.0), axis=-1, keepdims=True) * (1.0 / DV)
            outs.append(o * lax.rsqrt(ms + RMS_EPS))
        ot = jnp.where(lane < DV, outs[0], outs[1]) * g2_ref[...] * (1.0 - lam_init)
        o_ref[:, sl] = ot.astype(bf16)


def _attn_sample(q, k_new, v_new, cache_k, cache_v, rel_bias, lam_args, attn_g, lam_init):
    nb, past = cache_k.shape[0], cache_k.shape[1]
    T = q.shape[0]
    s = T // nb
    kc = cache_k.transpose(0, 2, 3, 1).reshape(nb, D_QK_ALL, past)
    vc = cache_v.transpose(0, 2, 3, 1).reshape(nb, D_ATT, past)
    bias = _sample_bias(rel_bias, past, s)
    bc, bn = bias[:, :, :past], bias[:, :, past:]
    g2 = jnp.tile(attn_g.astype(f32), 2).reshape(1, LANES)
    full = lambda a: pl.BlockSpec(a.shape, lambda b: (0,) * a.ndim)
    row = lambda n: pl.BlockSpec((s, n), lambda b: (b, 0))
    return pl.pallas_call(
        functools.partial(_attn_sample_kernel, lam_init=lam_init),
        grid=(nb,),
        in_specs=[row(D_QK_ALL), row(D_QK_ALL), row(D_ATT),
                  pl.BlockSpec((1, D_QK_ALL, past), lambda b: (b, 0, 0)),
                  pl.BlockSpec((1, D_ATT, past), lambda b: (b, 0, 0)),
                  full(bc), full(bn)] + [full(a) for a in lam_args] + [full(g2)],
        out_specs=row(D_ATT), out_shape=jax.ShapeDtypeStruct((T, D_ATT), bf16),
        compiler_params=_params(("parallel",)),
    )(q, k_new, v_new, kc, vc, bc, bn, *lam_args, g2)


def _ssd_kernel(z_ref, xbc_ref, dt_ref, dtT_ref, h0_ref, c0_ref, cw_ref, cb_ref, dtb_ref, dtbT_ref,
                alog_ref, alogT_ref, dsk_ref, g_ref, y_ref, hout_ref, cout_ref, xpad_sc, h_sc):
    q = xbc_ref.shape[0]
    c = pl.program_id(1)
    gw = HEADS_PER_GROUP * SSM_HEADDIM

    @pl.when(c == 0)
    def _():
        xpad_sc[0:SUBLANES] = c0_ref[0]
        h_sc[...] = h0_ref[0]

    xpad_sc[SUBLANES:SUBLANES + q] = xbc_ref[...]
    first = SUBLANES - (CONV_W - 1)
    conv = cb_ref[...]
    for i in range(CONV_W):
        conv = conv + xpad_sc[first + i:first + i + q] * cw_ref[i:i + 1, :]
    tail = xpad_sc[q:q + SUBLANES]
    xpad_sc[0:SUBLANES] = tail
    cout_ref[0] = tail
    act = conv * _sigmoid(conv)
    xs = act[:, :D_SSM]

    lane = lax.broadcasted_iota(i32, (1, LANES), 1)
    a_row = jnp.where(lane < H_S, -jnp.exp(alog_ref[...]), 0.0)
    dt = _softplus(dt_ref[...] + dtb_ref[...])
    ii = lax.broadcasted_iota(i32, (q, q), 0)
    jj = lax.broadcasted_iota(i32, (q, q), 1)
    causal = jj <= ii
    acs = _dot_f32_rhs(causal.astype(bf16), dt * a_row)
    sub = lax.broadcasted_iota(i32, (BF16_ROWS, 1), 0)
    a_col = jnp.where(sub < H_S, -jnp.exp(alogT_ref[...]), 0.0)
    dtT = _softplus(dtT_ref[0] + dtbT_ref[...])
    acsT = _dot_f32_lhs(dtT * a_col, (ii <= jj).astype(bf16))

    er = lax.broadcasted_iota(i32, (LANES, D_SSM), 0)
    ec = lax.broadcasted_iota(i32, (LANES, D_SSM), 1)
    expand = (ec // SSM_HEADDIM == er).astype(bf16)
    dt_x = _dot_f32_lhs(dt, expand)
    acs_x = _dot_f32_lhs(acs, expand)
    e_acs = jnp.exp(acs_x)
    acs_last = acs_x[q - 1:q, :]
    decay = jnp.exp(acs_last - acs_x) * dt_x
    dtx = xs * dt_x
    xd = xs * decay
    glane = lax.broadcasted_iota(i32, (1, gw), 1)

    ys = []
    for g in range(SSM_GROUPS):
        gs = slice(g * gw, (g + 1) * gw)
        bg = act[:, D_SSM + g * D_STATE:D_SSM + (g + 1) * D_STATE].astype(bf16)
        cg = act[:, D_SSM + (SSM_GROUPS + g) * D_STATE:D_SSM + (SSM_GROUPS + g + 1) * D_STATE].astype(bf16)
        cb = _dot_nt(cg, bg)
        h_old = h_sc[g]
        yg = _dot(cg, h_old.astype(bf16)) * e_acs[:, gs]
        dtx_g = dtx[:, gs]
        for e4 in range(HEADS_PER_GROUP):
            e = g * HEADS_PER_GROUP + e4
            seg = acs[:, e:e + 1] - acsT[e:e + 1, :]
            lmat = jnp.exp(jnp.where(causal, seg, -jnp.inf))
            rhs = jnp.where(glane // SSM_HEADDIM == e4, dtx_g, 0.0).astype(bf16)
            yg = yg + _dot((cb * lmat).astype(bf16), rhs)
        h_sc[g] = h_old * jnp.exp(acs_last[:, gs]) + _dot_tn(bg, xd[:, gs].astype(bf16))
        ys.append(yg)
    y = jnp.concatenate(ys, axis=1) + dsk_ref[...] * xs
    zz = z_ref[...]
    y = y * (zz * _sigmoid(zz))
    for g in range(SSM_GROUPS):
        gs = slice(g * gw, (g + 1) * gw)
        yg = y[:, gs]
        r = lax.rsqrt(jnp.mean(yg * yg, axis=-1, keepdims=True) + RMS_EPS)
        y_ref[:, gs] = (yg * r * g_ref[:, gs]).astype(bf16)
    hout_ref[0] = h_sc[...]


def _ssd(z, xbc, dt, dtT, h0, conv0, pw, B, L):
    q = min(SSD_TILE, L)
    nc = L // q
    gw = HEADS_PER_GROUP * SSM_HEADDIM
    dtT3 = dtT.reshape(BF16_ROWS, B, L).transpose(1, 0, 2)
    hT0 = h0.astype(f32).reshape(B, SSM_GROUPS, gw, D_STATE).transpose(0, 1, 3, 2)
    c0 = jnp.pad(conv0.astype(f32), ((0, 0), (SUBLANES - (CONV_W - 1), 0), (0, 0)))
    full = lambda a: pl.BlockSpec(a.shape, lambda b, c: (0,) * a.ndim)
    row = lambda n: pl.BlockSpec((q, n), lambda b, c: (b * nc + c, 0))
    params = [pw['conv_w'], pw['conv_b'], pw['dt_bias'], pw['dt_biasT'], pw['a_log'], pw['a_logT'],
              pw['d_skip'], pw['ssm_g']]
    y, hT, ctail = pl.pallas_call(
        _ssd_kernel, grid=(B, nc),
        in_specs=[row(D_SSM), row(CONV_DIM), row(LANES),
                  pl.BlockSpec((1, BF16_ROWS, q), lambda b, c: (b, 0, c)),
                  pl.BlockSpec((1, SSM_GROUPS, D_STATE, gw), lambda b, c: (b, 0, 0, 0)),
                  pl.BlockSpec((1, SUBLANES, CONV_DIM), lambda b, c: (b, 0, 0))] + [full(a) for a in params],
        out_specs=[row(D_SSM),
                   pl.BlockSpec((1, SSM_GROUPS, D_STATE, gw), lambda b, c: (b, 0, 0, 0)),
                   pl.BlockSpec((1, SUBLANES, CONV_DIM), lambda b, c: (b, 0, 0))],
        out_shape=[jax.ShapeDtypeStruct((B * L, D_SSM), bf16),
                   jax.ShapeDtypeStruct((B, SSM_GROUPS, D_STATE, gw), f32),
                   jax.ShapeDtypeStruct((B, SUBLANES, CONV_DIM), f32)],
        scratch_shapes=[pltpu.VMEM((q + SUBLANES, CONV_DIM), f32), pltpu.VMEM((SSM_GROUPS, D_STATE, gw), f32)],
        compiler_params=_params(("parallel", "arbitrary")),
    )(z, xbc, dt, dtT3, hT0, c0, *params)
    h_new = hT.transpose(0, 1, 3, 2).reshape(B, H_S, SSM_HEADDIM, D_STATE)
    return y, h_new, ctail[:, SUBLANES - (CONV_W - 1):, :]


def _pack_bf16_pairs(x):
    half = x.shape[1] // 2
    bits = lambda a: lax.bitcast_convert_type(a.astype(bf16).astype(f32), jnp.uint32)
    return (bits(x[:, half:]) & jnp.uint32(0xFFFF0000)) | (bits(x[:, :half]) >> 16)


def _unpack_bf16_pairs(word):
    lo = lax.bitcast_convert_type(word << 16, f32)
    hi = lax.bitcast_convert_type(word & jnp.uint32(0xFFFF0000), f32)
    return jnp.concatenate([lo, hi], axis=1)


def _mix_router_kernel(att_ref, ssm_ref, x_ref, woa_ref, wos_ref, g_ref, b_ref, wrh_ref, wrhl_ref, br_ref,
                       cnt0_ref, *rest):
    x1_ref, x1p_ref, te_ref, gate_ref, cnt_ref, run_sc = rest[-6:]

    @pl.when(pl.program_id(0) == 0)
    def _():
        run_sc[...] = cnt0_ref[...]

    mix = _dot(att_ref[...], woa_ref[...]) + _dot(ssm_ref[...], wos_ref[...])
    x1 = _layer_norm(DEEPNORM_ALPHA * x_ref[...] + mix, g_ref[...], b_ref[...])
    x1_ref[...] = x1
    x1p_ref[...] = _pack_bf16_pairs(x1)
    hi = x1.astype(bf16)
    lo = (x1 - hi.astype(f32)).astype(bf16)
    both = _dot(hi, wrhl_ref[...])
    logits = both[:, :LANES] + both[:, LANES:] + _dot(lo, wrh_ref[...]) + br_ref[...]
    lane =lax.broadcasted_iota(i32, logits.shape, 1)
    vals, idxs = [], []
    for _ in range(TOP_K):
        m = jnp.max(logits, axis=-1, keepdims=True)
        idx = jnp.min(jnp.where(logits == m, lane, LANES), axis=-1, keepdims=True)
        vals.append(m)
        idxs.append(idx)
        logits = jnp.where(lane == idx, -jnp.inf, logits)
    es = [jnp.exp(v - vals[0]) for v in vals]
    tot = es[0]
    for e in es[1:]:
        tot = tot + e
    tl = logits.shape[0]
    chosen = jnp.zeros(logits.shape, f32)
    for k in range(TOP_K):
        chosen = chosen + (lane == idxs[k]).astype(f32)
    ii = lax.broadcasted_iota(i32, (tl, tl), 0)
    jj = lax.broadcasted_iota(i32, (tl, tl), 1)
    before = _dot((jj < ii).astype(bf16), chosen.astype(bf16)) + run_sc[...]
    te = jnp.zeros(logits.shape, i32)
    gate = jnp.zeros(logits.shape, f32)
    for k in range(TOP_K):
        rank = jnp.sum(jnp.where(lane == idxs[k], before, 0.0), axis=-1, keepdims=True).astype(i32)
        te = jnp.where(lane == k, idxs[k], te)
        te = jnp.where(lane == TOP_K + k, rank, te)
        gate = jnp.where(lane == k, es[k] / tot, gate)
    te_ref[...] = te
    gate_ref[...] = gate
    run_sc[...] = run_sc[...] + jnp.sum(chosen, axis=0, keepdims=True)
    cnt_ref[...] = run_sc[...]


def _mix_router(att, ssm, x2d, w, counts0, packed, row0, rows_total):
    T, D = x2d.shape
    tl = min(ROW_TILE, T)
    assert row0 % tl == 0
    row = lambda n: pl.BlockSpec((tl, n), lambda i: (i, 0))
    full = lambda a: pl.BlockSpec(a.shape, lambda i: (0,) * a.ndim)
    ws = [w['wo_att'], w['wo_ssm'], w['ln1_g'], w['ln1_b'], w['wr_hi'], w['wr_hi_lo'], w['b_router'], counts0]
    ins = [att, ssm, x2d] + ws
    in_specs = [row(D_ATT), row(D_SSM), row(D)] + [full(a) for a in ws]
    aliases = {}
    if packed is not None:
        ins.append(packed)
        in_specs.append(pl.BlockSpec(memory_space=pl.ANY))
        aliases = {len(ins) - 1: 1}
    return pl.pallas_call(
        _mix_router_kernel, grid=(T // tl,), in_specs=in_specs,
        out_specs=[row(D), pl.BlockSpec((tl, D // 2), lambda i: (row0 // tl + i, 0)), row(LANES), row(LANES),
                   pl.BlockSpec((1, LANES), lambda i: (0, 0))],
        out_shape=[jax.ShapeDtypeStruct((T, D), f32), jax.ShapeDtypeStruct((rows_total, D // 2), jnp.uint32),
                   jax.ShapeDtypeStruct((T, LANES), i32),
                   jax.ShapeDtypeStruct((T, LANES), f32), jax.ShapeDtypeStruct((1, LANES), f32)],
        scratch_shapes=[pltpu.VMEM((1, LANES), f32)], input_output_aliases=aliases,
        compiler_params=_params(("arbitrary",)),
    )(*ins)


def _route(te, counts_f):
    T = te.shape[0]
    n_assign = T * TOP_K
    counts = counts_f[0, :N_EXPERTS].astype(i32)
    padded = (counts + MOE_BLOCK - 1) // MOE_BLOCK * MOE_BLOCK
    pad_end = jnp.cumsum(padded)
    pad_start = pad_end - padded
    top_e, rank = te[:, :TOP_K], te[:, TOP_K:2 * TOP_K]
    experts = jnp.arange(N_EXPERTS, dtype=i32)
    slot = rank + jnp.sum(jnp.where(top_e[..., None] == experts, pad_start, 0), axis=-1)
    n_blocks = -(-n_assign // MOE_BLOCK) + N_EXPERTS
    n_blocks += n_blocks % FFN_BLOCKS_PER_STEP
    block_start = jnp.arange(n_blocks, dtype=i32) * MOE_BLOCK
    block_e = jnp.minimum(jnp.sum(block_start[:, None] >= pad_end[None, :], axis=-1), N_EXPERTS - 1).astype(i32)
    n_used = (pad_end[-1] // MOE_BLOCK).astype(i32).reshape(1)
    row_end = jnp.sum(jnp.where(block_e[:, None] == experts, pad_start + counts, 0), axis=-1)
    rows_valid = jnp.clip(row_end - block_start, 0, MOE_BLOCK).astype(i32)
    return block_e, n_used, rows_valid, slot.astype(i32), n_blocks


SC_INDEX_WINDOW = 128
SC_ROWS = 32


def _sc_mesh():
    return plsc.VectorSubcoreMesh(core_axis_name="c", subcore_axis_name="s")


def _sc_move_rows(src, src_idx, dst_idx_list, n_out):
    M = src_idx.shape[0]
    D = src.shape[1]
    idx = [a.reshape(1, M) for a in [src_idx] + list(dst_idx_list)]

    @functools.partial(pl.kernel, out_type=jax.ShapeDtypeStruct((n_out, D), src.dtype), mesh=_sc_mesh(),
                       scratch_types=[pltpu.VMEM((2, SC_ROWS, D), src.dtype), pltpu.SemaphoreType.DMA((2,))])
    def move(s_hbm, *rest):
        i_hbm, o_hbm, buf, sem = rest[:-3], rest[-3], rest[-2], rest[-1]
        n_parts = SC_INDEX_WINDOW // SC_ROWS

        def body(si_vmem, *di_vmem):
            def fetch(j):
                part = pl.ds(j * SC_ROWS, SC_ROWS)
                return pltpu.async_copy(s_hbm.at[si_vmem.at[0, part]], buf.at[j % 2], sem.at[j % 2])

            pending = fetch(0)
            for j in range(n_parts):
                nxt = fetch(j + 1) if j + 1 < n_parts else None
                pending.wait()
                part = pl.ds(j * SC_ROWS, SC_ROWS)
                for dv in di_vmem:
                    pltpu.sync_copy(buf.at[j % 2], o_hbm.at[dv.at[0, part]])
                pending = nxt

        pltpu.emit_pipeline(
            body, grid=(M // SC_INDEX_WINDOW,),
            in_specs=[pl.BlockSpec((1, SC_INDEX_WINDOW), lambda i: (0, i))] * len(idx),
            out_specs=[], core_axis_name=("c", "s"), dimension_semantics=(pltpu.PARALLEL,),
        )(*i_hbm)

    return move(src, *idx)


def _sc_scatter_rows(x, idx_k, n_rows):
    return _sc_move_rows(x, jnp.arange(x.shape[0], dtype=i32), idx_k, n_rows)


def _sc_gather_rows(src, idx):
    M = idx.shape[0]
    return _sc_move_rows(src, idx, [jnp.arange(M, dtype=i32)], M)


def _deinterleave_kernel(w_ref, g_ref, u_ref):
    tn = 2 * LANES
    r = lax.broadcasted_iota(i32, (2 * tn, tn), 0)
    c = lax.broadcasted_iota(i32, (2 * tn, tn), 1)
    pick_g = (r == 2 * c).astype(bf16)
    pick_u = (r == 2 * c + 1).astype(bf16)
    for j in range(g_ref.shape[-1] // tn):
        wb = w_ref[0, :, j * 2 * tn:(j + 1) * 2 * tn].astype(bf16)
        g_ref[0, :, j * tn:(j + 1) * tn] = _dot(wb, pick_g).astype(bf16)
        u_ref[0, :, j * tn:(j + 1) * tn] = _dot(wb, pick_u).astype(bf16)


def _deinterleave(w_gu):
    E, D, F2 = w_gu.shape
    out = jax.ShapeDtypeStruct((E, D, F2 // 2), bf16)
    return pl.pallas_call(
        _deinterleave_kernel, grid=(E,),
        in_specs=[pl.BlockSpec((1, D, F2), lambda e: (e, 0, 0))],
        out_specs=[pl.BlockSpec((1, D, F2 // 2), lambda e: (e, 0, 0))] * 2,
        out_shape=[out, out],
        compiler_params=_params(("parallel",)),
    )(w_gu)


def _ffn_kernel(be_ref, nu_ref, rv_ref, xs_ref, *refs):
    del be_ref
    y_ref = refs[-1]
    wsets = [refs[6 * j:6 * j + 6] for j in range(FFN_BLOCKS_PER_STEP)]
    first = pl.program_id(0) * FFN_BLOCKS_PER_STEP
    n_live = jnp.clip(nu_ref[0] - first, 0, FFN_BLOCKS_PER_STEP)
    row_id = lax.broadcasted_iota(i32, (MOE_BLOCK, 1), 0)

    def block(j):
        wg_ref, wu_ref, bg_ref, bu_ref, wd_ref, bd_ref = wsets[j]
        rows = pl.ds(j * MOE_BLOCK, MOE_BLOCK)
        words = jnp.where(row_id < rv_ref[first + j], xs_ref[rows, :], jnp.uint32(0))
        xb = _unpack_bf16_pairs(words).astype(bf16)
        g = _dot(xb, wg_ref[0]) + bg_ref[0]
        u = _dot(xb, wu_ref[0]) + bu_ref[0]
        g = jnp.minimum(g, SWIGLU_LIMIT)
        u = jnp.clip(u, -SWIGLU_LIMIT, SWIGLU_LIMIT)
        act = (u + 1.0) * g * _sigmoid(SWIGLU_ALPHA * g)
        y_ref[rows, :] = _pack_bf16_pairs(_dot(act.astype(bf16), wd_ref[0]) + bd_ref[0])

    for live in range(FFN_BLOCKS_PER_STEP + 1):
        @pl.when(n_live == live)
        def _(live=live):
            for j in range(live):
                block(j)
            for j in range(live, FFN_BLOCKS_PER_STEP):
                y_ref[pl.ds(j * MOE_BLOCK, MOE_BLOCK), :] = jnp.zeros((MOE_BLOCK, y_ref.shape[1]), jnp.uint32)


def _expert_ffn(xs, block_e, n_used, rows_valid, w):
    n_rows, half = xs.shape
    D = 2 * half
    nps = FFN_BLOCKS_PER_STEP
    n_steps = n_rows // (MOE_BLOCK * nps)
    F = w['w_g'].shape[2]
    wspecs, wargs = [], []
    for j in range(nps):
        pick = lambda i, be, nu, rv, j=j: (be[i * nps + j], 0, 0)
        wspecs += [pl.BlockSpec((1, D, F), pick), pl.BlockSpec((1, D, F), pick), pl.BlockSpec((1, 1, F), pick),
                   pl.BlockSpec((1, 1, F), pick), pl.BlockSpec((1, F, D), pick), pl.BlockSpec((1, 1, D), pick)]
        wargs += [w['w_g'], w['w_u'], w['b_g'], w['b_u'], w['w_d'], w['b_d']]
    grid_spec = pltpu.PrefetchScalarGridSpec(
        num_scalar_prefetch=3, grid=(n_steps,),
        in_specs=[pl.BlockSpec((nps * MOE_BLOCK, half),
                               lambda i, be, nu, rv: (jnp.minimum(i, (nu[0] - 1) // nps), 0))] + wspecs,
        out_specs=pl.BlockSpec((nps * MOE_BLOCK, half), lambda i, be, nu, rv: (i, 0)))
    return pl.pallas_call(
        _ffn_kernel, grid_spec=grid_spec,
        out_shape=jax.ShapeDtypeStruct((n_rows, half), jnp.uint32),
        compiler_params=_params(("arbitrary",)),
    )(block_e, n_used, rows_valid, xs, *wargs)


def _combine_dense_kernel(rows_ref, x1_ref, gate_ref, g_ref, b_ref, *rest):
    y_ref = rest[-1]
    gate = gate_ref[...]
    ff = gate[:, 0:1] * _unpack_bf16_pairs(rows_ref[0])
    for k in range(1, TOP_K):
        ff = ff + gate[:, k:k + 1] * _unpack_bf16_pairs(rows_ref[k])
    y_ref[...] = _layer_norm(DEEPNORM_ALPHA * x1_ref[...] + ff, g_ref[...], b_ref[...])


def _combine_dense(rows, x1, gate, ln_g, ln_b, chunk, y_partial):
    T, D = x1.shape
    tc = rows.shape[1]
    tl = min(ROW_TILE, tc)
    first = chunk * (tc // tl)
    row = lambda n: pl.BlockSpec((tl, n), lambda i: (first + i, 0))
    full = lambda a: pl.BlockSpec(a.shape, lambda i: (0,) * a.ndim)
    ins = [rows, x1, gate, ln_g, ln_b]
    in_specs = [pl.BlockSpec((TOP_K, tl, D // 2), lambda i: (0, i, 0)), row(D), row(LANES), full(ln_g), full(ln_b)]
    aliases = {}
    if y_partial is not None:
        ins.append(y_partial)
        in_specs.append(pl.BlockSpec(memory_space=pl.ANY))
        aliases = {len(ins) - 1: 0}
    return pl.pallas_call(
        _combine_dense_kernel, grid=(tc // tl,), in_specs=in_specs,
        out_specs=row(D), out_shape=jax.ShapeDtypeStruct((T, D), f32), input_output_aliases=aliases,
        compiler_params=_params(("parallel",)),
    )(*ins)


def _prep_weights(l, w_in, conv_w, conv_b, dt_bias, a_log, d_skip, ssm_norm_g, w_o, ln1_g, ln1_b,
                  w_router, b_router, w_gate_up, b_gate_up, w_down, b_down, ln2_g, ln2_b):
    wi = w_in[l]
    c0, c1, c2, c3 = D_QK_ALL, 2 * D_QK_ALL, 2 * D_QK_ALL + D_ATT, 2 * D_QK_ALL + D_ATT + D_SSM
    c4 = c3 + CONV_DIM
    wdt = wi[:, c4:c4 + H_S]
    pad_lane = lambda v, fill=0.0: jnp.pad(v.astype(f32).reshape(1, -1), ((0, 0), (0, LANES - v.shape[-1])),
                                            constant_values=fill)
    pad_col = lambda v: jnp.pad(v.astype(f32).reshape(-1, 1), ((0, BF16_ROWS - v.shape[-1]), (0, 0)))
    wr = jnp.pad(w_router[l].astype(f32), ((0, 0), (0, LANES - N_EXPERTS)))
    wr_hi = wr.astype(bf16)
    wgu = w_gate_up[l]
    return {
        'wq': wi[:, :c0].astype(bf16), 'wqT': wi[:, :c0].T.astype(bf16),
        'wk': wi[:, c0:c1].astype(bf16), 'wkT': wi[:, c0:c1].T.astype(bf16), 'wv': wi[:, c1:c2].astype(bf16), 'wvT': wi[:, c1:c2].T.astype(bf16),
        'wz': wi[:, c2:c3].astype(bf16), 'wx': wi[:, c3:c4].astype(bf16),
        'wdt': jnp.pad(wdt, ((0, 0), (0, LANES - H_S))).astype(bf16),
        'wdtT': jnp.pad(wdt.T, ((0, BF16_ROWS - H_S), (0, 0))).astype(bf16),
        'conv_w': conv_w[l].astype(f32), 'conv_b': conv_b[l].astype(f32).reshape(1, -1),
        'dt_bias': pad_lane(dt_bias[l]), 'dt_biasT': pad_col(dt_bias[l]),
        'a_log': pad_lane(a_log[l]), 'a_logT': pad_col(a_log[l]),
        'd_skip': jnp.repeat(d_skip[l].astype(f32), SSM_HEADDIM).reshape(1, -1),
        'ssm_g': ssm_norm_g[l].astype(f32).reshape(1, -1),
        'wo_att': w_o[l][:D_ATT].astype(bf16), 'wo_ssm': w_o[l][D_ATT:].astype(bf16),
        'ln1_g': ln1_g[l].astype(f32).reshape(1, -1), 'ln1_b': ln1_b[l].astype(f32).reshape(1, -1),
        'wr_hi': wr_hi, 'wr_hi_lo': jnp.concatenate([wr_hi, (wr - wr_hi.astype(f32)).astype(bf16)], axis=1),
        'b_router': pad_lane(b_router[l], NEG_INF),
        'w_gu': wgu,
        'b_g': b_gate_up[l][:, None, 0::2].astype(f32), 'b_u': b_gate_up[l][:, None, 1::2].astype(f32),
        'w_d': w_down[l].astype(bf16), 'b_d': b_down[l][:, None, :].astype(f32),
        'ln2_g': ln2_g[l].astype(f32).reshape(1, -1), 'ln2_b': ln2_b[l].astype(f32).reshape(1, -1),
    }


def _mixers(x, l, w, rel_bias, lam_args, attn_g, k_past, v_past, h0, conv0):
    B, L, D = x.shape
    x2d = x.reshape(B * L, D)
    lam_init = 0.8 - 0.6 * math.exp(-0.3 * l)
    prompt = k_past is None
    proj = _in_proj(x2d, w, prompt, B)
    z, xbc, dt, dtT = proj[:4]
    if prompt:
        kT, vT, kb, qT, v1T = proj[4:]
        att = _attn_prompt(qT, kb, v1T, rel_bias, lam_args, attn_g, B, L, lam_init)
        k_rows = kT.reshape(B, H_A, 2 * DQK, L).transpose(0, 3, 1, 2)
        v_rows = vT.reshape(B, H_A, DV, L).transpose(0, 3, 1, 2)
    else:
        k, v, q = proj[4:]
        att = _attn_sample(q, k, v, k_past, v_past, rel_bias, lam_args, attn_g, lam_init)
        k_rows = k.reshape(B, L, H_A, 2 * DQK)
        v_rows = v.reshape(B, L, H_A, DV)
    ssm, h_new, conv_new = _ssd(z, xbc, dt, dtT, h0, conv0, w, B, L)
    return (att, ssm, x2d), (k_rows, v_rows, h_new, conv_new)


def _moe_layer(streams, w):
    rows_total = sum(x2d.shape[0] for _, _, x2d in streams)
    counts = jnp.zeros((1, LANES), f32)
    packed, row0, routed = None, 0, []
    for att, ssm, x2d in streams:
        x1, packed, te, gate, counts = _mix_router(att, ssm, x2d, w, counts, packed, row0, rows_total)
        routed.append((x1, gate, row0))
        row0 += x2d.shape[0]
        te_all = te[:, :2 * TOP_K] if len(routed) == 1 else jnp.concatenate([te_all, te[:, :2 * TOP_K]])
    block_e, n_used, rows_valid, slot, n_blocks = _route(te_all, counts)
    slot_k = slot.T
    xs = _sc_scatter_rows(packed, [slot_k[k] for k in range(TOP_K)], n_blocks * MOE_BLOCK)
    ys = _expert_ffn(xs, block_e, n_used, rows_valid, w)
    outs = []
    for x1, gate, r0 in routed:
        T, D = x1.shape
        n_groups = COMBINE_GROUPS if T % (COMBINE_GROUPS * ROW_TILE) == 0 else 1
        tc = T // n_groups
        y = None
        for c in range(n_groups):
            idx = slot_k[:, r0 + c * tc:r0 + (c + 1) * tc].reshape(-1)
            rows = _sc_gather_rows(ys, idx).reshape(TOP_K, tc, D // 2)
            y = _combine_dense(rows, x1, gate, w['ln2_g'], w['ln2_b'], c, y)
        outs.append(y)
    return outs


def kernel(x_prompt, x_sample, cache_k, cache_v, state_ssm, state_conv, rel_bias, w_in, lambda_q1, lambda_k1, lambda_q2, lambda_k2, attn_norm_g, conv_w, conv_b, dt_bias, a_log, d_skip, ssm_norm_g, w_o, ln1_g, ln1_b, w_router, b_router, w_gate_up, b_gate_up, w_down, b_down, ln2_g, ln2_b):
    yp, ys = x_prompt, x_sample
    bp = x_prompt.shape[0]
    depth = w_in.shape[0]
    outs = [[] for _ in range(8)]
    for l in range(depth):
        w = _prep_weights(l, w_in, conv_w, conv_b, dt_bias, a_log, d_skip, ssm_norm_g, w_o, ln1_g, ln1_b,
                          w_router, b_router, w_gate_up, b_gate_up, w_down, b_down, ln2_g, ln2_b)
        w['w_g'], w['w_u'] = _deinterleave(w.pop('w_gu'))
        lam_args = [a[l].astype(f32).reshape(1, -1) for a in (lambda_q1, lambda_k1, lambda_q2, lambda_k2)]
        h0 = jnp.zeros((bp, H_S, SSM_HEADDIM, D_STATE), f32)
        c0 = jnp.zeros((bp, CONV_W - 1, CONV_DIM), f32)
        mix_p, state_p = _mixers(yp, l, w, rel_bias, lam_args, attn_norm_g[l], None, None, h0, c0)
        mix_s, state_s = _mixers(ys, l, w, rel_bias, lam_args, attn_norm_g[l], cache_k[l], cache_v[l],
                                 state_ssm[l], state_conv[l])
        y2p, y2s = _moe_layer([mix_p, mix_s], w)
        yp, ys = y2p.reshape(yp.shape), y2s.reshape(ys.shape)
        for lst, a in zip(outs, state_p + state_s):
            lst.append(a)
    return (yp, ys) + tuple(jnp.stack(o) for o in outs)
```

```python
import functools
import math

import numpy as np
import jax
import jax.numpy as jnp
from jax import lax
from jax.experimental import pallas as pl
from jax.experimental.pallas import tpu as pltpu
from jax.experimental.pallas import tpu_sc as plsc

f32 = jnp.float32
bf16 = jnp.bfloat16
i32 = jnp.int32

CHUNK = 64
H_A = 8
DQK = 32
DV = 2 * DQK
D_ATT = H_A * DV
SSM_HEADDIM = 64
H_S = 8
D_SSM = H_S * SSM_HEADDIM
SSM_GROUPS = 2
HEADS_PER_GROUP = H_S // SSM_GROUPS
D_STATE = 128
CONV_W = 4
CONV_DIM = D_SSM + 2 * SSM_GROUPS * D_STATE
D_QK_ALL = H_A * 2 * DQK
N_BUCKETS = 32
MAX_DISTANCE = 128
N_EXPERTS = 32
TOP_K = 4
SWIGLU_LIMIT = 7.0
SWIGLU_ALPHA = 1.702
MOE_BLOCK = 256
LN_EPS = 1e-5
RMS_EPS = 1e-5
NEG_INF = -1e30
DEPTH = 1
DEEPNORM_ALPHA = (2.0 * DEPTH) ** 0.25
LOG2E = math.log2(math.e)

LANES = 128
SUBLANES = 8
BF16_ROWS = 16
VMEM_LIMIT = 48 * 1024 * 1024

ROW_TILE = 512
ATT_TILE = 512
ATT_SUB = 256
SSD_TILE = 256
V_ROWS = DV + BF16_ROWS
FFN_BLOCKS_PER_STEP = 2
COMBINE_GROUPS = 4


def _params(semantics):
    return pltpu.CompilerParams(dimension_semantics=semantics, vmem_limit_bytes=VMEM_LIMIT)


def _dot(a, b):
    return jnp.dot(a, b, preferred_element_type=f32)


def _dot_nt(a, b):
    return lax.dot_general(a, b, (((1,), (1,)), ((), ())), preferred_element_type=f32)


def _dot_tn(a, b):
    return lax.dot_general(a, b, (((0,), (0,)), ((), ())), preferred_element_type=f32)


def _split3(a):
    hi = a.astype(bf16)
    r1 = a - hi.astype(f32)
    mid = r1.astype(bf16)
    lo = (r1 - mid.astype(f32)).astype(bf16)
    return hi, mid, lo


def _dot_f32_lhs(a, b_exact):
    hi, mid, lo = _split3(a)
    return _dot(hi, b_exact) + _dot(mid, b_exact) + _dot(lo, b_exact)


def _dot_f32_rhs(a_exact, b):
    hi, mid, lo = _split3(b)
    return _dot(a_exact, hi) + _dot(a_exact, mid) + _dot(a_exact, lo)


def _softplus(x):
    return jnp.maximum(x, 0.0) + jnp.log1p(jnp.exp(-jnp.abs(x)))


def _sigmoid(x):
    return 1.0 / (1.0 + jnp.exp(-x))


def _layer_norm(y, g, b):
    mu = jnp.mean(y, axis=-1, keepdims=True)
    yc = y - mu
    var = jnp.mean(yc * yc, axis=-1, keepdims=True)
    return yc * lax.rsqrt(var + LN_EPS) * g + b


def _lambda(lq1_ref, lk1_ref, lq2_ref, lk2_ref, lam_init):
    s1 = jnp.sum(lq1_ref[...] * lk1_ref[...], axis=-1, keepdims=True)
    s2 = jnp.sum(lq2_ref[...] * lk2_ref[...], axis=-1, keepdims=True)
    return jnp.exp(s1) - jnp.exp(s2) + lam_init


def _in_proj_kernel(x_ref, wk_ref, wv_ref, wz_ref, wx_ref, wdt_ref, wdtT_ref, wq_ref, wkT_ref, wvT_ref, *outs,
                    prompt):
    xb = x_ref[...].astype(bf16)
    z_ref, xbc_ref, dt_ref, dtT_ref = outs[:4]
    k = _dot(xb, wk_ref[...])
    z_ref[...] = _dot(xb, wz_ref[...])
    xbc_ref[...] = _dot(xb, wx_ref[...])
    dt_ref[...] = _dot(xb, wdt_ref[...])
    dtT_ref[...] = _dot_nt(wdtT_ref[...], xb)
    scale = DQK ** -0.5
    if prompt:
        kT_ref, vT_ref, kb_ref, qT_ref, v1T_ref = outs[4:]
        kb_ref[...] = k.astype(bf16)
        kT_ref[0] = _dot_nt(wkT_ref[...], xb)
        qT_ref[...] = (_dot_nt(wq_ref[...], xb) * (scale * LOG2E)).astype(bf16)
        vT32 = _dot_nt(wvT_ref[...], xb)
        vT_ref[0] = vT32
        vT = vT32.astype(bf16)
        ones = jnp.ones((BF16_ROWS, vT.shape[1]), bf16)
        for h in range(H_A):
            v1T_ref[h * V_ROWS:h * V_ROWS + DV, :] = vT[h * DV:(h + 1) * DV, :]
            v1T_ref[h * V_ROWS + DV:(h + 1) * V_ROWS, :] = ones
    else:
        k_ref, v_ref, q_ref = outs[4:]
        k_ref[...] = k
        v_ref[...] = _dot(xb, wv_ref[...])
        q_ref[...] = (_dot(xb, wq_ref[...]) * scale).astype(bf16)


def _in_proj(x2d, w, prompt, B):
    T, D = x2d.shape
    tl = min(ROW_TILE, T)
    L = T // B
    n_l = L // tl
    grid = (T // tl,)
    row = lambda n: pl.BlockSpec((tl, n), lambda i: (i, 0))
    col = lambda n: pl.BlockSpec((n, tl), lambda i: (0, i))
    per_stream = lambda n: pl.BlockSpec((1, n, tl), lambda i: (i // n_l, 0, i % n_l))
    full = lambda a: pl.BlockSpec(a.shape, lambda i: (0,) * a.ndim)
    wq = w['wqT'] if prompt else w['wq']
    ins = [x2d, w['wk'], w['wv'], w['wz'], w['wx'], w['wdt'], w['wdtT'], wq, w['wkT'], w['wvT']]
    in_specs = [row(D)] + [full(a) for a in ins[1:]]
    out_shape = [jax.ShapeDtypeStruct((T, D_SSM), f32), jax.ShapeDtypeStruct((T, CONV_DIM), f32),
                 jax.ShapeDtypeStruct((T, LANES), f32), jax.ShapeDtypeStruct((BF16_ROWS, T), f32)]
    out_specs = [row(D_SSM), row(CONV_DIM), row(LANES), col(BF16_ROWS)]
    if prompt:
        assert L % tl == 0
        out_shape += [jax.ShapeDtypeStruct((B, D_QK_ALL, L), f32), jax.ShapeDtypeStruct((B, D_ATT, L), f32),
                      jax.ShapeDtypeStruct((T, D_QK_ALL), bf16), jax.ShapeDtypeStruct((D_QK_ALL, T), bf16),
                      jax.ShapeDtypeStruct((H_A * V_ROWS, T), bf16)]
        out_specs += [per_stream(D_QK_ALL), per_stream(D_ATT), row(D_QK_ALL), col(D_QK_ALL), col(H_A * V_ROWS)]
    else:
        out_shape += [jax.ShapeDtypeStruct((T, D_QK_ALL), f32), jax.ShapeDtypeStruct((T, D_ATT), f32),
                      jax.ShapeDtypeStruct((T, D_QK_ALL), bf16)]
        out_specs += [row(D_QK_ALL), row(D_ATT), row(D_QK_ALL)]
    return pl.pallas_call(
        functools.partial(_in_proj_kernel, prompt=prompt),
        grid=grid, in_specs=in_specs, out_specs=out_specs, out_shape=out_shape,
        compiler_params=_params(("parallel",)),
    )(*ins)


def _t5_bucket(rel):
    half = N_BUCKETS // 2
    max_exact = half // 2
    n = jnp.abs(rel)
    large = max_exact + (jnp.log(jnp.maximum(n, 1).astype(f32) / max_exact)
                         / math.log(MAX_DISTANCE / max_exact) * (half - max_exact)).astype(i32)
    large = jnp.minimum(large, half - 1)
    return jnp.where(rel > 0, half, 0) + jnp.where(n < max_exact, n, large)


def _far_bucket(min_dist):
    half = N_BUCKETS // 2
    max_exact = half // 2
    v = max_exact + int(math.log(min_dist / max_exact) / math.log(MAX_DISTANCE / max_exact) * (half - max_exact))
    return half - 1 if (min_dist >= max_exact and v - 1 >= half - 1) else None


def _bias_lookup(rel_bias, bucket):
    onehot = (bucket[..., None] == jnp.arange(N_BUCKETS, dtype=i32)).astype(f32)
    return jnp.dot(onehot, rel_bias.astype(f32), precision=lax.Precision.HIGHEST)


def _prompt_bias_tiles(rel_bias, ta):
    j = jnp.arange(ta, dtype=i32)[:, None]
    i = jnp.arange(ta, dtype=i32)[None, :]
    tiles = []
    for d in range(2):
        rel = j - i - d * ta
        b = _bias_lookup(rel_bias, _t5_bucket(rel))
        allowed = ((j // CHUNK) <= (i // CHUNK)) if d == 0 else jnp.ones((ta, ta), bool)
        tiles.append(jnp.where(allowed[..., None], b, NEG_INF))
    return jnp.transpose(jnp.stack(tiles), (3, 0, 1, 2))


def _sample_bias(rel_bias, past, s):
    q_pos = past + jnp.arange(s, dtype=i32)
    k_pos = jnp.arange(past + s, dtype=i32)
    rel = k_pos[None, :] - q_pos[:, None]
    b = _bias_lookup(rel_bias, _t5_bucket(rel))
    allowed = (k_pos[None, :] // CHUNK) <= (q_pos[:, None] // CHUNK)
    return jnp.transpose(jnp.where(allowed[..., None], b, NEG_INF), (2, 0, 1))


def _attn_prompt_kernel(qi_tab, ki_tab, qT_ref, k_ref, v1T_ref, bias_ref, cfar_ref,
                        lq1_ref, lk1_ref, lq2_ref, lk2_ref, g_ref, o_ref,
                        qm_sc, m_sc, acc_sc, oT_sc, *, lam_init):
    p = pl.program_id(1)
    qi = qi_tab[p]
    ki = ki_tab[p]
    d = qi - ki
    ta = k_ref.shape[0]
    n_hp = H_A // 2

    @pl.when(ki == 0)
    def _():
        m_sc[...] = jnp.full(m_sc.shape, NEG_INF, f32)
        acc_sc[...] = jnp.zeros(acc_sc.shape, f32)
        row = lax.broadcasted_iota(i32, (LANES, ta), 0)
        for hp in range(n_hp):
            qt = qT_ref[hp * LANES:(hp + 1) * LANES, :]
            for gi in range(4):
                keep = (row >= gi * DQK) & (row < (gi + 1) * DQK)
                qm_sc[hp, :, gi * ta:(gi + 1) * ta] = jnp.where(keep, qt, jnp.zeros_like(qt))

    def scores(hp):
        return _dot(k_ref[:, hp * LANES:(hp + 1) * LANES], qm_sc[hp])

    nb = ta // ATT_SUB

    def near_bias(hp, step_d):
        rows = []
        for jb in range(nb):
            cols = []
            for g in range(4):
                for ib in range(nb):
                    dd = step_d * nb + ib - jb
                    if dd < 0:
                        blk = jnp.full((ATT_SUB, ATT_SUB), NEG_INF * LOG2E, f32)
                    elif dd < 2:
                        blk = bias_ref[2 * hp + g // 2, dd]
                    else:
                        lo = g * ta + ib * ATT_SUB
                        blk = jnp.broadcast_to(cfar_ref[hp][:, lo:lo + ATT_SUB], (ATT_SUB, ATT_SUB))
                    cols.append(blk)
            rows.append(jnp.concatenate(cols, axis=1))
        return jnp.concatenate(rows, axis=0)

    def update(step_d):
        s_next = scores(0)
        for hp in range(n_hp):
            s = s_next
            if hp + 1 < n_hp:
                s_next = scores(hp + 1)
            m_old = m_sc[hp]
            if step_d is not None:
                s = s + near_bias(hp, step_d)
                m_new = jnp.maximum(m_old, jnp.max(s, axis=0, keepdims=True))
                shift = m_new
            else:
                c = cfar_ref[hp]
                m_new = jnp.maximum(m_old, jnp.max(s, axis=0, keepdims=True) + c)
                shift = m_new - c
            alpha = jnp.exp2(m_old - m_new)
            pT = jnp.exp2(s - shift).astype(bf16)
            for hh in range(2):
                h = 2 * hp + hh
                cols = slice(hh * 2 * ta, (hh + 1) * 2 * ta)
                pv = _dot(v1T_ref[h * V_ROWS:(h + 1) * V_ROWS, :], pT[:, cols])
                acc_sc[h] = alpha[:, cols] * acc_sc[h] + pv
            m_sc[hp] = m_new

    @pl.when(d == 0)
    def _():
        update(0)

    @pl.when(d == 1)
    def _():
        update(1)

    @pl.when(d >= 2)
    def _():
        update(None)

    @pl.when(d == 0)
    def _():
        lam = _lambda(lq1_ref, lk1_ref, lq2_ref, lk2_ref, lam_init)
        for h in range(H_A):
            a = acc_sc[h]
            a0 = a[:, :ta]
            a1 = a[:, ta:]
            o = a0[:DV] / a0[DV:DV + 1] - lam * (a1[:DV] / a1[DV:DV + 1])
            o = o * lax.rsqrt(jnp.mean(o * o, axis=0, keepdims=True) + RMS_EPS) * g_ref[...] * (1.0 - lam_init)
            oT_sc[h * DV:(h + 1) * DV, :] = o
        o_ref[...] = oT_sc[...].T.astype(bf16)


def _attn_prompt(qT, kb, v1T, rel_bias, lam_args, attn_g, B, L, lam_init):
    ta = min(ATT_TILE, L)
    assert L % ta == 0 and ta % ATT_SUB == 0 and ATT_SUB % CHUNK == 0
    nt = L // ta
    far = _far_bucket(ATT_SUB + 1)
    assert L <= 2 * ATT_SUB or far is not None, "key blocks two or more behind must share one bias bucket"
    far = far if far is not None else N_BUCKETS // 2 - 1
    pairs = [(q, k) for q in range(nt) for k in range(q + 1)]
    qi_tab = jnp.asarray([q for q, _ in pairs], i32)
    ki_tab = jnp.asarray([k for _, k in pairs], i32)
    bias = _prompt_bias_tiles(rel_bias, ATT_SUB) * LOG2E
    cfar = jnp.repeat((rel_bias.astype(f32)[far] * LOG2E).reshape(H_A // 2, 2), 2 * ta, axis=1)[:, None, :]
    g_col = attn_g.astype(f32).reshape(DV, 1)
    T = B * L
    n_hp = H_A // 2
    full = lambda a: pl.BlockSpec(a.shape, lambda b, p, qt, kt: (0,) * a.ndim)
    grid_spec = pltpu.PrefetchScalarGridSpec(
        num_scalar_prefetch=2, grid=(B, len(pairs)),
        in_specs=[
            pl.BlockSpec((D_QK_ALL, ta), lambda b, p, qt, kt: (0, b * nt + qt[p])),
            pl.BlockSpec((ta, D_QK_ALL), lambda b, p, qt, kt: (b * nt + kt[p], 0)),
            pl.BlockSpec((H_A * V_ROWS, ta), lambda b, p, qt, kt: (0, b * nt + kt[p])),
            full(bias), full(cfar)] + [full(a) for a in lam_args] + [full(g_col)],
        out_specs=pl.BlockSpec((ta, D_ATT), lambda b, p, qt, kt: (b * nt + qt[p], 0)),
        scratch_shapes=[pltpu.VMEM((n_hp, LANES, 4 * ta), bf16), pltpu.VMEM((n_hp, 1, 4 * ta), f32),
                        pltpu.VMEM((H_A, V_ROWS, 2 * ta), f32), pltpu.VMEM((D_ATT, ta), f32)])
    return pl.pallas_call(
        functools.partial(_attn_prompt_kernel, lam_init=lam_init),
        grid_spec=grid_spec, out_shape=jax.ShapeDtypeStruct((T, D_ATT), bf16),
        compiler_params=_params(("parallel", "arbitrary")),
    )(qi_tab, ki_tab, qT, kb, v1T, bias, cfar, *lam_args, g_col)


def _attn_sample_kernel(q_ref, kn_ref, vn_ref, kc_ref, vc_ref, bc_ref, bn_ref,
                        lq1_ref, lk1_ref, lq2_ref, lk2_ref, g2_ref, o_ref, *, lam_init):
    s_len = q_ref.shape[0]
    lane = lax.broadcasted_iota(i32, (s_len, LANES), 1)
    lam = _lambda(lq1_ref, lk1_ref, lq2_ref, lk2_ref, lam_init)
    for hp in range(H_A // 2):
        sl = slice(hp * LANES, (hp + 1) * LANES)
        qt = q_ref[:, sl]
        kct = kc_ref[0, sl, :].astype(bf16)
        vct = vc_ref[0, sl, :].astype(bf16)
        knt = kn_ref[:, sl].astype(bf16)
        vnt = vn_ref[:, sl].astype(bf16)
        outs = []
        for hh in range(2):
            h = 2 * hp + hh
            oc = []
            for c in range(2):
                gi = 2 * hh + c
                qm = jnp.where((lane >= gi * DQK) & (lane < (gi + 1) * DQK), qt, jnp.zeros_like(qt))
                s1 = _dot(qm, kct) + bc_ref[h]
                s2 = _dot_nt(qm, knt) + bn_ref[h]
                m = jnp.maximum(jnp.max(s1, axis=-1, keepdims=True), jnp.max(s2, axis=-1, keepdims=True))
                p1 = jnp.exp(s1 - m)
                p2 = jnp.exp(s2 - m)
                l = jnp.sum(p1, axis=-1, keepdims=True) + jnp.sum(p2, axis=-1, keepdims=True)
                pv = _dot_nt(p1.astype(bf16), vct) + _dot(p2.astype(bf16), vnt)
                oc.append(pv / l)
            o = oc[0] - lam * oc[1]
            in_head = (lane >= hh * DV) & (lane < (hh + 1) * DV)
            ms = jnp.sum(jnp.where(in_head, o * o, 0.0), axis=-1, keepdims=True) * (1.0 / DV)
            outs.append(o * lax.rsqrt(ms + RMS_EPS))
        ot = jnp.where(lane < DV, outs[0], outs[1]) * g2_ref[...] * (1.0 - lam_init)
        o_ref[:, sl] = ot.astype(bf16)


def _attn_sample(q, k_new, v_new, cache_k, cache_v, rel_bias, lam_args, attn_g, lam_init):
    nb, past = cache_k.shape[0], cache_k.shape[1]
    T = q.shape[0]
    s = T // nb
    kc = cache_k.transpose(0, 2, 3, 1).reshape(nb, D_QK_ALL, past)
    vc = cache_v.transpose(0, 2, 3, 1).reshape(nb, D_ATT, past)
    bias = _sample_bias(rel_bias, past, s)
    bc, bn = bias[:, :, :past], bias[:, :, past:]
    g2 = jnp.tile(attn_g.astype(f32), 2).reshape(1, LANES)
    full = lambda a: pl.BlockSpec(a.shape, lambda b: (0,) * a.ndim)
    row = lambda n: pl.BlockSpec((s, n), lambda b: (b, 0))
    return pl.pallas_call(
        functools.partial(_attn_sample_kernel, lam_init=lam_init),
        grid=(nb,),
        in_specs=[row(D_QK_ALL), row(D_QK_ALL), row(D_ATT),
                  pl.BlockSpec((1, D_QK_ALL, past), lambda b: (b, 0, 0)),
                  pl.BlockSpec((1, D_ATT, past), lambda b: (b, 0, 0)),
                  full(bc), full(bn)] + [full(a) for a in lam_args] + [full(g2)],
        out_specs=row(D_ATT), out_shape=jax.ShapeDtypeStruct((T, D_ATT), bf16),
        compiler_params=_params(("parallel",)),
    )(q, k_new, v_new, kc, vc, bc, bn, *lam_args, g2)


def _ssd_kernel(z_ref, xbc_ref, dt_ref, dtT_ref, h0_ref, c0_ref, cw_ref, cb_ref, dtb_ref, dtbT_ref,
                alog_ref, alogT_ref, dsk_ref, g_ref, y_ref, hout_ref, cout_ref, xpad_sc, h_sc):
    q = xbc_ref.shape[0]
    c = pl.program_id(1)
    gw = HEADS_PER_GROUP * SSM_HEADDIM

    @pl.when(c == 0)
    def _():
        xpad_sc[0:SUBLANES] = c0_ref[0]
        h_sc[...] = h0_ref[0]

    xpad_sc[SUBLANES:SUBLANES + q] = xbc_ref[...]
    first = SUBLANES - (CONV_W - 1)
    conv = cb_ref[...]
    for i in range(CONV_W):
        conv = conv + xpad_sc[first + i:first + i + q] * cw_ref[i:i + 1, :]
    tail = xpad_sc[q:q + SUBLANES]
    xpad_sc[0:SUBLANES] = tail
    cout_ref[0] = tail
    act = conv * _sigmoid(conv)
    xs = act[:, :D_SSM]

    lane = lax.broadcasted_iota(i32, (1, LANES), 1)
    a_row = jnp.where(lane < H_S, -jnp.exp(alog_ref[...]), 0.0)
    dt = _softplus(dt_ref[...] + dtb_ref[...])
    ii = lax.broadcasted_iota(i32, (q, q), 0)
    jj = lax.broadcasted_iota(i32, (q, q), 1)
    causal = jj <= ii
    acs = _dot_f32_rhs(causal.astype(bf16), dt * a_row)
    sub = lax.broadcasted_iota(i32, (BF16_ROWS, 1), 0)
    a_col = jnp.where(sub < H_S, -jnp.exp(alogT_ref[...]), 0.0)
    dtT = _softplus(dtT_ref[0] + dtbT_ref[...])
    acsT = _dot_f32_lhs(dtT * a_col, (ii <= jj).astype(bf16))

    er = lax.broadcasted_iota(i32, (LANES, D_SSM), 0)
    ec = lax.broadcasted_iota(i32, (LANES, D_SSM), 1)
    expand = (ec // SSM_HEADDIM == er).astype(bf16)
    dt_x = _dot_f32_lhs(dt, expand)
    acs_x = _dot_f32_lhs(acs, expand)
    e_acs = jnp.exp(acs_x)
    acs_last = acs_x[q - 1:q, :]
    decay = jnp.exp(acs_last - acs_x) * dt_x
    dtx = xs * dt_x
    xd = xs * decay
    glane = lax.broadcasted_iota(i32, (1, gw), 1)

    ys = []
    for g in range(SSM_GROUPS):
        gs = slice(g * gw, (g + 1) * gw)
        bg = act[:, D_SSM + g * D_STATE:D_SSM + (g + 1) * D_STATE].astype(bf16)
        cg = act[:, D_SSM + (SSM_GROUPS + g) * D_STATE:D_SSM + (SSM_GROUPS + g + 1) * D_STATE].astype(bf16)
        cb = _dot_nt(cg, bg)
        h_old = h_sc[g]
        yg = _dot(cg, h_old.astype(bf16)) * e_acs[:, gs]
        dtx_g = dtx[:, gs]
        for e4 in range(HEADS_PER_GROUP):
            e = g * HEADS_PER_GROUP + e4
            seg = acs[:, e:e + 1] - acsT[e:e + 1, :]
            lmat = jnp.exp(jnp.where(causal, seg, -jnp.inf))
            rhs = jnp.where(glane // SSM_HEADDIM == e4, dtx_g, 0.0).astype(bf16)
            yg = yg + _dot((cb * lmat).astype(bf16), rhs)
        h_sc[g] = h_old * jnp.exp(acs_last[:, gs]) + _dot_tn(bg, xd[:, gs].astype(bf16))
        ys.append(yg)
    y = jnp.concatenate(ys, axis=1) + dsk_ref[...] * xs
    zz = z_ref[...]
    y = y * (zz * _sigmoid(zz))
    for g in range(SSM_GROUPS):
        gs = slice(g * gw, (g + 1) * gw)
        yg = y[:, gs]
        r = lax.rsqrt(jnp.mean(yg * yg, axis=-1, keepdims=True) + RMS_EPS)
        y_ref[:, gs] = (yg * r * g_ref[:, gs]).astype(bf16)
    hout_ref[0] = h_sc[...]


def _ssd(z, xbc, dt, dtT, h0, conv0, pw, B, L):
    q = min(SSD_TILE, L)
    nc = L // q
    gw = HEADS_PER_GROUP * SSM_HEADDIM
    dtT3 = dtT.reshape(BF16_ROWS, B, L).transpose(1, 0, 2)
    hT0 = h0.astype(f32).reshape(B, SSM_GROUPS, gw, D_STATE).transpose(0, 1, 3, 2)
    c0 = jnp.pad(conv0.astype(f32), ((0, 0), (SUBLANES - (CONV_W - 1), 0), (0, 0)))
    full = lambda a: pl.BlockSpec(a.shape, lambda b, c: (0,) * a.ndim)
    row = lambda n: pl.BlockSpec((q, n), lambda b, c: (b * nc + c, 0))
    params = [pw['conv_w'], pw['conv_b'], pw['dt_bias'], pw['dt_biasT'], pw['a_log'], pw['a_logT'],
              pw['d_skip'], pw['ssm_g']]
    y, hT, ctail = pl.pallas_call(
        _ssd_kernel, grid=(B, nc),
        in_specs=[row(D_SSM), row(CONV_DIM), row(LANES),
                  pl.BlockSpec((1, BF16_ROWS, q), lambda b, c: (b, 0, c)),
                  pl.BlockSpec((1, SSM_GROUPS, D_STATE, gw), lambda b, c: (b, 0, 0, 0)),
                  pl.BlockSpec((1, SUBLANES, CONV_DIM), lambda b, c: (b, 0, 0))] + [full(a) for a in params],
        out_specs=[row(D_SSM),
                   pl.BlockSpec((1, SSM_GROUPS, D_STATE, gw), lambda b, c: (b, 0, 0, 0)),
                   pl.BlockSpec((1, SUBLANES, CONV_DIM), lambda b, c: (b, 0, 0))],
        out_shape=[jax.ShapeDtypeStruct((B * L, D_SSM), bf16),
                   jax.ShapeDtypeStruct((B, SSM_GROUPS, D_STATE, gw), f32),
                   jax.ShapeDtypeStruct((B, SUBLANES, CONV_DIM), f32)],
        scratch_shapes=[pltpu.VMEM((q + SUBLANES, CONV_DIM), f32), pltpu.VMEM((SSM_GROUPS, D_STATE, gw), f32)],
        compiler_params=_params(("parallel", "arbitrary")),
    )(z, xbc, dt, dtT3, hT0, c0, *params)
    h_new = hT.transpose(0, 1, 3, 2).reshape(B, H_S, SSM_HEADDIM, D_STATE)
    return y, h_new, ctail[:, SUBLANES - (CONV_W - 1):, :]


def _pack_bf16_pairs(x):
    half = x.shape[1] // 2
    bits = lambda a: lax.bitcast_convert_type(a.astype(bf16).astype(f32), jnp.uint32)
    return (bits(x[:, half:]) & jnp.uint32(0xFFFF0000)) | (bits(x[:, :half]) >> 16)


def _unpack_bf16_pairs(word):
    lo = lax.bitcast_convert_type(word << 16, f32)
    hi = lax.bitcast_convert_type(word & jnp.uint32(0xFFFF0000), f32)
    return jnp.concatenate([lo, hi], axis=1)


def _mix_router_kernel(att_ref, ssm_ref, x_ref, woa_ref, wos_ref, g_ref, b_ref, wrh_ref, wrhl_ref, br_ref,
                       cnt0_ref, *rest):
    x1_ref, x1p_ref, te_ref, gate_ref, cnt_ref, run_sc = rest[-6:]

    @pl.when(pl.program_id(0) == 0)
    def _():
        run_sc[...] = cnt0_ref[...]

    mix = _dot(att_ref[...], woa_ref[...]) + _dot(ssm_ref[...], wos_ref[...])
    x1 = _layer_norm(DEEPNORM_ALPHA * x_ref[...] + mix, g_ref[...], b_ref[...])
    x1_ref[...] = x1
    x1p_ref[...] = _pack_bf16_pairs(x1)
    hi = x1.astype(bf16)
    lo = (x1 - hi.astype(f32)).astype(bf16)
    both = _dot(hi, wrhl_ref[...])
    logits = both[:, :LANES] + both[:, LANES:] + _dot(lo, wrh_ref[...]) + br_ref[...]
    lane =lax.broadcasted_iota(i32, logits.shape, 1)
    vals, idxs = [], []
    for _ in range(TOP_K):
        m = jnp.max(logits, axis=-1, keepdims=True)
        idx = jnp.min(jnp.where(logits == m, lane, LANES), axis=-1, keepdims=True)
        vals.append(m)
        idxs.append(idx)
        logits = jnp.where(lane == idx, -jnp.inf, logits)
    es = [jnp.exp(v - vals[0]) for v in vals]
    tot = es[0]
    for e in es[1:]:
        tot = tot + e
    tl = logits.shape[0]
    chosen = jnp.zeros(logits.shape, f32)
    for k in range(TOP_K):
        chosen = chosen + (lane == idxs[k]).astype(f32)
    ii = lax.broadcasted_iota(i32, (tl, tl), 0)
    jj = lax.broadcasted_iota(i32, (tl, tl), 1)
    before = _dot((jj < ii).astype(bf16), chosen.astype(bf16)) + run_sc[...]
    te = jnp.zeros(logits.shape, i32)
    gate = jnp.zeros(logits.shape, f32)
    for k in range(TOP_K):
        rank = jnp.sum(jnp.where(lane == idxs[k], before, 0.0), axis=-1, keepdims=True).astype(i32)
        te = jnp.where(lane == k, idxs[k], te)
        te = jnp.where(lane == TOP_K + k, rank, te)
        gate = jnp.where(lane == k, es[k] / tot, gate)
    te_ref[...] = te
    gate_ref[...] = gate
    run_sc[...] = run_sc[...] + jnp.sum(chosen, axis=0, keepdims=True)
    cnt_ref[...] = run_sc[...]


def _mix_router(att, ssm, x2d, w, counts0, packed, row0, rows_total):
    T, D = x2d.shape
    tl = min(ROW_TILE, T)
    assert row0 % tl == 0
    row = lambda n: pl.BlockSpec((tl, n), lambda i: (i, 0))
    full = lambda a: pl.BlockSpec(a.shape, lambda i: (0,) * a.ndim)
    ws = [w['wo_att'], w['wo_ssm'], w['ln1_g'], w['ln1_b'], w['wr_hi'], w['wr_hi_lo'], w['b_router'], counts0]
    ins = [att, ssm, x2d] + ws + [packed]
    in_specs = [row(D_ATT), row(D_SSM), row(D)] + [full(a) for a in ws] + [pl.BlockSpec(memory_space=pl.ANY)]
    aliases = {len(ins) - 1: 1}
    return pl.pallas_call(
        _mix_router_kernel, grid=(T // tl,), in_specs=in_specs,
        out_specs=[row(D), pl.BlockSpec((tl, D // 2), lambda i: (row0 // tl + i, 0)), row(LANES), row(LANES),
                   pl.BlockSpec((1, LANES), lambda i: (0, 0))],
        out_shape=[jax.ShapeDtypeStruct((T, D), f32), jax.ShapeDtypeStruct((rows_total, D // 2), jnp.uint32),
                   jax.ShapeDtypeStruct((T, LANES), i32),
                   jax.ShapeDtypeStruct((T, LANES), f32), jax.ShapeDtypeStruct((1, LANES), f32)],
        scratch_shapes=[pltpu.VMEM((1, LANES), f32)], input_output_aliases=aliases,
        compiler_params=_params(("arbitrary",)),
    )(*ins)


def _route(te, counts_f):
    T = te.shape[0]
    n_assign = T * TOP_K
    counts = counts_f[0, :N_EXPERTS].astype(i32)
    padded = (counts + MOE_BLOCK - 1) // MOE_BLOCK * MOE_BLOCK
    pad_end = jnp.cumsum(padded)
    pad_start = pad_end - padded
    top_e, rank = te[:, :TOP_K], te[:, TOP_K:2 * TOP_K]
    experts = jnp.arange(N_EXPERTS, dtype=i32)
    slot = rank + jnp.sum(jnp.where(top_e[..., None] == experts, pad_start, 0), axis=-1)
    n_blocks = -(-n_assign // MOE_BLOCK) + N_EXPERTS
    n_blocks += n_blocks % FFN_BLOCKS_PER_STEP
    block_start = jnp.arange(n_blocks, dtype=i32) * MOE_BLOCK
    block_e = jnp.minimum(jnp.sum(block_start[:, None] >= pad_end[None, :], axis=-1), N_EXPERTS - 1).astype(i32)
    n_used = (pad_end[-1] // MOE_BLOCK).astype(i32).reshape(1)
    row_end = jnp.sum(jnp.where(block_e[:, None] == experts, pad_start + counts, 0), axis=-1)
    rows_valid = jnp.clip(row_end - block_start, 0, MOE_BLOCK).astype(i32)
    return block_e, n_used, rows_valid, slot.astype(i32), n_blocks


SC_INDEX_WINDOW = 128
SC_ROWS = 32


def _sc_mesh():
    return plsc.VectorSubcoreMesh(core_axis_name="c", subcore_axis_name="s")


def _sc_move_rows(src, src_idx, dst_idx_list, n_out):
    M = src_idx.shape[0]
    D = src.shape[1]
    idx = [a.reshape(1, M) for a in [src_idx] + list(dst_idx_list)]

    @functools.partial(pl.kernel, out_type=jax.ShapeDtypeStruct((n_out, D), src.dtype), mesh=_sc_mesh(),
                       scratch_types=[pltpu.VMEM((2, SC_ROWS, D), src.dtype), pltpu.SemaphoreType.DMA((2,))])
    def move(s_hbm, *rest):
        i_hbm, o_hbm, buf, sem = rest[:-3], rest[-3], rest[-2], rest[-1]
        n_parts = SC_INDEX_WINDOW // SC_ROWS

        def body(si_vmem, *di_vmem):
            def fetch(j):
                part = pl.ds(j * SC_ROWS, SC_ROWS)
                return pltpu.async_copy(s_hbm.at[si_vmem.at[0, part]], buf.at[j % 2], sem.at[j % 2])

            pending = fetch(0)
            for j in range(n_parts):
                nxt = fetch(j + 1) if j + 1 < n_parts else None
                pending.wait()
                part = pl.ds(j * SC_ROWS, SC_ROWS)
                for dv in di_vmem:
                    pltpu.sync_copy(buf.at[j % 2], o_hbm.at[dv.at[0, part]])
                pending = nxt

        pltpu.emit_pipeline(
            body, grid=(M // SC_INDEX_WINDOW,),
            in_specs=[pl.BlockSpec((1, SC_INDEX_WINDOW), lambda i: (0, i))] * len(idx),
            out_specs=[], core_axis_name=("c", "s"), dimension_semantics=(pltpu.PARALLEL,),
        )(*i_hbm)

    return move(src, *idx)


def _sc_scatter_rows(x, idx_k, n_rows):
    return _sc_move_rows(x, jnp.arange(x.shape[0], dtype=i32), idx_k, n_rows)


def _sc_gather_rows(src, idx):
    M = idx.shape[0]
    return _sc_move_rows(src, idx, [jnp.arange(M, dtype=i32)], M)


def _deinterleave_kernel(w_ref, g_ref, u_ref):
    tn = 2 * LANES
    r = lax.broadcasted_iota(i32, (2 * tn, tn), 0)
    c = lax.broadcasted_iota(i32, (2 * tn, tn), 1)
    pick_g = (r == 2 * c).astype(bf16)
    pick_u = (r == 2 * c + 1).astype(bf16)
    for j in range(g_ref.shape[-1] // tn):
        wb = w_ref[0, :, j * 2 * tn:(j + 1) * 2 * tn].astype(bf16)
        g_ref[0, :, j * tn:(j + 1) * tn] = _dot(wb, pick_g).astype(bf16)
        u_ref[0, :, j * tn:(j + 1) * tn] = _dot(wb, pick_u).astype(bf16)


def _deinterleave(w_gu):
    E, D, F2 = w_gu.shape
    out = jax.ShapeDtypeStruct((E, D, F2 // 2), bf16)
    return pl.pallas_call(
        _deinterleave_kernel, grid=(E,),
        in_specs=[pl.BlockSpec((1, D, F2), lambda e: (e, 0, 0))],
        out_specs=[pl.BlockSpec((1, D, F2 // 2), lambda e: (e, 0, 0))] * 2,
        out_shape=[out, out],
        compiler_params=_params(("parallel",)),
    )(w_gu)


def _ffn_kernel(be_ref, nu_ref, rv_ref, xs_ref, *refs):
    del be_ref
    y_ref = refs[-1]
    wsets = [refs[6 * j:6 * j + 6] for j in range(FFN_BLOCKS_PER_STEP)]
    first = pl.program_id(0) * FFN_BLOCKS_PER_STEP
    n_live = jnp.clip(nu_ref[0] - first, 0, FFN_BLOCKS_PER_STEP)
    row_id = lax.broadcasted_iota(i32, (MOE_BLOCK, 1), 0)

    def block(j):
        wg_ref, wu_ref, bg_ref, bu_ref, wd_ref, bd_ref = wsets[j]
        rows = pl.ds(j * MOE_BLOCK, MOE_BLOCK)
        words = jnp.where(row_id < rv_ref[first + j], xs_ref[rows, :], jnp.uint32(0))
        xb = _unpack_bf16_pairs(words).astype(bf16)
        g = _dot(xb, wg_ref[0]) + bg_ref[0]
        u = _dot(xb, wu_ref[0]) + bu_ref[0]
        g = jnp.minimum(g, SWIGLU_LIMIT)
        u = jnp.clip(u, -SWIGLU_LIMIT, SWIGLU_LIMIT)
        act = (u + 1.0) * g * _sigmoid(SWIGLU_ALPHA * g)
        y_ref[rows, :] = _pack_bf16_pairs(_dot(act.astype(bf16), wd_ref[0]) + bd_ref[0])

    for live in range(FFN_BLOCKS_PER_STEP + 1):
        @pl.when(n_live == live)
        def _(live=live):
            for j in range(live):
                block(j)
            for j in range(live, FFN_BLOCKS_PER_STEP):
                y_ref[pl.ds(j * MOE_BLOCK, MOE_BLOCK), :] = jnp.zeros((MOE_BLOCK, y_ref.shape[1]), jnp.uint32)


def _expert_ffn(xs, block_e, n_used, rows_valid, w):
    n_rows, half = xs.shape
    D = 2 * half
    nps = FFN_BLOCKS_PER_STEP
    n_steps = n_rows // (MOE_BLOCK * nps)
    F = w['w_g'].shape[2]
    wspecs, wargs = [], []
    for j in range(nps):
        pick = lambda i, be, nu, rv, j=j: (be[i * nps + j], 0, 0)
        wspecs += [pl.BlockSpec((1, D, F), pick), pl.BlockSpec((1, D, F), pick), pl.BlockSpec((1, 1, F), pick),
                   pl.BlockSpec((1, 1, F), pick), pl.BlockSpec((1, F, D), pick), pl.BlockSpec((1, 1, D), pick)]
        wargs += [w['w_g'], w['w_u'], w['b_g'], w['b_u'], w['w_d'], w['b_d']]
    grid_spec = pltpu.PrefetchScalarGridSpec(
        num_scalar_prefetch=3, grid=(n_steps,),
        in_specs=[pl.BlockSpec((nps * MOE_BLOCK, half),
                               lambda i, be, nu, rv: (jnp.minimum(i, (nu[0] - 1) // nps), 0))] + wspecs,
        out_specs=pl.BlockSpec((nps * MOE_BLOCK, half), lambda i, be, nu, rv: (i, 0)))
    return pl.pallas_call(
        _ffn_kernel, grid_spec=grid_spec,
        out_shape=jax.ShapeDtypeStruct((n_rows, half), jnp.uint32),
        compiler_params=_params(("arbitrary",)),
    )(block_e, n_used, rows_valid, xs, *wargs)


def _combine_dense_kernel(rows_ref, x1_ref, gate_ref, g_ref, b_ref, *rest):
    y_ref = rest[-1]
    gate = gate_ref[...]
    ff = gate[:, 0:1] * _unpack_bf16_pairs(rows_ref[0])
    for k in range(1, TOP_K):
        ff = ff + gate[:, k:k + 1] * _unpack_bf16_pairs(rows_ref[k])
    y_ref[...] = _layer_norm(DEEPNORM_ALPHA * x1_ref[...] + ff, g_ref[...], b_ref[...])


def _combine_dense(rows, x1, gate, ln_g, ln_b, chunk, y_partial):
    T, D = x1.shape
    tc = rows.shape[1]
    tl = min(ROW_TILE, tc)
    first = chunk * (tc // tl)
    row = lambda n: pl.BlockSpec((tl, n), lambda i: (first + i, 0))
    full = lambda a: pl.BlockSpec(a.shape, lambda i: (0,) * a.ndim)
    ins = [rows, x1, gate, ln_g, ln_b]
    in_specs = [pl.BlockSpec((TOP_K, tl, D // 2), lambda i: (0, i, 0)), row(D), row(LANES), full(ln_g), full(ln_b)]
    aliases = {}
    if y_partial is not None:
        ins.append(y_partial)
        in_specs.append(pl.BlockSpec(memory_space=pl.ANY))
        aliases = {len(ins) - 1: 0}
    return pl.pallas_call(
        _combine_dense_kernel, grid=(tc // tl,), in_specs=in_specs,
        out_specs=row(D), out_shape=jax.ShapeDtypeStruct((T, D), f32), input_output_aliases=aliases,
        compiler_params=_params(("parallel",)),
    )(*ins)


def _prep_weights(l, w_in, conv_w, conv_b, dt_bias, a_log, d_skip, ssm_norm_g, w_o, ln1_g, ln1_b,
                  w_router, b_router, w_gate_up, b_gate_up, w_down, b_down, ln2_g, ln2_b):
    wi = w_in[l]
    c0, c1, c2, c3 = D_QK_ALL, 2 * D_QK_ALL, 2 * D_QK_ALL + D_ATT, 2 * D_QK_ALL + D_ATT + D_SSM
    c4 = c3 + CONV_DIM
    wdt = wi[:, c4:c4 + H_S]
    pad_lane = lambda v, fill=0.0: jnp.pad(v.astype(f32).reshape(1, -1), ((0, 0), (0, LANES - v.shape[-1])),
                                            constant_values=fill)
    pad_col = lambda v: jnp.pad(v.astype(f32).reshape(-1, 1), ((0, BF16_ROWS - v.shape[-1]), (0, 0)))
    wr = jnp.pad(w_router[l].astype(f32), ((0, 0), (0, LANES - N_EXPERTS)))
    wr_hi = wr.astype(bf16)
    wgu = w_gate_up[l]
    return {
        'wq': wi[:, :c0].astype(bf16), 'wqT': wi[:, :c0].T.astype(bf16),
        'wk': wi[:, c0:c1].astype(bf16), 'wkT': wi[:, c0:c1].T.astype(bf16), 'wv': wi[:, c1:c2].astype(bf16), 'wvT': wi[:, c1:c2].T.astype(bf16),
        'wz': wi[:, c2:c3].astype(bf16), 'wx': wi[:, c3:c4].astype(bf16),
        'wdt': jnp.pad(wdt, ((0, 0), (0, LANES - H_S))).astype(bf16),
        'wdtT': jnp.pad(wdt.T, ((0, BF16_ROWS - H_S), (0, 0))).astype(bf16),
        'conv_w': conv_w[l].astype(f32), 'conv_b': conv_b[l].astype(f32).reshape(1, -1),
        'dt_bias': pad_lane(dt_bias[l]), 'dt_biasT': pad_col(dt_bias[l]),
        'a_log': pad_lane(a_log[l]), 'a_logT': pad_col(a_log[l]),
        'd_skip': jnp.repeat(d_skip[l].astype(f32), SSM_HEADDIM).reshape(1, -1),
        'ssm_g': ssm_norm_g[l].astype(f32).reshape(1, -1),
        'wo_att': w_o[l][:D_ATT].astype(bf16), 'wo_ssm': w_o[l][D_ATT:].astype(bf16),
        'ln1_g': ln1_g[l].astype(f32).reshape(1, -1), 'ln1_b': ln1_b[l].astype(f32).reshape(1, -1),
        'wr_hi': wr_hi, 'wr_hi_lo': jnp.concatenate([wr_hi, (wr - wr_hi.astype(f32)).astype(bf16)], axis=1),
        'b_router': pad_lane(b_router[l], NEG_INF),
        'w_gu': wgu,
        'b_g': b_gate_up[l][:, None, 0::2].astype(f32), 'b_u': b_gate_up[l][:, None, 1::2].astype(f32),
        'w_d': w_down[l].astype(bf16), 'b_d': b_down[l][:, None, :].astype(f32),
        'ln2_g': ln2_g[l].astype(f32).reshape(1, -1), 'ln2_b': ln2_b[l].astype(f32).reshape(1, -1),
    }


def _mixers(x, l, w, rel_bias, lam_args, attn_g, k_past, v_past, h0, conv0):
    B, L, D = x.shape
    x2d = x.reshape(B * L, D)
    lam_init = 0.8 - 0.6 * math.exp(-0.3 * l)
    prompt = k_past is None
    proj = _in_proj(x2d, w, prompt, B)
    z, xbc, dt, dtT = proj[:4]
    if prompt:
        kT, vT, kb, qT, v1T = proj[4:]
        att = _attn_prompt(qT, kb, v1T, rel_bias, lam_args, attn_g, B, L, lam_init)
        k_rows = kT.reshape(B, H_A, 2 * DQK, L).transpose(0, 3, 1, 2)
        v_rows = vT.reshape(B, H_A, DV, L).transpose(0, 3, 1, 2)
    else:
        k, v, q = proj[4:]
        att = _attn_sample(q, k, v, k_past, v_past, rel_bias, lam_args, attn_g, lam_init)
        k_rows = k.reshape(B, L, H_A, 2 * DQK)
        v_rows = v.reshape(B, L, H_A, DV)
    ssm, h_new, conv_new = _ssd(z, xbc, dt, dtT, h0, conv0, w, B, L)
    return (att, ssm, x2d), (k_rows, v_rows, h_new, conv_new)


def _moe_layer(streams, w):
    rows_total = sum(x2d.shape[0] for _, _, x2d in streams)
    counts = jnp.zeros((1, LANES), f32)
    packed = jnp.zeros((rows_total, streams[0][2].shape[1] // 2), jnp.uint32)
    row0, routed = 0, []
    for att, ssm, x2d in streams:
        x1, packed, te, gate, counts = _mix_router(att, ssm, x2d, w, counts, packed, row0, rows_total)
        routed.append((x1, gate, row0))
        row0 += x2d.shape[0]
        te_all = te[:, :2 * TOP_K] if len(routed) == 1 else jnp.concatenate([te_all, te[:, :2 * TOP_K]])
    block_e, n_used, rows_valid, slot, n_blocks = _route(te_all, counts)
    slot_k = slot.T
    xs = _sc_scatter_rows(packed, [slot_k[k] for k in range(TOP_K)], n_blocks * MOE_BLOCK)
    ys = _expert_ffn(xs, block_e, n_used, rows_valid, w)
    outs = []
    for x1, gate, r0 in routed:
        T, D = x1.shape
        n_groups = COMBINE_GROUPS if T % (COMBINE_GROUPS * ROW_TILE) == 0 else 1
        tc = T // n_groups
        y = None
        for c in range(n_groups):
            idx = slot_k[:, r0 + c * tc:r0 + (c + 1) * tc].reshape(-1)
            rows = _sc_gather_rows(ys, idx).reshape(TOP_K, tc, D // 2)
            y = _combine_dense(rows, x1, gate, w['ln2_g'], w['ln2_b'], c, y)
        outs.append(y)
    return outs


def kernel(x_prompt, x_sample, cache_k, cache_v, state_ssm, state_conv, rel_bias, w_in, lambda_q1, lambda_k1, lambda_q2, lambda_k2, attn_norm_g, conv_w, conv_b, dt_bias, a_log, d_skip, ssm_norm_g, w_o, ln1_g, ln1_b, w_router, b_router, w_gate_up, b_gate_up, w_down, b_down, ln2_g, ln2_b):
    yp, ys = x_prompt, x_sample
    bp = x_prompt.shape[0]
    depth = w_in.shape[0]
    outs = [[] for _ in range(8)]
    for l in range(depth):
        w = _prep_weights(l, w_in, conv_w, conv_b, dt_bias, a_log, d_skip, ssm_norm_g, w_o, ln1_g, ln1_b,
                          w_router, b_router, w_gate_up, b_gate_up, w_down, b_down, ln2_g, ln2_b)
        w['w_g'], w['w_u'] = _deinterleave(w.pop('w_gu'))
        lam_args = [a[l].astype(f32).reshape(1, -1) for a in (lambda_q1, lambda_k1, lambda_q2, lambda_k2)]
        h0 = jnp.zeros((bp, H_S, SSM_HEADDIM, D_STATE), f32)
        c0 = jnp.zeros((bp, CONV_W - 1, CONV_DIM), f32)
        mix_p, state_p = _mixers(yp, l, w, rel_bias, lam_args, attn_norm_g[l], None, None, h0, c0)
        mix_s, state_s = _mixers(ys, l, w, rel_bias, lam_args, attn_norm_g[l], cache_k[l], cache_v[l],
                                 state_ssm[l], state_conv[l])
        y2p, y2s = _moe_layer([mix_p, mix_s], w)
        yp, ys = y2p.reshape(yp.shape), y2s.reshape(ys.shape)
        for lst, a in zip(outs, state_p + state_s):
            lst.append(a)
    return (yp, ys) + tuple(jnp.stack(o) for o in outs)
```

```python
import functools
import math

import numpy as np
import jax
import jax.numpy as jnp
from jax import lax
from jax.experimental import pallas as pl
from jax.experimental.pallas import tpu as pltpu
from jax.experimental.pallas import tpu_sc as plsc

f32 = jnp.float32
bf16 = jnp.bfloat16
i32 = jnp.int32

CHUNK = 64
H_A = 8
DQK = 32
DV = 2 * DQK
D_ATT = H_A * DV
SSM_HEADDIM = 64
H_S = 8
D_SSM = H_S * SSM_HEADDIM
SSM_GROUPS = 2
HEADS_PER_GROUP = H_S // SSM_GROUPS
D_STATE = 128
CONV_W = 4
CONV_DIM = D_SSM + 2 * SSM_GROUPS * D_STATE
D_QK_ALL = H_A * 2 * DQK
N_BUCKETS = 32
MAX_DISTANCE = 128
N_EXPERTS = 32
TOP_K = 4
SWIGLU_LIMIT = 7.0
SWIGLU_ALPHA = 1.702
MOE_BLOCK = 256
LN_EPS = 1e-5
RMS_EPS = 1e-5
NEG_INF = -1e30
DEPTH = 1
DEEPNORM_ALPHA = (2.0 * DEPTH) ** 0.25
LOG2E = math.log2(math.e)

LANES = 128
SUBLANES = 8
BF16_ROWS = 16
VMEM_LIMIT = 48 * 1024 * 1024

ROW_TILE = 512
ATT_TILE = 512
ATT_SUB = 256
SSD_TILE = 256
V_ROWS = DV + BF16_ROWS
FFN_BLOCKS_PER_STEP = 2
COMBINE_GROUPS = 4


def _params(semantics):
    return pltpu.CompilerParams(dimension_semantics=semantics, vmem_limit_bytes=VMEM_LIMIT)


def _dot(a, b):
    return jnp.dot(a, b, preferred_element_type=f32)


def _dot_nt(a, b):
    return lax.dot_general(a, b, (((1,), (1,)), ((), ())), preferred_element_type=f32)


def _dot_tn(a, b):
    return lax.dot_general(a, b, (((0,), (0,)), ((), ())), preferred_element_type=f32)


def _split3(a):
    hi = a.astype(bf16)
    r1 = a - hi.astype(f32)
    mid = r1.astype(bf16)
    lo = (r1 - mid.astype(f32)).astype(bf16)
    return hi, mid, lo


def _dot_f32_lhs(a, b_exact):
    hi, mid, lo = _split3(a)
    return _dot(hi, b_exact) + _dot(mid, b_exact) + _dot(lo, b_exact)


def _dot_f32_rhs(a_exact, b):
    hi, mid, lo = _split3(b)
    return _dot(a_exact, hi) + _dot(a_exact, mid) + _dot(a_exact, lo)


def _softplus(x):
    return jnp.maximum(x, 0.0) + jnp.log1p(jnp.exp(-jnp.abs(x)))


def _sigmoid(x):
    return 1.0 / (1.0 + jnp.exp(-x))


def _layer_norm(y, g, b):
    mu = jnp.mean(y, axis=-1, keepdims=True)
    yc = y - mu
    var = jnp.mean(yc * yc, axis=-1, keepdims=True)
    return yc * lax.rsqrt(var + LN_EPS) * g + b


def _lambda(lq1_ref, lk1_ref, lq2_ref, lk2_ref, lam_init):
    s1 = jnp.sum(lq1_ref[...] * lk1_ref[...], axis=-1, keepdims=True)
    s2 = jnp.sum(lq2_ref[...] * lk2_ref[...], axis=-1, keepdims=True)
    return jnp.exp(s1) - jnp.exp(s2) + lam_init


def _in_proj_kernel(x_ref, wk_ref, wv_ref, wz_ref, wx_ref, wdt_ref, wdtT_ref, wq_ref, wkT_ref, wvT_ref, *outs,
                    prompt):
    xb = x_ref[...].astype(bf16)
    z_ref, xbc_ref, dt_ref, dtT_ref = outs[:4]
    k = _dot(xb, wk_ref[...])
    z_ref[...] = _dot(xb, wz_ref[...])
    xbc_ref[...] = _dot(xb, wx_ref[...])
    dt_ref[...] = _dot(xb, wdt_ref[...])
    dtT_ref[...] = _dot_nt(wdtT_ref[...], xb)
    scale = DQK ** -0.5
    if prompt:
        kT_ref, vT_ref, kb_ref, qT_ref, v1T_ref = outs[4:]
        kb_ref[...] = k.astype(bf16)
        kT_ref[0] = _dot_nt(wkT_ref[...], xb)
        qT_ref[...] = (_dot_nt(wq_ref[...], xb) * (scale * LOG2E)).astype(bf16)
        vT32 = _dot_nt(wvT_ref[...], xb)
        vT_ref[0] = vT32
        vT = vT32.astype(bf16)
        ones = jnp.ones((BF16_ROWS, vT.shape[1]), bf16)
        for h in range(H_A):
            v1T_ref[h * V_ROWS:h * V_ROWS + DV, :] = vT[h * DV:(h + 1) * DV, :]
            v1T_ref[h * V_ROWS + DV:(h + 1) * V_ROWS, :] = ones
    else:
        k_ref, v_ref, q_ref = outs[4:]
        k_ref[...] = k
        v_ref[...] = _dot(xb, wv_ref[...])
        q_ref[...] = (_dot(xb, wq_ref[...]) * scale).astype(bf16)


def _in_proj(x2d, w, prompt, B):
    T, D = x2d.shape
    tl = min(ROW_TILE, T)
    L = T // B
    n_l = L // tl
    grid = (T // tl,)
    row = lambda n: pl.BlockSpec((tl, n), lambda i: (i, 0))
    col = lambda n: pl.BlockSpec((n, tl), lambda i: (0, i))
    per_stream = lambda n: pl.BlockSpec((1, n, tl), lambda i: (i // n_l, 0, i % n_l))
    full = lambda a: pl.BlockSpec(a.shape, lambda i: (0,) * a.ndim)
    wq = w['wqT'] if prompt else w['wq']
    ins = [x2d, w['wk'], w['wv'], w['wz'], w['wx'], w['wdt'], w['wdtT'], wq, w['wkT'], w['wvT']]
    in_specs = [row(D)] + [full(a) for a in ins[1:]]
    out_shape = [jax.ShapeDtypeStruct((T, D_SSM), f32), jax.ShapeDtypeStruct((T, CONV_DIM), f32),
                 jax.ShapeDtypeStruct((T, LANES), f32), jax.ShapeDtypeStruct((BF16_ROWS, T), f32)]
    out_specs = [row(D_SSM), row(CONV_DIM), row(LANES), col(BF16_ROWS)]
    if prompt:
        assert L % tl == 0
        out_shape += [jax.ShapeDtypeStruct((B, D_QK_ALL, L), f32), jax.ShapeDtypeStruct((B, D_ATT, L), f32),
                      jax.ShapeDtypeStruct((T, D_QK_ALL), bf16), jax.ShapeDtypeStruct((D_QK_ALL, T), bf16),
                      jax.ShapeDtypeStruct((H_A * V_ROWS, T), bf16)]
        out_specs += [per_stream(D_QK_ALL), per_stream(D_ATT), row(D_QK_ALL), col(D_QK_ALL), col(H_A * V_ROWS)]
    else:
        out_shape += [jax.ShapeDtypeStruct((T, D_QK_ALL), f32), jax.ShapeDtypeStruct((T, D_ATT), f32),
                      jax.ShapeDtypeStruct((T, D_QK_ALL), bf16)]
        out_specs += [row(D_QK_ALL), row(D_ATT), row(D_QK_ALL)]
    return pl.pallas_call(
        functools.partial(_in_proj_kernel, prompt=prompt),
        grid=grid, in_specs=in_specs, out_specs=out_specs, out_shape=out_shape,
        compiler_params=_params(("parallel",)),
    )(*ins)


def _t5_bucket(rel):
    half = N_BUCKETS // 2
    max_exact = half // 2
    n = jnp.abs(rel)
    large = max_exact + (jnp.log(jnp.maximum(n, 1).astype(f32) / max_exact)
                         / math.log(MAX_DISTANCE / max_exact) * (half - max_exact)).astype(i32)
    large = jnp.minimum(large, half - 1)
    return jnp.where(rel > 0, half, 0) + jnp.where(n < max_exact, n, large)


def _far_bucket(min_dist):
    half = N_BUCKETS // 2
    max_exact = half // 2
    v = max_exact + int(math.log(min_dist / max_exact) / math.log(MAX_DISTANCE / max_exact) * (half - max_exact))
    return half - 1 if (min_dist >= max_exact and v - 1 >= half - 1) else None


def _bias_lookup(rel_bias, bucket):
    onehot = (bucket[..., None] == jnp.arange(N_BUCKETS, dtype=i32)).astype(f32)
    return jnp.dot(onehot, rel_bias.astype(f32), precision=lax.Precision.HIGHEST)


def _prompt_bias_tiles(rel_bias, ta):
    j = jnp.arange(ta, dtype=i32)[:, None]
    i = jnp.arange(ta, dtype=i32)[None, :]
    tiles = []
    for d in range(2):
        rel = j - i - d * ta
        b = _bias_lookup(rel_bias, _t5_bucket(rel))
        allowed = ((j // CHUNK) <= (i // CHUNK)) if d == 0 else jnp.ones((ta, ta), bool)
        tiles.append(jnp.where(allowed[..., None], b, NEG_INF))
    return jnp.transpose(jnp.stack(tiles), (3, 0, 1, 2))


def _sample_bias(rel_bias, past, s):
    q_pos = past + jnp.arange(s, dtype=i32)
    k_pos = jnp.arange(past + s, dtype=i32)
    rel = k_pos[None, :] - q_pos[:, None]
    b = _bias_lookup(rel_bias, _t5_bucket(rel))
    allowed = (k_pos[None, :] // CHUNK) <= (q_pos[:, None] // CHUNK)
    return jnp.transpose(jnp.where(allowed[..., None], b, NEG_INF), (2, 0, 1))


ATT_DIAG, ATT_SUBDIAG, ATT_FAR1, ATT_FAR2 = 0, 1, 2, 3


def _attn_prompt_kernel(qi_tab, ka_tab, kb_tab, kind_tab, qT_ref, k_ref, v1T_ref, k2_ref, v2T_ref, bias_ref,
                        cfar_ref, lq1_ref, lk1_ref, lq2_ref, lk2_ref, g_ref, o_ref,
                        qm_sc, m_sc, acc_sc, oT_sc, *, lam_init):
    del qi_tab, kb_tab
    p = pl.program_id(1)
    kind = kind_tab[p]
    ta = k_ref.shape[0]
    n_hp = H_A // 2

    @pl.when(ka_tab[p] == 0)
    def _():
        m_sc[...] = jnp.full(m_sc.shape, NEG_INF, f32)
        acc_sc[...] = jnp.zeros(acc_sc.shape, f32)
        row = lax.broadcasted_iota(i32, (LANES, ta), 0)
        for hp in range(n_hp):
            qt = qT_ref[hp * LANES:(hp + 1) * LANES, :]
            for gi in range(4):
                keep = (row >= gi * DQK) & (row < (gi + 1) * DQK)
                qm_sc[hp, :, gi * ta:(gi + 1) * ta] = jnp.where(keep, qt, jnp.zeros_like(qt))

    def scores(keys_ref, hp):
        return _dot(keys_ref[:, hp * LANES:(hp + 1) * LANES], qm_sc[hp])

    nb = ta // ATT_SUB

    def near_bias(hp, step_d):
        rows = []
        for jb in range(nb):
            cols = []
            for g in range(4):
                for ib in range(nb):
                    dd = step_d * nb + ib - jb
                    if dd < 0:
                        blk = jnp.full((ATT_SUB, ATT_SUB), NEG_INF * LOG2E, f32)
                    elif dd < 2:
                        blk = bias_ref[2 * hp + g // 2, dd]
                    else:
                        lo = g * ta + ib * ATT_SUB
                        blk = jnp.broadcast_to(cfar_ref[hp][:, lo:lo + ATT_SUB], (ATT_SUB, ATT_SUB))
                    cols.append(blk)
            rows.append(jnp.concatenate(cols, axis=1))
        return jnp.concatenate(rows, axis=0)

    def update(step_d, tiles):
        chain = [(kr, vr, hp) for kr, vr in tiles for hp in range(n_hp)]
        s_next = scores(chain[0][0], chain[0][2])
        for link, (_, values_ref, hp) in enumerate(chain):
            s = s_next
            if link + 1 < len(chain):
                s_next = scores(chain[link + 1][0], chain[link + 1][2])
            m_old = m_sc[hp]
            if step_d is not None:
                s = s + near_bias(hp, step_d)
                m_new = jnp.maximum(m_old, jnp.max(s, axis=0, keepdims=True))
                shift = m_new
            else:
                c = cfar_ref[hp]
                m_new = jnp.maximum(m_old, jnp.max(s, axis=0, keepdims=True) + c)
                shift = m_new - c
            alpha = jnp.exp2(m_old - m_new)
            pT = jnp.exp2(s - shift).astype(bf16)
            for hh in range(2):
                h = 2 * hp + hh
                cols = slice(hh * 2 * ta, (hh + 1) * 2 * ta)
                pv = _dot(values_ref[h * V_ROWS:(h + 1) * V_ROWS, :], pT[:, cols])
                acc_sc[h] = alpha[:, cols] * acc_sc[h] + pv
            m_sc[hp] = m_new

    @pl.when(kind == ATT_DIAG)
    def _():
        update(0, [(k_ref, v1T_ref)])

    @pl.when(kind == ATT_SUBDIAG)
    def _():
        update(1, [(k_ref, v1T_ref)])

    @pl.when(kind == ATT_FAR1)
    def _():
        update(None, [(k_ref, v1T_ref)])

    @pl.when(kind == ATT_FAR2)
    def _():
        update(None, [(k_ref, v1T_ref), (k2_ref, v2T_ref)])

    @pl.when(kind == ATT_DIAG)
    def _():
        lam = _lambda(lq1_ref, lk1_ref, lq2_ref, lk2_ref, lam_init)
        for h in range(H_A):
            a = acc_sc[h]
            a0 = a[:, :ta]
            a1 = a[:, ta:]
            o = a0[:DV] / a0[DV:DV + 1] - lam * (a1[:DV] / a1[DV:DV + 1])
            o = o * lax.rsqrt(jnp.mean(o * o, axis=0, keepdims=True) + RMS_EPS) * g_ref[...] * (1.0 - lam_init)
            oT_sc[h * DV:(h + 1) * DV, :] = o
        o_ref[...] = oT_sc[...].T.astype(bf16)


def _attn_prompt(qT, kb, v1T, rel_bias, lam_args, attn_g, B, L, lam_init):
    ta = min(ATT_TILE, L)
    assert L % ta == 0 and ta % ATT_SUB == 0 and ATT_SUB % CHUNK == 0
    nt = L // ta
    far = _far_bucket(ATT_SUB + 1)
    assert L <= 2 * ATT_SUB or far is not None, "key blocks two or more behind must share one bias bucket"
    far = far if far is not None else N_BUCKETS // 2 - 1
    steps = []
    idle_b = 0
    for q in range(nt):
        far_tiles = list(range(max(q - 1, 0)))
        for a in range(0, len(far_tiles) - 1, 2):
            steps.append((q, a, a + 1, ATT_FAR2))
            idle_b = a + 1
        if len(far_tiles) % 2:
            steps.append((q, far_tiles[-1], idle_b, ATT_FAR1))
        if q >= 1:
            steps.append((q, q - 1, idle_b, ATT_SUBDIAG))
        steps.append((q, q, idle_b, ATT_DIAG))
    qi_tab, ka_tab, kb_tab, kind_tab = (jnp.asarray([s[j] for s in steps], i32) for j in range(4))
    bias =_prompt_bias_tiles(rel_bias, ATT_SUB) * LOG2E
    cfar = jnp.repeat((rel_bias.astype(f32)[far] * LOG2E).reshape(H_A // 2, 2), 2 * ta, axis=1)[:, None, :]
    g_col = attn_g.astype(f32).reshape(DV, 1)
    T = B * L
    n_hp = H_A // 2
    full = lambda a: pl.BlockSpec(a.shape, lambda b, p, qt, ka, kb, kd: (0,) * a.ndim)
    grid_spec = pltpu.PrefetchScalarGridSpec(
        num_scalar_prefetch=4, grid=(B, len(steps)),
        in_specs=[
            pl.BlockSpec((D_QK_ALL, ta), lambda b, p, qt, ka, kb, kd: (0, b * nt + qt[p])),
            pl.BlockSpec((ta, D_QK_ALL), lambda b, p, qt, ka, kb, kd: (b * nt + ka[p], 0)),
            pl.BlockSpec((H_A * V_ROWS, ta), lambda b, p, qt, ka, kb, kd: (0, b * nt + ka[p])),
            pl.BlockSpec((ta, D_QK_ALL), lambda b, p, qt, ka, kb, kd: (b * nt + kb[p], 0)),
            pl.BlockSpec((H_A * V_ROWS, ta), lambda b, p, qt, ka, kb, kd: (0, b * nt + kb[p])),
            full(bias), full(cfar)] + [full(a) for a in lam_args] + [full(g_col)],
        out_specs=pl.BlockSpec((ta, D_ATT), lambda b, p, qt, ka, kb, kd: (b * nt + qt[p], 0)),
        scratch_shapes=[pltpu.VMEM((n_hp, LANES, 4 * ta), bf16), pltpu.VMEM((n_hp, 1, 4 * ta), f32),
                        pltpu.VMEM((H_A, V_ROWS, 2 * ta), f32), pltpu.VMEM((D_ATT, ta), f32)])
    return pl.pallas_call(
        functools.partial(_attn_prompt_kernel, lam_init=lam_init),
        grid_spec=grid_spec, out_shape=jax.ShapeDtypeStruct((T, D_ATT), bf16),
        compiler_params=_params(("parallel", "arbitrary")),
    )(qi_tab, ka_tab, kb_tab, kind_tab, qT, kb, v1T, kb, v1T, bias, cfar, *lam_args, g_col)


def _attn_sample_kernel(q_ref, kn_ref, vn_ref, kc_ref, vc_ref, bc_ref, bn_ref,
                        lq1_ref, lk1_ref, lq2_ref, lk2_ref, g2_ref, o_ref, *, lam_init):
    s_len = q_ref.shape[0]
    lane = lax.broadcasted_iota(i32, (s_len, LANES), 1)
    lam = _lambda(lq1_ref, lk1_ref, lq2_ref, lk2_ref, lam_init)
    for hp in range(H_A // 2):
        sl = slice(hp * LANES, (hp + 1) * LANES)
        qt = q_ref[:, sl]
        kct = kc_ref[0, sl, :].astype(bf16)
        vct = vc_ref[0, sl, :].astype(bf16)
        knt = kn_ref[:, sl].astype(bf16)
        vnt = vn_ref[:, sl].astype(bf16)
        outs = []
        for hh in range(2):
            h = 2 * hp + hh
            oc = []
            for c in range(2):
                gi = 2 * hh + c
                qm = jnp.where((lane >= gi * DQK) & (lane < (gi + 1) * DQK), qt, jnp.zeros_like(qt))
                s1 = _dot(qm, kct) + bc_ref[h]
                s2 = _dot_nt(qm, knt) + bn_ref[h]
                m = jnp.maximum(jnp.max(s1, axis=-1, keepdims=True), jnp.max(s2, axis=-1, keepdims=True))
                p1 = jnp.exp(s1 - m)
                p2 = jnp.exp(s2 - m)
                l = jnp.sum(p1, axis=-1, keepdims=True) + jnp.sum(p2, axis=-1, keepdims=True)
                pv = _dot_nt(p1.astype(bf16), vct) + _dot(p2.astype(bf16), vnt)
                oc.append(pv / l)
            o = oc[0] - lam * oc[1]
            in_head = (lane >= hh * DV) & (lane < (hh + 1) * DV)
            ms = jnp.sum(jnp.where(in_head, o * o, 0.0), axis=-1, keepdims=True) * (1.0 / DV)
            outs.append(o * lax.rsqrt(ms + RMS_EPS))
        ot = jnp.where(lane < DV, outs[0], outs[1]) * g2_ref[...] * (1.0 - lam_init)
        o_ref[:, sl] = ot.astype(bf16)


def _attn_sample(q, k_new, v_new, cache_k, cache_v, rel_bias, lam_args, attn_g, lam_init):
    nb, past = cache_k.shape[0], cache_k.shape[1]
    T = q.shape[0]
    s = T // nb
    kc = cache_k.transpose(0, 2, 3, 1).reshape(nb, D_QK_ALL, past)
    vc = cache_v.transpose(0, 2, 3, 1).reshape(nb, D_ATT, past)
    bias = _sample_bias(rel_bias, past, s)
    bc, bn = bias[:, :, :past], bias[:, :, past:]
    g2 = jnp.tile(attn_g.astype(f32), 2).reshape(1, LANES)
    full = lambda a: pl.BlockSpec(a.shape, lambda b: (0,) * a.ndim)
    row = lambda n: pl.BlockSpec((s, n), lambda b: (b, 0))
    return pl.pallas_call(
        functools.partial(_attn_sample_kernel, lam_init=lam_init),
        grid=(nb,),
        in_specs=[row(D_QK_ALL), row(D_QK_ALL), row(D_ATT),
                  pl.BlockSpec((1, D_QK_ALL, past), lambda b: (b, 0, 0)),
                  pl.BlockSpec((1, D_ATT, past), lambda b: (b, 0, 0)),
                  full(bc), full(bn)] + [full(a) for a in lam_args] + [full(g2)],
        out_specs=row(D_ATT), out_shape=jax.ShapeDtypeStruct((T, D_ATT), bf16),
        compiler_params=_params(("parallel",)),
    )(q, k_new, v_new, kc, vc, bc, bn, *lam_args, g2)


def _ssd_kernel(z_ref, xbc_ref, dt_ref, dtT_ref, h0_ref, c0_ref, cw_ref, cb_ref, dtb_ref, dtbT_ref,
                alog_ref, alogT_ref, dsk_ref, g_ref, y_ref, hout_ref, cout_ref, xpad_sc, h_sc):
    q = xbc_ref.shape[0]
    c = pl.program_id(1)
    gw = HEADS_PER_GROUP * SSM_HEADDIM

    @pl.when(c == 0)
    def _():
        xpad_sc[0:SUBLANES] = c0_ref[0]
        h_sc[...] = h0_ref[0]

    xpad_sc[SUBLANES:SUBLANES + q] = xbc_ref[...]
    first = SUBLANES - (CONV_W - 1)
    conv = cb_ref[...]
    for i in range(CONV_W):
        conv = conv + xpad_sc[first + i:first + i + q] * cw_ref[i:i + 1, :]
    tail = xpad_sc[q:q + SUBLANES]
    xpad_sc[0:SUBLANES] = tail
    cout_ref[0] = tail
    act = conv * _sigmoid(conv)
    xs = act[:, :D_SSM]

    lane = lax.broadcasted_iota(i32, (1, LANES), 1)
    a_row = jnp.where(lane < H_S, -jnp.exp(alog_ref[...]), 0.0)
    dt = _softplus(dt_ref[...] + dtb_ref[...])
    ii = lax.broadcasted_iota(i32, (q, q), 0)
    jj = lax.broadcasted_iota(i32, (q, q), 1)
    causal = jj <= ii
    acs = _dot_f32_rhs(causal.astype(bf16), dt * a_row)
    sub = lax.broadcasted_iota(i32, (BF16_ROWS, 1), 0)
    a_col = jnp.where(sub < H_S, -jnp.exp(alogT_ref[...]), 0.0)
    dtT = _softplus(dtT_ref[0] + dtbT_ref[...])
    acsT = _dot_f32_lhs(dtT * a_col, (ii <= jj).astype(bf16))

    er = lax.broadcasted_iota(i32, (LANES, D_SSM), 0)
    ec = lax.broadcasted_iota(i32, (LANES, D_SSM), 1)
    expand = (ec // SSM_HEADDIM == er).astype(bf16)
    dt_x = _dot_f32_lhs(dt, expand)
    acs_x = _dot_f32_lhs(acs, expand)
    e_acs = jnp.exp(acs_x)
    acs_last = acs_x[q - 1:q, :]
    decay = jnp.exp(acs_last - acs_x) * dt_x
    dtx = xs * dt_x
    xd = xs * decay
    glane = lax.broadcasted_iota(i32, (1, gw), 1)

    ys = []
    for g in range(SSM_GROUPS):
        gs = slice(g * gw, (g + 1) * gw)
        bg = act[:, D_SSM + g * D_STATE:D_SSM + (g + 1) * D_STATE].astype(bf16)
        cg = act[:, D_SSM + (SSM_GROUPS + g) * D_STATE:D_SSM + (SSM_GROUPS + g + 1) * D_STATE].astype(bf16)
        cb = _dot_nt(cg, bg)
        h_old = h_sc[g]
        yg = _dot(cg, h_old.astype(bf16)) * e_acs[:, gs]
        dtx_g = dtx[:, gs]
        for e4 in range(HEADS_PER_GROUP):
            e = g * HEADS_PER_GROUP + e4
            seg = acs[:, e:e + 1] - acsT[e:e + 1, :]
            lmat = jnp.exp(jnp.where(causal, seg, -jnp.inf))
            rhs = jnp.where(glane // SSM_HEADDIM == e4, dtx_g, 0.0).astype(bf16)
            yg = yg + _dot((cb * lmat).astype(bf16), rhs)
        h_sc[g] = h_old * jnp.exp(acs_last[:, gs]) + _dot_tn(bg, xd[:, gs].astype(bf16))
        ys.append(yg)
    y = jnp.concatenate(ys, axis=1) + dsk_ref[...] * xs
    zz = z_ref[...]
    y = y * (zz * _sigmoid(zz))
    for g in range(SSM_GROUPS):
        gs = slice(g * gw, (g + 1) * gw)
        yg = y[:, gs]
        r = lax.rsqrt(jnp.mean(yg * yg, axis=-1, keepdims=True) + RMS_EPS)
        y_ref[:, gs] = (yg * r * g_ref[:, gs]).astype(bf16)
    hout_ref[0] = h_sc[...]


def _ssd(z, xbc, dt, dtT, h0, conv0, pw, B, L):
    q = min(SSD_TILE, L)
    nc = L // q
    gw = HEADS_PER_GROUP * SSM_HEADDIM
    dtT3 = dtT.reshape(BF16_ROWS, B, L).transpose(1, 0, 2)
    hT0 = h0.astype(f32).reshape(B, SSM_GROUPS, gw, D_STATE).transpose(0, 1, 3, 2)
    c0 = jnp.pad(conv0.astype(f32), ((0, 0), (SUBLANES - (CONV_W - 1), 0), (0, 0)))
    full = lambda a: pl.BlockSpec(a.shape, lambda b, c: (0,) * a.ndim)
    row = lambda n: pl.BlockSpec((q, n), lambda b, c: (b * nc + c, 0))
    params = [pw['conv_w'], pw['conv_b'], pw['dt_bias'], pw['dt_biasT'], pw['a_log'], pw['a_logT'],
              pw['d_skip'], pw['ssm_g']]
    y, hT, ctail = pl.pallas_call(
        _ssd_kernel, grid=(B, nc),
        in_specs=[row(D_SSM), row(CONV_DIM), row(LANES),
                  pl.BlockSpec((1, BF16_ROWS, q), lambda b, c: (b, 0, c)),
                  pl.BlockSpec((1, SSM_GROUPS, D_STATE, gw), lambda b, c: (b, 0, 0, 0)),
                  pl.BlockSpec((1, SUBLANES, CONV_DIM), lambda b, c: (b, 0, 0))] + [full(a) for a in params],
        out_specs=[row(D_SSM),
                   pl.BlockSpec((1, SSM_GROUPS, D_STATE, gw), lambda b, c: (b, 0, 0, 0)),
                   pl.BlockSpec((1, SUBLANES, CONV_DIM), lambda b, c: (b, 0, 0))],
        out_shape=[jax.ShapeDtypeStruct((B * L, D_SSM), bf16),
                   jax.ShapeDtypeStruct((B, SSM_GROUPS, D_STATE, gw), f32),
                   jax.ShapeDtypeStruct((B, SUBLANES, CONV_DIM), f32)],
        scratch_shapes=[pltpu.VMEM((q + SUBLANES, CONV_DIM), f32), pltpu.VMEM((SSM_GROUPS, D_STATE, gw), f32)],
        compiler_params=_params(("parallel", "arbitrary")),
    )(z, xbc, dt, dtT3, hT0, c0, *params)
    h_new = hT.transpose(0, 1, 3, 2).reshape(B, H_S, SSM_HEADDIM, D_STATE)
    return y, h_new, ctail[:, SUBLANES - (CONV_W - 1):, :]


def _pack_bf16_pairs(x):
    half = x.shape[1] // 2
    bits = lambda a: lax.bitcast_convert_type(a.astype(bf16).astype(f32), jnp.uint32)
    return (bits(x[:, half:]) & jnp.uint32(0xFFFF0000)) | (bits(x[:, :half]) >> 16)


def _unpack_bf16_pairs(word):
    lo = lax.bitcast_convert_type(word << 16, f32)
    hi = lax.bitcast_convert_type(word & jnp.uint32(0xFFFF0000), f32)
    return jnp.concatenate([lo, hi], axis=1)


def _mix_router_kernel(att_ref, ssm_ref, x_ref, woa_ref, wos_ref, g_ref, b_ref, wrh_ref, wrhl_ref, br_ref,
                       cnt0_ref, *rest):
    x1_ref, x1p_ref, te_ref, gate_ref, cnt_ref, run_sc = rest[-6:]

    @pl.when(pl.program_id(0) == 0)
    def _():
        run_sc[...] = cnt0_ref[...]

    mix = _dot(att_ref[...], woa_ref[...]) + _dot(ssm_ref[...], wos_ref[...])
    x1 = _layer_norm(DEEPNORM_ALPHA * x_ref[...] + mix, g_ref[...], b_ref[...])
    x1_ref[...] = x1
    x1p_ref[...] = _pack_bf16_pairs(x1)
    hi = x1.astype(bf16)
    lo = (x1 - hi.astype(f32)).astype(bf16)
    both = _dot(hi, wrhl_ref[...])
    logits = both[:, :LANES] + both[:, LANES:] + _dot(lo, wrh_ref[...]) + br_ref[...]
    lane =lax.broadcasted_iota(i32, logits.shape, 1)
    vals, idxs = [], []
    for _ in range(TOP_K):
        m = jnp.max(logits, axis=-1, keepdims=True)
        idx = jnp.min(jnp.where(logits == m, lane, LANES), axis=-1, keepdims=True)
        vals.append(m)
        idxs.append(idx)
        logits = jnp.where(lane == idx, -jnp.inf, logits)
    es = [jnp.exp(v - vals[0]) for v in vals]
    tot = es[0]
    for e in es[1:]:
        tot = tot + e
    tl = logits.shape[0]
    chosen = jnp.zeros(logits.shape, f32)
    for k in range(TOP_K):
        chosen = chosen + (lane == idxs[k]).astype(f32)
    ii = lax.broadcasted_iota(i32, (tl, tl), 0)
    jj = lax.broadcasted_iota(i32, (tl, tl), 1)
    before = _dot((jj < ii).astype(bf16), chosen.astype(bf16)) + run_sc[...]
    te = jnp.zeros(logits.shape, i32)
    gate = jnp.zeros(logits.shape, f32)
    for k in range(TOP_K):
        rank = jnp.sum(jnp.where(lane == idxs[k], before, 0.0), axis=-1, keepdims=True).astype(i32)
        te = jnp.where(lane == k, idxs[k], te)
        te = jnp.where(lane == TOP_K + k, rank, te)
        gate = jnp.where(lane == k, es[k] / tot, gate)
    te_ref[...] = te
    gate_ref[...] = gate
    run_sc[...] = run_sc[...] + jnp.sum(chosen, axis=0, keepdims=True)
    cnt_ref[...] = run_sc[...]


def _mix_router(att, ssm, x2d, w, counts0, packed, row0, rows_total):
    T, D = x2d.shape
    tl = min(ROW_TILE, T)
    assert row0 % tl == 0
    row = lambda n: pl.BlockSpec((tl, n), lambda i: (i, 0))
    full = lambda a: pl.BlockSpec(a.shape, lambda i: (0,) * a.ndim)
    ws = [w['wo_att'], w['wo_ssm'], w['ln1_g'], w['ln1_b'], w['wr_hi'], w['wr_hi_lo'], w['b_router'], counts0]
    ins = [att, ssm, x2d] + ws + [packed]
    in_specs = [row(D_ATT), row(D_SSM), row(D)] + [full(a) for a in ws] + [pl.BlockSpec(memory_space=pl.ANY)]
    aliases = {len(ins) - 1: 1}
    return pl.pallas_call(
        _mix_router_kernel, grid=(T // tl,), in_specs=in_specs,
        out_specs=[row(D), pl.BlockSpec((tl, D // 2), lambda i: (row0 // tl + i, 0)), row(LANES), row(LANES),
                   pl.BlockSpec((1, LANES), lambda i: (0, 0))],
        out_shape=[jax.ShapeDtypeStruct((T, D), f32), jax.ShapeDtypeStruct((rows_total, D // 2), jnp.uint32),
                   jax.ShapeDtypeStruct((T, LANES), i32),
                   jax.ShapeDtypeStruct((T, LANES), f32), jax.ShapeDtypeStruct((1, LANES), f32)],
        scratch_shapes=[pltpu.VMEM((1, LANES), f32)], input_output_aliases=aliases,
        compiler_params=_params(("arbitrary",)),
    )(*ins)


def _route(te, counts_f):
    T = te.shape[0]
    n_assign = T * TOP_K
    counts = counts_f[0, :N_EXPERTS].astype(i32)
    padded = (counts + MOE_BLOCK - 1) // MOE_BLOCK * MOE_BLOCK
    pad_end = jnp.cumsum(padded)
    pad_start = pad_end - padded
    top_e, rank = te[:, :TOP_K], te[:, TOP_K:2 * TOP_K]
    experts = jnp.arange(N_EXPERTS, dtype=i32)
    slot = rank + jnp.sum(jnp.where(top_e[..., None] == experts, pad_start, 0), axis=-1)
    n_blocks = -(-n_assign // MOE_BLOCK) + N_EXPERTS
    n_blocks += n_blocks % FFN_BLOCKS_PER_STEP
    block_start = jnp.arange(n_blocks, dtype=i32) * MOE_BLOCK
    block_e = jnp.minimum(jnp.sum(block_start[:, None] >= pad_end[None, :], axis=-1), N_EXPERTS - 1).astype(i32)
    n_used = (pad_end[-1] // MOE_BLOCK).astype(i32).reshape(1)
    row_end = jnp.sum(jnp.where(block_e[:, None] == experts, pad_start + counts, 0), axis=-1)
    rows_valid = jnp.clip(row_end - block_start, 0, MOE_BLOCK).astype(i32)
    return block_e, n_used, rows_valid, slot.astype(i32), n_blocks


SC_INDEX_WINDOW = 128
SC_ROWS = 32


def _sc_mesh():
    return plsc.VectorSubcoreMesh(core_axis_name="c", subcore_axis_name="s")


def _sc_move_rows(src, src_idx, dst_idx_list, n_out):
    M = src_idx.shape[0]
    D = src.shape[1]
    idx = [a.reshape(1, M) for a in [src_idx] + list(dst_idx_list)]

    @functools.partial(pl.kernel, out_type=jax.ShapeDtypeStruct((n_out, D), src.dtype), mesh=_sc_mesh(),
                       scratch_types=[pltpu.VMEM((2, SC_ROWS, D), src.dtype), pltpu.SemaphoreType.DMA((2,))])
    def move(s_hbm, *rest):
        i_hbm, o_hbm, buf, sem = rest[:-3], rest[-3], rest[-2], rest[-1]
        n_parts = SC_INDEX_WINDOW // SC_ROWS

        def body(si_vmem, *di_vmem):
            def fetch(j):
                part = pl.ds(j * SC_ROWS, SC_ROWS)
                return pltpu.async_copy(s_hbm.at[si_vmem.at[0, part]], buf.at[j % 2], sem.at[j % 2])

            pending = fetch(0)
            for j in range(n_parts):
                nxt = fetch(j + 1) if j + 1 < n_parts else None
                pending.wait()
                part = pl.ds(j * SC_ROWS, SC_ROWS)
                for dv in di_vmem:
                    pltpu.sync_copy(buf.at[j % 2], o_hbm.at[dv.at[0, part]])
                pending = nxt

        pltpu.emit_pipeline(
            body, grid=(M // SC_INDEX_WINDOW,),
            in_specs=[pl.BlockSpec((1, SC_INDEX_WINDOW), lambda i: (0, i))] * len(idx),
            out_specs=[], core_axis_name=("c", "s"), dimension_semantics=(pltpu.PARALLEL,),
        )(*i_hbm)

    return move(src, *idx)


def _sc_scatter_rows(x, idx_k, n_rows):
    return _sc_move_rows(x, jnp.arange(x.shape[0], dtype=i32), idx_k, n_rows)


def _sc_gather_rows(src, idx):
    M = idx.shape[0]
    return _sc_move_rows(src, idx, [jnp.arange(M, dtype=i32)], M)


def _deinterleave_kernel(w_ref, g_ref, u_ref):
    tn = 2 * LANES
    r = lax.broadcasted_iota(i32, (2 * tn, tn), 0)
    c = lax.broadcasted_iota(i32, (2 * tn, tn), 1)
    pick_g = (r == 2 * c).astype(bf16)
    pick_u = (r == 2 * c + 1).astype(bf16)
    for j in range(g_ref.shape[-1] // tn):
        wb = w_ref[0, :, j * 2 * tn:(j + 1) * 2 * tn].astype(bf16)
        g_ref[0, :, j * tn:(j + 1) * tn] = _dot(wb, pick_g).astype(bf16)
        u_ref[0, :, j * tn:(j + 1) * tn] = _dot(wb, pick_u).astype(bf16)


def _deinterleave(w_gu):
    E, D, F2 = w_gu.shape
    out = jax.ShapeDtypeStruct((E, D, F2 // 2), bf16)
    return pl.pallas_call(
        _deinterleave_kernel, grid=(E,),
        in_specs=[pl.BlockSpec((1, D, F2), lambda e: (e, 0, 0))],
        out_specs=[pl.BlockSpec((1, D, F2 // 2), lambda e: (e, 0, 0))] * 2,
        out_shape=[out, out],
        compiler_params=_params(("parallel",)),
    )(w_gu)


def _ffn_kernel(be_ref, nu_ref, rv_ref, xs_ref, *refs):
    del be_ref
    y_ref = refs[-1]
    wsets = [refs[6 * j:6 * j + 6] for j in range(FFN_BLOCKS_PER_STEP)]
    first = pl.program_id(0) * FFN_BLOCKS_PER_STEP
    n_live = jnp.clip(nu_ref[0] - first, 0, FFN_BLOCKS_PER_STEP)
    row_id = lax.broadcasted_iota(i32, (MOE_BLOCK, 1), 0)

    def block(j):
        wg_ref, wu_ref, bg_ref, bu_ref, wd_ref, bd_ref = wsets[j]
        rows = pl.ds(j * MOE_BLOCK, MOE_BLOCK)
        words = jnp.where(row_id < rv_ref[first + j], xs_ref[rows, :], jnp.uint32(0))
        xb = _unpack_bf16_pairs(words).astype(bf16)
        g = _dot(xb, wg_ref[0]) + bg_ref[0]
        u = _dot(xb, wu_ref[0]) + bu_ref[0]
        g = jnp.minimum(g, SWIGLU_LIMIT)
        u = jnp.clip(u, -SWIGLU_LIMIT, SWIGLU_LIMIT)
        act = (u + 1.0) * g * _sigmoid(SWIGLU_ALPHA * g)
        y_ref[rows, :] = _pack_bf16_pairs(_dot(act.astype(bf16), wd_ref[0]) + bd_ref[0])

    for live in range(FFN_BLOCKS_PER_STEP + 1):
        @pl.when(n_live == live)
        def _(live=live):
            for j in range(live):
                block(j)
            for j in range(live, FFN_BLOCKS_PER_STEP):
                y_ref[pl.ds(j * MOE_BLOCK, MOE_BLOCK), :] = jnp.zeros((MOE_BLOCK, y_ref.shape[1]), jnp.uint32)


def _expert_ffn(xs, block_e, n_used, rows_valid, w):
    n_rows, half = xs.shape
    D = 2 * half
    nps = FFN_BLOCKS_PER_STEP
    n_steps = n_rows // (MOE_BLOCK * nps)
    F = w['w_g'].shape[2]
    wspecs, wargs = [], []
    for j in range(nps):
        pick = lambda i, be, nu, rv, j=j: (be[i * nps + j], 0, 0)
        wspecs += [pl.BlockSpec((1, D, F), pick), pl.BlockSpec((1, D, F), pick), pl.BlockSpec((1, 1, F), pick),
                   pl.BlockSpec((1, 1, F), pick), pl.BlockSpec((1, F, D), pick), pl.BlockSpec((1, 1, D), pick)]
        wargs += [w['w_g'], w['w_u'], w['b_g'], w['b_u'], w['w_d'], w['b_d']]
    grid_spec = pltpu.PrefetchScalarGridSpec(
        num_scalar_prefetch=3, grid=(n_steps,),
        in_specs=[pl.BlockSpec((nps * MOE_BLOCK, half),
                               lambda i, be, nu, rv: (jnp.minimum(i, (nu[0] - 1) // nps), 0))] + wspecs,
        out_specs=pl.BlockSpec((nps * MOE_BLOCK, half), lambda i, be, nu, rv: (i, 0)))
    return pl.pallas_call(
        _ffn_kernel, grid_spec=grid_spec,
        out_shape=jax.ShapeDtypeStruct((n_rows, half), jnp.uint32),
        compiler_params=_params(("arbitrary",)),
    )(block_e, n_used, rows_valid, xs, *wargs)


def _combine_dense_kernel(rows_ref, x1_ref, gate_ref, g_ref, b_ref, *rest):
    y_ref = rest[-1]
    gate = gate_ref[...]
    ff = gate[:, 0:1] * _unpack_bf16_pairs(rows_ref[0])
    for k in range(1, TOP_K):
        ff = ff + gate[:, k:k + 1] * _unpack_bf16_pairs(rows_ref[k])
    y_ref[...] = _layer_norm(DEEPNORM_ALPHA * x1_ref[...] + ff, g_ref[...], b_ref[...])


def _combine_dense(rows, x1, gate, ln_g, ln_b, chunk, y_partial):
    T, D = x1.shape
    tc = rows.shape[1]
    tl = min(ROW_TILE, tc)
    first = chunk * (tc // tl)
    row = lambda n: pl.BlockSpec((tl, n), lambda i: (first + i, 0))
    full = lambda a: pl.BlockSpec(a.shape, lambda i: (0,) * a.ndim)
    ins = [rows, x1, gate, ln_g, ln_b]
    in_specs = [pl.BlockSpec((TOP_K, tl, D // 2), lambda i: (0, i, 0)), row(D), row(LANES), full(ln_g), full(ln_b)]
    aliases = {}
    if y_partial is not None:
        ins.append(y_partial)
        in_specs.append(pl.BlockSpec(memory_space=pl.ANY))
        aliases = {len(ins) - 1: 0}
    return pl.pallas_call(
        _combine_dense_kernel, grid=(tc // tl,), in_specs=in_specs,
        out_specs=row(D), out_shape=jax.ShapeDtypeStruct((T, D), f32), input_output_aliases=aliases,
        compiler_params=_params(("parallel",)),
    )(*ins)


def _prep_weights(l, w_in, conv_w, conv_b, dt_bias, a_log, d_skip, ssm_norm_g, w_o, ln1_g, ln1_b,
                  w_router, b_router, w_gate_up, b_gate_up, w_down, b_down, ln2_g, ln2_b):
    wi = w_in[l]
    c0, c1, c2, c3 = D_QK_ALL, 2 * D_QK_ALL, 2 * D_QK_ALL + D_ATT, 2 * D_QK_ALL + D_ATT + D_SSM
    c4 = c3 + CONV_DIM
    wdt = wi[:, c4:c4 + H_S]
    pad_lane = lambda v, fill=0.0: jnp.pad(v.astype(f32).reshape(1, -1), ((0, 0), (0, LANES - v.shape[-1])),
                                            constant_values=fill)
    pad_col = lambda v: jnp.pad(v.astype(f32).reshape(-1, 1), ((0, BF16_ROWS - v.shape[-1]), (0, 0)))
    wr = jnp.pad(w_router[l].astype(f32), ((0, 0), (0, LANES - N_EXPERTS)))
    wr_hi = wr.astype(bf16)
    wgu = w_gate_up[l]
    return {
        'wq': wi[:, :c0].astype(bf16), 'wqT': wi[:, :c0].T.astype(bf16),
        'wk': wi[:, c0:c1].astype(bf16), 'wkT': wi[:, c0:c1].T.astype(bf16), 'wv': wi[:, c1:c2].astype(bf16), 'wvT': wi[:, c1:c2].T.astype(bf16),
        'wz': wi[:, c2:c3].astype(bf16), 'wx': wi[:, c3:c4].astype(bf16),
        'wdt': jnp.pad(wdt, ((0, 0), (0, LANES - H_S))).astype(bf16),
        'wdtT': jnp.pad(wdt.T, ((0, BF16_ROWS - H_S), (0, 0))).astype(bf16),
        'conv_w': conv_w[l].astype(f32), 'conv_b': conv_b[l].astype(f32).reshape(1, -1),
        'dt_bias': pad_lane(dt_bias[l]), 'dt_biasT': pad_col(dt_bias[l]),
        'a_log': pad_lane(a_log[l]), 'a_logT': pad_col(a_log[l]),
        'd_skip': jnp.repeat(d_skip[l].astype(f32), SSM_HEADDIM).reshape(1, -1),
        'ssm_g': ssm_norm_g[l].astype(f32).reshape(1, -1),
        'wo_att': w_o[l][:D_ATT].astype(bf16), 'wo_ssm': w_o[l][D_ATT:].astype(bf16),
        'ln1_g': ln1_g[l].astype(f32).reshape(1, -1), 'ln1_b': ln1_b[l].astype(f32).reshape(1, -1),
        'wr_hi': wr_hi, 'wr_hi_lo': jnp.concatenate([wr_hi, (wr - wr_hi.astype(f32)).astype(bf16)], axis=1),
        'b_router': pad_lane(b_router[l], NEG_INF),
        'w_gu': wgu,
        'b_g': b_gate_up[l][:, None, 0::2].astype(f32), 'b_u': b_gate_up[l][:, None, 1::2].astype(f32),
        'w_d': w_down[l].astype(bf16), 'b_d': b_down[l][:, None, :].astype(f32),
        'ln2_g': ln2_g[l].astype(f32).reshape(1, -1), 'ln2_b': ln2_b[l].astype(f32).reshape(1, -1),
    }


def _mixers(x, l, w, rel_bias, lam_args, attn_g, k_past, v_past, h0, conv0):
    B, L, D = x.shape
    x2d = x.reshape(B * L, D)
    lam_init = 0.8 - 0.6 * math.exp(-0.3 * l)
    prompt = k_past is None
    proj = _in_proj(x2d, w, prompt, B)
    z, xbc, dt, dtT = proj[:4]
    if prompt:
        kT, vT, kb, qT, v1T = proj[4:]
        att = _attn_prompt(qT, kb, v1T, rel_bias, lam_args, attn_g, B, L, lam_init)
        k_rows = kT.reshape(B, H_A, 2 * DQK, L).transpose(0, 3, 1, 2)
        v_rows = vT.reshape(B, H_A, DV, L).transpose(0, 3, 1, 2)
    else:
        k, v, q = proj[4:]
        att = _attn_sample(q, k, v, k_past, v_past, rel_bias, lam_args, attn_g, lam_init)
        k_rows = k.reshape(B, L, H_A, 2 * DQK)
        v_rows = v.reshape(B, L, H_A, DV)
    ssm, h_new, conv_new = _ssd(z, xbc, dt, dtT, h0, conv0, w, B, L)
    return (att, ssm, x2d), (k_rows, v_rows, h_new, conv_new)


def _moe_layer(streams, w):
    rows_total = sum(x2d.shape[0] for _, _, x2d in streams)
    counts = jnp.zeros((1, LANES), f32)
    packed = jnp.zeros((rows_total, streams[0][2].shape[1] // 2), jnp.uint32)
    row0, routed = 0, []
    for att, ssm, x2d in streams:
        x1, packed, te, gate, counts = _mix_router(att, ssm, x2d, w, counts, packed, row0, rows_total)
        routed.append((x1, gate, row0))
        row0 += x2d.shape[0]
        te_all = te[:, :2 * TOP_K] if len(routed) == 1 else jnp.concatenate([te_all, te[:, :2 * TOP_K]])
    block_e, n_used, rows_valid, slot, n_blocks = _route(te_all, counts)
    slot_k = slot.T
    xs = _sc_scatter_rows(packed, [slot_k[k] for k in range(TOP_K)], n_blocks * MOE_BLOCK)
    ys = _expert_ffn(xs, block_e, n_used, rows_valid, w)
    outs = []
    for x1, gate, r0 in routed:
        T, D = x1.shape
        n_groups = COMBINE_GROUPS if T % (COMBINE_GROUPS * ROW_TILE) == 0 else 1
        tc = T // n_groups
        y = None
        for c in range(n_groups):
            idx = slot_k[:, r0 + c * tc:r0 + (c + 1) * tc].reshape(-1)
            rows = _sc_gather_rows(ys, idx).reshape(TOP_K, tc, D // 2)
            y = _combine_dense(rows, x1, gate, w['ln2_g'], w['ln2_b'], c, y)
        outs.append(y)
    return outs


def kernel(x_prompt, x_sample, cache_k, cache_v, state_ssm, state_conv, rel_bias, w_in, lambda_q1, lambda_k1, lambda_q2, lambda_k2, attn_norm_g, conv_w, conv_b, dt_bias, a_log, d_skip, ssm_norm_g, w_o, ln1_g, ln1_b, w_router, b_router, w_gate_up, b_gate_up, w_down, b_down, ln2_g, ln2_b):
    yp, ys = x_prompt, x_sample
    bp = x_prompt.shape[0]
    depth = w_in.shape[0]
    outs = [[] for _ in range(8)]
    for l in range(depth):
        w = _prep_weights(l, w_in, conv_w, conv_b, dt_bias, a_log, d_skip, ssm_norm_g, w_o, ln1_g, ln1_b,
                          w_router, b_router, w_gate_up, b_gate_up, w_down, b_down, ln2_g, ln2_b)
        w['w_g'], w['w_u'] = _deinterleave(w.pop('w_gu'))
        lam_args = [a[l].astype(f32).reshape(1, -1) for a in (lambda_q1, lambda_k1, lambda_q2, lambda_k2)]
        h0 = jnp.zeros((bp, H_S, SSM_HEADDIM, D_STATE), f32)
        c0 = jnp.zeros((bp, CONV_W - 1, CONV_DIM), f32)
        mix_p, state_p = _mixers(yp, l, w, rel_bias, lam_args, attn_norm_g[l], None, None, h0, c0)
        mix_s, state_s = _mixers(ys, l, w, rel_bias, lam_args, attn_norm_g[l], cache_k[l], cache_v[l],
                                 state_ssm[l], state_conv[l])
        y2p, y2s = _moe_layer([mix_p, mix_s], w)
        yp, ys = y2p.reshape(yp.shape), y2s.reshape(ys.shape)
        for lst, a in zip(outs, state_p + state_s):
            lst.append(a)
    return (yp, ys) + tuple(jnp.stack(o) for o in outs)
```

```python
import functools
import math

import jax
import jax.numpy as jnp
from jax import lax
from jax.experimental import pallas as pl
from jax.experimental.pallas import tpu as pltpu
from jax.experimental.pallas import tpu_sc as plsc

f32 = jnp.float32
bf16 = jnp.bfloat16
i32 = jnp.int32

CHUNK = 64
H_A = 8
DQK = 32
DV = 2 * DQK
D_ATT = H_A * DV
SSM_HEADDIM = 64
H_S = 8
D_SSM = H_S * SSM_HEADDIM
SSM_GROUPS = 2
HEADS_PER_GROUP = H_S // SSM_GROUPS
D_STATE = 128
CONV_W = 4
CONV_DIM = D_SSM + 2 * SSM_GROUPS * D_STATE
D_QK_ALL = H_A * 2 * DQK
N_BUCKETS = 32
MAX_DISTANCE = 128
N_EXPERTS = 32
TOP_K = 4
SWIGLU_LIMIT = 7.0
SWIGLU_ALPHA = 1.702
MOE_BLOCK = 256
LN_EPS = 1e-5
RMS_EPS = 1e-5
NEG_INF = -1e30
DEPTH = 1
DEEPNORM_ALPHA = (2.0 * DEPTH) ** 0.25
LOG2E = math.log2(math.e)

LANES = 128
SUBLANES = 8
BF16_ROWS = 16
VMEM_LIMIT = 48 * 1024 * 1024

ROW_TILE = 512
ATT_TILE = 512
ATT_SUB = 256
SSD_TILE = 256
V_ROWS = DV + BF16_ROWS
FFN_BLOCKS_PER_STEP = 2
COMBINE_GROUPS = 4


def _params(semantics):
    return pltpu.CompilerParams(dimension_semantics=semantics, vmem_limit_bytes=VMEM_LIMIT)


def _dot(a, b):
    return jnp.dot(a, b, preferred_element_type=f32)


def _dot_nt(a, b):
    return lax.dot_general(a, b, (((1,), (1,)), ((), ())), preferred_element_type=f32)


def _dot_tn(a, b):
    return lax.dot_general(a, b, (((0,), (0,)), ((), ())), preferred_element_type=f32)


def _split3(a):
    hi = a.astype(bf16)
    r1 = a - hi.astype(f32)
    mid = r1.astype(bf16)
    lo = (r1 - mid.astype(f32)).astype(bf16)
    return hi, mid, lo


def _dot_f32_lhs(a, b_exact):
    hi, mid, lo = _split3(a)
    return _dot(hi, b_exact) + _dot(mid, b_exact) + _dot(lo, b_exact)


def _dot_f32_rhs(a_exact, b):
    hi, mid, lo = _split3(b)
    return _dot(a_exact, hi) + _dot(a_exact, mid) + _dot(a_exact, lo)


def _softplus(x):
    return jnp.maximum(x, 0.0) + jnp.log1p(jnp.exp(-jnp.abs(x)))


def _sigmoid(x):
    return 1.0 / (1.0 + jnp.exp(-x))


def _layer_norm(y, g, b):
    mu = jnp.mean(y, axis=-1, keepdims=True)
    yc = y - mu
    var = jnp.mean(yc * yc, axis=-1, keepdims=True)
    return yc * lax.rsqrt(var + LN_EPS) * g + b


def _lambda(lq1_ref, lk1_ref, lq2_ref, lk2_ref, lam_init):
    s1 = jnp.sum(lq1_ref[...] * lk1_ref[...], axis=-1, keepdims=True)
    s2 = jnp.sum(lq2_ref[...] * lk2_ref[...], axis=-1, keepdims=True)
    return jnp.exp(s1) - jnp.exp(s2) + lam_init


def _in_proj_kernel(x_ref, wk_ref, wv_ref, wz_ref, wx_ref, wdt_ref, wdtT_ref, wq_ref, wkT_ref, wvT_ref, *outs,
                    prompt):
    xb = x_ref[...].astype(bf16)
    z_ref, xbc_ref, dt_ref, dtT_ref = outs[:4]
    k = _dot(xb, wk_ref[...])
    z_ref[...] = _dot(xb, wz_ref[...])
    xbc_ref[...] = _dot(xb, wx_ref[...])
    dt_ref[...] = _dot(xb, wdt_ref[...])
    dtT_ref[...] = _dot_nt(wdtT_ref[...], xb)
    scale = DQK ** -0.5
    if prompt:
        kT_ref, vT_ref, kb_ref, qT_ref, v1T_ref = outs[4:]
        kb_ref[...] = k.astype(bf16)
        kT_ref[0] = _dot_nt(wkT_ref[...], xb)
        qT_ref[...] = (_dot_nt(wq_ref[...], xb) * (scale * LOG2E)).astype(bf16)
        vT32 = _dot_nt(wvT_ref[...], xb)
        vT_ref[0] = vT32
        vT = vT32.astype(bf16)
        ones = jnp.ones((BF16_ROWS, vT.shape[1]), bf16)
        for h in range(H_A):
            v1T_ref[h * V_ROWS:h * V_ROWS + DV, :] = vT[h * DV:(h + 1) * DV, :]
            v1T_ref[h * V_ROWS + DV:(h + 1) * V_ROWS, :] = ones
    else:
        k_ref, v_ref, q_ref = outs[4:]
        k_ref[...] = k
        v_ref[...] = _dot(xb, wv_ref[...])
        q_ref[...] = (_dot(xb, wq_ref[...]) * scale).astype(bf16)


def _in_proj(x2d, w, prompt, B):
    T, D = x2d.shape
    tl = min(ROW_TILE, T)
    L = T // B
    n_l = L // tl
    grid = (T // tl,)
    row = lambda n: pl.BlockSpec((tl, n), lambda i: (i, 0))
    col = lambda n: pl.BlockSpec((n, tl), lambda i: (0, i))
    per_stream = lambda n: pl.BlockSpec((1, n, tl), lambda i: (i // n_l, 0, i % n_l))
    full = lambda a: pl.BlockSpec(a.shape, lambda i: (0,) * a.ndim)
    wq = w['wqT'] if prompt else w['wq']
    ins = [x2d, w['wk'], w['wv'], w['wz'], w['wx'], w['wdt'], w['wdtT'], wq, w['wkT'], w['wvT']]
    in_specs = [row(D)] + [full(a) for a in ins[1:]]
    out_shape = [jax.ShapeDtypeStruct((T, D_SSM), f32), jax.ShapeDtypeStruct((T, CONV_DIM), f32),
                 jax.ShapeDtypeStruct((T, LANES), f32), jax.ShapeDtypeStruct((BF16_ROWS, T), f32)]
    out_specs = [row(D_SSM), row(CONV_DIM), row(LANES), col(BF16_ROWS)]
    if prompt:
        assert L % tl == 0
        out_shape += [jax.ShapeDtypeStruct((B, D_QK_ALL, L), f32), jax.ShapeDtypeStruct((B, D_ATT, L), f32),
                      jax.ShapeDtypeStruct((T, D_QK_ALL), bf16), jax.ShapeDtypeStruct((D_QK_ALL, T), bf16),
                      jax.ShapeDtypeStruct((H_A * V_ROWS, T), bf16)]
        out_specs += [per_stream(D_QK_ALL), per_stream(D_ATT), row(D_QK_ALL), col(D_QK_ALL), col(H_A * V_ROWS)]
    else:
        out_shape += [jax.ShapeDtypeStruct((T, D_QK_ALL), f32), jax.ShapeDtypeStruct((T, D_ATT), f32),
                      jax.ShapeDtypeStruct((T, D_QK_ALL), bf16)]
        out_specs += [row(D_QK_ALL), row(D_ATT), row(D_QK_ALL)]
    return pl.pallas_call(
        functools.partial(_in_proj_kernel, prompt=prompt),
        grid=grid, in_specs=in_specs, out_specs=out_specs, out_shape=out_shape,
        compiler_params=_params(("parallel",)),
    )(*ins)


def _t5_bucket(rel):
    half = N_BUCKETS // 2
    max_exact = half // 2
    n = jnp.abs(rel)
    large = max_exact + (jnp.log(jnp.maximum(n, 1).astype(f32) / max_exact)
                         / math.log(MAX_DISTANCE / max_exact) * (half - max_exact)).astype(i32)
    large = jnp.minimum(large, half - 1)
    return jnp.where(rel > 0, half, 0) + jnp.where(n < max_exact, n, large)


def _far_bucket(min_dist):
    half = N_BUCKETS // 2
    max_exact = half // 2
    v = max_exact + int(math.log(min_dist / max_exact) / math.log(MAX_DISTANCE / max_exact) * (half - max_exact))
    return half - 1 if (min_dist >= max_exact and v - 1 >= half - 1) else None


def _bias_lookup(rel_bias, bucket):
    onehot = (bucket[..., None] == jnp.arange(N_BUCKETS, dtype=i32)).astype(f32)
    return jnp.dot(onehot, rel_bias.astype(f32), precision=lax.Precision.HIGHEST)


def _prompt_bias_tiles(rel_bias, ta):
    j = jnp.arange(ta, dtype=i32)[:, None]
    i = jnp.arange(ta, dtype=i32)[None, :]
    tiles = []
    for d in range(2):
        rel = j - i - d * ta
        b = _bias_lookup(rel_bias, _t5_bucket(rel))
        allowed = ((j // CHUNK) <= (i // CHUNK)) if d == 0 else jnp.ones((ta, ta), bool)
        tiles.append(jnp.where(allowed[..., None], b, NEG_INF))
    return jnp.transpose(jnp.stack(tiles), (3, 0, 1, 2))


def _sample_bias(rel_bias, past, s):
    q_pos = past + jnp.arange(s, dtype=i32)
    k_pos = jnp.arange(past + s, dtype=i32)
    rel = k_pos[None, :] - q_pos[:, None]
    b = _bias_lookup(rel_bias, _t5_bucket(rel))
    allowed = (k_pos[None, :] // CHUNK) <= (q_pos[:, None] // CHUNK)
    return jnp.transpose(jnp.where(allowed[..., None], b, NEG_INF), (2, 0, 1))


ATT_DIAG, ATT_SUBDIAG, ATT_FAR1, ATT_FAR2 = 0, 1, 2, 3


def _attn_prompt_kernel(qi_tab, ka_tab, kb_tab, kind_tab, qT_ref, k_ref, v1T_ref, k2_ref, v2T_ref, bias_ref,
                        cfar_ref, lq1_ref, lk1_ref, lq2_ref, lk2_ref, g_ref, o_ref,
                        qm_sc, m_sc, acc_sc, oT_sc, *, lam_init):
    del qi_tab, kb_tab
    p = pl.program_id(1)
    kind = kind_tab[p]
    ta = k_ref.shape[0]
    n_hp = H_A // 2

    @pl.when(ka_tab[p] == 0)
    def _():
        m_sc[...] = jnp.full(m_sc.shape, NEG_INF, f32)
        acc_sc[...] = jnp.zeros(acc_sc.shape, f32)
        row = lax.broadcasted_iota(i32, (LANES, ta), 0)
        for hp in range(n_hp):
            qt = qT_ref[hp * LANES:(hp + 1) * LANES, :]
            for gi in range(4):
                keep = (row >= gi * DQK) & (row < (gi + 1) * DQK)
                qm_sc[hp, :, gi * ta:(gi + 1) * ta] = jnp.where(keep, qt, jnp.zeros_like(qt))

    def scores(keys_ref, hp):
        return _dot(keys_ref[:, hp * LANES:(hp + 1) * LANES], qm_sc[hp])

    nb = ta // ATT_SUB

    def near_bias(hp, step_d):
        rows = []
        for jb in range(nb):
            cols = []
            for g in range(4):
                for ib in range(nb):
                    dd = step_d * nb + ib - jb
                    if dd < 0:
                        blk = jnp.full((ATT_SUB, ATT_SUB), NEG_INF * LOG2E, f32)
                    elif dd < 2:
                        blk = bias_ref[2 * hp + g // 2, dd]
                    else:
                        lo = g * ta + ib * ATT_SUB
                        blk = jnp.broadcast_to(cfar_ref[hp][:, lo:lo + ATT_SUB], (ATT_SUB, ATT_SUB))
                    cols.append(blk)
            rows.append(jnp.concatenate(cols, axis=1))
        return jnp.concatenate(rows, axis=0)

    def update(step_d, tiles):
        chain = [(kr, vr, hp) for kr, vr in tiles for hp in range(n_hp)]
        s_next = scores(chain[0][0], chain[0][2])
        for link, (_, values_ref, hp) in enumerate(chain):
            s = s_next
            if link + 1 < len(chain):
                s_next = scores(chain[link + 1][0], chain[link + 1][2])
            m_old = m_sc[hp]
            if step_d is not None:
                s = s + near_bias(hp, step_d)
                m_new = jnp.maximum(m_old, jnp.max(s, axis=0, keepdims=True))
                shift = m_new
            else:
                c = cfar_ref[hp]
                m_new = jnp.maximum(m_old, jnp.max(s, axis=0, keepdims=True) + c)
                shift = m_new - c
            alpha = jnp.exp2(m_old - m_new)
            pT = jnp.exp2(s - shift).astype(bf16)
            for hh in range(2):
                h = 2 * hp + hh
                cols = slice(hh * 2 * ta, (hh + 1) * 2 * ta)
                pv = _dot(values_ref[h * V_ROWS:(h + 1) * V_ROWS, :], pT[:, cols])
                acc_sc[h] = alpha[:, cols] * acc_sc[h] + pv
            m_sc[hp] = m_new

    @pl.when(kind == ATT_DIAG)
    def _():
        update(0, [(k_ref, v1T_ref)])

    @pl.when(kind == ATT_SUBDIAG)
    def _():
        update(1, [(k_ref, v1T_ref)])

    @pl.when(kind == ATT_FAR1)
    def _():
        update(None, [(k_ref, v1T_ref)])

    @pl.when(kind == ATT_FAR2)
    def _():
        update(None, [(k_ref, v1T_ref), (k2_ref, v2T_ref)])

    @pl.when(kind == ATT_DIAG)
    def _():
        lam = _lambda(lq1_ref, lk1_ref, lq2_ref, lk2_ref, lam_init)
        for h in range(H_A):
            a = acc_sc[h]
            a0 = a[:, :ta]
            a1 = a[:, ta:]
            o = a0[:DV] / a0[DV:DV + 1] - lam * (a1[:DV] / a1[DV:DV + 1])
            o = o * lax.rsqrt(jnp.mean(o * o, axis=0, keepdims=True) + RMS_EPS) * g_ref[...] * (1.0 - lam_init)
            oT_sc[h * DV:(h + 1) * DV, :] = o
        o_ref[...] = oT_sc[...].T.astype(bf16)


def _attn_prompt(qT, kb, v1T, rel_bias, lam_args, attn_g, B, L, lam_init):
    ta = min(ATT_TILE, L)
    assert L % ta == 0 and ta % ATT_SUB == 0 and ATT_SUB % CHUNK == 0
    nt = L // ta
    far = _far_bucket(ATT_SUB + 1)
    assert L <= 2 * ATT_SUB or far is not None, "key blocks two or more behind must share one bias bucket"
    far = far if far is not None else N_BUCKETS // 2 - 1
    steps = []
    idle_b = 0
    for q in range(nt):
        far_tiles = list(range(max(q - 1, 0)))
        for a in range(0, len(far_tiles) - 1, 2):
            steps.append((q, a, a + 1, ATT_FAR2))
            idle_b = a + 1
        if len(far_tiles) % 2:
            steps.append((q, far_tiles[-1], idle_b, ATT_FAR1))
        if q >= 1:
            steps.append((q, q - 1, idle_b, ATT_SUBDIAG))
        steps.append((q, q, idle_b, ATT_DIAG))
    qi_tab, ka_tab, kb_tab, kind_tab = (jnp.asarray([s[j] for s in steps], i32) for j in range(4))
    bias =_prompt_bias_tiles(rel_bias, ATT_SUB) * LOG2E
    cfar = jnp.repeat((rel_bias.astype(f32)[far] * LOG2E).reshape(H_A // 2, 2), 2 * ta, axis=1)[:, None, :]
    g_col = attn_g.astype(f32).reshape(DV, 1)
    T = B * L
    n_hp = H_A // 2
    full = lambda a: pl.BlockSpec(a.shape, lambda b, p, qt, ka, kb, kd: (0,) * a.ndim)
    grid_spec = pltpu.PrefetchScalarGridSpec(
        num_scalar_prefetch=4, grid=(B, len(steps)),
        in_specs=[
            pl.BlockSpec((D_QK_ALL, ta), lambda b, p, qt, ka, kb, kd: (0, b * nt + qt[p])),
            pl.BlockSpec((ta, D_QK_ALL), lambda b, p, qt, ka, kb, kd: (b * nt + ka[p], 0)),
            pl.BlockSpec((H_A * V_ROWS, ta), lambda b, p, qt, ka, kb, kd: (0, b * nt + ka[p])),
            pl.BlockSpec((ta, D_QK_ALL), lambda b, p, qt, ka, kb, kd: (b * nt + kb[p], 0)),
            pl.BlockSpec((H_A * V_ROWS, ta), lambda b, p, qt, ka, kb, kd: (0, b * nt + kb[p])),
            full(bias), full(cfar)] + [full(a) for a in lam_args] + [full(g_col)],
        out_specs=pl.BlockSpec((ta, D_ATT), lambda b, p, qt, ka, kb, kd: (b * nt + qt[p], 0)),
        scratch_shapes=[pltpu.VMEM((n_hp, LANES, 4 * ta), bf16), pltpu.VMEM((n_hp, 1, 4 * ta), f32),
                        pltpu.VMEM((H_A, V_ROWS, 2 * ta), f32), pltpu.VMEM((D_ATT, ta), f32)])
    return pl.pallas_call(
        functools.partial(_attn_prompt_kernel, lam_init=lam_init),
        grid_spec=grid_spec, out_shape=jax.ShapeDtypeStruct((T, D_ATT), bf16),
        compiler_params=_params(("parallel", "arbitrary")),
    )(qi_tab, ka_tab, kb_tab, kind_tab, qT, kb, v1T, kb, v1T, bias, cfar, *lam_args, g_col)


def _attn_sample_kernel(q_ref, kn_ref, vn_ref, kc_ref, vc_ref, bc_ref, bn_ref,
                        lq1_ref, lk1_ref, lq2_ref, lk2_ref, g2_ref, o_ref, *, lam_init):
    s_len = q_ref.shape[0]
    lane = lax.broadcasted_iota(i32, (s_len, LANES), 1)
    lam = _lambda(lq1_ref, lk1_ref, lq2_ref, lk2_ref, lam_init)
    for hp in range(H_A // 2):
        sl = slice(hp * LANES, (hp + 1) * LANES)
        qt = q_ref[:, sl]
        kct = kc_ref[0, sl, :].astype(bf16)
        vct = vc_ref[0, sl, :].astype(bf16)
        knt = kn_ref[:, sl].astype(bf16)
        vnt = vn_ref[:, sl].astype(bf16)
        outs = []
        for hh in range(2):
            h = 2 * hp + hh
            oc = []
            for c in range(2):
                gi = 2 * hh + c
                qm = jnp.where((lane >= gi * DQK) & (lane < (gi + 1) * DQK), qt, jnp.zeros_like(qt))
                s1 = _dot(qm, kct) + bc_ref[h]
                s2 = _dot_nt(qm, knt) + bn_ref[h]
                m = jnp.maximum(jnp.max(s1, axis=-1, keepdims=True), jnp.max(s2, axis=-1, keepdims=True))
                p1 = jnp.exp(s1 - m)
                p2 = jnp.exp(s2 - m)
                l = jnp.sum(p1, axis=-1, keepdims=True) + jnp.sum(p2, axis=-1, keepdims=True)
                pv = _dot_nt(p1.astype(bf16), vct) + _dot(p2.astype(bf16), vnt)
                oc.append(pv / l)
            o = oc[0] - lam * oc[1]
            in_head = (lane >= hh * DV) & (lane < (hh + 1) * DV)
            ms = jnp.sum(jnp.where(in_head, o * o, 0.0), axis=-1, keepdims=True) * (1.0 / DV)
            outs.append(o * lax.rsqrt(ms + RMS_EPS))
        ot = jnp.where(lane < DV, outs[0], outs[1]) * g2_ref[...] * (1.0 - lam_init)
        o_ref[:, sl] = ot.astype(bf16)


def _attn_sample(q, k_new, v_new, cache_k, cache_v, rel_bias, lam_args, attn_g, lam_init):
    nb, past = cache_k.shape[0], cache_k.shape[1]
    T = q.shape[0]
    s = T // nb
    kc = cache_k.transpose(0, 2, 3, 1).reshape(nb, D_QK_ALL, past)
    vc = cache_v.transpose(0, 2, 3, 1).reshape(nb, D_ATT, past)
    bias = _sample_bias(rel_bias, past, s)
    bc, bn = bias[:, :, :past], bias[:, :, past:]
    g2 = jnp.tile(attn_g.astype(f32), 2).reshape(1, LANES)
    full = lambda a: pl.BlockSpec(a.shape, lambda b: (0,) * a.ndim)
    row = lambda n: pl.BlockSpec((s, n), lambda b: (b, 0))
    return pl.pallas_call(
        functools.partial(_attn_sample_kernel, lam_init=lam_init),
        grid=(nb,),
        in_specs=[row(D_QK_ALL), row(D_QK_ALL), row(D_ATT),
                  pl.BlockSpec((1, D_QK_ALL, past), lambda b: (b, 0, 0)),
                  pl.BlockSpec((1, D_ATT, past), lambda b: (b, 0, 0)),
                  full(bc), full(bn)] + [full(a) for a in lam_args] + [full(g2)],
        out_specs=row(D_ATT), out_shape=jax.ShapeDtypeStruct((T, D_ATT), bf16),
        compiler_params=_params(("parallel",)),
    )(q, k_new, v_new, kc, vc, bc, bn, *lam_args, g2)


def _ssd_kernel(z_ref, xbc_ref, dt_ref, dtT_ref, h0_ref, c0_ref, cw_ref, cb_ref, dtb_ref, dtbT_ref,
                alog_ref, alogT_ref, dsk_ref, g_ref, y_ref, hout_ref, cout_ref, xpad_sc, h_sc):
    q = xbc_ref.shape[0]
    c = pl.program_id(1)
    gw = HEADS_PER_GROUP * SSM_HEADDIM

    @pl.when(c == 0)
    def _():
        xpad_sc[0:SUBLANES] = c0_ref[0]
        h_sc[...] = h0_ref[0]

    xpad_sc[SUBLANES:SUBLANES + q] = xbc_ref[...]
    first = SUBLANES - (CONV_W - 1)
    conv = cb_ref[...]
    for i in range(CONV_W):
        conv = conv + xpad_sc[first + i:first + i + q] * cw_ref[i:i + 1, :]
    tail = xpad_sc[q:q + SUBLANES]
    xpad_sc[0:SUBLANES] = tail
    cout_ref[0] = tail
    act = conv * _sigmoid(conv)
    xs = act[:, :D_SSM]

    lane = lax.broadcasted_iota(i32, (1, LANES), 1)
    a_row = jnp.where(lane < H_S, -jnp.exp(alog_ref[...]), 0.0)
    dt = _softplus(dt_ref[...] + dtb_ref[...])
    ii = lax.broadcasted_iota(i32, (q, q), 0)
    jj = lax.broadcasted_iota(i32, (q, q), 1)
    causal = jj <= ii
    acs = _dot_f32_rhs(causal.astype(bf16), dt * a_row)
    sub = lax.broadcasted_iota(i32, (BF16_ROWS, 1), 0)
    a_col = jnp.where(sub < H_S, -jnp.exp(alogT_ref[...]), 0.0)
    dtT = _softplus(dtT_ref[0] + dtbT_ref[...])
    acsT = _dot_f32_lhs(dtT * a_col, (ii <= jj).astype(bf16))

    er = lax.broadcasted_iota(i32, (LANES, D_SSM), 0)
    ec = lax.broadcasted_iota(i32, (LANES, D_SSM), 1)
    expand = (ec // SSM_HEADDIM == er).astype(bf16)
    dt_x = _dot_f32_lhs(dt, expand)
    acs_x = _dot_f32_lhs(acs, expand)
    e_acs = jnp.exp(acs_x)
    acs_last = acs_x[q - 1:q, :]
    decay = jnp.exp(acs_last - acs_x) * dt_x
    dtx = xs * dt_x
    xd = xs * decay
    glane = lax.broadcasted_iota(i32, (1, gw), 1)

    ys = []
    for g in range(SSM_GROUPS):
        gs = slice(g * gw, (g + 1) * gw)
        bg = act[:, D_SSM + g * D_STATE:D_SSM + (g + 1) * D_STATE].astype(bf16)
        cg = act[:, D_SSM + (SSM_GROUPS + g) * D_STATE:D_SSM + (SSM_GROUPS + g + 1) * D_STATE].astype(bf16)
        cb = _dot_nt(cg, bg)
        h_old = h_sc[g]
        yg = _dot(cg, h_old.astype(bf16)) * e_acs[:, gs]
        dtx_g = dtx[:, gs]
        for e4 in range(HEADS_PER_GROUP):
            e = g * HEADS_PER_GROUP + e4
            seg = acs[:, e:e + 1] - acsT[e:e + 1, :]
            lmat = jnp.exp(jnp.where(causal, seg, -jnp.inf))
            rhs = jnp.where(glane // SSM_HEADDIM == e4, dtx_g, 0.0).astype(bf16)
            yg = yg + _dot((cb * lmat).astype(bf16), rhs)
        h_sc[g] = h_old * jnp.exp(acs_last[:, gs]) + _dot_tn(bg, xd[:, gs].astype(bf16))
        ys.append(yg)
    y = jnp.concatenate(ys, axis=1) + dsk_ref[...] * xs
    zz = z_ref[...]
    y = y * (zz * _sigmoid(zz))
    for g in range(SSM_GROUPS):
        gs = slice(g * gw, (g + 1) * gw)
        yg = y[:, gs]
        r = lax.rsqrt(jnp.mean(yg * yg, axis=-1, keepdims=True) + RMS_EPS)
        y_ref[:, gs] = (yg * r * g_ref[:, gs]).astype(bf16)
    hout_ref[0] = h_sc[...]


def _ssd(z, xbc, dt, dtT, h0, conv0, pw, B, L):
    q = min(SSD_TILE, L)
    nc = L // q
    gw = HEADS_PER_GROUP * SSM_HEADDIM
    dtT3 = dtT.reshape(BF16_ROWS, B, L).transpose(1, 0, 2)
    hT0 = h0.astype(f32).reshape(B, SSM_GROUPS, gw, D_STATE).transpose(0, 1, 3, 2)
    c0 = jnp.pad(conv0.astype(f32), ((0, 0), (SUBLANES - (CONV_W - 1), 0), (0, 0)))
    full = lambda a: pl.BlockSpec(a.shape, lambda b, c: (0,) * a.ndim)
    row = lambda n: pl.BlockSpec((q, n), lambda b, c: (b * nc + c, 0))
    params = [pw['conv_w'], pw['conv_b'], pw['dt_bias'], pw['dt_biasT'], pw['a_log'], pw['a_logT'],
              pw['d_skip'], pw['ssm_g']]
    y, hT, ctail = pl.pallas_call(
        _ssd_kernel, grid=(B, nc),
        in_specs=[row(D_SSM), row(CONV_DIM), row(LANES),
                  pl.BlockSpec((1, BF16_ROWS, q), lambda b, c: (b, 0, c)),
                  pl.BlockSpec((1, SSM_GROUPS, D_STATE, gw), lambda b, c: (b, 0, 0, 0)),
                  pl.BlockSpec((1, SUBLANES, CONV_DIM), lambda b, c: (b, 0, 0))] + [full(a) for a in params],
        out_specs=[row(D_SSM),
                   pl.BlockSpec((1, SSM_GROUPS, D_STATE, gw), lambda b, c: (b, 0, 0, 0)),
                   pl.BlockSpec((1, SUBLANES, CONV_DIM), lambda b, c: (b, 0, 0))],
        out_shape=[jax.ShapeDtypeStruct((B * L, D_SSM), bf16),
                   jax.ShapeDtypeStruct((B, SSM_GROUPS, D_STATE, gw), f32),
                   jax.ShapeDtypeStruct((B, SUBLANES, CONV_DIM), f32)],
        scratch_shapes=[pltpu.VMEM((q + SUBLANES, CONV_DIM), f32), pltpu.VMEM((SSM_GROUPS, D_STATE, gw), f32)],
        compiler_params=_params(("parallel", "arbitrary")),
    )(z, xbc, dt, dtT3, hT0, c0, *params)
    h_new = hT.transpose(0, 1, 3, 2).reshape(B, H_S, SSM_HEADDIM, D_STATE)
    return y, h_new, ctail[:, SUBLANES - (CONV_W - 1):, :]


def _pack_bf16_pairs(x):
    half = x.shape[1] // 2
    bits = lambda a: lax.bitcast_convert_type(a.astype(bf16).astype(f32), jnp.uint32)
    return (bits(x[:, half:]) & jnp.uint32(0xFFFF0000)) | (bits(x[:, :half]) >> 16)


def _unpack_bf16_pairs(word):
    lo = lax.bitcast_convert_type(word << 16, f32)
    hi = lax.bitcast_convert_type(word & jnp.uint32(0xFFFF0000), f32)
    return jnp.concatenate([lo, hi], axis=1)


def _mix_router_kernel(att_ref, ssm_ref, x_ref, woa_ref, wos_ref, g_ref, b_ref, wrh_ref, wrhl_ref, br_ref,
                       cnt0_ref, *rest):
    x1_ref, x1p_ref, te_ref, gate_ref, cnt_ref, run_sc, earlier_sc = rest[-7:]

    @pl.when(pl.program_id(0) == 0)
    def _():
        run_sc[...] = cnt0_ref[...]
        ii = lax.broadcasted_iota(i32, earlier_sc.shape, 0)
        jj = lax.broadcasted_iota(i32, earlier_sc.shape, 1)
        earlier_sc[...] = (jj < ii).astype(bf16)

    mix = _dot(att_ref[...], woa_ref[...]) + _dot(ssm_ref[...], wos_ref[...])
    x1 = _layer_norm(DEEPNORM_ALPHA * x_ref[...] + mix, g_ref[...], b_ref[...])
    x1_ref[...] = x1
    x1p_ref[...] = _pack_bf16_pairs(x1)
    hi = x1.astype(bf16)
    lo = (x1 - hi.astype(f32)).astype(bf16)
    both = _dot(hi, wrhl_ref[...])
    logits = both[:, :LANES] + both[:, LANES:] + _dot(lo, wrh_ref[...]) + br_ref[...]
    lane =lax.broadcasted_iota(i32, logits.shape, 1)
    vals, idxs = [], []
    for _ in range(TOP_K):
        m = jnp.max(logits, axis=-1, keepdims=True)
        idx = jnp.min(jnp.where(logits == m, lane, LANES), axis=-1, keepdims=True)
        vals.append(m)
        idxs.append(idx)
        logits = jnp.where(lane == idx, -jnp.inf, logits)
    es = [jnp.exp(v - vals[0]) for v in vals]
    tot = es[0]
    for e in es[1:]:
        tot = tot + e
    chosen = jnp.zeros(logits.shape, f32)
    for k in range(TOP_K):
        chosen = chosen + (lane == idxs[k]).astype(f32)
    before = _dot(earlier_sc[...], chosen.astype(bf16)) + run_sc[...]
    te = jnp.zeros(logits.shape, i32)
    gate = jnp.zeros(logits.shape, f32)
    for k in range(TOP_K):
        rank = jnp.sum(jnp.where(lane == idxs[k], before, 0.0), axis=-1, keepdims=True).astype(i32)
        te = jnp.where(lane == k, idxs[k], te)
        te = jnp.where(lane == TOP_K + k, rank, te)
        gate = jnp.where(lane == k, es[k] / tot, gate)
    te_ref[...] = te
    gate_ref[...] = gate
    run_sc[...] = run_sc[...] + jnp.sum(chosen, axis=0, keepdims=True)
    cnt_ref[...] = run_sc[...]


def _mix_router(att, ssm, x2d, w, counts0, packed, row0, rows_total):
    T, D = x2d.shape
    tl = min(ROW_TILE, T)
    assert row0 % tl == 0
    row = lambda n: pl.BlockSpec((tl, n), lambda i: (i, 0))
    full = lambda a: pl.BlockSpec(a.shape, lambda i: (0,) * a.ndim)
    ws = [w['wo_att'], w['wo_ssm'], w['ln1_g'], w['ln1_b'], w['wr_hi'], w['wr_hi_lo'], w['b_router'], counts0]
    ins = [att, ssm, x2d] + ws + [packed]
    in_specs = [row(D_ATT), row(D_SSM), row(D)] + [full(a) for a in ws] + [pl.BlockSpec(memory_space=pl.ANY)]
    aliases = {len(ins) - 1: 1}
    return pl.pallas_call(
        _mix_router_kernel, grid=(T // tl,), in_specs=in_specs,
        out_specs=[row(D), pl.BlockSpec((tl, D // 2), lambda i: (row0 // tl + i, 0)), row(LANES), row(LANES),
                   pl.BlockSpec((1, LANES), lambda i: (0, 0))],
        out_shape=[jax.ShapeDtypeStruct((T, D), f32), jax.ShapeDtypeStruct((rows_total, D // 2), jnp.uint32),
                   jax.ShapeDtypeStruct((T, LANES), i32),
                   jax.ShapeDtypeStruct((T, LANES), f32), jax.ShapeDtypeStruct((1, LANES), f32)],
        scratch_shapes=[pltpu.VMEM((1, LANES), f32), pltpu.VMEM((tl, tl), bf16)], input_output_aliases=aliases,
        compiler_params=_params(("arbitrary",)),
    )(*ins)


def _route(te, counts_f):
    T = te.shape[0]
    n_assign = T * TOP_K
    counts = counts_f[0, :N_EXPERTS].astype(i32)
    padded = (counts + MOE_BLOCK - 1) // MOE_BLOCK * MOE_BLOCK
    pad_end = jnp.cumsum(padded)
    pad_start = pad_end - padded
    top_e, rank = te[:, :TOP_K], te[:, TOP_K:2 * TOP_K]
    experts = jnp.arange(N_EXPERTS, dtype=i32)
    slot = rank + jnp.sum(jnp.where(top_e[..., None] == experts, pad_start, 0), axis=-1)
    n_blocks = -(-n_assign // MOE_BLOCK) + N_EXPERTS
    n_blocks += n_blocks % FFN_BLOCKS_PER_STEP
    block_start = jnp.arange(n_blocks, dtype=i32) * MOE_BLOCK
    block_e = jnp.minimum(jnp.sum(block_start[:, None] >= pad_end[None, :], axis=-1), N_EXPERTS - 1).astype(i32)
    n_used = (pad_end[-1] // MOE_BLOCK).astype(i32).reshape(1)
    row_end = jnp.sum(jnp.where(block_e[:, None] == experts, pad_start + counts, 0), axis=-1)
    rows_valid = jnp.clip(row_end - block_start, 0, MOE_BLOCK).astype(i32)
    return block_e, n_used, rows_valid, slot.astype(i32), n_blocks


SC_INDEX_WINDOW = 128
SC_ROWS = 64


def _sc_mesh():
    return plsc.VectorSubcoreMesh(core_axis_name="c", subcore_axis_name="s")


def _sc_move_rows(src, src_idx, dst_idx_list, n_out):
    M = src_idx.shape[0]
    D = src.shape[1]
    idx = [a.reshape(1, M) for a in [src_idx] + list(dst_idx_list)]

    @functools.partial(pl.kernel, out_type=jax.ShapeDtypeStruct((n_out, D), src.dtype), mesh=_sc_mesh(),
                       scratch_types=[pltpu.VMEM((2, SC_ROWS, D), src.dtype), pltpu.SemaphoreType.DMA((2,))])
    def move(s_hbm, *rest):
        i_hbm, o_hbm, buf, sem = rest[:-3], rest[-3], rest[-2], rest[-1]
        n_parts = SC_INDEX_WINDOW // SC_ROWS

        def body(si_vmem, *di_vmem):
            def fetch(j):
                part = pl.ds(j * SC_ROWS, SC_ROWS)
                return pltpu.async_copy(s_hbm.at[si_vmem.at[0, part]], buf.at[j % 2], sem.at[j % 2])

            pending = fetch(0)
            for j in range(n_parts):
                nxt = fetch(j + 1) if j + 1 < n_parts else None
                pending.wait()
                part = pl.ds(j * SC_ROWS, SC_ROWS)
                for dv in di_vmem:
                    pltpu.sync_copy(buf.at[j % 2], o_hbm.at[dv.at[0, part]])
                pending = nxt

        pltpu.emit_pipeline(
            body, grid=(M // SC_INDEX_WINDOW,),
            in_specs=[pl.BlockSpec((1, SC_INDEX_WINDOW), lambda i: (0, i))] * len(idx),
            out_specs=[], core_axis_name=("c", "s"), dimension_semantics=(pltpu.PARALLEL,),
        )(*i_hbm)

    return move(src, *idx)


def _sc_scatter_rows(x, idx_k, n_rows):
    return _sc_move_rows(x, jnp.arange(x.shape[0], dtype=i32), idx_k, n_rows)


def _sc_gather_rows(src, idx):
    M = idx.shape[0]
    return _sc_move_rows(src, idx, [jnp.arange(M, dtype=i32)], M)


def _deinterleave_kernel(w_ref, g_ref, u_ref):
    tn = 2 * LANES
    r = lax.broadcasted_iota(i32, (2 * tn, tn), 0)
    c = lax.broadcasted_iota(i32, (2 * tn, tn), 1)
    pick_g = (r == 2 * c).astype(bf16)
    pick_u = (r == 2 * c + 1).astype(bf16)
    for j in range(g_ref.shape[-1] // tn):
        wb = w_ref[0, :, j * 2 * tn:(j + 1) * 2 * tn].astype(bf16)
        g_ref[0, :, j * tn:(j + 1) * tn] = _dot(wb, pick_g).astype(bf16)
        u_ref[0, :, j * tn:(j + 1) * tn] = _dot(wb, pick_u).astype(bf16)


def _deinterleave(w_gu):
    E, D, F2 = w_gu.shape
    out = jax.ShapeDtypeStruct((E, D, F2 // 2), bf16)
    return pl.pallas_call(
        _deinterleave_kernel, grid=(E,),
        in_specs=[pl.BlockSpec((1, D, F2), lambda e: (e, 0, 0))],
        out_specs=[pl.BlockSpec((1, D, F2 // 2), lambda e: (e, 0, 0))] * 2,
        out_shape=[out, out],
        compiler_params=_params(("parallel",)),
    )(w_gu)


def _ffn_kernel(be_ref, nu_ref, rv_ref, xs_ref, *refs):
    del be_ref
    y_ref = refs[-1]
    wsets = [refs[6 * j:6 * j + 6] for j in range(FFN_BLOCKS_PER_STEP)]
    first = pl.program_id(0) * FFN_BLOCKS_PER_STEP
    n_live = jnp.clip(nu_ref[0] - first, 0, FFN_BLOCKS_PER_STEP)
    row_id = lax.broadcasted_iota(i32, (MOE_BLOCK, 1), 0)

    def block(j):
        wg_ref, wu_ref, bg_ref, bu_ref, wd_ref, bd_ref = wsets[j]
        rows = pl.ds(j * MOE_BLOCK, MOE_BLOCK)
        words = jnp.where(row_id < rv_ref[first + j], xs_ref[rows, :], jnp.uint32(0))
        xb = _unpack_bf16_pairs(words).astype(bf16)
        g = _dot(xb, wg_ref[0]) + bg_ref[0]
        u = _dot(xb, wu_ref[0]) + bu_ref[0]
        g = jnp.minimum(g, SWIGLU_LIMIT)
        u = jnp.clip(u, -SWIGLU_LIMIT, SWIGLU_LIMIT)
        act = (u + 1.0) * g * _sigmoid(SWIGLU_ALPHA * g)
        y_ref[rows, :] = _pack_bf16_pairs(_dot(act.astype(bf16), wd_ref[0]) + bd_ref[0])

    for live in range(FFN_BLOCKS_PER_STEP + 1):
        @pl.when(n_live == live)
        def _(live=live):
            for j in range(live):
                block(j)
            for j in range(live, FFN_BLOCKS_PER_STEP):
                y_ref[pl.ds(j * MOE_BLOCK, MOE_BLOCK), :] = jnp.zeros((MOE_BLOCK, y_ref.shape[1]), jnp.uint32)


def _expert_ffn(xs, block_e, n_used, rows_valid, w):
    n_rows, half = xs.shape
    D = 2 * half
    nps = FFN_BLOCKS_PER_STEP
    n_steps = n_rows // (MOE_BLOCK * nps)
    F = w['w_g'].shape[2]
    wspecs, wargs = [], []
    for j in range(nps):
        pick = lambda i, be, nu, rv, j=j: (be[i * nps + j], 0, 0)
        wspecs += [pl.BlockSpec((1, D, F), pick), pl.BlockSpec((1, D, F), pick), pl.BlockSpec((1, 1, F), pick),
                   pl.BlockSpec((1, 1, F), pick), pl.BlockSpec((1, F, D), pick), pl.BlockSpec((1, 1, D), pick)]
        wargs += [w['w_g'], w['w_u'], w['b_g'], w['b_u'], w['w_d'], w['b_d']]
    grid_spec = pltpu.PrefetchScalarGridSpec(
        num_scalar_prefetch=3, grid=(n_steps,),
        in_specs=[pl.BlockSpec((nps * MOE_BLOCK, half),
                               lambda i, be, nu, rv: (jnp.minimum(i, (nu[0] - 1) // nps), 0))] + wspecs,
        out_specs=pl.BlockSpec((nps * MOE_BLOCK, half), lambda i, be, nu, rv: (i, 0)))
    return pl.pallas_call(
        _ffn_kernel, grid_spec=grid_spec,
        out_shape=jax.ShapeDtypeStruct((n_rows, half), jnp.uint32),
        compiler_params=_params(("arbitrary",)),
    )(block_e, n_used, rows_valid, xs, *wargs)


def _combine_dense_kernel(rows_ref, x1_ref, gate_ref, g_ref, b_ref, *rest):
    y_ref = rest[-1]
    gate = gate_ref[...]
    ff = gate[:, 0:1] * _unpack_bf16_pairs(rows_ref[0])
    for k in range(1, TOP_K):
        ff = ff + gate[:, k:k + 1] * _unpack_bf16_pairs(rows_ref[k])
    y_ref[...] = _layer_norm(DEEPNORM_ALPHA * x1_ref[...] + ff, g_ref[...], b_ref[...])


def _combine_dense(rows, x1, gate, ln_g, ln_b, chunk, y_partial):
    T, D = x1.shape
    tc = rows.shape[1]
    tl = min(ROW_TILE, tc)
    first = chunk * (tc // tl)
    row = lambda n: pl.BlockSpec((tl, n), lambda i: (first + i, 0))
    full = lambda a: pl.BlockSpec(a.shape, lambda i: (0,) * a.ndim)
    ins = [rows, x1, gate, ln_g, ln_b]
    in_specs = [pl.BlockSpec((TOP_K, tl, D // 2), lambda i: (0, i, 0)), row(D), row(LANES), full(ln_g), full(ln_b)]
    aliases = {}
    if y_partial is not None:
        ins.append(y_partial)
        in_specs.append(pl.BlockSpec(memory_space=pl.ANY))
        aliases = {len(ins) - 1: 0}
    return pl.pallas_call(
        _combine_dense_kernel, grid=(tc // tl,), in_specs=in_specs,
        out_specs=row(D), out_shape=jax.ShapeDtypeStruct((T, D), f32), input_output_aliases=aliases,
        compiler_params=_params(("parallel",)),
    )(*ins)


def _prep_weights(l, w_in, conv_w, conv_b, dt_bias, a_log, d_skip, ssm_norm_g, w_o, ln1_g, ln1_b,
                  w_router, b_router, w_gate_up, b_gate_up, w_down, b_down, ln2_g, ln2_b):
    wi = w_in[l]
    c0, c1, c2, c3 = D_QK_ALL, 2 * D_QK_ALL, 2 * D_QK_ALL + D_ATT, 2 * D_QK_ALL + D_ATT + D_SSM
    c4 = c3 + CONV_DIM
    wdt = wi[:, c4:c4 + H_S]
    pad_lane = lambda v, fill=0.0: jnp.pad(v.astype(f32).reshape(1, -1), ((0, 0), (0, LANES - v.shape[-1])),
                                            constant_values=fill)
    pad_col = lambda v: jnp.pad(v.astype(f32).reshape(-1, 1), ((0, BF16_ROWS - v.shape[-1]), (0, 0)))
    wr = jnp.pad(w_router[l].astype(f32), ((0, 0), (0, LANES - N_EXPERTS)))
    wr_hi = wr.astype(bf16)
    wgu = w_gate_up[l]
    return {
        'wq': wi[:, :c0].astype(bf16), 'wqT': wi[:, :c0].T.astype(bf16),
        'wk': wi[:, c0:c1].astype(bf16), 'wkT': wi[:, c0:c1].T.astype(bf16), 'wv': wi[:, c1:c2].astype(bf16), 'wvT': wi[:, c1:c2].T.astype(bf16),
        'wz': wi[:, c2:c3].astype(bf16), 'wx': wi[:, c3:c4].astype(bf16),
        'wdt': jnp.pad(wdt, ((0, 0), (0, LANES - H_S))).astype(bf16),
        'wdtT': jnp.pad(wdt.T, ((0, BF16_ROWS - H_S), (0, 0))).astype(bf16),
        'conv_w': conv_w[l].astype(f32), 'conv_b': conv_b[l].astype(f32).reshape(1, -1),
        'dt_bias': pad_lane(dt_bias[l]), 'dt_biasT': pad_col(dt_bias[l]),
        'a_log': pad_lane(a_log[l]), 'a_logT': pad_col(a_log[l]),
        'd_skip': jnp.repeat(d_skip[l].astype(f32), SSM_HEADDIM).reshape(1, -1),
        'ssm_g': ssm_norm_g[l].astype(f32).reshape(1, -1),
        'wo_att': w_o[l][:D_ATT].astype(bf16), 'wo_ssm': w_o[l][D_ATT:].astype(bf16),
        'ln1_g': ln1_g[l].astype(f32).reshape(1, -1), 'ln1_b': ln1_b[l].astype(f32).reshape(1, -1),
        'wr_hi': wr_hi, 'wr_hi_lo': jnp.concatenate([wr_hi, (wr - wr_hi.astype(f32)).astype(bf16)], axis=1),
        'b_router': pad_lane(b_router[l], NEG_INF),
        'w_gu': wgu,
        'b_g': b_gate_up[l][:, None, 0::2].astype(f32), 'b_u': b_gate_up[l][:, None, 1::2].astype(f32),
        'w_d': w_down[l].astype(bf16), 'b_d': b_down[l][:, None, :].astype(f32),
        'ln2_g': ln2_g[l].astype(f32).reshape(1, -1), 'ln2_b': ln2_b[l].astype(f32).reshape(1, -1),
    }


def _mixers(x, l, w, rel_bias, lam_args, attn_g, k_past, v_past, h0, conv0):
    B, L, D = x.shape
    x2d = x.reshape(B * L, D)
    lam_init = 0.8 - 0.6 * math.exp(-0.3 * l)
    prompt = k_past is None
    proj = _in_proj(x2d, w, prompt, B)
    z, xbc, dt, dtT = proj[:4]
    if prompt:
        kT, vT, kb, qT, v1T = proj[4:]
        att = _attn_prompt(qT, kb, v1T, rel_bias, lam_args, attn_g, B, L, lam_init)
        k_rows = kT.reshape(B, H_A, 2 * DQK, L).transpose(0, 3, 1, 2)
        v_rows = vT.reshape(B, H_A, DV, L).transpose(0, 3, 1, 2)
    else:
        k, v, q = proj[4:]
        att = _attn_sample(q, k, v, k_past, v_past, rel_bias, lam_args, attn_g, lam_init)
        k_rows = k.reshape(B, L, H_A, 2 * DQK)
        v_rows = v.reshape(B, L, H_A, DV)
    ssm, h_new, conv_new = _ssd(z, xbc, dt, dtT, h0, conv0, w, B, L)
    return (att, ssm, x2d), (k_rows, v_rows, h_new, conv_new)


def _moe_layer(streams, w):
    rows_total = sum(x2d.shape[0] for _, _, x2d in streams)
    counts = jnp.zeros((1, LANES), f32)
    packed = jnp.zeros((rows_total, streams[0][2].shape[1] // 2), jnp.uint32)
    row0, routed = 0, []
    for att, ssm, x2d in streams:
        x1, packed, te, gate, counts = _mix_router(att, ssm, x2d, w, counts, packed, row0, rows_total)
        routed.append((x1, gate, row0))
        row0 += x2d.shape[0]
        te_all = te[:, :2 * TOP_K] if len(routed) == 1 else jnp.concatenate([te_all, te[:, :2 * TOP_K]])
    block_e, n_used, rows_valid, slot, n_blocks = _route(te_all, counts)
    slot_k = slot.T
    xs = _sc_scatter_rows(packed, [slot_k[k] for k in range(TOP_K)], n_blocks * MOE_BLOCK)
    ys = _expert_ffn(xs, block_e, n_used, rows_valid, w)
    outs = []
    for x1, gate, r0 in routed:
        T, D = x1.shape
        n_groups = COMBINE_GROUPS if T % (COMBINE_GROUPS * ROW_TILE) == 0 else 1
        tc = T // n_groups
        y = None
        for c in range(n_groups):
            idx = slot_k[:, r0 + c * tc:r0 + (c + 1) * tc].reshape(-1)
            rows = _sc_gather_rows(ys, idx).reshape(TOP_K, tc, D // 2)
            y = _combine_dense(rows, x1, gate, w['ln2_g'], w['ln2_b'], c, y)
        outs.append(y)
    return outs


def kernel(x_prompt, x_sample, cache_k, cache_v, state_ssm, state_conv, rel_bias, w_in, lambda_q1, lambda_k1, lambda_q2, lambda_k2, attn_norm_g, conv_w, conv_b, dt_bias, a_log, d_skip, ssm_norm_g, w_o, ln1_g, ln1_b, w_router, b_router, w_gate_up, b_gate_up, w_down, b_down, ln2_g, ln2_b):
    yp, ys = x_prompt, x_sample
    bp = x_prompt.shape[0]
    depth = w_in.shape[0]
    outs = [[] for _ in range(8)]
    for l in range(depth):
        w = _prep_weights(l, w_in, conv_w, conv_b, dt_bias, a_log, d_skip, ssm_norm_g, w_o, ln1_g, ln1_b,
                          w_router, b_router, w_gate_up, b_gate_up, w_down, b_down, ln2_g, ln2_b)
        w['w_g'], w['w_u'] = _deinterleave(w.pop('w_gu'))
        lam_args = [a[l].astype(f32).reshape(1, -1) for a in (lambda_q1, lambda_k1, lambda_q2, lambda_k2)]
        h0 = jnp.zeros((bp, H_S, SSM_HEADDIM, D_STATE), f32)
        c0 = jnp.zeros((bp, CONV_W - 1, CONV_DIM), f32)
        mix_p, state_p = _mixers(yp, l, w, rel_bias, lam_args, attn_norm_g[l], None, None, h0, c0)
        mix_s, state_s = _mixers(ys, l, w, rel_bias, lam_args, attn_norm_g[l], cache_k[l], cache_v[l],
                                 state_ssm[l], state_conv[l])
        y2p, y2s = _moe_layer([mix_p, mix_s], w)
        yp, ys = y2p.reshape(yp.shape), y2s.reshape(ys.shape)
        for lst, a in zip(outs, state_p + state_s):
            lst.append(a)
    return (yp, ys) + tuple(jnp.stack(o) for o in outs)
```

```python
import functools
import math

import jax
import jax.numpy as jnp
from jax import lax
from jax.experimental import pallas as pl
from jax.experimental.pallas import tpu as pltpu
from jax.experimental.pallas import tpu_sc as plsc

f32 = jnp.float32
bf16 = jnp.bfloat16
i32 = jnp.int32

CHUNK = 64
H_A = 8
DQK = 32
DV = 2 * DQK
D_ATT = H_A * DV
SSM_HEADDIM = 64
H_S = 8
D_SSM = H_S * SSM_HEADDIM
SSM_GROUPS = 2
HEADS_PER_GROUP = H_S // SSM_GROUPS
D_STATE = 128
CONV_W = 4
CONV_DIM = D_SSM + 2 * SSM_GROUPS * D_STATE
D_QK_ALL = H_A * 2 * DQK
N_BUCKETS = 32
MAX_DISTANCE = 128
N_EXPERTS = 32
TOP_K = 4
SWIGLU_LIMIT = 7.0
SWIGLU_ALPHA = 1.702
MOE_BLOCK = 256
LN_EPS = 1e-5
RMS_EPS = 1e-5
NEG_INF = -1e30
DEPTH = 1
DEEPNORM_ALPHA = (2.0 * DEPTH) ** 0.25
LOG2E = math.log2(math.e)

LANES = 128
SUBLANES = 8
BF16_ROWS = 16
VMEM_LIMIT = 48 * 1024 * 1024

ROW_TILE = 512
ATT_TILE = 512
ATT_SUB = 256
SSD_TILE = 256
V_ROWS = DV + BF16_ROWS
FFN_BLOCKS_PER_STEP = 2
COMBINE_GROUPS = 4


def _params(semantics):
    return pltpu.CompilerParams(dimension_semantics=semantics, vmem_limit_bytes=VMEM_LIMIT)


def _dot(a, b):
    return jnp.dot(a, b, preferred_element_type=f32)


def _dot_nt(a, b):
    return lax.dot_general(a, b, (((1,), (1,)), ((), ())), preferred_element_type=f32)


def _dot_tn(a, b):
    return lax.dot_general(a, b, (((0,), (0,)), ((), ())), preferred_element_type=f32)


def _split3(a):
    hi = a.astype(bf16)
    r1 = a - hi.astype(f32)
    mid = r1.astype(bf16)
    lo = (r1 - mid.astype(f32)).astype(bf16)
    return hi, mid, lo


def _dot_f32_lhs(a, b_exact):
    hi, mid, lo = _split3(a)
    return _dot(hi, b_exact) + _dot(mid, b_exact) + _dot(lo, b_exact)


def _dot_f32_rhs(a_exact, b):
    hi, mid, lo = _split3(b)
    return _dot(a_exact, hi) + _dot(a_exact, mid) + _dot(a_exact, lo)


def _softplus(x):
    return jnp.maximum(x, 0.0) + jnp.log1p(jnp.exp(-jnp.abs(x)))


def _sigmoid(x):
    return 1.0 / (1.0 + jnp.exp(-x))


def _layer_norm(y, g, b):
    mu = jnp.mean(y, axis=-1, keepdims=True)
    yc = y - mu
    var = jnp.mean(yc * yc, axis=-1, keepdims=True)
    return yc * lax.rsqrt(var + LN_EPS) * g + b


def _lambda(lq1_ref, lk1_ref, lq2_ref, lk2_ref, lam_init):
    s1 = jnp.sum(lq1_ref[...] * lk1_ref[...], axis=-1, keepdims=True)
    s2 = jnp.sum(lq2_ref[...] * lk2_ref[...], axis=-1, keepdims=True)
    return jnp.exp(s1) - jnp.exp(s2) + lam_init


def _in_proj_kernel(x_ref, wk_ref, wv_ref, wz_ref, wx_ref, wdt_ref, wdtT_ref, wq_ref, wkT_ref, wvT_ref, *outs,
                    prompt):
    xb = x_ref[...].astype(bf16)
    z_ref, xbc_ref, dt_ref, dtT_ref = outs[:4]
    k = _dot(xb, wk_ref[...])
    z_ref[...] = _dot(xb, wz_ref[...])
    xbc_ref[...] = _dot(xb, wx_ref[...])
    dt_ref[...] = _dot(xb, wdt_ref[...])
    dtT_ref[...] = _dot_nt(wdtT_ref[...], xb)
    scale = DQK ** -0.5
    if prompt:
        kT_ref, vT_ref, kb_ref, qT_ref, v1T_ref = outs[4:]
        kb_ref[...] = k.astype(bf16)
        kT_ref[0] = _dot_nt(wkT_ref[...], xb)
        qT_ref[...] = (_dot_nt(wq_ref[...], xb) * (scale * LOG2E)).astype(bf16)
        vT32 = _dot_nt(wvT_ref[...], xb)
        vT_ref[0] = vT32
        vT = vT32.astype(bf16)
        ones = jnp.ones((BF16_ROWS, vT.shape[1]), bf16)
        for h in range(H_A):
            v1T_ref[h * V_ROWS:h * V_ROWS + DV, :] = vT[h * DV:(h + 1) * DV, :]
            v1T_ref[h * V_ROWS + DV:(h + 1) * V_ROWS, :] = ones
    else:
        k_ref, v_ref, q_ref = outs[4:]
        k_ref[...] = k
        v_ref[...] = _dot(xb, wv_ref[...])
        q_ref[...] = (_dot(xb, wq_ref[...]) * scale).astype(bf16)


def _in_proj(x2d, w, prompt, B):
    T, D = x2d.shape
    tl = min(ROW_TILE, T)
    L = T // B
    n_l = L // tl
    grid = (T // tl,)
    row = lambda n: pl.BlockSpec((tl, n), lambda i: (i, 0))
    col = lambda n: pl.BlockSpec((n, tl), lambda i: (0, i))
    per_stream = lambda n: pl.BlockSpec((1, n, tl), lambda i: (i // n_l, 0, i % n_l))
    full = lambda a: pl.BlockSpec(a.shape, lambda i: (0,) * a.ndim)
    wq = w['wqT'] if prompt else w['wq']
    ins = [x2d, w['wk'], w['wv'], w['wz'], w['wx'], w['wdt'], w['wdtT'], wq, w['wkT'], w['wvT']]
    in_specs = [row(D)] + [full(a) for a in ins[1:]]
    out_shape = [jax.ShapeDtypeStruct((T, D_SSM), f32), jax.ShapeDtypeStruct((T, CONV_DIM), f32),
                 jax.ShapeDtypeStruct((T, LANES), f32), jax.ShapeDtypeStruct((BF16_ROWS, T), f32)]
    out_specs = [row(D_SSM), row(CONV_DIM), row(LANES), col(BF16_ROWS)]
    if prompt:
        assert L % tl == 0
        out_shape += [jax.ShapeDtypeStruct((B, D_QK_ALL, L), f32), jax.ShapeDtypeStruct((B, D_ATT, L), f32),
                      jax.ShapeDtypeStruct((T, D_QK_ALL), bf16), jax.ShapeDtypeStruct((D_QK_ALL, T), bf16),
                      jax.ShapeDtypeStruct((H_A * V_ROWS, T), bf16)]
        out_specs += [per_stream(D_QK_ALL), per_stream(D_ATT), row(D_QK_ALL), col(D_QK_ALL), col(H_A * V_ROWS)]
    else:
        out_shape += [jax.ShapeDtypeStruct((T, D_QK_ALL), f32), jax.ShapeDtypeStruct((T, D_ATT), f32),
                      jax.ShapeDtypeStruct((T, D_QK_ALL), bf16)]
        out_specs += [row(D_QK_ALL), row(D_ATT), row(D_QK_ALL)]
    return pl.pallas_call(
        functools.partial(_in_proj_kernel, prompt=prompt),
        grid=grid, in_specs=in_specs, out_specs=out_specs, out_shape=out_shape,
        compiler_params=_params(("parallel",)),
    )(*ins)


def _t5_bucket(rel):
    half = N_BUCKETS // 2
    max_exact = half // 2
    n = jnp.abs(rel)
    large = max_exact + (jnp.log(jnp.maximum(n, 1).astype(f32) / max_exact)
                         / math.log(MAX_DISTANCE / max_exact) * (half - max_exact)).astype(i32)
    large = jnp.minimum(large, half - 1)
    return jnp.where(rel > 0, half, 0) + jnp.where(n < max_exact, n, large)


def _far_bucket(min_dist):
    half = N_BUCKETS // 2
    max_exact = half // 2
    v = max_exact + int(math.log(min_dist / max_exact) / math.log(MAX_DISTANCE / max_exact) * (half - max_exact))
    return half - 1 if (min_dist >= max_exact and v - 1 >= half - 1) else None


def _bias_lookup(rel_bias, bucket):
    onehot = (bucket[..., None] == jnp.arange(N_BUCKETS, dtype=i32)).astype(f32)
    return jnp.dot(onehot, rel_bias.astype(f32), precision=lax.Precision.HIGHEST)


def _prompt_bias_tiles(rel_bias, ta):
    j = jnp.arange(ta, dtype=i32)[:, None]
    i = jnp.arange(ta, dtype=i32)[None, :]
    tiles = []
    for d in range(2):
        rel = j - i - d * ta
        b = _bias_lookup(rel_bias, _t5_bucket(rel))
        allowed = ((j // CHUNK) <= (i // CHUNK)) if d == 0 else jnp.ones((ta, ta), bool)
        tiles.append(jnp.where(allowed[..., None], b, NEG_INF))
    return jnp.transpose(jnp.stack(tiles), (3, 0, 1, 2))


def _sample_bias(rel_bias, past, s):
    q_pos = past + jnp.arange(s, dtype=i32)
    k_pos = jnp.arange(past + s, dtype=i32)
    rel = k_pos[None, :] - q_pos[:, None]
    b = _bias_lookup(rel_bias, _t5_bucket(rel))
    allowed = (k_pos[None, :] // CHUNK) <= (q_pos[:, None] // CHUNK)
    return jnp.transpose(jnp.where(allowed[..., None], b, NEG_INF), (2, 0, 1))


ATT_DIAG, ATT_SUBDIAG, ATT_FAR1, ATT_FAR2 = 0, 1, 2, 3


def _attn_prompt_kernel(qi_tab, ka_tab, kb_tab, kind_tab, qT_ref, k_ref, v1T_ref, k2_ref, v2T_ref, bias_ref,
                        cfar_ref, lq1_ref, lk1_ref, lq2_ref, lk2_ref, g_ref, o_ref,
                        qm_sc, m_sc, acc_sc, oT_sc, *, lam_init):
    del qi_tab, kb_tab
    p = pl.program_id(1)
    kind = kind_tab[p]
    ta = k_ref.shape[0]
    n_hp = H_A // 2

    @pl.when(ka_tab[p] == 0)
    def _():
        m_sc[...] = jnp.full(m_sc.shape, NEG_INF, f32)
        acc_sc[...] = jnp.zeros(acc_sc.shape, f32)
        row = lax.broadcasted_iota(i32, (LANES, ta), 0)
        for hp in range(n_hp):
            qt = qT_ref[hp * LANES:(hp + 1) * LANES, :]
            for gi in range(4):
                keep = (row >= gi * DQK) & (row < (gi + 1) * DQK)
                qm_sc[hp, :, gi * ta:(gi + 1) * ta] = jnp.where(keep, qt, jnp.zeros_like(qt))

    def scores(keys_ref, hp):
        return _dot(keys_ref[:, hp * LANES:(hp + 1) * LANES], qm_sc[hp])

    nb = ta // ATT_SUB

    def near_bias(hp, step_d):
        rows = []
        for jb in range(nb):
            cols = []
            for g in range(4):
                for ib in range(nb):
                    dd = step_d * nb + ib - jb
                    if dd < 0:
                        blk = jnp.full((ATT_SUB, ATT_SUB), NEG_INF * LOG2E, f32)
                    elif dd < 2:
                        blk = bias_ref[2 * hp + g // 2, dd]
                    else:
                        lo = g * ta + ib * ATT_SUB
                        blk = jnp.broadcast_to(cfar_ref[hp][:, lo:lo + ATT_SUB], (ATT_SUB, ATT_SUB))
                    cols.append(blk)
            rows.append(jnp.concatenate(cols, axis=1))
        return jnp.concatenate(rows, axis=0)

    def update(step_d, tiles):
        chain = [(kr, vr, hp) for kr, vr in tiles for hp in range(n_hp)]
        s_next = scores(chain[0][0], chain[0][2])
        for link, (_, values_ref, hp) in enumerate(chain):
            s = s_next
            if link + 1 < len(chain):
                s_next = scores(chain[link + 1][0], chain[link + 1][2])
            m_old = m_sc[hp]
            if step_d is not None:
                s = s + near_bias(hp, step_d)
                m_new = jnp.maximum(m_old, jnp.max(s, axis=0, keepdims=True))
                shift = m_new
            else:
                c = cfar_ref[hp]
                m_new = jnp.maximum(m_old, jnp.max(s, axis=0, keepdims=True) + c)
                shift = m_new - c
            alpha = jnp.exp2(m_old - m_new)
            pT = jnp.exp2(s - shift).astype(bf16)
            for hh in range(2):
                h = 2 * hp + hh
                cols = slice(hh * 2 * ta, (hh + 1) * 2 * ta)
                pv = _dot(values_ref[h * V_ROWS:(h + 1) * V_ROWS, :], pT[:, cols])
                acc_sc[h] = alpha[:, cols] * acc_sc[h] + pv
            m_sc[hp] = m_new

    @pl.when(kind == ATT_DIAG)
    def _():
        update(0, [(k_ref, v1T_ref)])

    @pl.when(kind == ATT_SUBDIAG)
    def _():
        update(1, [(k_ref, v1T_ref)])

    @pl.when(kind == ATT_FAR1)
    def _():
        update(None, [(k_ref, v1T_ref)])

    @pl.when(kind == ATT_FAR2)
    def _():
        update(None, [(k_ref, v1T_ref), (k2_ref, v2T_ref)])

    @pl.when(kind == ATT_DIAG)
    def _():
        lam = _lambda(lq1_ref, lk1_ref, lq2_ref, lk2_ref, lam_init)
        for h in range(H_A):
            a = acc_sc[h]
            a0 = a[:, :ta]
            a1 = a[:, ta:]
            o = a0[:DV] / a0[DV:DV + 1] - lam * (a1[:DV] / a1[DV:DV + 1])
            o = o * lax.rsqrt(jnp.mean(o * o, axis=0, keepdims=True) + RMS_EPS) * g_ref[...] * (1.0 - lam_init)
            oT_sc[h * DV:(h + 1) * DV, :] = o
        o_ref[...] = oT_sc[...].T.astype(bf16)


def _attn_prompt(qT, kb, v1T, rel_bias, lam_args, attn_g, B, L, lam_init):
    ta = min(ATT_TILE, L)
    assert L % ta == 0 and ta % ATT_SUB == 0 and ATT_SUB % CHUNK == 0
    nt = L // ta
    far = _far_bucket(ATT_SUB + 1)
    assert L <= 2 * ATT_SUB or far is not None, "key blocks two or more behind must share one bias bucket"
    far = far if far is not None else N_BUCKETS // 2 - 1
    steps = []
    idle_b = 0
    for q in range(nt):
        far_tiles = list(range(max(q - 1, 0)))
        for a in range(0, len(far_tiles) - 1, 2):
            steps.append((q, a, a + 1, ATT_FAR2))
            idle_b = a + 1
        if len(far_tiles) % 2:
            steps.append((q, far_tiles[-1], idle_b, ATT_FAR1))
        if q >= 1:
            steps.append((q, q - 1, idle_b, ATT_SUBDIAG))
        steps.append((q, q, idle_b, ATT_DIAG))
    qi_tab, ka_tab, kb_tab, kind_tab = (jnp.asarray([s[j] for s in steps], i32) for j in range(4))
    bias =_prompt_bias_tiles(rel_bias, ATT_SUB) * LOG2E
    cfar = jnp.repeat((rel_bias.astype(f32)[far] * LOG2E).reshape(H_A // 2, 2), 2 * ta, axis=1)[:, None, :]
    g_col = attn_g.astype(f32).reshape(DV, 1)
    T = B * L
    n_hp = H_A // 2
    full = lambda a: pl.BlockSpec(a.shape, lambda b, p, qt, ka, kb, kd: (0,) * a.ndim)
    grid_spec = pltpu.PrefetchScalarGridSpec(
        num_scalar_prefetch=4, grid=(B, len(steps)),
        in_specs=[
            pl.BlockSpec((D_QK_ALL, ta), lambda b, p, qt, ka, kb, kd: (0, b * nt + qt[p])),
            pl.BlockSpec((ta, D_QK_ALL), lambda b, p, qt, ka, kb, kd: (b * nt + ka[p], 0)),
            pl.BlockSpec((H_A * V_ROWS, ta), lambda b, p, qt, ka, kb, kd: (0, b * nt + ka[p])),
            pl.BlockSpec((ta, D_QK_ALL), lambda b, p, qt, ka, kb, kd: (b * nt + kb[p], 0)),
            pl.BlockSpec((H_A * V_ROWS, ta), lambda b, p, qt, ka, kb, kd: (0, b * nt + kb[p])),
            full(bias), full(cfar)] + [full(a) for a in lam_args] + [full(g_col)],
        out_specs=pl.BlockSpec((ta, D_ATT), lambda b, p, qt, ka, kb, kd: (b * nt + qt[p], 0)),
        scratch_shapes=[pltpu.VMEM((n_hp, LANES, 4 * ta), bf16), pltpu.VMEM((n_hp, 1, 4 * ta), f32),
                        pltpu.VMEM((H_A, V_ROWS, 2 * ta), f32), pltpu.VMEM((D_ATT, ta), f32)])
    return pl.pallas_call(
        functools.partial(_attn_prompt_kernel, lam_init=lam_init),
        grid_spec=grid_spec, out_shape=jax.ShapeDtypeStruct((T, D_ATT), bf16),
        compiler_params=_params(("parallel", "arbitrary")),
    )(qi_tab, ka_tab, kb_tab, kind_tab, qT, kb, v1T, kb, v1T, bias, cfar, *lam_args, g_col)


def _attn_sample_kernel(q_ref, kn_ref, vn_ref, kc_ref, vc_ref, bc_ref, bn_ref,
                        lq1_ref, lk1_ref, lq2_ref, lk2_ref, g2_ref, o_ref, *, lam_init):
    s_len = q_ref.shape[0]
    lane = lax.broadcasted_iota(i32, (s_len, LANES), 1)
    lam = _lambda(lq1_ref, lk1_ref, lq2_ref, lk2_ref, lam_init)
    for hp in range(H_A // 2):
        sl = slice(hp * LANES, (hp + 1) * LANES)
        qt = q_ref[:, sl]
        kct = kc_ref[0, sl, :].astype(bf16)
        vct = vc_ref[0, sl, :].astype(bf16)
        knt = kn_ref[:, sl].astype(bf16)
        vnt = vn_ref[:, sl].astype(bf16)
        outs = []
        for hh in range(2):
            h = 2 * hp + hh
            oc = []
            for c in range(2):
                gi = 2 * hh + c
                qm = jnp.where((lane >= gi * DQK) & (lane < (gi + 1) * DQK), qt, jnp.zeros_like(qt))
                s1 = _dot(qm, kct) + bc_ref[h]
                s2 = _dot_nt(qm, knt) + bn_ref[h]
                m = jnp.maximum(jnp.max(s1, axis=-1, keepdims=True), jnp.max(s2, axis=-1, keepdims=True))
                p1 = jnp.exp(s1 - m)
                p2 = jnp.exp(s2 - m)
                l = jnp.sum(p1, axis=-1, keepdims=True) + jnp.sum(p2, axis=-1, keepdims=True)
                pv = _dot_nt(p1.astype(bf16), vct) + _dot(p2.astype(bf16), vnt)
                oc.append(pv / l)
            o = oc[0] - lam * oc[1]
            in_head = (lane >= hh * DV) & (lane < (hh + 1) * DV)
            ms = jnp.sum(jnp.where(in_head, o * o, 0.0), axis=-1, keepdims=True) * (1.0 / DV)
            outs.append(o * lax.rsqrt(ms + RMS_EPS))
        ot = jnp.where(lane < DV, outs[0], outs[1]) * g2_ref[...] * (1.0 - lam_init)
        o_ref[:, sl] = ot.astype(bf16)


def _attn_sample(q, k_new, v_new, cache_k, cache_v, rel_bias, lam_args, attn_g, lam_init):
    nb, past = cache_k.shape[0], cache_k.shape[1]
    T = q.shape[0]
    s = T // nb
    kc = cache_k.transpose(0, 2, 3, 1).reshape(nb, D_QK_ALL, past)
    vc = cache_v.transpose(0, 2, 3, 1).reshape(nb, D_ATT, past)
    bias = _sample_bias(rel_bias, past, s)
    bc, bn = bias[:, :, :past], bias[:, :, past:]
    g2 = jnp.tile(attn_g.astype(f32), 2).reshape(1, LANES)
    full = lambda a: pl.BlockSpec(a.shape, lambda b: (0,) * a.ndim)
    row = lambda n: pl.BlockSpec((s, n), lambda b: (b, 0))
    return pl.pallas_call(
        functools.partial(_attn_sample_kernel, lam_init=lam_init),
        grid=(nb,),
        in_specs=[row(D_QK_ALL), row(D_QK_ALL), row(D_ATT),
                  pl.BlockSpec((1, D_QK_ALL, past), lambda b: (b, 0, 0)),
                  pl.BlockSpec((1, D_ATT, past), lambda b: (b, 0, 0)),
                  full(bc), full(bn)] + [full(a) for a in lam_args] + [full(g2)],
        out_specs=row(D_ATT), out_shape=jax.ShapeDtypeStruct((T, D_ATT), bf16),
        compiler_params=_params(("parallel",)),
    )(q, k_new, v_new, kc, vc, bc, bn, *lam_args, g2)


def _ssd_kernel(z_ref, xbc_ref, dt_ref, dtT_ref, h0_ref, c0_ref, cw_ref, cb_ref, dtb_ref, dtbT_ref,
                alog_ref, alogT_ref, dsk_ref, g_ref, y_ref, hout_ref, cout_ref, xpad_sc, h_sc):
    q = xbc_ref.shape[0]
    c = pl.program_id(1)
    gw = HEADS_PER_GROUP * SSM_HEADDIM

    @pl.when(c == 0)
    def _():
        xpad_sc[0:SUBLANES] = c0_ref[0]
        h_sc[...] = h0_ref[0]

    xpad_sc[SUBLANES:SUBLANES + q] = xbc_ref[...]
    first = SUBLANES - (CONV_W - 1)
    conv = cb_ref[...]
    for i in range(CONV_W):
        conv = conv + xpad_sc[first + i:first + i + q] * cw_ref[i:i + 1, :]
    tail = xpad_sc[q:q + SUBLANES]
    xpad_sc[0:SUBLANES] = tail
    cout_ref[0] = tail
    act = conv * _sigmoid(conv)
    xs = act[:, :D_SSM]

    lane = lax.broadcasted_iota(i32, (1, LANES), 1)
    a_row = jnp.where(lane < H_S, -jnp.exp(alog_ref[...]), 0.0)
    dt = _softplus(dt_ref[...] + dtb_ref[...])
    ii = lax.broadcasted_iota(i32, (q, q), 0)
    jj = lax.broadcasted_iota(i32, (q, q), 1)
    causal = jj <= ii
    acs = _dot_f32_rhs(causal.astype(bf16), dt * a_row)
    sub = lax.broadcasted_iota(i32, (BF16_ROWS, 1), 0)
    a_col = jnp.where(sub < H_S, -jnp.exp(alogT_ref[...]), 0.0)
    dtT = _softplus(dtT_ref[0] + dtbT_ref[...])
    acsT = _dot_f32_lhs(dtT * a_col, (ii <= jj).astype(bf16))

    er = lax.broadcasted_iota(i32, (LANES, D_SSM), 0)
    ec = lax.broadcasted_iota(i32, (LANES, D_SSM), 1)
    expand = (ec // SSM_HEADDIM == er).astype(bf16)
    dt_x = _dot_f32_lhs(dt, expand)
    acs_x = _dot_f32_lhs(acs, expand)
    e_acs = jnp.exp(acs_x)
    acs_last = acs_x[q - 1:q, :]
    decay = jnp.exp(acs_last - acs_x) * dt_x
    dtx = xs * dt_x
    xd = xs * decay
    glane = lax.broadcasted_iota(i32, (1, gw), 1)

    ys = []
    for g in range(SSM_GROUPS):
        gs = slice(g * gw, (g + 1) * gw)
        bg = act[:, D_SSM + g * D_STATE:D_SSM + (g + 1) * D_STATE].astype(bf16)
        cg = act[:, D_SSM + (SSM_GROUPS + g) * D_STATE:D_SSM + (SSM_GROUPS + g + 1) * D_STATE].astype(bf16)
        cb = _dot_nt(cg, bg)
        h_old = h_sc[g]
        yg = _dot(cg, h_old.astype(bf16)) * e_acs[:, gs]
        dtx_g = dtx[:, gs]
        for e4 in range(HEADS_PER_GROUP):
            e = g * HEADS_PER_GROUP + e4
            seg = acs[:, e:e + 1] - acsT[e:e + 1, :]
            lmat = jnp.exp(jnp.where(causal, seg, -jnp.inf))
            rhs = jnp.where(glane // SSM_HEADDIM == e4, dtx_g, 0.0).astype(bf16)
            yg = yg + _dot((cb * lmat).astype(bf16), rhs)
        h_sc[g] = h_old * jnp.exp(acs_last[:, gs]) + _dot_tn(bg, xd[:, gs].astype(bf16))
        ys.append(yg)
    y = jnp.concatenate(ys, axis=1) + dsk_ref[...] * xs
    zz = z_ref[...]
    y = y * (zz * _sigmoid(zz))
    for g in range(SSM_GROUPS):
        gs = slice(g * gw, (g + 1) * gw)
        yg = y[:, gs]
        r = lax.rsqrt(jnp.mean(yg * yg, axis=-1, keepdims=True) + RMS_EPS)
        y_ref[:, gs] = (yg * r * g_ref[:, gs]).astype(bf16)
    hout_ref[0] = h_sc[...]


def _ssd(z, xbc, dt, dtT, h0, conv0, pw, B, L):
    q = min(SSD_TILE, L)
    nc = L // q
    gw = HEADS_PER_GROUP * SSM_HEADDIM
    dtT3 = dtT.reshape(BF16_ROWS, B, L).transpose(1, 0, 2)
    hT0 = h0.astype(f32).reshape(B, SSM_GROUPS, gw, D_STATE).transpose(0, 1, 3, 2)
    c0 = jnp.pad(conv0.astype(f32), ((0, 0), (SUBLANES - (CONV_W - 1), 0), (0, 0)))
    full = lambda a: pl.BlockSpec(a.shape, lambda b, c: (0,) * a.ndim)
    row = lambda n: pl.BlockSpec((q, n), lambda b, c: (b * nc + c, 0))
    params = [pw['conv_w'], pw['conv_b'], pw['dt_bias'], pw['dt_biasT'], pw['a_log'], pw['a_logT'],
              pw['d_skip'], pw['ssm_g']]
    y, hT, ctail = pl.pallas_call(
        _ssd_kernel, grid=(B, nc),
        in_specs=[row(D_SSM), row(CONV_DIM), row(LANES),
                  pl.BlockSpec((1, BF16_ROWS, q), lambda b, c: (b, 0, c)),
                  pl.BlockSpec((1, SSM_GROUPS, D_STATE, gw), lambda b, c: (b, 0, 0, 0)),
                  pl.BlockSpec((1, SUBLANES, CONV_DIM), lambda b, c: (b, 0, 0))] + [full(a) for a in params],
        out_specs=[row(D_SSM),
                   pl.BlockSpec((1, SSM_GROUPS, D_STATE, gw), lambda b, c: (b, 0, 0, 0)),
                   pl.BlockSpec((1, SUBLANES, CONV_DIM), lambda b, c: (b, 0, 0))],
        out_shape=[jax.ShapeDtypeStruct((B * L, D_SSM), bf16),
                   jax.ShapeDtypeStruct((B, SSM_GROUPS, D_STATE, gw), f32),
                   jax.ShapeDtypeStruct((B, SUBLANES, CONV_DIM), f32)],
        scratch_shapes=[pltpu.VMEM((q + SUBLANES, CONV_DIM), f32), pltpu.VMEM((SSM_GROUPS, D_STATE, gw), f32)],
        compiler_params=_params(("parallel", "arbitrary")),
    )(z, xbc, dt, dtT3, hT0, c0, *params)
    h_new = hT.transpose(0, 1, 3, 2).reshape(B, H_S, SSM_HEADDIM, D_STATE)
    return y, h_new, ctail[:, SUBLANES - (CONV_W - 1):, :]


def _pack_bf16_pairs(x):
    half = x.shape[1] // 2
    bits = lambda a: lax.bitcast_convert_type(a.astype(bf16).astype(f32), jnp.uint32)
    return (bits(x[:, half:]) & jnp.uint32(0xFFFF0000)) | (bits(x[:, :half]) >> 16)


def _unpack_bf16_pairs(word):
    lo = lax.bitcast_convert_type(word << 16, f32)
    hi = lax.bitcast_convert_type(word & jnp.uint32(0xFFFF0000), f32)
    return jnp.concatenate([lo, hi], axis=1)


def _mix_router_kernel(att_ref, ssm_ref, x_ref, woa_ref, wos_ref, g_ref, b_ref, wrh_ref, wrhl_ref, br_ref,
                       cnt0_ref, *rest):
    x1_ref, x1p_ref, te_ref, gate_ref, cnt_ref, run_sc, earlier_sc = rest[-7:]

    @pl.when(pl.program_id(0) == 0)
    def _():
        run_sc[...] = cnt0_ref[...]
        ii = lax.broadcasted_iota(i32, earlier_sc.shape, 0)
        jj = lax.broadcasted_iota(i32, earlier_sc.shape, 1)
        earlier_sc[...] = (jj < ii).astype(bf16)

    mix = _dot(att_ref[...], woa_ref[...]) + _dot(ssm_ref[...], wos_ref[...])
    x1 = _layer_norm(DEEPNORM_ALPHA * x_ref[...] + mix, g_ref[...], b_ref[...])
    x1_ref[...] = x1
    x1p_ref[...] = _pack_bf16_pairs(x1)
    hi = x1.astype(bf16)
    lo = (x1 - hi.astype(f32)).astype(bf16)
    both = _dot(hi, wrhl_ref[...])
    logits = both[:, :LANES] + both[:, LANES:] + _dot(lo, wrh_ref[...]) + br_ref[...]
    lane =lax.broadcasted_iota(i32, logits.shape, 1)
    vals, idxs = [], []
    for _ in range(TOP_K):
        m = jnp.max(logits, axis=-1, keepdims=True)
        idx = jnp.min(jnp.where(logits == m, lane, LANES), axis=-1, keepdims=True)
        vals.append(m)
        idxs.append(idx)
        logits = jnp.where(lane == idx, -jnp.inf, logits)
    es = [jnp.exp(v - vals[0]) for v in vals]
    tot = es[0]
    for e in es[1:]:
        tot = tot + e
    chosen = jnp.zeros(logits.shape, f32)
    for k in range(TOP_K):
        chosen = chosen + (lane == idxs[k]).astype(f32)
    before = _dot(earlier_sc[...], chosen.astype(bf16)) + run_sc[...]
    te = jnp.zeros(logits.shape, i32)
    gate = jnp.zeros(logits.shape, f32)
    for k in range(TOP_K):
        rank = jnp.sum(jnp.where(lane == idxs[k], before, 0.0), axis=-1, keepdims=True).astype(i32)
        te = jnp.where(lane == k, idxs[k], te)
        te = jnp.where(lane == TOP_K + k, rank, te)
        gate = jnp.where(lane == k, es[k] / tot, gate)
    te_ref[...] = te
    gate_ref[...] = gate
    run_sc[...] = run_sc[...] + jnp.sum(chosen, axis=0, keepdims=True)
    cnt_ref[...] = run_sc[...]


def _mix_router(att, ssm, x2d, w, counts0, packed, row0, rows_total):
    T, D = x2d.shape
    tl = min(ROW_TILE, T)
    assert row0 % tl == 0
    row = lambda n: pl.BlockSpec((tl, n), lambda i: (i, 0))
    full = lambda a: pl.BlockSpec(a.shape, lambda i: (0,) * a.ndim)
    ws = [w['wo_att'], w['wo_ssm'], w['ln1_g'], w['ln1_b'], w['wr_hi'], w['wr_hi_lo'], w['b_router'], counts0]
    ins = [att, ssm, x2d] + ws + [packed]
    in_specs = [row(D_ATT), row(D_SSM), row(D)] + [full(a) for a in ws] + [pl.BlockSpec(memory_space=pl.ANY)]
    aliases = {len(ins) - 1: 1}
    return pl.pallas_call(
        _mix_router_kernel, grid=(T // tl,), in_specs=in_specs,
        out_specs=[row(D), pl.BlockSpec((tl, D // 2), lambda i: (row0 // tl + i, 0)), row(LANES), row(LANES),
                   pl.BlockSpec((1, LANES), lambda i: (0, 0))],
        out_shape=[jax.ShapeDtypeStruct((T, D), f32), jax.ShapeDtypeStruct((rows_total, D // 2), jnp.uint32),
                   jax.ShapeDtypeStruct((T, LANES), i32),
                   jax.ShapeDtypeStruct((T, LANES), f32), jax.ShapeDtypeStruct((1, LANES), f32)],
        scratch_shapes=[pltpu.VMEM((1, LANES), f32), pltpu.VMEM((tl, tl), bf16)], input_output_aliases=aliases,
        compiler_params=_params(("arbitrary",)),
    )(*ins)


def _route(te, counts_f):
    T = te.shape[0]
    n_assign = T * TOP_K
    counts = counts_f[0, :N_EXPERTS].astype(i32)
    padded = (counts + MOE_BLOCK - 1) // MOE_BLOCK * MOE_BLOCK
    pad_end = jnp.cumsum(padded)
    pad_start = pad_end - padded
    top_e, rank = te[:, :TOP_K], te[:, TOP_K:2 * TOP_K]
    experts = jnp.arange(N_EXPERTS, dtype=i32)
    slot = rank + jnp.sum(jnp.where(top_e[..., None] == experts, pad_start, 0), axis=-1)
    n_blocks = -(-n_assign // MOE_BLOCK) + N_EXPERTS
    n_blocks += n_blocks % FFN_BLOCKS_PER_STEP
    block_start = jnp.arange(n_blocks, dtype=i32) * MOE_BLOCK
    block_e = jnp.minimum(jnp.sum(block_start[:, None] >= pad_end[None, :], axis=-1), N_EXPERTS - 1).astype(i32)
    n_used = (pad_end[-1] // MOE_BLOCK).astype(i32).reshape(1)
    row_end = jnp.sum(jnp.where(block_e[:, None] == experts, pad_start + counts, 0), axis=-1)
    rows_valid = jnp.clip(row_end - block_start, 0, MOE_BLOCK).astype(i32)
    return block_e, n_used, rows_valid, slot.astype(i32), n_blocks


SC_INDEX_WINDOW = 128
SC_ROWS = 64


def _sc_mesh():
    return plsc.VectorSubcoreMesh(core_axis_name="c", subcore_axis_name="s")


def _sc_move_rows(src, src_idx, dst_idx_list, n_out):
    M = src_idx.shape[0]
    D = src.shape[1]
    idx = [a.reshape(1, M) for a in [src_idx] + list(dst_idx_list)]

    @functools.partial(pl.kernel, out_type=jax.ShapeDtypeStruct((n_out, D), src.dtype), mesh=_sc_mesh(),
                       scratch_types=[pltpu.VMEM((2, SC_ROWS, D), src.dtype), pltpu.SemaphoreType.DMA((2,))])
    def move(s_hbm, *rest):
        i_hbm, o_hbm, buf, sem = rest[:-3], rest[-3], rest[-2], rest[-1]
        n_parts = SC_INDEX_WINDOW // SC_ROWS

        def body(si_vmem, *di_vmem):
            def fetch(j):
                part = pl.ds(j * SC_ROWS, SC_ROWS)
                return pltpu.async_copy(s_hbm.at[si_vmem.at[0, part]], buf.at[j % 2], sem.at[j % 2])

            pending = fetch(0)
            for j in range(n_parts):
                nxt = fetch(j + 1) if j + 1 < n_parts else None
                pending.wait()
                part = pl.ds(j * SC_ROWS, SC_ROWS)
                for dv in di_vmem:
                    pltpu.sync_copy(buf.at[j % 2], o_hbm.at[dv.at[0, part]])
                pending = nxt

        pltpu.emit_pipeline(
            body, grid=(M // SC_INDEX_WINDOW,),
            in_specs=[pl.BlockSpec((1, SC_INDEX_WINDOW), lambda i: (0, i))] * len(idx),
            out_specs=[], core_axis_name=("c", "s"), dimension_semantics=(pltpu.PARALLEL,),
        )(*i_hbm)

    return move(src, *idx)


def _sc_scatter_rows(x, idx_k, n_rows):
    return _sc_move_rows(x, jnp.arange(x.shape[0], dtype=i32), idx_k, n_rows)


def _sc_gather_rows(src, idx):
    M = idx.shape[0]
    return _sc_move_rows(src, idx, [jnp.arange(M, dtype=i32)], M)


def _deinterleave_kernel(w_ref, g_ref, u_ref):
    tn = 2 * LANES
    r = lax.broadcasted_iota(i32, (2 * tn, tn), 0)
    c = lax.broadcasted_iota(i32, (2 * tn, tn), 1)
    pick_g = (r == 2 * c).astype(bf16)
    pick_u = (r == 2 * c + 1).astype(bf16)
    for j in range(g_ref.shape[-1] // tn):
        wb = w_ref[0, :, j * 2 * tn:(j + 1) * 2 * tn].astype(bf16)
        g_ref[0, :, j * tn:(j + 1) * tn] = _dot(wb, pick_g).astype(bf16)
        u_ref[0, :, j * tn:(j + 1) * tn] = _dot(wb, pick_u).astype(bf16)


def _deinterleave(w_gu):
    E, D, F2 = w_gu.shape
    out = jax.ShapeDtypeStruct((E, D, F2 // 2), bf16)
    return pl.pallas_call(
        _deinterleave_kernel, grid=(E,),
        in_specs=[pl.BlockSpec((1, D, F2), lambda e: (e, 0, 0))],
        out_specs=[pl.BlockSpec((1, D, F2 // 2), lambda e: (e, 0, 0))] * 2,
        out_shape=[out, out],
        compiler_params=_params(("parallel",)),
    )(w_gu)


def _ffn_kernel(be_ref, nu_ref, rv_ref, xs_ref, *refs):
    del be_ref
    y_ref = refs[-1]
    wsets = [refs[6 * j:6 * j + 6] for j in range(FFN_BLOCKS_PER_STEP)]
    first = pl.program_id(0) * FFN_BLOCKS_PER_STEP
    n_live = jnp.clip(nu_ref[0] - first, 0, FFN_BLOCKS_PER_STEP)
    row_id = lax.broadcasted_iota(i32, (MOE_BLOCK, 1), 0)

    def block(j):
        wg_ref, wu_ref, bg_ref, bu_ref, wd_ref, bd_ref = wsets[j]
        rows = pl.ds(j * MOE_BLOCK, MOE_BLOCK)
        words = jnp.where(row_id < rv_ref[first + j], xs_ref[rows, :], jnp.uint32(0))
        xb = _unpack_bf16_pairs(words).astype(bf16)
        g = _dot(xb, wg_ref[0]) + bg_ref[0]
        u = _dot(xb, wu_ref[0]) + bu_ref[0]
        g = jnp.minimum(g, SWIGLU_LIMIT)
        u = jnp.clip(u, -SWIGLU_LIMIT, SWIGLU_LIMIT)
        act = (u + 1.0) * g * _sigmoid(SWIGLU_ALPHA * g)
        y_ref[rows, :] = _pack_bf16_pairs(_dot(act.astype(bf16), wd_ref[0].astype(bf16)) + bd_ref[0])

    for live in range(FFN_BLOCKS_PER_STEP + 1):
        @pl.when(n_live == live)
        def _(live=live):
            for j in range(live):
                block(j)
            for j in range(live, FFN_BLOCKS_PER_STEP):
                y_ref[pl.ds(j * MOE_BLOCK, MOE_BLOCK), :] = jnp.zeros((MOE_BLOCK, y_ref.shape[1]), jnp.uint32)


def _expert_ffn(xs, block_e, n_used, rows_valid, w):
    n_rows, half = xs.shape
    D = 2 * half
    nps = FFN_BLOCKS_PER_STEP
    n_steps = n_rows // (MOE_BLOCK * nps)
    F = w['w_g'].shape[2]
    wspecs, wargs = [], []
    for j in range(nps):
        pick = lambda i, be, nu, rv, j=j: (be[i * nps + j], 0, 0)
        wspecs += [pl.BlockSpec((1, D, F), pick), pl.BlockSpec((1, D, F), pick), pl.BlockSpec((1, 1, F), pick),
                   pl.BlockSpec((1, 1, F), pick), pl.BlockSpec((1, F, D), pick), pl.BlockSpec((1, 1, D), pick)]
        wargs += [w['w_g'], w['w_u'], w['b_g'], w['b_u'], w['w_d'], w['b_d']]
    grid_spec = pltpu.PrefetchScalarGridSpec(
        num_scalar_prefetch=3, grid=(n_steps,),
        in_specs=[pl.BlockSpec((nps * MOE_BLOCK, half),
                               lambda i, be, nu, rv: (jnp.minimum(i, (nu[0] - 1) // nps), 0))] + wspecs,
        out_specs=pl.BlockSpec((nps * MOE_BLOCK, half), lambda i, be, nu, rv: (i, 0)))
    return pl.pallas_call(
        _ffn_kernel, grid_spec=grid_spec,
        out_shape=jax.ShapeDtypeStruct((n_rows, half), jnp.uint32),
        compiler_params=_params(("arbitrary",)),
    )(block_e, n_used, rows_valid, xs, *wargs)


def _combine_dense_kernel(rows_ref, x1_ref, gate_ref, g_ref, b_ref, *rest):
    y_ref = rest[-1]
    gate = gate_ref[...]
    ff = gate[:, 0:1] * _unpack_bf16_pairs(rows_ref[0])
    for k in range(1, TOP_K):
        ff = ff + gate[:, k:k + 1] * _unpack_bf16_pairs(rows_ref[k])
    y_ref[...] = _layer_norm(DEEPNORM_ALPHA * x1_ref[...] + ff, g_ref[...], b_ref[...])


def _combine_dense(rows, x1, gate, ln_g, ln_b, chunk, y_partial):
    T, D = x1.shape
    tc = rows.shape[1]
    tl = min(ROW_TILE, tc)
    first = chunk * (tc // tl)
    row = lambda n: pl.BlockSpec((tl, n), lambda i: (first + i, 0))
    full = lambda a: pl.BlockSpec(a.shape, lambda i: (0,) * a.ndim)
    ins = [rows, x1, gate, ln_g, ln_b]
    in_specs = [pl.BlockSpec((TOP_K, tl, D // 2), lambda i: (0, i, 0)), row(D), row(LANES), full(ln_g), full(ln_b)]
    aliases = {}
    if y_partial is not None:
        ins.append(y_partial)
        in_specs.append(pl.BlockSpec(memory_space=pl.ANY))
        aliases = {len(ins) - 1: 0}
    return pl.pallas_call(
        _combine_dense_kernel, grid=(tc // tl,), in_specs=in_specs,
        out_specs=row(D), out_shape=jax.ShapeDtypeStruct((T, D), f32), input_output_aliases=aliases,
        compiler_params=_params(("parallel",)),
    )(*ins)


def _prep_weights(l, w_in, conv_w, conv_b, dt_bias, a_log, d_skip, ssm_norm_g, w_o, ln1_g, ln1_b,
                  w_router, b_router, w_gate_up, b_gate_up, w_down, b_down, ln2_g, ln2_b):
    wi = w_in[l]
    c0, c1, c2, c3 = D_QK_ALL, 2 * D_QK_ALL, 2 * D_QK_ALL + D_ATT, 2 * D_QK_ALL + D_ATT + D_SSM
    c4 = c3 + CONV_DIM
    wdt = wi[:, c4:c4 + H_S]
    pad_lane = lambda v, fill=0.0: jnp.pad(v.astype(f32).reshape(1, -1), ((0, 0), (0, LANES - v.shape[-1])),
                                            constant_values=fill)
    pad_col = lambda v: jnp.pad(v.astype(f32).reshape(-1, 1), ((0, BF16_ROWS - v.shape[-1]), (0, 0)))
    wr = jnp.pad(w_router[l].astype(f32), ((0, 0), (0, LANES - N_EXPERTS)))
    wr_hi = wr.astype(bf16)
    wgu = w_gate_up[l]
    return {
        'wq': wi[:, :c0].astype(bf16), 'wqT': wi[:, :c0].T.astype(bf16),
        'wk': wi[:, c0:c1].astype(bf16), 'wkT': wi[:, c0:c1].T.astype(bf16), 'wv': wi[:, c1:c2].astype(bf16), 'wvT': wi[:, c1:c2].T.astype(bf16),
        'wz': wi[:, c2:c3].astype(bf16), 'wx': wi[:, c3:c4].astype(bf16),
        'wdt': jnp.pad(wdt, ((0, 0), (0, LANES - H_S))).astype(bf16),
        'wdtT': jnp.pad(wdt.T, ((0, BF16_ROWS - H_S), (0, 0))).astype(bf16),
        'conv_w': conv_w[l].astype(f32), 'conv_b': conv_b[l].astype(f32).reshape(1, -1),
        'dt_bias': pad_lane(dt_bias[l]), 'dt_biasT': pad_col(dt_bias[l]),
        'a_log': pad_lane(a_log[l]), 'a_logT': pad_col(a_log[l]),
        'd_skip': jnp.repeat(d_skip[l].astype(f32), SSM_HEADDIM).reshape(1, -1),
        'ssm_g': ssm_norm_g[l].astype(f32).reshape(1, -1),
        'wo_att': w_o[l][:D_ATT].astype(bf16), 'wo_ssm': w_o[l][D_ATT:].astype(bf16),
        'ln1_g': ln1_g[l].astype(f32).reshape(1, -1), 'ln1_b': ln1_b[l].astype(f32).reshape(1, -1),
        'wr_hi': wr_hi, 'wr_hi_lo': jnp.concatenate([wr_hi, (wr - wr_hi.astype(f32)).astype(bf16)], axis=1),
        'b_router': pad_lane(b_router[l], NEG_INF),
        'w_gu': wgu,
        'b_g': b_gate_up[l][:, None, 0::2].astype(f32), 'b_u': b_gate_up[l][:, None, 1::2].astype(f32),
        'w_d': w_down[l], 'b_d': b_down[l][:, None, :].astype(f32),
        'ln2_g': ln2_g[l].astype(f32).reshape(1, -1), 'ln2_b': ln2_b[l].astype(f32).reshape(1, -1),
    }


def _mixers(x, l, w, rel_bias, lam_args, attn_g, k_past, v_past, h0, conv0):
    B, L, D = x.shape
    x2d = x.reshape(B * L, D)
    lam_init = 0.8 - 0.6 * math.exp(-0.3 * l)
    prompt = k_past is None
    proj = _in_proj(x2d, w, prompt, B)
    z, xbc, dt, dtT = proj[:4]
    if prompt:
        kT, vT, kb, qT, v1T = proj[4:]
        att = _attn_prompt(qT, kb, v1T, rel_bias, lam_args, attn_g, B, L, lam_init)
        k_rows = kT.reshape(B, H_A, 2 * DQK, L).transpose(0, 3, 1, 2)
        v_rows = vT.reshape(B, H_A, DV, L).transpose(0, 3, 1, 2)
    else:
        k, v, q = proj[4:]
        att = _attn_sample(q, k, v, k_past, v_past, rel_bias, lam_args, attn_g, lam_init)
        k_rows = k.reshape(B, L, H_A, 2 * DQK)
        v_rows = v.reshape(B, L, H_A, DV)
    ssm, h_new, conv_new = _ssd(z, xbc, dt, dtT, h0, conv0, w, B, L)
    return (att, ssm, x2d), (k_rows, v_rows, h_new, conv_new)


def _moe_layer(streams, w):
    rows_total = sum(x2d.shape[0] for _, _, x2d in streams)
    counts = jnp.zeros((1, LANES), f32)
    packed = jnp.zeros((rows_total, streams[0][2].shape[1] // 2), jnp.uint32)
    row0, routed = 0, []
    for att, ssm, x2d in streams:
        x1, packed, te, gate, counts = _mix_router(att, ssm, x2d, w, counts, packed, row0, rows_total)
        routed.append((x1, gate, row0))
        row0 += x2d.shape[0]
        te_all = te[:, :2 * TOP_K] if len(routed) == 1 else jnp.concatenate([te_all, te[:, :2 * TOP_K]])
    block_e, n_used, rows_valid, slot, n_blocks = _route(te_all, counts)
    slot_k = slot.T
    xs = _sc_scatter_rows(packed, [slot_k[k] for k in range(TOP_K)], n_blocks * MOE_BLOCK)
    ys = _expert_ffn(xs, block_e, n_used, rows_valid, w)
    outs = []
    for x1, gate, r0 in routed:
        T, D = x1.shape
        n_groups = COMBINE_GROUPS if T % (COMBINE_GROUPS * ROW_TILE) == 0 else 1
        tc = T // n_groups
        y = None
        for c in range(n_groups):
            idx = slot_k[:, r0 + c * tc:r0 + (c + 1) * tc].reshape(-1)
            rows = _sc_gather_rows(ys, idx).reshape(TOP_K, tc, D // 2)
            y = _combine_dense(rows, x1, gate, w['ln2_g'], w['ln2_b'], c, y)
        outs.append(y)
    return outs


def kernel(x_prompt, x_sample, cache_k, cache_v, state_ssm, state_conv, rel_bias, w_in, lambda_q1, lambda_k1, lambda_q2, lambda_k2, attn_norm_g, conv_w, conv_b, dt_bias, a_log, d_skip, ssm_norm_g, w_o, ln1_g, ln1_b, w_router, b_router, w_gate_up, b_gate_up, w_down, b_down, ln2_g, ln2_b):
    yp, ys = x_prompt, x_sample
    bp = x_prompt.shape[0]
    depth = w_in.shape[0]
    outs = [[] for _ in range(8)]
    for l in range(depth):
        w = _prep_weights(l, w_in, conv_w, conv_b, dt_bias, a_log, d_skip, ssm_norm_g, w_o, ln1_g, ln1_b,
                          w_router, b_router, w_gate_up, b_gate_up, w_down, b_down, ln2_g, ln2_b)
        w['w_g'], w['w_u'] = _deinterleave(w.pop('w_gu'))
        lam_args = [a[l].astype(f32).reshape(1, -1) for a in (lambda_q1, lambda_k1, lambda_q2, lambda_k2)]
        h0 = jnp.zeros((bp, H_S, SSM_HEADDIM, D_STATE), f32)
        c0 = jnp.zeros((bp, CONV_W - 1, CONV_DIM), f32)
        mix_p, state_p = _mixers(yp, l, w, rel_bias, lam_args, attn_norm_g[l], None, None, h0, c0)
        mix_s, state_s = _mixers(ys, l, w, rel_bias, lam_args, attn_norm_g[l], cache_k[l], cache_v[l],
                                 state_ssm[l], state_conv[l])
        y2p, y2s = _moe_layer([mix_p, mix_s], w)
        yp, ys = y2p.reshape(yp.shape), y2s.reshape(ys.shape)
        for lst, a in zip(outs, state_p + state_s):
            lst.append(a)
    return (yp, ys) + tuple(jnp.stack(o) for o in outs)
```

```python
import functools
import math

import jax
import jax.numpy as jnp
from jax import lax
from jax.experimental import pallas as pl
from jax.experimental.pallas import tpu as pltpu
from jax.experimental.pallas import tpu_sc as plsc

f32 = jnp.float32
bf16 = jnp.bfloat16
i32 = jnp.int32

CHUNK = 64
H_A = 8
DQK = 32
DV = 2 * DQK
D_ATT = H_A * DV
SSM_HEADDIM = 64
H_S = 8
D_SSM = H_S * SSM_HEADDIM
SSM_GROUPS = 2
HEADS_PER_GROUP = H_S // SSM_GROUPS
D_STATE = 128
CONV_W = 4
CONV_DIM = D_SSM + 2 * SSM_GROUPS * D_STATE
D_QK_ALL = H_A * 2 * DQK
N_BUCKETS = 32
MAX_DISTANCE = 128
N_EXPERTS = 32
TOP_K = 4
SWIGLU_LIMIT = 7.0
SWIGLU_ALPHA = 1.702
MOE_BLOCK = 256
LN_EPS = 1e-5
RMS_EPS = 1e-5
NEG_INF = -1e30
DEPTH = 1
DEEPNORM_ALPHA = (2.0 * DEPTH) ** 0.25
LOG2E = math.log2(math.e)

LANES = 128
SUBLANES = 8
BF16_ROWS = 16
VMEM_LIMIT = 48 * 1024 * 1024

ROW_TILE = 512
ATT_TILE = 512
ATT_SUB = 256
SSD_TILE = 256
V_ROWS = DV + BF16_ROWS
FFN_BLOCKS_PER_STEP = 2
COMBINE_GROUPS = 4


def _params(semantics):
    return pltpu.CompilerParams(dimension_semantics=semantics, vmem_limit_bytes=VMEM_LIMIT)


def _dot(a, b):
    return jnp.dot(a, b, preferred_element_type=f32)


def _dot_nt(a, b):
    return lax.dot_general(a, b, (((1,), (1,)), ((), ())), preferred_element_type=f32)


def _dot_tn(a, b):
    return lax.dot_general(a, b, (((0,), (0,)), ((), ())), preferred_element_type=f32)


def _split3(a):
    hi = a.astype(bf16)
    r1 = a - hi.astype(f32)
    mid = r1.astype(bf16)
    lo = (r1 - mid.astype(f32)).astype(bf16)
    return hi, mid, lo


def _dot_f32_lhs(a, b_exact):
    hi, mid, lo = _split3(a)
    return _dot(hi, b_exact) + _dot(mid, b_exact) + _dot(lo, b_exact)


def _dot_f32_rhs(a_exact, b):
    hi, mid, lo = _split3(b)
    return _dot(a_exact, hi) + _dot(a_exact, mid) + _dot(a_exact, lo)


def _softplus(x):
    return jnp.maximum(x, 0.0) + jnp.log1p(jnp.exp(-jnp.abs(x)))


def _sigmoid(x):
    return 1.0 / (1.0 + jnp.exp(-x))


def _layer_norm(y, g, b):
    mu = jnp.mean(y, axis=-1, keepdims=True)
    yc = y - mu
    var = jnp.mean(yc * yc, axis=-1, keepdims=True)
    return yc * lax.rsqrt(var + LN_EPS) * g + b


def _lambda(lq1_ref, lk1_ref, lq2_ref, lk2_ref, lam_init):
    s1 = jnp.sum(lq1_ref[...] * lk1_ref[...], axis=-1, keepdims=True)
    s2 = jnp.sum(lq2_ref[...] * lk2_ref[...], axis=-1, keepdims=True)
    return jnp.exp(s1) - jnp.exp(s2) + lam_init


def _in_proj_kernel(x_ref, wk_ref, wv_ref, wz_ref, wx_ref, wdt_ref, wdtT_ref, wq_ref, wkT_ref, wvT_ref, *outs,
                    prompt):
    xb = x_ref[...].astype(bf16)
    z_ref, xbc_ref, dt_ref, dtT_ref = outs[:4]
    z_ref[...] = _dot(xb, wz_ref[...])
    xbc_ref[...] = _dot(xb, wx_ref[...])
    dt_ref[...] = _dot(xb, wdt_ref[...])
    dtT_ref[...] = _dot_nt(wdtT_ref[...], xb)
    scale = DQK ** -0.5
    if prompt:
        kT_ref, vT_ref, kb_ref, qT_ref, v1T_ref = outs[4:]
        kT = _dot_nt(wkT_ref[...], xb)
        kT_ref[0] = kT
        kb_ref[...] = kT.T.astype(bf16)
        qT_ref[...] = (_dot_nt(wq_ref[...], xb) * (scale * LOG2E)).astype(bf16)
        vT32 = _dot_nt(wvT_ref[...], xb)
        vT_ref[0] = vT32
        vT = vT32.astype(bf16)
        ones = jnp.ones((BF16_ROWS, vT.shape[1]), bf16)
        for h in range(H_A):
            v1T_ref[h * V_ROWS:h * V_ROWS + DV, :] = vT[h * DV:(h + 1) * DV, :]
            v1T_ref[h * V_ROWS + DV:(h + 1) * V_ROWS, :] = ones
    else:
        k_ref, v_ref, q_ref = outs[4:]
        k_ref[...] = _dot(xb, wk_ref[...])
        v_ref[...] = _dot(xb, wv_ref[...])
        q_ref[...] = (_dot(xb, wq_ref[...]) * scale).astype(bf16)


def _in_proj(x2d, w, prompt, B):
    T, D = x2d.shape
    tl = min(ROW_TILE, T)
    L = T // B
    n_l = L // tl
    grid = (T // tl,)
    row = lambda n: pl.BlockSpec((tl, n), lambda i: (i, 0))
    col = lambda n: pl.BlockSpec((n, tl), lambda i: (0, i))
    per_stream = lambda n: pl.BlockSpec((1, n, tl), lambda i: (i // n_l, 0, i % n_l))
    full = lambda a: pl.BlockSpec(a.shape, lambda i: (0,) * a.ndim)
    wq = w['wqT'] if prompt else w['wq']
    ins = [x2d, w['wk'], w['wv'], w['wz'], w['wx'], w['wdt'], w['wdtT'], wq, w['wkT'], w['wvT']]
    in_specs = [row(D)] + [full(a) for a in ins[1:]]
    out_shape = [jax.ShapeDtypeStruct((T, D_SSM), f32), jax.ShapeDtypeStruct((T, CONV_DIM), f32),
                 jax.ShapeDtypeStruct((T, LANES), f32), jax.ShapeDtypeStruct((BF16_ROWS, T), f32)]
    out_specs = [row(D_SSM), row(CONV_DIM), row(LANES), col(BF16_ROWS)]
    if prompt:
        assert L % tl == 0
        out_shape += [jax.ShapeDtypeStruct((B, D_QK_ALL, L), f32), jax.ShapeDtypeStruct((B, D_ATT, L), f32),
                      jax.ShapeDtypeStruct((T, D_QK_ALL), bf16), jax.ShapeDtypeStruct((D_QK_ALL, T), bf16),
                      jax.ShapeDtypeStruct((H_A * V_ROWS, T), bf16)]
        out_specs += [per_stream(D_QK_ALL), per_stream(D_ATT), row(D_QK_ALL), col(D_QK_ALL), col(H_A * V_ROWS)]
    else:
        out_shape += [jax.ShapeDtypeStruct((T, D_QK_ALL), f32), jax.ShapeDtypeStruct((T, D_ATT), f32),
                      jax.ShapeDtypeStruct((T, D_QK_ALL), bf16)]
        out_specs += [row(D_QK_ALL), row(D_ATT), row(D_QK_ALL)]
    return pl.pallas_call(
        functools.partial(_in_proj_kernel, prompt=prompt),
        grid=grid, in_specs=in_specs, out_specs=out_specs, out_shape=out_shape,
        compiler_params=_params(("parallel",)),
    )(*ins)


def _t5_bucket(rel):
    half = N_BUCKETS // 2
    max_exact = half // 2
    n = jnp.abs(rel)
    large = max_exact + (jnp.log(jnp.maximum(n, 1).astype(f32) / max_exact)
                         / math.log(MAX_DISTANCE / max_exact) * (half - max_exact)).astype(i32)
    large = jnp.minimum(large, half - 1)
    return jnp.where(rel > 0, half, 0) + jnp.where(n < max_exact, n, large)


def _far_bucket(min_dist):
    half = N_BUCKETS // 2
    max_exact = half // 2
    v = max_exact + int(math.log(min_dist / max_exact) / math.log(MAX_DISTANCE / max_exact) * (half - max_exact))
    return half - 1 if (min_dist >= max_exact and v - 1 >= half - 1) else None


def _bias_lookup(rel_bias, bucket):
    onehot = (bucket[..., None] == jnp.arange(N_BUCKETS, dtype=i32)).astype(f32)
    return jnp.dot(onehot, rel_bias.astype(f32), precision=lax.Precision.HIGHEST)


def _prompt_bias_tiles(rel_bias, ta):
    j = jnp.arange(ta, dtype=i32)[:, None]
    i = jnp.arange(ta, dtype=i32)[None, :]
    tiles = []
    for d in range(2):
        rel = j - i - d * ta
        b = _bias_lookup(rel_bias, _t5_bucket(rel))
        allowed = ((j // CHUNK) <= (i // CHUNK)) if d == 0 else jnp.ones((ta, ta), bool)
        tiles.append(jnp.where(allowed[..., None], b, NEG_INF))
    return jnp.transpose(jnp.stack(tiles), (3, 0, 1, 2))


def _sample_bias(rel_bias, past, s):
    q_pos = past + jnp.arange(s, dtype=i32)
    k_pos = jnp.arange(past + s, dtype=i32)
    rel = k_pos[None, :] - q_pos[:, None]
    b = _bias_lookup(rel_bias, _t5_bucket(rel))
    allowed = (k_pos[None, :] // CHUNK) <= (q_pos[:, None] // CHUNK)
    return jnp.transpose(jnp.where(allowed[..., None], b, NEG_INF), (2, 0, 1))


ATT_DIAG, ATT_SUBDIAG, ATT_FAR1, ATT_FAR2 = 0, 1, 2, 3


def _attn_prompt_kernel(qi_tab, ka_tab, kb_tab, kind_tab, qT_ref, k_ref, v1T_ref, k2_ref, v2T_ref, bias_ref,
                        cfar_ref, lq1_ref, lk1_ref, lq2_ref, lk2_ref, g_ref, o_ref,
                        qm_sc, m_sc, acc_sc, oT_sc, *, lam_init):
    del qi_tab, kb_tab
    p = pl.program_id(1)
    kind = kind_tab[p]
    ta = k_ref.shape[0]
    n_hp = H_A // 2

    @pl.when(ka_tab[p] == 0)
    def _():
        m_sc[...] = jnp.full(m_sc.shape, NEG_INF, f32)
        acc_sc[...] = jnp.zeros(acc_sc.shape, f32)
        row = lax.broadcasted_iota(i32, (LANES, ta), 0)
        for hp in range(n_hp):
            qt = qT_ref[hp * LANES:(hp + 1) * LANES, :]
            for gi in range(4):
                keep = (row >= gi * DQK) & (row < (gi + 1) * DQK)
                qm_sc[hp, :, gi * ta:(gi + 1) * ta] = jnp.where(keep, qt, jnp.zeros_like(qt))

    def scores(keys_ref, hp):
        return _dot(keys_ref[:, hp * LANES:(hp + 1) * LANES], qm_sc[hp])

    nb = ta // ATT_SUB

    def near_bias(hp, step_d):
        rows = []
        for jb in range(nb):
            cols = []
            for g in range(4):
                for ib in range(nb):
                    dd = step_d * nb + ib - jb
                    if dd < 0:
                        blk = jnp.full((ATT_SUB, ATT_SUB), NEG_INF * LOG2E, f32)
                    elif dd < 2:
                        blk = bias_ref[2 * hp + g // 2, dd]
                    else:
                        lo = g * ta + ib * ATT_SUB
                        blk = jnp.broadcast_to(cfar_ref[hp][:, lo:lo + ATT_SUB], (ATT_SUB, ATT_SUB))
                    cols.append(blk)
            rows.append(jnp.concatenate(cols, axis=1))
        return jnp.concatenate(rows, axis=0)

    def update(step_d, tiles):
        chain = [(kr, vr, hp) for kr, vr in tiles for hp in range(n_hp)]
        s_next = scores(chain[0][0], chain[0][2])
        for link, (_, values_ref, hp) in enumerate(chain):
            s = s_next
            if link + 1 < len(chain):
                s_next = scores(chain[link + 1][0], chain[link + 1][2])
            m_old = m_sc[hp]
            if step_d is not None:
                s = s + near_bias(hp, step_d)
                m_new = jnp.maximum(m_old, jnp.max(s, axis=0, keepdims=True))
                shift = m_new
            else:
                c = cfar_ref[hp]
                m_new = jnp.maximum(m_old, jnp.max(s, axis=0, keepdims=True) + c)
                shift = m_new - c
            alpha = jnp.exp2(m_old - m_new)
            pT = jnp.exp2(s - shift).astype(bf16)
            for hh in range(2):
                h = 2 * hp + hh
                cols = slice(hh * 2 * ta, (hh + 1) * 2 * ta)
                pv = _dot(values_ref[h * V_ROWS:(h + 1) * V_ROWS, :], pT[:, cols])
                acc_sc[h] = alpha[:, cols] * acc_sc[h] + pv
            m_sc[hp] = m_new

    @pl.when(kind == ATT_DIAG)
    def _():
        update(0, [(k_ref, v1T_ref)])

    @pl.when(kind == ATT_SUBDIAG)
    def _():
        update(1, [(k_ref, v1T_ref)])

    @pl.when(kind == ATT_FAR1)
    def _():
        update(None, [(k_ref, v1T_ref)])

    @pl.when(kind == ATT_FAR2)
    def _():
        update(None, [(k_ref, v1T_ref), (k2_ref, v2T_ref)])

    @pl.when(kind == ATT_DIAG)
    def _():
        lam = _lambda(lq1_ref, lk1_ref, lq2_ref, lk2_ref, lam_init)
        for h in range(H_A):
            a = acc_sc[h]
            a0 = a[:, :ta]
            a1 = a[:, ta:]
            o = a0[:DV] / a0[DV:DV + 1] - lam * (a1[:DV] / a1[DV:DV + 1])
            o = o * lax.rsqrt(jnp.mean(o * o, axis=0, keepdims=True) + RMS_EPS) * g_ref[...] * (1.0 - lam_init)
            oT_sc[h * DV:(h + 1) * DV, :] = o
        o_ref[...] = oT_sc[...].T.astype(bf16)


def _attn_prompt(qT, kb, v1T, rel_bias, lam_args, attn_g, B, L, lam_init):
    ta = min(ATT_TILE, L)
    assert L % ta == 0 and ta % ATT_SUB == 0 and ATT_SUB % CHUNK == 0
    nt = L // ta
    far = _far_bucket(ATT_SUB + 1)
    assert L <= 2 * ATT_SUB or far is not None, "key blocks two or more behind must share one bias bucket"
    far = far if far is not None else N_BUCKETS // 2 - 1
    steps = []
    idle_b = 0
    for q in range(nt):
        far_tiles = list(range(max(q - 1, 0)))
        for a in range(0, len(far_tiles) - 1, 2):
            steps.append((q, a, a + 1, ATT_FAR2))
            idle_b = a + 1
        if len(far_tiles) % 2:
            steps.append((q, far_tiles[-1], idle_b, ATT_FAR1))
        if q >= 1:
            steps.append((q, q - 1, idle_b, ATT_SUBDIAG))
        steps.append((q, q, idle_b, ATT_DIAG))
    qi_tab, ka_tab, kb_tab, kind_tab = (jnp.asarray([s[j] for s in steps], i32) for j in range(4))
    bias =_prompt_bias_tiles(rel_bias, ATT_SUB) * LOG2E
    cfar = jnp.repeat((rel_bias.astype(f32)[far] * LOG2E).reshape(H_A // 2, 2), 2 * ta, axis=1)[:, None, :]
    g_col = attn_g.astype(f32).reshape(DV, 1)
    T = B * L
    n_hp = H_A // 2
    full = lambda a: pl.BlockSpec(a.shape, lambda b, p, qt, ka, kb, kd: (0,) * a.ndim)
    grid_spec = pltpu.PrefetchScalarGridSpec(
        num_scalar_prefetch=4, grid=(B, len(steps)),
        in_specs=[
            pl.BlockSpec((D_QK_ALL, ta), lambda b, p, qt, ka, kb, kd: (0, b * nt + qt[p])),
            pl.BlockSpec((ta, D_QK_ALL), lambda b, p, qt, ka, kb, kd: (b * nt + ka[p], 0)),
            pl.BlockSpec((H_A * V_ROWS, ta), lambda b, p, qt, ka, kb, kd: (0, b * nt + ka[p])),
            pl.BlockSpec((ta, D_QK_ALL), lambda b, p, qt, ka, kb, kd: (b * nt + kb[p], 0)),
            pl.BlockSpec((H_A * V_ROWS, ta), lambda b, p, qt, ka, kb, kd: (0, b * nt + kb[p])),
            full(bias), full(cfar)] + [full(a) for a in lam_args] + [full(g_col)],
        out_specs=pl.BlockSpec((ta, D_ATT), lambda b, p, qt, ka, kb, kd: (b * nt + qt[p], 0)),
        scratch_shapes=[pltpu.VMEM((n_hp, LANES, 4 * ta), bf16), pltpu.VMEM((n_hp, 1, 4 * ta), f32),
                        pltpu.VMEM((H_A, V_ROWS, 2 * ta), f32), pltpu.VMEM((D_ATT, ta), f32)])
    return pl.pallas_call(
        functools.partial(_attn_prompt_kernel, lam_init=lam_init),
        grid_spec=grid_spec, out_shape=jax.ShapeDtypeStruct((T, D_ATT), bf16),
        compiler_params=_params(("parallel", "arbitrary")),
    )(qi_tab, ka_tab, kb_tab, kind_tab, qT, kb, v1T, kb, v1T, bias, cfar, *lam_args, g_col)


def _attn_sample_kernel(q_ref, kn_ref, vn_ref, kc_ref, vc_ref, bc_ref, bn_ref,
                        lq1_ref, lk1_ref, lq2_ref, lk2_ref, g2_ref, o_ref, *, lam_init):
    s_len = q_ref.shape[0]
    lane = lax.broadcasted_iota(i32, (s_len, LANES), 1)
    lam = _lambda(lq1_ref, lk1_ref, lq2_ref, lk2_ref, lam_init)
    for hp in range(H_A // 2):
        sl = slice(hp * LANES, (hp + 1) * LANES)
        qt = q_ref[:, sl]
        kct = kc_ref[0, sl, :].astype(bf16)
        vct = vc_ref[0, sl, :].astype(bf16)
        knt = kn_ref[:, sl].astype(bf16)
        vnt = vn_ref[:, sl].astype(bf16)
        outs = []
        for hh in range(2):
            h = 2 * hp + hh
            oc = []
            for c in range(2):
                gi = 2 * hh + c
                qm = jnp.where((lane >= gi * DQK) & (lane < (gi + 1) * DQK), qt, jnp.zeros_like(qt))
                s1 = _dot(qm, kct) + bc_ref[h]
                s2 = _dot_nt(qm, knt) + bn_ref[h]
                m = jnp.maximum(jnp.max(s1, axis=-1, keepdims=True), jnp.max(s2, axis=-1, keepdims=True))
                p1 = jnp.exp(s1 - m)
                p2 = jnp.exp(s2 - m)
                l = jnp.sum(p1, axis=-1, keepdims=True) + jnp.sum(p2, axis=-1, keepdims=True)
                pv = _dot_nt(p1.astype(bf16), vct) + _dot(p2.astype(bf16), vnt)
                oc.append(pv / l)
            o = oc[0] - lam * oc[1]
            in_head = (lane >= hh * DV) & (lane < (hh + 1) * DV)
            ms = jnp.sum(jnp.where(in_head, o * o, 0.0), axis=-1, keepdims=True) * (1.0 / DV)
            outs.append(o * lax.rsqrt(ms + RMS_EPS))
        ot = jnp.where(lane < DV, outs[0], outs[1]) * g2_ref[...] * (1.0 - lam_init)
        o_ref[:, sl] = ot.astype(bf16)


def _attn_sample(q, k_new, v_new, cache_k, cache_v, rel_bias, lam_args, attn_g, lam_init):
    nb, past = cache_k.shape[0], cache_k.shape[1]
    T = q.shape[0]
    s = T // nb
    kc = cache_k.transpose(0, 2, 3, 1).reshape(nb, D_QK_ALL, past)
    vc = cache_v.transpose(0, 2, 3, 1).reshape(nb, D_ATT, past)
    bias = _sample_bias(rel_bias, past, s)
    bc, bn = bias[:, :, :past], bias[:, :, past:]
    g2 = jnp.tile(attn_g.astype(f32), 2).reshape(1, LANES)
    full = lambda a: pl.BlockSpec(a.shape, lambda b: (0,) * a.ndim)
    row = lambda n: pl.BlockSpec((s, n), lambda b: (b, 0))
    return pl.pallas_call(
        functools.partial(_attn_sample_kernel, lam_init=lam_init),
        grid=(nb,),
        in_specs=[row(D_QK_ALL), row(D_QK_ALL), row(D_ATT),
                  pl.BlockSpec((1, D_QK_ALL, past), lambda b: (b, 0, 0)),
                  pl.BlockSpec((1, D_ATT, past), lambda b: (b, 0, 0)),
                  full(bc), full(bn)] + [full(a) for a in lam_args] + [full(g2)],
        out_specs=row(D_ATT), out_shape=jax.ShapeDtypeStruct((T, D_ATT), bf16),
        compiler_params=_params(("parallel",)),
    )(q, k_new, v_new, kc, vc, bc, bn, *lam_args, g2)


def _ssd_kernel(z_ref, xbc_ref, dt_ref, dtT_ref, h0_ref, c0_ref, cw_ref, cb_ref, dtb_ref, dtbT_ref,
                alog_ref, alogT_ref, dsk_ref, g_ref, y_ref, hout_ref, cout_ref, xpad_sc, h_sc):
    q = xbc_ref.shape[0]
    c = pl.program_id(1)
    gw = HEADS_PER_GROUP * SSM_HEADDIM

    @pl.when(c == 0)
    def _():
        xpad_sc[0:SUBLANES] = c0_ref[0]
        h_sc[...] = h0_ref[0]

    xpad_sc[SUBLANES:SUBLANES + q] = xbc_ref[...]
    first = SUBLANES - (CONV_W - 1)
    conv = cb_ref[...]
    for i in range(CONV_W):
        conv = conv + xpad_sc[first + i:first + i + q] * cw_ref[i:i + 1, :]
    tail = xpad_sc[q:q + SUBLANES]
    xpad_sc[0:SUBLANES] = tail
    cout_ref[0] = tail
    act = conv * _sigmoid(conv)
    xs = act[:, :D_SSM]

    lane = lax.broadcasted_iota(i32, (1, LANES), 1)
    a_row = jnp.where(lane < H_S, -jnp.exp(alog_ref[...]) * LOG2E, 0.0)
    dt = _softplus(dt_ref[...] + dtb_ref[...])
    ii = lax.broadcasted_iota(i32, (q, q), 0)
    jj = lax.broadcasted_iota(i32, (q, q), 1)
    causal = jj <= ii
    acs = _dot_f32_rhs(causal.astype(bf16), dt * a_row)
    sub = lax.broadcasted_iota(i32, (BF16_ROWS, 1), 0)
    a_col = jnp.where(sub < H_S, -jnp.exp(alogT_ref[...]) * LOG2E, 0.0)
    dtT = _softplus(dtT_ref[0] + dtbT_ref[...])
    acsT = _dot_f32_lhs(dtT * a_col, (ii <= jj).astype(bf16))

    er = lax.broadcasted_iota(i32, (LANES, D_SSM), 0)
    ec = lax.broadcasted_iota(i32, (LANES, D_SSM), 1)
    expand = (ec // SSM_HEADDIM == er).astype(bf16)
    dt_x = _dot_f32_lhs(dt, expand)
    acs_x = _dot_f32_lhs(acs, expand)
    e_acs = jnp.exp2(acs_x)
    acs_last = acs_x[q - 1:q, :]
    decay = jnp.exp2(acs_last - acs_x) * dt_x
    dtx = xs * dt_x
    xd = xs * decay
    glane = lax.broadcasted_iota(i32, (1, gw), 1)

    ys = []
    for g in range(SSM_GROUPS):
        gs = slice(g * gw, (g + 1) * gw)
        bg = act[:, D_SSM + g * D_STATE:D_SSM + (g + 1) * D_STATE].astype(bf16)
        cg = act[:, D_SSM + (SSM_GROUPS + g) * D_STATE:D_SSM + (SSM_GROUPS + g + 1) * D_STATE].astype(bf16)
        cb = _dot_nt(cg, bg)
        h_old = h_sc[g]
        yg = _dot(cg, h_old.astype(bf16)) * e_acs[:, gs]
        dtx_g = dtx[:, gs]
        for e4 in range(HEADS_PER_GROUP):
            e = g * HEADS_PER_GROUP + e4
            seg = acs[:, e:e + 1] - acsT[e:e + 1, :]
            lmat = jnp.exp2(jnp.where(causal, seg, -jnp.inf))
            rhs = jnp.where(glane // SSM_HEADDIM == e4, dtx_g, 0.0).astype(bf16)
            yg = yg + _dot((cb * lmat).astype(bf16), rhs)
        h_sc[g] = h_old * jnp.exp2(acs_last[:, gs]) + _dot_tn(bg, xd[:, gs].astype(bf16))
        ys.append(yg)
    y = jnp.concatenate(ys, axis=1) + dsk_ref[...] * xs
    zz = z_ref[...]
    y = y * (zz * _sigmoid(zz))
    for g in range(SSM_GROUPS):
        gs = slice(g * gw, (g + 1) * gw)
        yg = y[:, gs]
        r = lax.rsqrt(jnp.mean(yg * yg, axis=-1, keepdims=True) + RMS_EPS)
        y_ref[:, gs] = (yg * r * g_ref[:, gs]).astype(bf16)
    hout_ref[0] = h_sc[...]


def _ssd(z, xbc, dt, dtT, h0, conv0, pw, B, L):
    q = min(SSD_TILE, L)
    nc = L // q
    gw = HEADS_PER_GROUP * SSM_HEADDIM
    dtT3 = dtT.reshape(BF16_ROWS, B, L).transpose(1, 0, 2)
    hT0 = h0.astype(f32).reshape(B, SSM_GROUPS, gw, D_STATE).transpose(0, 1, 3, 2)
    c0 = jnp.pad(conv0.astype(f32), ((0, 0), (SUBLANES - (CONV_W - 1), 0), (0, 0)))
    full = lambda a: pl.BlockSpec(a.shape, lambda b, c: (0,) * a.ndim)
    row = lambda n: pl.BlockSpec((q, n), lambda b, c: (b * nc + c, 0))
    params = [pw['conv_w'], pw['conv_b'], pw['dt_bias'], pw['dt_biasT'], pw['a_log'], pw['a_logT'],
              pw['d_skip'], pw['ssm_g']]
    y, hT, ctail = pl.pallas_call(
        _ssd_kernel, grid=(B, nc),
        in_specs=[row(D_SSM), row(CONV_DIM), row(LANES),
                  pl.BlockSpec((1, BF16_ROWS, q), lambda b, c: (b, 0, c)),
                  pl.BlockSpec((1, SSM_GROUPS, D_STATE, gw), lambda b, c: (b, 0, 0, 0)),
                  pl.BlockSpec((1, SUBLANES, CONV_DIM), lambda b, c: (b, 0, 0))] + [full(a) for a in params],
        out_specs=[row(D_SSM),
                   pl.BlockSpec((1, SSM_GROUPS, D_STATE, gw), lambda b, c: (b, 0, 0, 0)),
                   pl.BlockSpec((1, SUBLANES, CONV_DIM), lambda b, c: (b, 0, 0))],
        out_shape=[jax.ShapeDtypeStruct((B * L, D_SSM), bf16),
                   jax.ShapeDtypeStruct((B, SSM_GROUPS, D_STATE, gw), f32),
                   jax.ShapeDtypeStruct((B, SUBLANES, CONV_DIM), f32)],
        scratch_shapes=[pltpu.VMEM((q + SUBLANES, CONV_DIM), f32), pltpu.VMEM((SSM_GROUPS, D_STATE, gw), f32)],
        compiler_params=_params(("parallel", "arbitrary")),
    )(z, xbc, dt, dtT3, hT0, c0, *params)
    h_new = hT.transpose(0, 1, 3, 2).reshape(B, H_S, SSM_HEADDIM, D_STATE)
    return y, h_new, ctail[:, SUBLANES - (CONV_W - 1):, :]


def _pack_bf16_pairs(x):
    half = x.shape[1] // 2
    bits = lambda a: lax.bitcast_convert_type(a.astype(bf16).astype(f32), jnp.uint32)
    return (bits(x[:, half:]) & jnp.uint32(0xFFFF0000)) | (bits(x[:, :half]) >> 16)


def _unpack_bf16_pairs(word):
    lo = lax.bitcast_convert_type(word << 16, f32)
    hi = lax.bitcast_convert_type(word & jnp.uint32(0xFFFF0000), f32)
    return jnp.concatenate([lo, hi], axis=1)


def _mix_router_kernel(att_ref, ssm_ref, x_ref, woa_ref, wos_ref, g_ref, b_ref, wrh_ref, wrhl_ref, br_ref,
                       cnt0_ref, *rest):
    x1_ref, x1p_ref, te_ref, gate_ref, cnt_ref, run_sc, earlier_sc = rest[-7:]

    @pl.when(pl.program_id(0) == 0)
    def _():
        run_sc[...] = cnt0_ref[...]
        ii = lax.broadcasted_iota(i32, earlier_sc.shape, 0)
        jj = lax.broadcasted_iota(i32, earlier_sc.shape, 1)
        earlier_sc[...] = (jj < ii).astype(bf16)

    mix = _dot(att_ref[...], woa_ref[...]) + _dot(ssm_ref[...], wos_ref[...])
    x1 = _layer_norm(DEEPNORM_ALPHA * x_ref[...] + mix, g_ref[...], b_ref[...])
    x1_ref[...] = x1
    x1p_ref[...] = _pack_bf16_pairs(x1)
    hi = x1.astype(bf16)
    lo = (x1 - hi.astype(f32)).astype(bf16)
    both = _dot(hi, wrhl_ref[...])
    logits = both[:, :LANES] + both[:, LANES:] + _dot(lo, wrh_ref[...]) + br_ref[...]
    lane =lax.broadcasted_iota(i32, logits.shape, 1)
    vals, idxs = [], []
    for _ in range(TOP_K):
        m = jnp.max(logits, axis=-1, keepdims=True)
        idx = jnp.min(jnp.where(logits == m, lane, LANES), axis=-1, keepdims=True)
        vals.append(m)
        idxs.append(idx)
        logits = jnp.where(lane == idx, -jnp.inf, logits)
    es = [jnp.exp(v - vals[0]) for v in vals]
    tot = es[0]
    for e in es[1:]:
        tot = tot + e
    chosen = jnp.zeros(logits.shape, f32)
    for k in range(TOP_K):
        chosen = chosen + (lane == idxs[k]).astype(f32)
    before = _dot(earlier_sc[...], chosen.astype(bf16)) + run_sc[...]
    te = jnp.zeros(logits.shape, i32)
    gate = jnp.zeros(logits.shape, f32)
    for k in range(TOP_K):
        rank = jnp.sum(jnp.where(lane == idxs[k], before, 0.0), axis=-1, keepdims=True).astype(i32)
        te = jnp.where(lane == k, idxs[k], te)
        te = jnp.where(lane == TOP_K + k, rank, te)
        gate = jnp.where(lane == k, es[k] / tot, gate)
    te_ref[...] = te
    gate_ref[...] = gate
    run_sc[...] = run_sc[...] + jnp.sum(chosen, axis=0, keepdims=True)
    cnt_ref[...] = run_sc[...]


def _mix_router(att, ssm, x2d, w, counts0, packed, row0, rows_total):
    T, D = x2d.shape
    tl = min(ROW_TILE, T)
    assert row0 % tl == 0
    row = lambda n: pl.BlockSpec((tl, n), lambda i: (i, 0))
    full = lambda a: pl.BlockSpec(a.shape, lambda i: (0,) * a.ndim)
    ws = [w['wo_att'], w['wo_ssm'], w['ln1_g'], w['ln1_b'], w['wr_hi'], w['wr_hi_lo'], w['b_router'], counts0]
    ins = [att, ssm, x2d] + ws + [packed]
    in_specs = [row(D_ATT), row(D_SSM), row(D)] + [full(a) for a in ws] + [pl.BlockSpec(memory_space=pl.ANY)]
    aliases = {len(ins) - 1: 1}
    return pl.pallas_call(
        _mix_router_kernel, grid=(T // tl,), in_specs=in_specs,
        out_specs=[row(D), pl.BlockSpec((tl, D // 2), lambda i: (row0 // tl + i, 0)), row(LANES), row(LANES),
                   pl.BlockSpec((1, LANES), lambda i: (0, 0))],
        out_shape=[jax.ShapeDtypeStruct((T, D), f32), jax.ShapeDtypeStruct((rows_total, D // 2), jnp.uint32),
                   jax.ShapeDtypeStruct((T, LANES), i32),
                   jax.ShapeDtypeStruct((T, LANES), f32), jax.ShapeDtypeStruct((1, LANES), f32)],
        scratch_shapes=[pltpu.VMEM((1, LANES), f32), pltpu.VMEM((tl, tl), bf16)], input_output_aliases=aliases,
        compiler_params=_params(("arbitrary",)),
    )(*ins)


def _route(te, counts_f):
    T = te.shape[0]
    n_assign = T * TOP_K
    counts = counts_f[0, :N_EXPERTS].astype(i32)
    padded = (counts + MOE_BLOCK - 1) // MOE_BLOCK * MOE_BLOCK
    pad_end = jnp.cumsum(padded)
    pad_start = pad_end - padded
    top_e, rank = te[:, :TOP_K], te[:, TOP_K:2 * TOP_K]
    experts = jnp.arange(N_EXPERTS, dtype=i32)
    slot = rank + jnp.sum(jnp.where(top_e[..., None] == experts, pad_start, 0), axis=-1)
    n_blocks = -(-n_assign // MOE_BLOCK) + N_EXPERTS
    n_blocks += n_blocks % FFN_BLOCKS_PER_STEP
    block_start = jnp.arange(n_blocks, dtype=i32) * MOE_BLOCK
    block_e = jnp.minimum(jnp.sum(block_start[:, None] >= pad_end[None, :], axis=-1), N_EXPERTS - 1).astype(i32)
    n_used = (pad_end[-1] // MOE_BLOCK).astype(i32).reshape(1)
    row_end = jnp.sum(jnp.where(block_e[:, None] == experts, pad_start + counts, 0), axis=-1)
    rows_valid = jnp.clip(row_end - block_start, 0, MOE_BLOCK).astype(i32)
    return block_e, n_used, rows_valid, slot.astype(i32), n_blocks


SC_INDEX_WINDOW = 128
SC_ROWS = 64


def _sc_mesh():
    return plsc.VectorSubcoreMesh(core_axis_name="c", subcore_axis_name="s")


def _sc_move_rows(src, src_idx, dst_idx_list, n_out):
    M = src_idx.shape[0]
    D = src.shape[1]
    idx = [a.reshape(1, M) for a in [src_idx] + list(dst_idx_list)]

    @functools.partial(pl.kernel, out_type=jax.ShapeDtypeStruct((n_out, D), src.dtype), mesh=_sc_mesh(),
                       scratch_types=[pltpu.VMEM((2, SC_ROWS, D), src.dtype), pltpu.SemaphoreType.DMA((2,))])
    def move(s_hbm, *rest):
        i_hbm, o_hbm, buf, sem = rest[:-3], rest[-3], rest[-2], rest[-1]
        n_parts = SC_INDEX_WINDOW // SC_ROWS

        def body(si_vmem, *di_vmem):
            def fetch(j):
                part = pl.ds(j * SC_ROWS, SC_ROWS)
                return pltpu.async_copy(s_hbm.at[si_vmem.at[0, part]], buf.at[j % 2], sem.at[j % 2])

            pending = fetch(0)
            for j in range(n_parts):
                nxt = fetch(j + 1) if j + 1 < n_parts else None
                pending.wait()
                part = pl.ds(j * SC_ROWS, SC_ROWS)
                for dv in di_vmem:
                    pltpu.sync_copy(buf.at[j % 2], o_hbm.at[dv.at[0, part]])
                pending = nxt

        pltpu.emit_pipeline(
            body, grid=(M // SC_INDEX_WINDOW,),
            in_specs=[pl.BlockSpec((1, SC_INDEX_WINDOW), lambda i: (0, i))] * len(idx),
            out_specs=[], core_axis_name=("c", "s"), dimension_semantics=(pltpu.PARALLEL,),
        )(*i_hbm)

    return move(src, *idx)


def _sc_scatter_rows(x, idx_k, n_rows):
    return _sc_move_rows(x, jnp.arange(x.shape[0], dtype=i32), idx_k, n_rows)


def _sc_gather_rows(src, idx):
    M = idx.shape[0]
    return _sc_move_rows(src, idx, [jnp.arange(M, dtype=i32)], M)


def _deinterleave_kernel(w_ref, g_ref, u_ref):
    tn = 2 * LANES
    r = lax.broadcasted_iota(i32, (2 * tn, tn), 0)
    c = lax.broadcasted_iota(i32, (2 * tn, tn), 1)
    pick_g = (r == 2 * c).astype(bf16)
    pick_u = (r == 2 * c + 1).astype(bf16)
    for j in range(g_ref.shape[-1] // tn):
        wb = w_ref[0, :, j * 2 * tn:(j + 1) * 2 * tn].astype(bf16)
        g_ref[0, :, j * tn:(j + 1) * tn] = _dot(wb, pick_g).astype(bf16)
        u_ref[0, :, j * tn:(j + 1) * tn] = _dot(wb, pick_u).astype(bf16)


def _deinterleave(w_gu):
    E, D, F2 = w_gu.shape
    out = jax.ShapeDtypeStruct((E, D, F2 // 2), bf16)
    return pl.pallas_call(
        _deinterleave_kernel, grid=(E,),
        in_specs=[pl.BlockSpec((1, D, F2), lambda e: (e, 0, 0))],
        out_specs=[pl.BlockSpec((1, D, F2 // 2), lambda e: (e, 0, 0))] * 2,
        out_shape=[out, out],
        compiler_params=_params(("parallel",)),
    )(w_gu)


def _ffn_kernel(be_ref, nu_ref, rv_ref, xs_ref, *refs):
    del be_ref
    y_ref = refs[-1]
    wsets = [refs[6 * j:6 * j + 6] for j in range(FFN_BLOCKS_PER_STEP)]
    first = pl.program_id(0) * FFN_BLOCKS_PER_STEP
    n_live = jnp.clip(nu_ref[0] - first, 0, FFN_BLOCKS_PER_STEP)
    row_id = lax.broadcasted_iota(i32, (MOE_BLOCK, 1), 0)

    def block(j):
        wg_ref, wu_ref, bg_ref, bu_ref, wd_ref, bd_ref = wsets[j]
        rows = pl.ds(j * MOE_BLOCK, MOE_BLOCK)
        words = jnp.where(row_id < rv_ref[first + j], xs_ref[rows, :], jnp.uint32(0))
        xb = _unpack_bf16_pairs(words).astype(bf16)
        g = _dot(xb, wg_ref[0]) + bg_ref[0]
        u = _dot(xb, wu_ref[0]) + bu_ref[0]
        g = jnp.minimum(g, SWIGLU_LIMIT)
        u = jnp.clip(u, -SWIGLU_LIMIT, SWIGLU_LIMIT)
        act = (u + 1.0) * g * _sigmoid(SWIGLU_ALPHA * g)
        y_ref[rows, :] = _pack_bf16_pairs(_dot(act.astype(bf16), wd_ref[0].astype(bf16)) + bd_ref[0])

    for live in range(FFN_BLOCKS_PER_STEP + 1):
        @pl.when(n_live == live)
        def _(live=live):
            for j in range(live):
                block(j)
            for j in range(live, FFN_BLOCKS_PER_STEP):
                y_ref[pl.ds(j * MOE_BLOCK, MOE_BLOCK), :] = jnp.zeros((MOE_BLOCK, y_ref.shape[1]), jnp.uint32)


def _expert_ffn(xs, block_e, n_used, rows_valid, w):
    n_rows, half = xs.shape
    D = 2 * half
    nps = FFN_BLOCKS_PER_STEP
    n_steps = n_rows // (MOE_BLOCK * nps)
    F = w['w_g'].shape[2]
    wspecs, wargs = [], []
    for j in range(nps):
        pick = lambda i, be, nu, rv, j=j: (be[i * nps + j], 0, 0)
        wspecs += [pl.BlockSpec((1, D, F), pick), pl.BlockSpec((1, D, F), pick), pl.BlockSpec((1, 1, F), pick),
                   pl.BlockSpec((1, 1, F), pick), pl.BlockSpec((1, F, D), pick), pl.BlockSpec((1, 1, D), pick)]
        wargs += [w['w_g'], w['w_u'], w['b_g'], w['b_u'], w['w_d'], w['b_d']]
    grid_spec = pltpu.PrefetchScalarGridSpec(
        num_scalar_prefetch=3, grid=(n_steps,),
        in_specs=[pl.BlockSpec((nps * MOE_BLOCK, half),
                               lambda i, be, nu, rv: (jnp.minimum(i, (nu[0] - 1) // nps), 0))] + wspecs,
        out_specs=pl.BlockSpec((nps * MOE_BLOCK, half), lambda i, be, nu, rv: (i, 0)))
    return pl.pallas_call(
        _ffn_kernel, grid_spec=grid_spec,
        out_shape=jax.ShapeDtypeStruct((n_rows, half), jnp.uint32),
        compiler_params=_params(("arbitrary",)),
    )(block_e, n_used, rows_valid, xs, *wargs)


def _combine_dense_kernel(rows_ref, x1_ref, gate_ref, g_ref, b_ref, *rest):
    y_ref = rest[-1]
    gate = gate_ref[...]
    ff = gate[:, 0:1] * _unpack_bf16_pairs(rows_ref[0])
    for k in range(1, TOP_K):
        ff = ff + gate[:, k:k + 1] * _unpack_bf16_pairs(rows_ref[k])
    y_ref[...] = _layer_norm(DEEPNORM_ALPHA * x1_ref[...] + ff, g_ref[...], b_ref[...])


def _combine_dense(rows, x1, gate, ln_g, ln_b, chunk, y_partial):
    T, D = x1.shape
    tc = rows.shape[1]
    tl = min(ROW_TILE, tc)
    first = chunk * (tc // tl)
    row = lambda n: pl.BlockSpec((tl, n), lambda i: (first + i, 0))
    full = lambda a: pl.BlockSpec(a.shape, lambda i: (0,) * a.ndim)
    ins = [rows, x1, gate, ln_g, ln_b]
    in_specs = [pl.BlockSpec((TOP_K, tl, D // 2), lambda i: (0, i, 0)), row(D), row(LANES), full(ln_g), full(ln_b)]
    aliases = {}
    if y_partial is not None:
        ins.append(y_partial)
        in_specs.append(pl.BlockSpec(memory_space=pl.ANY))
        aliases = {len(ins) - 1: 0}
    return pl.pallas_call(
        _combine_dense_kernel, grid=(tc // tl,), in_specs=in_specs,
        out_specs=row(D), out_shape=jax.ShapeDtypeStruct((T, D), f32), input_output_aliases=aliases,
        compiler_params=_params(("parallel",)),
    )(*ins)


def _prep_weights(l, w_in, conv_w, conv_b, dt_bias, a_log, d_skip, ssm_norm_g, w_o, ln1_g, ln1_b,
                  w_router, b_router, w_gate_up, b_gate_up, w_down, b_down, ln2_g, ln2_b):
    wi = w_in[l]
    c0, c1, c2, c3 = D_QK_ALL, 2 * D_QK_ALL, 2 * D_QK_ALL + D_ATT, 2 * D_QK_ALL + D_ATT + D_SSM
    c4 = c3 + CONV_DIM
    wdt = wi[:, c4:c4 + H_S]
    pad_lane = lambda v, fill=0.0: jnp.pad(v.astype(f32).reshape(1, -1), ((0, 0), (0, LANES - v.shape[-1])),
                                            constant_values=fill)
    pad_col = lambda v: jnp.pad(v.astype(f32).reshape(-1, 1), ((0, BF16_ROWS - v.shape[-1]), (0, 0)))
    wr = jnp.pad(w_router[l].astype(f32), ((0, 0), (0, LANES - N_EXPERTS)))
    wr_hi = wr.astype(bf16)
    wgu = w_gate_up[l]
    return {
        'wq': wi[:, :c0].astype(bf16), 'wqT': wi[:, :c0].T.astype(bf16),
        'wk': wi[:, c0:c1].astype(bf16), 'wkT': wi[:, c0:c1].T.astype(bf16), 'wv': wi[:, c1:c2].astype(bf16), 'wvT': wi[:, c1:c2].T.astype(bf16),
        'wz': wi[:, c2:c3].astype(bf16), 'wx': wi[:, c3:c4].astype(bf16),
        'wdt': jnp.pad(wdt, ((0, 0), (0, LANES - H_S))).astype(bf16),
        'wdtT': jnp.pad(wdt.T, ((0, BF16_ROWS - H_S), (0, 0))).astype(bf16),
        'conv_w': conv_w[l].astype(f32), 'conv_b': conv_b[l].astype(f32).reshape(1, -1),
        'dt_bias': pad_lane(dt_bias[l]), 'dt_biasT': pad_col(dt_bias[l]),
        'a_log': pad_lane(a_log[l]), 'a_logT': pad_col(a_log[l]),
        'd_skip': jnp.repeat(d_skip[l].astype(f32), SSM_HEADDIM).reshape(1, -1),
        'ssm_g': ssm_norm_g[l].astype(f32).reshape(1, -1),
        'wo_att': w_o[l][:D_ATT].astype(bf16), 'wo_ssm': w_o[l][D_ATT:].astype(bf16),
        'ln1_g': ln1_g[l].astype(f32).reshape(1, -1), 'ln1_b': ln1_b[l].astype(f32).reshape(1, -1),
        'wr_hi': wr_hi, 'wr_hi_lo': jnp.concatenate([wr_hi, (wr - wr_hi.astype(f32)).astype(bf16)], axis=1),
        'b_router': pad_lane(b_router[l], NEG_INF),
        'w_gu': wgu,
        'b_g': b_gate_up[l][:, None, 0::2].astype(f32), 'b_u': b_gate_up[l][:, None, 1::2].astype(f32),
        'w_d': w_down[l], 'b_d': b_down[l][:, None, :].astype(f32),
        'ln2_g': ln2_g[l].astype(f32).reshape(1, -1), 'ln2_b': ln2_b[l].astype(f32).reshape(1, -1),
    }


def _mixers(x, l, w, rel_bias, lam_args, attn_g, k_past, v_past, h0, conv0):
    B, L, D = x.shape
    x2d = x.reshape(B * L, D)
    lam_init = 0.8 - 0.6 * math.exp(-0.3 * l)
    prompt = k_past is None
    proj = _in_proj(x2d, w, prompt, B)
    z, xbc, dt, dtT = proj[:4]
    if prompt:
        kT, vT, kb, qT, v1T = proj[4:]
        att = _attn_prompt(qT, kb, v1T, rel_bias, lam_args, attn_g, B, L, lam_init)
        k_rows = kT.reshape(B, H_A, 2 * DQK, L).transpose(0, 3, 1, 2)
        v_rows = vT.reshape(B, H_A, DV, L).transpose(0, 3, 1, 2)
    else:
        k, v, q = proj[4:]
        att = _attn_sample(q, k, v, k_past, v_past, rel_bias, lam_args, attn_g, lam_init)
        k_rows = k.reshape(B, L, H_A, 2 * DQK)
        v_rows = v.reshape(B, L, H_A, DV)
    ssm, h_new, conv_new = _ssd(z, xbc, dt, dtT, h0, conv0, w, B, L)
    return (att, ssm, x2d), (k_rows, v_rows, h_new, conv_new)


def _moe_layer(streams, w):
    rows_total = sum(x2d.shape[0] for _, _, x2d in streams)
    counts = jnp.zeros((1, LANES), f32)
    packed = jnp.zeros((rows_total, streams[0][2].shape[1] // 2), jnp.uint32)
    row0, routed = 0, []
    for att, ssm, x2d in streams:
        x1, packed, te, gate, counts = _mix_router(att, ssm, x2d, w, counts, packed, row0, rows_total)
        routed.append((x1, gate, row0))
        row0 += x2d.shape[0]
        te_all = te[:, :2 * TOP_K] if len(routed) == 1 else jnp.concatenate([te_all, te[:, :2 * TOP_K]])
    block_e, n_used, rows_valid, slot, n_blocks = _route(te_all, counts)
    slot_k = slot.T
    xs = _sc_scatter_rows(packed, [slot_k[k] for k in range(TOP_K)], n_blocks * MOE_BLOCK)
    ys = _expert_ffn(xs, block_e, n_used, rows_valid, w)
    outs = []
    for x1, gate, r0 in routed:
        T, D = x1.shape
        n_groups = COMBINE_GROUPS if T % (COMBINE_GROUPS * ROW_TILE) == 0 else 1
        tc = T // n_groups
        y = None
        for c in range(n_groups):
            idx = slot_k[:, r0 + c * tc:r0 + (c + 1) * tc].reshape(-1)
            rows = _sc_gather_rows(ys, idx).reshape(TOP_K, tc, D // 2)
            y = _combine_dense(rows, x1, gate, w['ln2_g'], w['ln2_b'], c, y)
        outs.append(y)
    return outs


def kernel(x_prompt, x_sample, cache_k, cache_v, state_ssm, state_conv, rel_bias, w_in, lambda_q1, lambda_k1, lambda_q2, lambda_k2, attn_norm_g, conv_w, conv_b, dt_bias, a_log, d_skip, ssm_norm_g, w_o, ln1_g, ln1_b, w_router, b_router, w_gate_up, b_gate_up, w_down, b_down, ln2_g, ln2_b):
    yp, ys = x_prompt, x_sample
    bp = x_prompt.shape[0]
    depth = w_in.shape[0]
    outs = [[] for _ in range(8)]
    for l in range(depth):
        w = _prep_weights(l, w_in, conv_w, conv_b, dt_bias, a_log, d_skip, ssm_norm_g, w_o, ln1_g, ln1_b,
                          w_router, b_router, w_gate_up, b_gate_up, w_down, b_down, ln2_g, ln2_b)
        w['w_g'], w['w_u'] = _deinterleave(w.pop('w_gu'))
        lam_args = [a[l].astype(f32).reshape(1, -1) for a in (lambda_q1, lambda_k1, lambda_q2, lambda_k2)]
        h0 = jnp.zeros((bp, H_S, SSM_HEADDIM, D_STATE), f32)
        c0 = jnp.zeros((bp, CONV_W - 1, CONV_DIM), f32)
        mix_p, state_p = _mixers(yp, l, w, rel_bias, lam_args, attn_norm_g[l], None, None, h0, c0)
        mix_s, state_s = _mixers(ys, l, w, rel_bias, lam_args, attn_norm_g[l], cache_k[l], cache_v[l],
                                 state_ssm[l], state_conv[l])
        y2p, y2s = _moe_layer([mix_p, mix_s], w)
        yp, ys = y2p.reshape(yp.shape), y2s.reshape(ys.shape)
        for lst, a in zip(outs, state_p + state_s):
            lst.append(a)
    return (yp, ys) + tuple(jnp.stack(o) for o in outs)
```

```python
import functools
import math

import jax
import jax.numpy as jnp
from jax import lax
from jax.experimental import pallas as pl
from jax.experimental.pallas import tpu as pltpu
from jax.experimental.pallas import tpu_sc as plsc

f32 = jnp.float32
bf16 = jnp.bfloat16
i32 = jnp.int32

CHUNK = 64
H_A = 8
DQK = 32
DV = 2 * DQK
D_ATT = H_A * DV
SSM_HEADDIM = 64
H_S = 8
D_SSM = H_S * SSM_HEADDIM
SSM_GROUPS = 2
HEADS_PER_GROUP = H_S // SSM_GROUPS
D_STATE = 128
CONV_W = 4
CONV_DIM = D_SSM + 2 * SSM_GROUPS * D_STATE
D_QK_ALL = H_A * 2 * DQK
N_BUCKETS = 32
MAX_DISTANCE = 128
N_EXPERTS = 32
TOP_K = 4
SWIGLU_LIMIT = 7.0
SWIGLU_ALPHA = 1.702
MOE_BLOCK = 256
LN_EPS = 1e-5
RMS_EPS = 1e-5
NEG_INF = -1e30
DEPTH = 1
DEEPNORM_ALPHA = (2.0 * DEPTH) ** 0.25
LOG2E = math.log2(math.e)

LANES = 128
SUBLANES = 8
BF16_ROWS = 16
VMEM_LIMIT = 48 * 1024 * 1024

ROW_TILE = 512
ATT_TILE = 512
ATT_SUB = 256
SSD_TILE = 256
V_ROWS = DV + BF16_ROWS
FFN_BLOCKS_PER_STEP = 2
COMBINE_GROUPS = 4


def _params(semantics):
    return pltpu.CompilerParams(dimension_semantics=semantics, vmem_limit_bytes=VMEM_LIMIT)


def _dot(a, b):
    return jnp.dot(a, b, preferred_element_type=f32)


def _dot_nt(a, b):
    return lax.dot_general(a, b, (((1,), (1,)), ((), ())), preferred_element_type=f32)


def _dot_tn(a, b):
    return lax.dot_general(a, b, (((0,), (0,)), ((), ())), preferred_element_type=f32)


def _split3(a):
    hi = a.astype(bf16)
    r1 = a - hi.astype(f32)
    mid = r1.astype(bf16)
    lo = (r1 - mid.astype(f32)).astype(bf16)
    return hi, mid, lo


def _dot_f32_lhs(a, b_exact):
    hi, mid, lo = _split3(a)
    return _dot(hi, b_exact) + _dot(mid, b_exact) + _dot(lo, b_exact)


def _dot_f32_rhs(a_exact, b):
    hi, mid, lo = _split3(b)
    return _dot(a_exact, hi) + _dot(a_exact, mid) + _dot(a_exact, lo)


def _softplus(x):
    return jnp.maximum(x, 0.0) + jnp.log1p(jnp.exp(-jnp.abs(x)))


def _sigmoid(x):
    return 1.0 / (1.0 + jnp.exp(-x))


def _layer_norm(y, g, b):
    mu = jnp.mean(y, axis=-1, keepdims=True)
    yc = y - mu
    var = jnp.mean(yc * yc, axis=-1, keepdims=True)
    return yc * lax.rsqrt(var + LN_EPS) * g + b


def _lambda(lq1_ref, lk1_ref, lq2_ref, lk2_ref, lam_init):
    s1 = jnp.sum(lq1_ref[...] * lk1_ref[...], axis=-1, keepdims=True)
    s2 = jnp.sum(lq2_ref[...] * lk2_ref[...], axis=-1, keepdims=True)
    return jnp.exp(s1) - jnp.exp(s2) + lam_init


def _in_proj_kernel(x_ref, wk_ref, wv_ref, wz_ref, wx_ref, wdt_ref, wdtT_ref, wq_ref, wkT_ref, wvT_ref, *outs,
                    prompt):
    xb = x_ref[...].astype(bf16)
    z_ref, xbc_ref, dt_ref, dtT_ref = outs[:4]
    z_ref[...] = _dot(xb, wz_ref[...])
    xbc_ref[...] = _dot(xb, wx_ref[...])
    dt_ref[...] = _dot(xb, wdt_ref[...])
    dtT_ref[...] = _dot_nt(wdtT_ref[...], xb)
    scale = DQK ** -0.5
    if prompt:
        kT_ref, vT_ref, kb_ref, qT_ref, v1T_ref = outs[4:]
        kT = _dot_nt(wkT_ref[...], xb)
        kT_ref[0] = kT
        kb_ref[...] = kT.T.astype(bf16)
        qT_ref[...] = (_dot_nt(wq_ref[...], xb) * (scale * LOG2E)).astype(bf16)
        vT32 = _dot_nt(wvT_ref[...], xb)
        vT_ref[0] = vT32
        vT = vT32.astype(bf16)
        ones = jnp.ones((BF16_ROWS, vT.shape[1]), bf16)
        for h in range(H_A):
            v1T_ref[h * V_ROWS:h * V_ROWS + DV, :] = vT[h * DV:(h + 1) * DV, :]
            v1T_ref[h * V_ROWS + DV:(h + 1) * V_ROWS, :] = ones
    else:
        k_ref, v_ref, q_ref = outs[4:]
        k_ref[...] = _dot(xb, wk_ref[...])
        v_ref[...] = _dot(xb, wv_ref[...])
        q_ref[...] = (_dot(xb, wq_ref[...]) * scale).astype(bf16)


def _in_proj(x2d, w, prompt, B):
    T, D = x2d.shape
    tl = min(ROW_TILE, T)
    L = T // B
    n_l = L // tl
    grid = (T // tl,)
    row = lambda n: pl.BlockSpec((tl, n), lambda i: (i, 0))
    col = lambda n: pl.BlockSpec((n, tl), lambda i: (0, i))
    per_stream = lambda n: pl.BlockSpec((1, n, tl), lambda i: (i // n_l, 0, i % n_l))
    full = lambda a: pl.BlockSpec(a.shape, lambda i: (0,) * a.ndim)
    wq = w['wqT'] if prompt else w['wq']
    ins = [x2d, w['wk'], w['wv'], w['wz'], w['wx'], w['wdt'], w['wdtT'], wq, w['wkT'], w['wvT']]
    in_specs = [row(D)] + [full(a) for a in ins[1:]]
    out_shape = [jax.ShapeDtypeStruct((T, D_SSM), f32), jax.ShapeDtypeStruct((T, CONV_DIM), f32),
                 jax.ShapeDtypeStruct((T, LANES), f32), jax.ShapeDtypeStruct((BF16_ROWS, T), f32)]
    out_specs = [row(D_SSM), row(CONV_DIM), row(LANES), col(BF16_ROWS)]
    if prompt:
        assert L % tl == 0
        out_shape += [jax.ShapeDtypeStruct((B, D_QK_ALL, L), f32), jax.ShapeDtypeStruct((B, D_ATT, L), f32),
                      jax.ShapeDtypeStruct((T, D_QK_ALL), bf16), jax.ShapeDtypeStruct((D_QK_ALL, T), bf16),
                      jax.ShapeDtypeStruct((H_A * V_ROWS, T), bf16)]
        out_specs += [per_stream(D_QK_ALL), per_stream(D_ATT), row(D_QK_ALL), col(D_QK_ALL), col(H_A * V_ROWS)]
    else:
        out_shape += [jax.ShapeDtypeStruct((T, D_QK_ALL), f32), jax.ShapeDtypeStruct((T, D_ATT), f32),
                      jax.ShapeDtypeStruct((T, D_QK_ALL), bf16)]
        out_specs += [row(D_QK_ALL), row(D_ATT), row(D_QK_ALL)]
    return pl.pallas_call(
        functools.partial(_in_proj_kernel, prompt=prompt),
        grid=grid, in_specs=in_specs, out_specs=out_specs, out_shape=out_shape,
        compiler_params=_params(("parallel",)),
    )(*ins)


def _t5_bucket(rel):
    half = N_BUCKETS // 2
    max_exact = half // 2
    n = jnp.abs(rel)
    large = max_exact + (jnp.log(jnp.maximum(n, 1).astype(f32) / max_exact)
                         / math.log(MAX_DISTANCE / max_exact) * (half - max_exact)).astype(i32)
    large = jnp.minimum(large, half - 1)
    return jnp.where(rel > 0, half, 0) + jnp.where(n < max_exact, n, large)


def _far_bucket(min_dist):
    half = N_BUCKETS // 2
    max_exact = half // 2
    v = max_exact + int(math.log(min_dist / max_exact) / math.log(MAX_DISTANCE / max_exact) * (half - max_exact))
    return half - 1 if (min_dist >= max_exact and v - 1 >= half - 1) else None


def _bias_lookup(rel_bias, bucket):
    onehot = (bucket[..., None] == jnp.arange(N_BUCKETS, dtype=i32)).astype(f32)
    return jnp.dot(onehot, rel_bias.astype(f32), precision=lax.Precision.HIGHEST)


def _prompt_bias_tiles(rel_bias, ta):
    j = jnp.arange(ta, dtype=i32)[:, None]
    i = jnp.arange(ta, dtype=i32)[None, :]
    tiles = []
    for d in range(2):
        rel = j - i - d * ta
        b = _bias_lookup(rel_bias, _t5_bucket(rel))
        allowed = ((j // CHUNK) <= (i // CHUNK)) if d == 0 else jnp.ones((ta, ta), bool)
        tiles.append(jnp.where(allowed[..., None], b, NEG_INF))
    return jnp.transpose(jnp.stack(tiles), (3, 0, 1, 2))


def _sample_bias(rel_bias, past, s):
    q_pos = past + jnp.arange(s, dtype=i32)
    k_pos = jnp.arange(past + s, dtype=i32)
    rel = k_pos[None, :] - q_pos[:, None]
    b = _bias_lookup(rel_bias, _t5_bucket(rel))
    allowed = (k_pos[None, :] // CHUNK) <= (q_pos[:, None] // CHUNK)
    return jnp.transpose(jnp.where(allowed[..., None], b, NEG_INF), (2, 0, 1))


ATT_DIAG, ATT_SUBDIAG, ATT_FAR = 0, 1, 2
ATT_FAR_GROUPS = (4, 2, 1)
ATT_MAX_TILES = max(ATT_FAR_GROUPS)


def _attn_prompt_kernel(*refs, lam_init):
    nk = ATT_MAX_TILES
    k_tabs, kind_tab = refs[1:1 + nk], refs[1 + nk]
    qT_ref = refs[2 + nk]
    tile_refs = [(refs[3 + nk + 2 * j], refs[4 + nk + 2 * j]) for j in range(nk)]
    (bias_ref, cfar_ref, lq1_ref, lk1_ref, lq2_ref, lk2_ref, g_ref, o_ref,
     qm_sc, m_sc, acc_sc, oT_sc) = refs[3 + 3 * nk:]
    k_ref, v1T_ref = tile_refs[0]
    p = pl.program_id(1)
    kind = kind_tab[p]
    ta = k_ref.shape[0]
    n_hp = H_A // 2

    @pl.when(k_tabs[0][p] == 0)
    def _():
        m_sc[...] = jnp.full(m_sc.shape, NEG_INF, f32)
        acc_sc[...] = jnp.zeros(acc_sc.shape, f32)
        row = lax.broadcasted_iota(i32, (LANES, ta), 0)
        for hp in range(n_hp):
            qt = qT_ref[hp * LANES:(hp + 1) * LANES, :]
            for gi in range(4):
                keep = (row >= gi * DQK) & (row < (gi + 1) * DQK)
                qm_sc[hp, :, gi * ta:(gi + 1) * ta] = jnp.where(keep, qt, jnp.zeros_like(qt))

    def scores(keys_ref, hp):
        return _dot(keys_ref[:, hp * LANES:(hp + 1) * LANES], qm_sc[hp])

    nb = ta // ATT_SUB

    def near_bias(hp, step_d):
        rows = []
        for jb in range(nb):
            cols = []
            for g in range(4):
                for ib in range(nb):
                    dd = step_d * nb + ib - jb
                    if dd < 0:
                        blk = jnp.full((ATT_SUB, ATT_SUB), NEG_INF * LOG2E, f32)
                    elif dd < 2:
                        blk = bias_ref[2 * hp + g // 2, dd]
                    else:
                        lo = g * ta + ib * ATT_SUB
                        blk = jnp.broadcast_to(cfar_ref[hp][:, lo:lo + ATT_SUB], (ATT_SUB, ATT_SUB))
                    cols.append(blk)
            rows.append(jnp.concatenate(cols, axis=1))
        return jnp.concatenate(rows, axis=0)

    def update(step_d, tiles):
        chain = [(kr, vr, hp) for kr, vr in tiles for hp in range(n_hp)]
        s_next = scores(chain[0][0], chain[0][2])
        for link, (_, values_ref, hp) in enumerate(chain):
            s = s_next
            if link + 1 < len(chain):
                s_next = scores(chain[link + 1][0], chain[link + 1][2])
            m_old = m_sc[hp]
            if step_d is not None:
                s = s + near_bias(hp, step_d)
                m_new = jnp.maximum(m_old, jnp.max(s, axis=0, keepdims=True))
                shift = m_new
            else:
                c = cfar_ref[hp]
                m_new = jnp.maximum(m_old, jnp.max(s, axis=0, keepdims=True) + c)
                shift = m_new - c
            alpha = jnp.exp2(m_old - m_new)
            pT = jnp.exp2(s - shift).astype(bf16)
            for hh in range(2):
                h = 2 * hp + hh
                cols = slice(hh * 2 * ta, (hh + 1) * 2 * ta)
                pv = _dot(values_ref[h * V_ROWS:(h + 1) * V_ROWS, :], pT[:, cols])
                acc_sc[h] = alpha[:, cols] * acc_sc[h] + pv
            m_sc[hp] = m_new

    @pl.when(kind == ATT_DIAG)
    def _():
        update(0, [(k_ref, v1T_ref)])

    @pl.when(kind == ATT_SUBDIAG)
    def _():
        update(1, [(k_ref, v1T_ref)])

    for n_far in ATT_FAR_GROUPS:
        @pl.when(kind == ATT_FAR + n_far - 1)
        def _(n_far=n_far):
            update(None, tile_refs[:n_far])

    @pl.when(kind == ATT_DIAG)
    def _():
        lam = _lambda(lq1_ref, lk1_ref, lq2_ref, lk2_ref, lam_init)
        for h in range(H_A):
            a = acc_sc[h]
            a0 = a[:, :ta]
            a1 = a[:, ta:]
            o = a0[:DV] / a0[DV:DV + 1] - lam * (a1[:DV] / a1[DV:DV + 1])
            o = o * lax.rsqrt(jnp.mean(o * o, axis=0, keepdims=True) + RMS_EPS) * g_ref[...] * (1.0 - lam_init)
            oT_sc[h * DV:(h + 1) * DV, :] = o
        o_ref[...] = oT_sc[...].T.astype(bf16)


def _attn_prompt(qT, kb, v1T, rel_bias, lam_args, attn_g, B, L, lam_init):
    ta = min(ATT_TILE, L)
    assert L % ta == 0 and ta % ATT_SUB == 0 and ATT_SUB % CHUNK == 0
    nt = L // ta
    far = _far_bucket(ATT_SUB + 1)
    assert L <= 2 * ATT_SUB or far is not None, "key blocks two or more behind must share one bias bucket"
    far = far if far is not None else N_BUCKETS // 2 - 1
    steps = []
    slots = [0] * ATT_MAX_TILES
    for q in range(nt):
        first_far, n_far_left = 0, max(q - 1, 0)
        while n_far_left:
            n_far = next(g for g in ATT_FAR_GROUPS if g <= n_far_left)
            slots[:n_far] = range(first_far, first_far + n_far)
            steps.append((q, list(slots), ATT_FAR + n_far - 1))
            first_far, n_far_left = first_far + n_far, n_far_left - n_far
        for key_tile, kind in ((q - 1, ATT_SUBDIAG), (q, ATT_DIAG)):
            if key_tile >= 0:
                slots[0] = key_tile
                steps.append((q, list(slots), kind))
    qi_tab = jnp.asarray([s[0] for s in steps], i32)
    k_tabs = [jnp.asarray([s[1][j] for s in steps], i32) for j in range(ATT_MAX_TILES)]
    kind_tab = jnp.asarray([s[2] for s in steps], i32)
    bias = _prompt_bias_tiles(rel_bias, ATT_SUB) * LOG2E
    cfar = jnp.repeat((rel_bias.astype(f32)[far] * LOG2E).reshape(H_A // 2, 2), 2 * ta, axis=1)[:, None, :]
    g_col = attn_g.astype(f32).reshape(DV, 1)
    T = B * L
    n_hp = H_A // 2
    full = lambda a: pl.BlockSpec(a.shape, lambda b, p, *tabs: (0,) * a.ndim)
    tile_specs = []
    for j in range(ATT_MAX_TILES):
        tile_specs += [pl.BlockSpec((ta, D_QK_ALL), lambda b, p, *tabs, j=j: (b * nt + tabs[1 + j][p], 0)),
                       pl.BlockSpec((H_A * V_ROWS, ta), lambda b, p, *tabs, j=j: (0, b * nt + tabs[1 + j][p]))]
    grid_spec = pltpu.PrefetchScalarGridSpec(
        num_scalar_prefetch=2 + ATT_MAX_TILES, grid=(B, len(steps)),
        in_specs=[pl.BlockSpec((D_QK_ALL, ta), lambda b, p, *tabs: (0, b * nt + tabs[0][p]))] + tile_specs
                 + [full(bias), full(cfar)] + [full(a) for a in lam_args] + [full(g_col)],
        out_specs=pl.BlockSpec((ta, D_ATT), lambda b, p, *tabs: (b * nt + tabs[0][p], 0)),
        scratch_shapes=[pltpu.VMEM((n_hp, LANES, 4 * ta), bf16), pltpu.VMEM((n_hp, 1, 4 * ta), f32),
                        pltpu.VMEM((H_A, V_ROWS, 2 * ta), f32), pltpu.VMEM((D_ATT, ta), f32)])
    return pl.pallas_call(
        functools.partial(_attn_prompt_kernel, lam_init=lam_init),
        grid_spec=grid_spec, out_shape=jax.ShapeDtypeStruct((T, D_ATT), bf16),
        compiler_params=_params(("parallel", "arbitrary")),
    )(qi_tab, *k_tabs, kind_tab, qT, *([kb, v1T] * ATT_MAX_TILES), bias, cfar, *lam_args, g_col)


def _attn_sample_kernel(q_ref, kn_ref, vn_ref, kc_ref, vc_ref, bc_ref, bn_ref,
                        lq1_ref, lk1_ref, lq2_ref, lk2_ref, g2_ref, o_ref, *, lam_init):
    s_len = q_ref.shape[0]
    lane = lax.broadcasted_iota(i32, (s_len, LANES), 1)
    lam = _lambda(lq1_ref, lk1_ref, lq2_ref, lk2_ref, lam_init)
    for hp in range(H_A // 2):
        sl = slice(hp * LANES, (hp + 1) * LANES)
        qt = q_ref[:, sl]
        kct = kc_ref[0, sl, :].astype(bf16)
        vct = vc_ref[0, sl, :].astype(bf16)
        knt = kn_ref[:, sl].astype(bf16)
        vnt = vn_ref[:, sl].astype(bf16)
        outs = []
        for hh in range(2):
            h = 2 * hp + hh
            oc = []
            for c in range(2):
                gi = 2 * hh + c
                qm = jnp.where((lane >= gi * DQK) & (lane < (gi + 1) * DQK), qt, jnp.zeros_like(qt))
                s1 = _dot(qm, kct) + bc_ref[h]
                s2 = _dot_nt(qm, knt) + bn_ref[h]
                m = jnp.maximum(jnp.max(s1, axis=-1, keepdims=True), jnp.max(s2, axis=-1, keepdims=True))
                p1 = jnp.exp(s1 - m)
                p2 = jnp.exp(s2 - m)
                l = jnp.sum(p1, axis=-1, keepdims=True) + jnp.sum(p2, axis=-1, keepdims=True)
                pv = _dot_nt(p1.astype(bf16), vct) + _dot(p2.astype(bf16), vnt)
                oc.append(pv / l)
            o = oc[0] - lam * oc[1]
            in_head = (lane >= hh * DV) & (lane < (hh + 1) * DV)
            ms = jnp.sum(jnp.where(in_head, o * o, 0.0), axis=-1, keepdims=True) * (1.0 / DV)
            outs.append(o * lax.rsqrt(ms + RMS_EPS))
        ot = jnp.where(lane < DV, outs[0], outs[1]) * g2_ref[...] * (1.0 - lam_init)
        o_ref[:, sl] = ot.astype(bf16)


def _attn_sample(q, k_new, v_new, cache_k, cache_v, rel_bias, lam_args, attn_g, lam_init):
    nb, past = cache_k.shape[0], cache_k.shape[1]
    T = q.shape[0]
    s = T // nb
    kc = cache_k.transpose(0, 2, 3, 1).reshape(nb, D_QK_ALL, past)
    vc = cache_v.transpose(0, 2, 3, 1).reshape(nb, D_ATT, past)
    bias = _sample_bias(rel_bias, past, s)
    bc, bn = bias[:, :, :past], bias[:, :, past:]
    g2 = jnp.tile(attn_g.astype(f32), 2).reshape(1, LANES)
    full = lambda a: pl.BlockSpec(a.shape, lambda b: (0,) * a.ndim)
    row = lambda n: pl.BlockSpec((s, n), lambda b: (b, 0))
    return pl.pallas_call(
        functools.partial(_attn_sample_kernel, lam_init=lam_init),
        grid=(nb,),
        in_specs=[row(D_QK_ALL), row(D_QK_ALL), row(D_ATT),
                  pl.BlockSpec((1, D_QK_ALL, past), lambda b: (b, 0, 0)),
                  pl.BlockSpec((1, D_ATT, past), lambda b: (b, 0, 0)),
                  full(bc), full(bn)] + [full(a) for a in lam_args] + [full(g2)],
        out_specs=row(D_ATT), out_shape=jax.ShapeDtypeStruct((T, D_ATT), bf16),
        compiler_params=_params(("parallel",)),
    )(q, k_new, v_new, kc, vc, bc, bn, *lam_args, g2)


def _ssd_kernel(z_ref, xbc_ref, dt_ref, dtT_ref, h0_ref, c0_ref, cw_ref, cb_ref, dtb_ref, dtbT_ref,
                alog_ref, alogT_ref, dsk_ref, g_ref, y_ref, hout_ref, cout_ref, xpad_sc, h_sc):
    q = xbc_ref.shape[0]
    c = pl.program_id(1)
    gw = HEADS_PER_GROUP * SSM_HEADDIM

    @pl.when(c == 0)
    def _():
        xpad_sc[0:SUBLANES] = c0_ref[0]
        h_sc[...] = h0_ref[0]

    xpad_sc[SUBLANES:SUBLANES + q] = xbc_ref[...]
    first = SUBLANES - (CONV_W - 1)
    conv = cb_ref[...]
    for i in range(CONV_W):
        conv = conv + xpad_sc[first + i:first + i + q] * cw_ref[i:i + 1, :]
    tail = xpad_sc[q:q + SUBLANES]
    xpad_sc[0:SUBLANES] = tail
    cout_ref[0] = tail
    act = conv * _sigmoid(conv)
    xs = act[:, :D_SSM]

    lane = lax.broadcasted_iota(i32, (1, LANES), 1)
    a_row = jnp.where(lane < H_S, -jnp.exp(alog_ref[...]) * LOG2E, 0.0)
    dt = _softplus(dt_ref[...] + dtb_ref[...])
    ii = lax.broadcasted_iota(i32, (q, q), 0)
    jj = lax.broadcasted_iota(i32, (q, q), 1)
    causal = jj <= ii
    acs = _dot_f32_rhs(causal.astype(bf16), dt * a_row)
    sub = lax.broadcasted_iota(i32, (BF16_ROWS, 1), 0)
    a_col = jnp.where(sub < H_S, -jnp.exp(alogT_ref[...]) * LOG2E, 0.0)
    dtT = _softplus(dtT_ref[0] + dtbT_ref[...])
    acsT = _dot_f32_lhs(dtT * a_col, (ii <= jj).astype(bf16))

    er = lax.broadcasted_iota(i32, (LANES, D_SSM), 0)
    ec = lax.broadcasted_iota(i32, (LANES, D_SSM), 1)
    expand = (ec // SSM_HEADDIM == er).astype(bf16)
    dt_x = _dot_f32_lhs(dt, expand)
    acs_x = _dot_f32_lhs(acs, expand)
    e_acs = jnp.exp2(acs_x)
    acs_last = acs_x[q - 1:q, :]
    decay = jnp.exp2(acs_last - acs_x) * dt_x
    dtx = xs * dt_x
    xd = xs * decay
    glane = lax.broadcasted_iota(i32, (1, gw), 1)

    ys = []
    for g in range(SSM_GROUPS):
        gs = slice(g * gw, (g + 1) * gw)
        bg = act[:, D_SSM + g * D_STATE:D_SSM + (g + 1) * D_STATE].astype(bf16)
        cg = act[:, D_SSM + (SSM_GROUPS + g) * D_STATE:D_SSM + (SSM_GROUPS + g + 1) * D_STATE].astype(bf16)
        cb = _dot_nt(cg, bg)
        h_old = h_sc[g]
        yg = _dot(cg, h_old.astype(bf16)) * e_acs[:, gs]
        dtx_g = dtx[:, gs]
        for e4 in range(HEADS_PER_GROUP):
            e = g * HEADS_PER_GROUP + e4
            seg = acs[:, e:e + 1] - acsT[e:e + 1, :]
            lmat = jnp.exp2(jnp.where(causal, seg, -jnp.inf))
            rhs = jnp.where(glane // SSM_HEADDIM == e4, dtx_g, 0.0).astype(bf16)
            yg = yg + _dot((cb * lmat).astype(bf16), rhs)
        h_sc[g] = h_old * jnp.exp2(acs_last[:, gs]) + _dot_tn(bg, xd[:, gs].astype(bf16))
        ys.append(yg)
    y = jnp.concatenate(ys, axis=1) + dsk_ref[...] * xs
    zz = z_ref[...]
    y = y * (zz * _sigmoid(zz))
    for g in range(SSM_GROUPS):
        gs = slice(g * gw, (g + 1) * gw)
        yg = y[:, gs]
        r = lax.rsqrt(jnp.mean(yg * yg, axis=-1, keepdims=True) + RMS_EPS)
        y_ref[:, gs] = (yg * r * g_ref[:, gs]).astype(bf16)
    hout_ref[0] = h_sc[...]


def _ssd(z, xbc, dt, dtT, h0, conv0, pw, B, L):
    q = min(SSD_TILE, L)
    nc = L // q
    gw = HEADS_PER_GROUP * SSM_HEADDIM
    dtT3 = dtT.reshape(BF16_ROWS, B, L).transpose(1, 0, 2)
    hT0 = h0.astype(f32).reshape(B, SSM_GROUPS, gw, D_STATE).transpose(0, 1, 3, 2)
    c0 = jnp.pad(conv0.astype(f32), ((0, 0), (SUBLANES - (CONV_W - 1), 0), (0, 0)))
    full = lambda a: pl.BlockSpec(a.shape, lambda b, c: (0,) * a.ndim)
    row = lambda n: pl.BlockSpec((q, n), lambda b, c: (b * nc + c, 0))
    params = [pw['conv_w'], pw['conv_b'], pw['dt_bias'], pw['dt_biasT'], pw['a_log'], pw['a_logT'],
              pw['d_skip'], pw['ssm_g']]
    y, hT, ctail = pl.pallas_call(
        _ssd_kernel, grid=(B, nc),
        in_specs=[row(D_SSM), row(CONV_DIM), row(LANES),
                  pl.BlockSpec((1, BF16_ROWS, q), lambda b, c: (b, 0, c)),
                  pl.BlockSpec((1, SSM_GROUPS, D_STATE, gw), lambda b, c: (b, 0, 0, 0)),
                  pl.BlockSpec((1, SUBLANES, CONV_DIM), lambda b, c: (b, 0, 0))] + [full(a) for a in params],
        out_specs=[row(D_SSM),
                   pl.BlockSpec((1, SSM_GROUPS, D_STATE, gw), lambda b, c: (b, 0, 0, 0)),
                   pl.BlockSpec((1, SUBLANES, CONV_DIM), lambda b, c: (b, 0, 0))],
        out_shape=[jax.ShapeDtypeStruct((B * L, D_SSM), bf16),
                   jax.ShapeDtypeStruct((B, SSM_GROUPS, D_STATE, gw), f32),
                   jax.ShapeDtypeStruct((B, SUBLANES, CONV_DIM), f32)],
        scratch_shapes=[pltpu.VMEM((q + SUBLANES, CONV_DIM), f32), pltpu.VMEM((SSM_GROUPS, D_STATE, gw), f32)],
        compiler_params=_params(("parallel", "arbitrary")),
    )(z, xbc, dt, dtT3, hT0, c0, *params)
    h_new = hT.transpose(0, 1, 3, 2).reshape(B, H_S, SSM_HEADDIM, D_STATE)
    return y, h_new, ctail[:, SUBLANES - (CONV_W - 1):, :]


def _pack_bf16_pairs(x):
    half = x.shape[1] // 2
    bits = lambda a: lax.bitcast_convert_type(a.astype(bf16).astype(f32), jnp.uint32)
    return (bits(x[:, half:]) & jnp.uint32(0xFFFF0000)) | (bits(x[:, :half]) >> 16)


def _unpack_bf16_pairs(word):
    lo = lax.bitcast_convert_type(word << 16, f32)
    hi = lax.bitcast_convert_type(word & jnp.uint32(0xFFFF0000), f32)
    return jnp.concatenate([lo, hi], axis=1)


def _mix_router_kernel(att_ref, ssm_ref, x_ref, woa_ref, wos_ref, g_ref, b_ref, wrh_ref, wrhl_ref, br_ref,
                       cnt0_ref, *rest):
    x1_ref, x1p_ref, te_ref, gate_ref, cnt_ref, run_sc, earlier_sc = rest[-7:]

    @pl.when(pl.program_id(0) == 0)
    def _():
        run_sc[...] = cnt0_ref[...]
        ii = lax.broadcasted_iota(i32, earlier_sc.shape, 0)
        jj = lax.broadcasted_iota(i32, earlier_sc.shape, 1)
        earlier_sc[...] = (jj < ii).astype(bf16)

    mix = _dot(att_ref[...], woa_ref[...]) + _dot(ssm_ref[...], wos_ref[...])
    x1 = _layer_norm(DEEPNORM_ALPHA * x_ref[...] + mix, g_ref[...], b_ref[...])
    x1_ref[...] = x1
    x1p_ref[...] = _pack_bf16_pairs(x1)
    hi = x1.astype(bf16)
    lo = (x1 - hi.astype(f32)).astype(bf16)
    both = _dot(hi, wrhl_ref[...])
    logits = both[:, :LANES] + both[:, LANES:] + _dot(lo, wrh_ref[...]) + br_ref[...]
    lane =lax.broadcasted_iota(i32, logits.shape, 1)
    vals, idxs = [], []
    for _ in range(TOP_K):
        m = jnp.max(logits, axis=-1, keepdims=True)
        idx = jnp.min(jnp.where(logits == m, lane, LANES), axis=-1, keepdims=True)
        vals.append(m)
        idxs.append(idx)
        logits = jnp.where(lane == idx, -jnp.inf, logits)
    es = [jnp.exp(v - vals[0]) for v in vals]
    tot = es[0]
    for e in es[1:]:
        tot = tot + e
    chosen = jnp.zeros(logits.shape, f32)
    for k in range(TOP_K):
        chosen = chosen + (lane == idxs[k]).astype(f32)
    before = _dot(earlier_sc[...], chosen.astype(bf16)) + run_sc[...]
    te = jnp.zeros(logits.shape, i32)
    gate = jnp.zeros(logits.shape, f32)
    for k in range(TOP_K):
        rank = jnp.sum(jnp.where(lane == idxs[k], before, 0.0), axis=-1, keepdims=True).astype(i32)
        te = jnp.where(lane == k, idxs[k], te)
        te = jnp.where(lane == TOP_K + k, rank, te)
        gate = jnp.where(lane == k, es[k] / tot, gate)
    te_ref[...] = te
    gate_ref[...] = gate
    run_sc[...] = run_sc[...] + jnp.sum(chosen, axis=0, keepdims=True)
    cnt_ref[...] = run_sc[...]


def _mix_router(att, ssm, x2d, w, counts0, packed, row0, rows_total):
    T, D = x2d.shape
    tl = min(ROW_TILE, T)
    assert row0 % tl == 0
    row = lambda n: pl.BlockSpec((tl, n), lambda i: (i, 0))
    full = lambda a: pl.BlockSpec(a.shape, lambda i: (0,) * a.ndim)
    ws = [w['wo_att'], w['wo_ssm'], w['ln1_g'], w['ln1_b'], w['wr_hi'], w['wr_hi_lo'], w['b_router'], counts0]
    ins = [att, ssm, x2d] + ws + [packed]
    in_specs = [row(D_ATT), row(D_SSM), row(D)] + [full(a) for a in ws] + [pl.BlockSpec(memory_space=pl.ANY)]
    aliases = {len(ins) - 1: 1}
    return pl.pallas_call(
        _mix_router_kernel, grid=(T // tl,), in_specs=in_specs,
        out_specs=[row(D), pl.BlockSpec((tl, D // 2), lambda i: (row0 // tl + i, 0)), row(LANES), row(LANES),
                   pl.BlockSpec((1, LANES), lambda i: (0, 0))],
        out_shape=[jax.ShapeDtypeStruct((T, D), f32), jax.ShapeDtypeStruct((rows_total, D // 2), jnp.uint32),
                   jax.ShapeDtypeStruct((T, LANES), i32),
                   jax.ShapeDtypeStruct((T, LANES), f32), jax.ShapeDtypeStruct((1, LANES), f32)],
        scratch_shapes=[pltpu.VMEM((1, LANES), f32), pltpu.VMEM((tl, tl), bf16)], input_output_aliases=aliases,
        compiler_params=_params(("arbitrary",)),
    )(*ins)


def _route(te, counts_f):
    T = te.shape[0]
    n_assign = T * TOP_K
    counts = counts_f[0, :N_EXPERTS].astype(i32)
    padded = (counts + MOE_BLOCK - 1) // MOE_BLOCK * MOE_BLOCK
    pad_end = jnp.cumsum(padded)
    pad_start = pad_end - padded
    top_e, rank = te[:, :TOP_K], te[:, TOP_K:2 * TOP_K]
    experts = jnp.arange(N_EXPERTS, dtype=i32)
    slot = rank + jnp.sum(jnp.where(top_e[..., None] == experts, pad_start, 0), axis=-1)
    n_blocks = -(-n_assign // MOE_BLOCK) + N_EXPERTS
    n_blocks += n_blocks % FFN_BLOCKS_PER_STEP
    block_start = jnp.arange(n_blocks, dtype=i32) * MOE_BLOCK
    block_e = jnp.minimum(jnp.sum(block_start[:, None] >= pad_end[None, :], axis=-1), N_EXPERTS - 1).astype(i32)
    n_used = (pad_end[-1] // MOE_BLOCK).astype(i32).reshape(1)
    row_end = jnp.sum(jnp.where(block_e[:, None] == experts, pad_start + counts, 0), axis=-1)
    rows_valid = jnp.clip(row_end - block_start, 0, MOE_BLOCK).astype(i32)
    return block_e, n_used, rows_valid, slot.astype(i32), n_blocks


SC_INDEX_WINDOW = 128
SC_ROWS = 64


def _sc_mesh():
    return plsc.VectorSubcoreMesh(core_axis_name="c", subcore_axis_name="s")


def _sc_move_rows(src, src_idx, dst_idx_list, n_out):
    M = src_idx.shape[0]
    D = src.shape[1]
    idx = [a.reshape(1, M) for a in [src_idx] + list(dst_idx_list)]

    @functools.partial(pl.kernel, out_type=jax.ShapeDtypeStruct((n_out, D), src.dtype), mesh=_sc_mesh(),
                       scratch_types=[pltpu.VMEM((2, SC_ROWS, D), src.dtype), pltpu.SemaphoreType.DMA((2,))])
    def move(s_hbm, *rest):
        i_hbm, o_hbm, buf, sem = rest[:-3], rest[-3], rest[-2], rest[-1]
        n_parts = SC_INDEX_WINDOW // SC_ROWS

        def body(si_vmem, *di_vmem):
            def fetch(j):
                part = pl.ds(j * SC_ROWS, SC_ROWS)
                return pltpu.async_copy(s_hbm.at[si_vmem.at[0, part]], buf.at[j % 2], sem.at[j % 2])

            pending = fetch(0)
            for j in range(n_parts):
                nxt = fetch(j + 1) if j + 1 < n_parts else None
                pending.wait()
                part = pl.ds(j * SC_ROWS, SC_ROWS)
                for dv in di_vmem:
                    pltpu.sync_copy(buf.at[j % 2], o_hbm.at[dv.at[0, part]])
                pending = nxt

        pltpu.emit_pipeline(
            body, grid=(M // SC_INDEX_WINDOW,),
            in_specs=[pl.BlockSpec((1, SC_INDEX_WINDOW), lambda i: (0, i))] * len(idx),
            out_specs=[], core_axis_name=("c", "s"), dimension_semantics=(pltpu.PARALLEL,),
        )(*i_hbm)

    return move(src, *idx)


def _sc_scatter_rows(x, idx_k, n_rows):
    return _sc_move_rows(x, jnp.arange(x.shape[0], dtype=i32), idx_k, n_rows)


def _sc_gather_rows(src, idx):
    M = idx.shape[0]
    return _sc_move_rows(src, idx, [jnp.arange(M, dtype=i32)], M)


def _deinterleave_kernel(w_ref, g_ref, u_ref):
    tn = 2 * LANES
    r = lax.broadcasted_iota(i32, (2 * tn, tn), 0)
    c = lax.broadcasted_iota(i32, (2 * tn, tn), 1)
    pick_g = (r == 2 * c).astype(bf16)
    pick_u = (r == 2 * c + 1).astype(bf16)
    for j in range(g_ref.shape[-1] // tn):
        wb = w_ref[0, :, j * 2 * tn:(j + 1) * 2 * tn].astype(bf16)
        g_ref[0, :, j * tn:(j + 1) * tn] = _dot(wb, pick_g).astype(bf16)
        u_ref[0, :, j * tn:(j + 1) * tn] = _dot(wb, pick_u).astype(bf16)


def _deinterleave(w_gu):
    E, D, F2 = w_gu.shape
    out = jax.ShapeDtypeStruct((E, D, F2 // 2), bf16)
    return pl.pallas_call(
        _deinterleave_kernel, grid=(E,),
        in_specs=[pl.BlockSpec((1, D, F2), lambda e: (e, 0, 0))],
        out_specs=[pl.BlockSpec((1, D, F2 // 2), lambda e: (e, 0, 0))] * 2,
        out_shape=[out, out],
        compiler_params=_params(("parallel",)),
    )(w_gu)


def _ffn_kernel(be_ref, nu_ref, rv_ref, xs_ref, *refs):
    del be_ref
    y_ref = refs[-1]
    wsets = [refs[6 * j:6 * j + 6] for j in range(FFN_BLOCKS_PER_STEP)]
    first = pl.program_id(0) * FFN_BLOCKS_PER_STEP
    n_live = jnp.clip(nu_ref[0] - first, 0, FFN_BLOCKS_PER_STEP)
    row_id = lax.broadcasted_iota(i32, (MOE_BLOCK, 1), 0)

    def block(j):
        wg_ref, wu_ref, bg_ref, bu_ref, wd_ref, bd_ref = wsets[j]
        rows = pl.ds(j * MOE_BLOCK, MOE_BLOCK)
        words = jnp.where(row_id < rv_ref[first + j], xs_ref[rows, :], jnp.uint32(0))
        xb = _unpack_bf16_pairs(words).astype(bf16)
        g = _dot(xb, wg_ref[0]) + bg_ref[0]
        u = _dot(xb, wu_ref[0]) + bu_ref[0]
        g = jnp.minimum(g, SWIGLU_LIMIT)
        u = jnp.clip(u, -SWIGLU_LIMIT, SWIGLU_LIMIT)
        act = (u + 1.0) * g * _sigmoid(SWIGLU_ALPHA * g)
        y_ref[rows, :] = _pack_bf16_pairs(_dot(act.astype(bf16), wd_ref[0].astype(bf16)) + bd_ref[0])

    for live in range(FFN_BLOCKS_PER_STEP + 1):
        @pl.when(n_live == live)
        def _(live=live):
            for j in range(live):
                block(j)
            for j in range(live, FFN_BLOCKS_PER_STEP):
                y_ref[pl.ds(j * MOE_BLOCK, MOE_BLOCK), :] = jnp.zeros((MOE_BLOCK, y_ref.shape[1]), jnp.uint32)


def _expert_ffn(xs, block_e, n_used, rows_valid, w):
    n_rows, half = xs.shape
    D = 2 * half
    nps = FFN_BLOCKS_PER_STEP
    n_steps = n_rows // (MOE_BLOCK * nps)
    F = w['w_g'].shape[2]
    wspecs, wargs = [], []
    for j in range(nps):
        pick = lambda i, be, nu, rv, j=j: (be[i * nps + j], 0, 0)
        wspecs += [pl.BlockSpec((1, D, F), pick), pl.BlockSpec((1, D, F), pick), pl.BlockSpec((1, 1, F), pick),
                   pl.BlockSpec((1, 1, F), pick), pl.BlockSpec((1, F, D), pick), pl.BlockSpec((1, 1, D), pick)]
        wargs += [w['w_g'], w['w_u'], w['b_g'], w['b_u'], w['w_d'], w['b_d']]
    grid_spec = pltpu.PrefetchScalarGridSpec(
        num_scalar_prefetch=3, grid=(n_steps,),
        in_specs=[pl.BlockSpec((nps * MOE_BLOCK, half),
                               lambda i, be, nu, rv: (jnp.minimum(i, (nu[0] - 1) // nps), 0))] + wspecs,
        out_specs=pl.BlockSpec((nps * MOE_BLOCK, half), lambda i, be, nu, rv: (i, 0)))
    return pl.pallas_call(
        _ffn_kernel, grid_spec=grid_spec,
        out_shape=jax.ShapeDtypeStruct((n_rows, half), jnp.uint32),
        compiler_params=_params(("arbitrary",)),
    )(block_e, n_used, rows_valid, xs, *wargs)


def _combine_dense_kernel(rows_ref, x1_ref, gate_ref, g_ref, b_ref, *rest):
    y_ref = rest[-1]
    gate = gate_ref[...]
    ff = gate[:, 0:1] * _unpack_bf16_pairs(rows_ref[0])
    for k in range(1, TOP_K):
        ff = ff + gate[:, k:k + 1] * _unpack_bf16_pairs(rows_ref[k])
    y_ref[...] = _layer_norm(DEEPNORM_ALPHA * x1_ref[...] + ff, g_ref[...], b_ref[...])


def _combine_dense(rows, x1, gate, ln_g, ln_b, chunk, y_partial):
    T, D = x1.shape
    tc = rows.shape[1]
    tl = min(ROW_TILE, tc)
    first = chunk * (tc // tl)
    row = lambda n: pl.BlockSpec((tl, n), lambda i: (first + i, 0))
    full = lambda a: pl.BlockSpec(a.shape, lambda i: (0,) * a.ndim)
    ins = [rows, x1, gate, ln_g, ln_b]
    in_specs = [pl.BlockSpec((TOP_K, tl, D // 2), lambda i: (0, i, 0)), row(D), row(LANES), full(ln_g), full(ln_b)]
    aliases = {}
    if y_partial is not None:
        ins.append(y_partial)
        in_specs.append(pl.BlockSpec(memory_space=pl.ANY))
        aliases = {len(ins) - 1: 0}
    return pl.pallas_call(
        _combine_dense_kernel, grid=(tc // tl,), in_specs=in_specs,
        out_specs=row(D), out_shape=jax.ShapeDtypeStruct((T, D), f32), input_output_aliases=aliases,
        compiler_params=_params(("parallel",)),
    )(*ins)


def _prep_weights(l, w_in, conv_w, conv_b, dt_bias, a_log, d_skip, ssm_norm_g, w_o, ln1_g, ln1_b,
                  w_router, b_router, w_gate_up, b_gate_up, w_down, b_down, ln2_g, ln2_b):
    wi = w_in[l]
    c0, c1, c2, c3 = D_QK_ALL, 2 * D_QK_ALL, 2 * D_QK_ALL + D_ATT, 2 * D_QK_ALL + D_ATT + D_SSM
    c4 = c3 + CONV_DIM
    wdt = wi[:, c4:c4 + H_S]
    pad_lane = lambda v, fill=0.0: jnp.pad(v.astype(f32).reshape(1, -1), ((0, 0), (0, LANES - v.shape[-1])),
                                            constant_values=fill)
    pad_col = lambda v: jnp.pad(v.astype(f32).reshape(-1, 1), ((0, BF16_ROWS - v.shape[-1]), (0, 0)))
    wr = jnp.pad(w_router[l].astype(f32), ((0, 0), (0, LANES - N_EXPERTS)))
    wr_hi = wr.astype(bf16)
    wgu = w_gate_up[l]
    return {
        'wq': wi[:, :c0].astype(bf16), 'wqT': wi[:, :c0].T.astype(bf16),
        'wk': wi[:, c0:c1].astype(bf16), 'wkT': wi[:, c0:c1].T.astype(bf16), 'wv': wi[:, c1:c2].astype(bf16), 'wvT': wi[:, c1:c2].T.astype(bf16),
        'wz': wi[:, c2:c3].astype(bf16), 'wx': wi[:, c3:c4].astype(bf16),
        'wdt': jnp.pad(wdt, ((0, 0), (0, LANES - H_S))).astype(bf16),
        'wdtT': jnp.pad(wdt.T, ((0, BF16_ROWS - H_S), (0, 0))).astype(bf16),
        'conv_w': conv_w[l].astype(f32), 'conv_b': conv_b[l].astype(f32).reshape(1, -1),
        'dt_bias': pad_lane(dt_bias[l]), 'dt_biasT': pad_col(dt_bias[l]),
        'a_log': pad_lane(a_log[l]), 'a_logT': pad_col(a_log[l]),
        'd_skip': jnp.repeat(d_skip[l].astype(f32), SSM_HEADDIM).reshape(1, -1),
        'ssm_g': ssm_norm_g[l].astype(f32).reshape(1, -1),
        'wo_att': w_o[l][:D_ATT].astype(bf16), 'wo_ssm': w_o[l][D_ATT:].astype(bf16),
        'ln1_g': ln1_g[l].astype(f32).reshape(1, -1), 'ln1_b': ln1_b[l].astype(f32).reshape(1, -1),
        'wr_hi': wr_hi, 'wr_hi_lo': jnp.concatenate([wr_hi, (wr - wr_hi.astype(f32)).astype(bf16)], axis=1),
        'b_router': pad_lane(b_router[l], NEG_INF),
        'w_gu': wgu,
        'b_g': b_gate_up[l][:, None, 0::2].astype(f32), 'b_u': b_gate_up[l][:, None, 1::2].astype(f32),
        'w_d': w_down[l], 'b_d': b_down[l][:, None, :].astype(f32),
        'ln2_g': ln2_g[l].astype(f32).reshape(1, -1), 'ln2_b': ln2_b[l].astype(f32).reshape(1, -1),
    }


def _mixers(x, l, w, rel_bias, lam_args, attn_g, k_past, v_past, h0, conv0):
    B, L, D = x.shape
    x2d = x.reshape(B * L, D)
    lam_init = 0.8 - 0.6 * math.exp(-0.3 * l)
    prompt = k_past is None
    proj = _in_proj(x2d, w, prompt, B)
    z, xbc, dt, dtT = proj[:4]
    if prompt:
        kT, vT, kb, qT, v1T = proj[4:]
        att = _attn_prompt(qT, kb, v1T, rel_bias, lam_args, attn_g, B, L, lam_init)
        k_rows = kT.reshape(B, H_A, 2 * DQK, L).transpose(0, 3, 1, 2)
        v_rows = vT.reshape(B, H_A, DV, L).transpose(0, 3, 1, 2)
    else:
        k, v, q = proj[4:]
        att = _attn_sample(q, k, v, k_past, v_past, rel_bias, lam_args, attn_g, lam_init)
        k_rows = k.reshape(B, L, H_A, 2 * DQK)
        v_rows = v.reshape(B, L, H_A, DV)
    ssm, h_new, conv_new = _ssd(z, xbc, dt, dtT, h0, conv0, w, B, L)
    return (att, ssm, x2d), (k_rows, v_rows, h_new, conv_new)


def _moe_layer(streams, w):
    rows_total = sum(x2d.shape[0] for _, _, x2d in streams)
    counts = jnp.zeros((1, LANES), f32)
    packed = jnp.zeros((rows_total, streams[0][2].shape[1] // 2), jnp.uint32)
    row0, routed = 0, []
    for att, ssm, x2d in streams:
        x1, packed, te, gate, counts = _mix_router(att, ssm, x2d, w, counts, packed, row0, rows_total)
        routed.append((x1, gate, row0))
        row0 += x2d.shape[0]
        te_all = te[:, :2 * TOP_K] if len(routed) == 1 else jnp.concatenate([te_all, te[:, :2 * TOP_K]])
    block_e, n_used, rows_valid, slot, n_blocks = _route(te_all, counts)
    slot_k = slot.T
    xs = _sc_scatter_rows(packed, [slot_k[k] for k in range(TOP_K)], n_blocks * MOE_BLOCK)
    ys = _expert_ffn(xs, block_e, n_used, rows_valid, w)
    outs = []
    for x1, gate, r0 in routed:
        T, D = x1.shape
        n_groups = COMBINE_GROUPS if T % (COMBINE_GROUPS * ROW_TILE) == 0 else 1
        tc = T // n_groups
        y = None
        for c in range(n_groups):
            idx = slot_k[:, r0 + c * tc:r0 + (c + 1) * tc].reshape(-1)
            rows = _sc_gather_rows(ys, idx).reshape(TOP_K, tc, D // 2)
            y = _combine_dense(rows, x1, gate, w['ln2_g'], w['ln2_b'], c, y)
        outs.append(y)
    return outs


def kernel(x_prompt, x_sample, cache_k, cache_v, state_ssm, state_conv, rel_bias, w_in, lambda_q1, lambda_k1, lambda_q2, lambda_k2, attn_norm_g, conv_w, conv_b, dt_bias, a_log, d_skip, ssm_norm_g, w_o, ln1_g, ln1_b, w_router, b_router, w_gate_up, b_gate_up, w_down, b_down, ln2_g, ln2_b):
    yp, ys = x_prompt, x_sample
    bp = x_prompt.shape[0]
    depth = w_in.shape[0]
    outs = [[] for _ in range(8)]
    for l in range(depth):
        w = _prep_weights(l, w_in, conv_w, conv_b, dt_bias, a_log, d_skip, ssm_norm_g, w_o, ln1_g, ln1_b,
                          w_router, b_router, w_gate_up, b_gate_up, w_down, b_down, ln2_g, ln2_b)
        w['w_g'], w['w_u'] = _deinterleave(w.pop('w_gu'))
        lam_args = [a[l].astype(f32).reshape(1, -1) for a in (lambda_q1, lambda_k1, lambda_q2, lambda_k2)]
        h0 = jnp.zeros((bp, H_S, SSM_HEADDIM, D_STATE), f32)
        c0 = jnp.zeros((bp, CONV_W - 1, CONV_DIM), f32)
        mix_p, state_p = _mixers(yp, l, w, rel_bias, lam_args, attn_norm_g[l], None, None, h0, c0)
        mix_s, state_s = _mixers(ys, l, w, rel_bias, lam_args, attn_norm_g[l], cache_k[l], cache_v[l],
                                 state_ssm[l], state_conv[l])
        y2p, y2s = _moe_layer([mix_p, mix_s], w)
        yp, ys = y2p.reshape(yp.shape), y2s.reshape(ys.shape)
        for lst, a in zip(outs, state_p + state_s):
            lst.append(a)
    return (yp, ys) + tuple(jnp.stack(o) for o in outs)
```

```python
import functools
import math

import jax
import jax.numpy as jnp
from jax import lax
from jax.experimental import pallas as pl
from jax.experimental.pallas import tpu as pltpu
from jax.experimental.pallas import tpu_sc as plsc

f32 = jnp.float32
bf16 = jnp.bfloat16
i32 = jnp.int32

CHUNK = 64
H_A = 8
DQK = 32
DV = 2 * DQK
D_ATT = H_A * DV
SSM_HEADDIM = 64
H_S = 8
D_SSM = H_S * SSM_HEADDIM
SSM_GROUPS = 2
HEADS_PER_GROUP = H_S // SSM_GROUPS
D_STATE = 128
CONV_W = 4
CONV_DIM = D_SSM + 2 * SSM_GROUPS * D_STATE
D_QK_ALL = H_A * 2 * DQK
N_BUCKETS = 32
MAX_DISTANCE = 128
N_EXPERTS = 32
TOP_K = 4
SWIGLU_LIMIT = 7.0
SWIGLU_ALPHA = 1.702
MOE_BLOCK = 256
LN_EPS = 1e-5
RMS_EPS = 1e-5
NEG_INF = -1e30
DEPTH = 1
DEEPNORM_ALPHA = (2.0 * DEPTH) ** 0.25
LOG2E = math.log2(math.e)

LANES = 128
SUBLANES = 8
BF16_ROWS = 16
VMEM_LIMIT = 48 * 1024 * 1024

ROW_TILE = 512
ATT_TILE = 512
ATT_SUB = 256
ATT_HEAD_GROUPS = 2
SSD_TILE = 256
V_ROWS = DV + BF16_ROWS
FFN_BLOCKS_PER_STEP = 2
COMBINE_GROUPS = 4


def _params(semantics):
    return pltpu.CompilerParams(dimension_semantics=semantics, vmem_limit_bytes=VMEM_LIMIT)


def _dot(a, b):
    return jnp.dot(a, b, preferred_element_type=f32)


def _dot_nt(a, b):
    return lax.dot_general(a, b, (((1,), (1,)), ((), ())), preferred_element_type=f32)


def _dot_tn(a, b):
    return lax.dot_general(a, b, (((0,), (0,)), ((), ())), preferred_element_type=f32)


def _split3(a):
    hi = a.astype(bf16)
    r1 = a - hi.astype(f32)
    mid = r1.astype(bf16)
    lo = (r1 - mid.astype(f32)).astype(bf16)
    return hi, mid, lo


def _dot_f32_lhs(a, b_exact):
    hi, mid, lo = _split3(a)
    return _dot(hi, b_exact) + _dot(mid, b_exact) + _dot(lo, b_exact)


def _dot_f32_rhs(a_exact, b):
    hi, mid, lo = _split3(b)
    return _dot(a_exact, hi) + _dot(a_exact, mid) + _dot(a_exact, lo)


def _softplus(x):
    return jnp.maximum(x, 0.0) + jnp.log1p(jnp.exp(-jnp.abs(x)))


def _sigmoid(x):
    return 1.0 / (1.0 + jnp.exp(-x))


def _layer_norm(y, g, b):
    mu = jnp.mean(y, axis=-1, keepdims=True)
    yc = y - mu
    var = jnp.mean(yc * yc, axis=-1, keepdims=True)
    return yc * lax.rsqrt(var + LN_EPS) * g + b


def _lambda(lq1_ref, lk1_ref, lq2_ref, lk2_ref, lam_init):
    s1 = jnp.sum(lq1_ref[...] * lk1_ref[...], axis=-1, keepdims=True)
    s2 = jnp.sum(lq2_ref[...] * lk2_ref[...], axis=-1, keepdims=True)
    return jnp.exp(s1) - jnp.exp(s2) + lam_init


def _in_proj_kernel(x_ref, wk_ref, wv_ref, wz_ref, wx_ref, wdt_ref, wdtT_ref, wq_ref, wkT_ref, wvT_ref, *outs,
                    prompt):
    xb = x_ref[...].astype(bf16)
    z_ref, xbc_ref, dt_ref, dtT_ref = outs[:4]
    z_ref[...] = _dot(xb, wz_ref[...])
    xbc_ref[...] = _dot(xb, wx_ref[...])
    dt_ref[...] = _dot(xb, wdt_ref[...])
    dtT_ref[...] = _dot_nt(wdtT_ref[...], xb)
    scale = DQK ** -0.5
    if prompt:
        kT_ref, vT_ref, kb_ref, qT_ref, v1T_ref = outs[4:]
        kT = _dot_nt(wkT_ref[...], xb)
        kT_ref[0] = kT
        kb_ref[...] = kT.T.astype(bf16)
        qT_ref[...] = (_dot_nt(wq_ref[...], xb) * (scale * LOG2E)).astype(bf16)
        vT32 = _dot_nt(wvT_ref[...], xb)
        vT_ref[0] = vT32
        vT = vT32.astype(bf16)
        ones = jnp.ones((BF16_ROWS, vT.shape[1]), bf16)
        for h in range(H_A):
            v1T_ref[h * V_ROWS:h * V_ROWS + DV, :] = vT[h * DV:(h + 1) * DV, :]
            v1T_ref[h * V_ROWS + DV:(h + 1) * V_ROWS, :] = ones
    else:
        k_ref, v_ref, q_ref = outs[4:]
        k_ref[...] = _dot(xb, wk_ref[...])
        v_ref[...] = _dot(xb, wv_ref[...])
        q_ref[...] = (_dot(xb, wq_ref[...]) * scale).astype(bf16)


def _in_proj(x2d, w, prompt, B):
    T, D = x2d.shape
    tl = min(ROW_TILE, T)
    L = T // B
    n_l = L // tl
    grid = (T // tl,)
    row = lambda n: pl.BlockSpec((tl, n), lambda i: (i, 0))
    col = lambda n: pl.BlockSpec((n, tl), lambda i: (0, i))
    per_stream = lambda n: pl.BlockSpec((1, n, tl), lambda i: (i // n_l, 0, i % n_l))
    full = lambda a: pl.BlockSpec(a.shape, lambda i: (0,) * a.ndim)
    wq = w['wqT'] if prompt else w['wq']
    ins = [x2d, w['wk'], w['wv'], w['wz'], w['wx'], w['wdt'], w['wdtT'], wq, w['wkT'], w['wvT']]
    in_specs = [row(D)] + [full(a) for a in ins[1:]]
    out_shape = [jax.ShapeDtypeStruct((T, D_SSM), f32), jax.ShapeDtypeStruct((T, CONV_DIM), f32),
                 jax.ShapeDtypeStruct((T, LANES), f32), jax.ShapeDtypeStruct((BF16_ROWS, T), f32)]
    out_specs = [row(D_SSM), row(CONV_DIM), row(LANES), col(BF16_ROWS)]
    if prompt:
        assert L % tl == 0
        out_shape += [jax.ShapeDtypeStruct((B, D_QK_ALL, L), f32), jax.ShapeDtypeStruct((B, D_ATT, L), f32),
                      jax.ShapeDtypeStruct((T, D_QK_ALL), bf16), jax.ShapeDtypeStruct((D_QK_ALL, T), bf16),
                      jax.ShapeDtypeStruct((H_A * V_ROWS, T), bf16)]
        out_specs += [per_stream(D_QK_ALL), per_stream(D_ATT), row(D_QK_ALL), col(D_QK_ALL), col(H_A * V_ROWS)]
    else:
        out_shape += [jax.ShapeDtypeStruct((T, D_QK_ALL), f32), jax.ShapeDtypeStruct((T, D_ATT), f32),
                      jax.ShapeDtypeStruct((T, D_QK_ALL), bf16)]
        out_specs += [row(D_QK_ALL), row(D_ATT), row(D_QK_ALL)]
    return pl.pallas_call(
        functools.partial(_in_proj_kernel, prompt=prompt),
        grid=grid, in_specs=in_specs, out_specs=out_specs, out_shape=out_shape,
        compiler_params=_params(("parallel",)),
    )(*ins)


def _t5_bucket(rel):
    half = N_BUCKETS // 2
    max_exact = half // 2
    n = jnp.abs(rel)
    large = max_exact + (jnp.log(jnp.maximum(n, 1).astype(f32) / max_exact)
                         / math.log(MAX_DISTANCE / max_exact) * (half - max_exact)).astype(i32)
    large = jnp.minimum(large, half - 1)
    return jnp.where(rel > 0, half, 0) + jnp.where(n < max_exact, n, large)


def _far_bucket(min_dist):
    half = N_BUCKETS // 2
    max_exact = half // 2
    v = max_exact + int(math.log(min_dist / max_exact) / math.log(MAX_DISTANCE / max_exact) * (half - max_exact))
    return half - 1 if (min_dist >= max_exact and v - 1 >= half - 1) else None


def _bias_lookup(rel_bias, bucket):
    onehot = (bucket[..., None] == jnp.arange(N_BUCKETS, dtype=i32)).astype(f32)
    return jnp.dot(onehot, rel_bias.astype(f32), precision=lax.Precision.HIGHEST)


def _prompt_bias_tiles(rel_bias, ta):
    j = jnp.arange(ta, dtype=i32)[:, None]
    i = jnp.arange(ta, dtype=i32)[None, :]
    tiles = []
    for d in range(2):
        rel = j - i - d * ta
        b = _bias_lookup(rel_bias, _t5_bucket(rel))
        allowed = ((j // CHUNK) <= (i // CHUNK)) if d == 0 else jnp.ones((ta, ta), bool)
        tiles.append(jnp.where(allowed[..., None], b, NEG_INF))
    return jnp.transpose(jnp.stack(tiles), (3, 0, 1, 2))


def _sample_bias(rel_bias, past, s):
    q_pos = past + jnp.arange(s, dtype=i32)
    k_pos = jnp.arange(past + s, dtype=i32)
    rel = k_pos[None, :] - q_pos[:, None]
    b = _bias_lookup(rel_bias, _t5_bucket(rel))
    allowed = (k_pos[None, :] // CHUNK) <= (q_pos[:, None] // CHUNK)
    return jnp.transpose(jnp.where(allowed[..., None], b, NEG_INF), (2, 0, 1))


ATT_DIAG, ATT_SUBDIAG, ATT_FAR1, ATT_FAR2 = 0, 1, 2, 3


def _attn_prompt_kernel(qi_tab, ka_tab, kb_tab, kind_tab, qT_ref, k_ref, v1T_ref, k2_ref, v2T_ref, bias_ref,
                        cfar_ref, lq1_ref, lk1_ref, lq2_ref, lk2_ref, g_ref, o_ref,
                        qm_sc, m_sc, acc_sc, oT_sc, *, lam_init):
    del qi_tab, kb_tab
    p = pl.program_id(2)
    kind = kind_tab[p]
    ta = k_ref.shape[0]
    n_hp = k_ref.shape[1] // LANES

    @pl.when(ka_tab[p] == 0)
    def _():
        m_sc[...] = jnp.full(m_sc.shape, NEG_INF, f32)
        acc_sc[...] = jnp.zeros(acc_sc.shape, f32)
        row = lax.broadcasted_iota(i32, (LANES, ta), 0)
        for hp in range(n_hp):
            qt = qT_ref[hp * LANES:(hp + 1) * LANES, :]
            for gi in range(4):
                keep = (row >= gi * DQK) & (row < (gi + 1) * DQK)
                qm_sc[hp, :, gi * ta:(gi + 1) * ta] = jnp.where(keep, qt, jnp.zeros_like(qt))

    def scores(keys_ref, hp):
        return _dot(keys_ref[:, hp * LANES:(hp + 1) * LANES], qm_sc[hp])

    nb = ta // ATT_SUB

    def near_bias(hp, step_d):
        rows = []
        for jb in range(nb):
            cols = []
            for g in range(4):
                for ib in range(nb):
                    dd = step_d * nb + ib - jb
                    if dd < 0:
                        blk = jnp.full((ATT_SUB, ATT_SUB), NEG_INF * LOG2E, f32)
                    elif dd < 2:
                        blk = bias_ref[2 * hp + g // 2, dd]
                    else:
                        lo = g * ta + ib * ATT_SUB
                        blk = jnp.broadcast_to(cfar_ref[hp][:, lo:lo + ATT_SUB], (ATT_SUB, ATT_SUB))
                    cols.append(blk)
            rows.append(jnp.concatenate(cols, axis=1))
        return jnp.concatenate(rows, axis=0)

    def update(step_d, tiles):
        chain = [(kr, vr, hp) for kr, vr in tiles for hp in range(n_hp)]
        s_next = scores(chain[0][0], chain[0][2])
        for link, (_, values_ref, hp) in enumerate(chain):
            s = s_next
            if link + 1 < len(chain):
                s_next = scores(chain[link + 1][0], chain[link + 1][2])
            m_old = m_sc[hp]
            if step_d is not None:
                s = s + near_bias(hp, step_d)
                m_new = jnp.maximum(m_old, jnp.max(s, axis=0, keepdims=True))
                shift = m_new
            else:
                c = cfar_ref[hp]
                m_new = jnp.maximum(m_old, jnp.max(s, axis=0, keepdims=True) + c)
                shift = m_new - c
            alpha = jnp.exp2(m_old - m_new)
            pT = jnp.exp2(s - shift).astype(bf16)
            for hh in range(2):
                h = 2 * hp + hh
                cols = slice(hh * 2 * ta, (hh + 1) * 2 * ta)
                pv = _dot(values_ref[h * V_ROWS:(h + 1) * V_ROWS, :], pT[:, cols])
                acc_sc[h] = alpha[:, cols] * acc_sc[h] + pv
            m_sc[hp] = m_new

    @pl.when(kind == ATT_DIAG)
    def _():
        update(0, [(k_ref, v1T_ref)])

    @pl.when(kind == ATT_SUBDIAG)
    def _():
        update(1, [(k_ref, v1T_ref)])

    @pl.when(kind == ATT_FAR1)
    def _():
        update(None, [(k_ref, v1T_ref)])

    @pl.when(kind == ATT_FAR2)
    def _():
        update(None, [(k_ref, v1T_ref), (k2_ref, v2T_ref)])

    @pl.when(kind == ATT_DIAG)
    def _():
        lam = _lambda(lq1_ref, lk1_ref, lq2_ref, lk2_ref, lam_init)
        for h in range(2 * n_hp):
            a = acc_sc[h]
            a0 = a[:, :ta]
            a1 = a[:, ta:]
            o = a0[:DV] / a0[DV:DV + 1] - lam * (a1[:DV] / a1[DV:DV + 1])
            o = o * lax.rsqrt(jnp.mean(o * o, axis=0, keepdims=True) + RMS_EPS) * g_ref[...] * (1.0 - lam_init)
            oT_sc[h * DV:(h + 1) * DV, :] = o
        o_ref[...] = oT_sc[...].T.astype(bf16)


def _attn_prompt(qT, kb, v1T, rel_bias, lam_args, attn_g, B, L, lam_init):
    ta = min(ATT_TILE, L)
    assert L % ta == 0 and ta % ATT_SUB == 0 and ATT_SUB % CHUNK == 0
    nt = L // ta
    far = _far_bucket(ATT_SUB + 1)
    assert L <= 2 * ATT_SUB or far is not None, "key blocks two or more behind must share one bias bucket"
    far = far if far is not None else N_BUCKETS // 2 - 1
    steps = []
    idle_b = 0
    for q in range(nt):
        far_tiles = list(range(max(q - 1, 0)))
        for a in range(0, len(far_tiles) - 1, 2):
            steps.append((q, a, a + 1, ATT_FAR2))
            idle_b = a + 1
        if len(far_tiles) % 2:
            steps.append((q, far_tiles[-1], idle_b, ATT_FAR1))
        if q >= 1:
            steps.append((q, q - 1, idle_b, ATT_SUBDIAG))
        steps.append((q, q, idle_b, ATT_DIAG))
    qi_tab, ka_tab, kb_tab, kind_tab = (jnp.asarray([s[j] for s in steps], i32) for j in range(4))
    bias =_prompt_bias_tiles(rel_bias, ATT_SUB) * LOG2E
    cfar = jnp.repeat((rel_bias.astype(f32)[far] * LOG2E).reshape(H_A // 2, 2), 2 * ta, axis=1)[:, None, :]
    g_col = attn_g.astype(f32).reshape(DV, 1)
    T = B * L
    ng = ATT_HEAD_GROUPS
    n_hp = H_A // 2 // ng
    qk_w, v_w, o_w, n_h = D_QK_ALL // ng, H_A * V_ROWS // ng, D_ATT // ng, H_A // ng
    full = lambda a: pl.BlockSpec(a.shape, lambda b, g, p, qt, ka, kb, kd: (0,) * a.ndim)
    grid_spec = pltpu.PrefetchScalarGridSpec(
        num_scalar_prefetch=4, grid=(B, ng, len(steps)),
        in_specs=[
            pl.BlockSpec((qk_w, ta), lambda b, g, p, qt, ka, kb, kd: (g, b * nt + qt[p])),
            pl.BlockSpec((ta, qk_w), lambda b, g, p, qt, ka, kb, kd: (b * nt + ka[p], g)),
            pl.BlockSpec((v_w, ta), lambda b, g, p, qt, ka, kb, kd: (g, b * nt + ka[p])),
            pl.BlockSpec((ta, qk_w), lambda b, g, p, qt, ka, kb, kd: (b * nt + kb[p], g)),
            pl.BlockSpec((v_w, ta), lambda b, g, p, qt, ka, kb, kd: (g, b * nt + kb[p])),
            pl.BlockSpec((n_h,) + bias.shape[1:], lambda b, g, p, qt, ka, kb, kd: (g, 0, 0, 0)),
            pl.BlockSpec((n_hp,) + cfar.shape[1:], lambda b, g, p, qt, ka, kb, kd: (g, 0, 0))]
                 + [full(a) for a in lam_args] + [full(g_col)],
        out_specs=pl.BlockSpec((ta, o_w), lambda b, g, p, qt, ka, kb, kd: (b * nt + qt[p], g)),
        scratch_shapes=[pltpu.VMEM((n_hp, LANES, 4 * ta), bf16), pltpu.VMEM((n_hp, 1, 4 * ta), f32),
                        pltpu.VMEM((n_h, V_ROWS, 2 * ta), f32), pltpu.VMEM((o_w, ta), f32)])
    return pl.pallas_call(
        functools.partial(_attn_prompt_kernel, lam_init=lam_init),
        grid_spec=grid_spec, out_shape=jax.ShapeDtypeStruct((T, D_ATT), bf16),
        compiler_params=_params(("parallel", "parallel", "arbitrary")),
    )(qi_tab, ka_tab, kb_tab, kind_tab, qT, kb, v1T, kb, v1T, bias, cfar, *lam_args, g_col)


def _attn_sample_kernel(q_ref, kn_ref, vn_ref, kc_ref, vc_ref, bc_ref, bn_ref,
                        lq1_ref, lk1_ref, lq2_ref, lk2_ref, g2_ref, o_ref, *, lam_init):
    s_len = q_ref.shape[0]
    lane = lax.broadcasted_iota(i32, (s_len, LANES), 1)
    lam = _lambda(lq1_ref, lk1_ref, lq2_ref, lk2_ref, lam_init)
    for hp in range(H_A // 2):
        sl = slice(hp * LANES, (hp + 1) * LANES)
        qt = q_ref[:, sl]
        kct = kc_ref[0, sl, :].astype(bf16)
        vct = vc_ref[0, sl, :].astype(bf16)
        knt = kn_ref[:, sl].astype(bf16)
        vnt = vn_ref[:, sl].astype(bf16)
        outs = []
        for hh in range(2):
            h = 2 * hp + hh
            oc = []
            for c in range(2):
                gi = 2 * hh + c
                qm = jnp.where((lane >= gi * DQK) & (lane < (gi + 1) * DQK), qt, jnp.zeros_like(qt))
                s1 = _dot(qm, kct) + bc_ref[h]
                s2 = _dot_nt(qm, knt) + bn_ref[h]
                m = jnp.maximum(jnp.max(s1, axis=-1, keepdims=True), jnp.max(s2, axis=-1, keepdims=True))
                p1 = jnp.exp(s1 - m)
                p2 = jnp.exp(s2 - m)
                l = jnp.sum(p1, axis=-1, keepdims=True) + jnp.sum(p2, axis=-1, keepdims=True)
                pv = _dot_nt(p1.astype(bf16), vct) + _dot(p2.astype(bf16), vnt)
                oc.append(pv / l)
            o = oc[0] - lam * oc[1]
            in_head = (lane >= hh * DV) & (lane < (hh + 1) * DV)
            ms = jnp.sum(jnp.where(in_head, o * o, 0.0), axis=-1, keepdims=True) * (1.0 / DV)
            outs.append(o * lax.rsqrt(ms + RMS_EPS))
        ot = jnp.where(lane < DV, outs[0], outs[1]) * g2_ref[...] * (1.0 - lam_init)
        o_ref[:, sl] = ot.astype(bf16)


def _attn_sample(q, k_new, v_new, cache_k, cache_v, rel_bias, lam_args, attn_g, lam_init):
    nb, past = cache_k.shape[0], cache_k.shape[1]
    T = q.shape[0]
    s = T // nb
    kc = cache_k.transpose(0, 2, 3, 1).reshape(nb, D_QK_ALL, past)
    vc = cache_v.transpose(0, 2, 3, 1).reshape(nb, D_ATT, past)
    bias = _sample_bias(rel_bias, past, s)
    bc, bn = bias[:, :, :past], bias[:, :, past:]
    g2 = jnp.tile(attn_g.astype(f32), 2).reshape(1, LANES)
    full = lambda a: pl.BlockSpec(a.shape, lambda b: (0,) * a.ndim)
    row = lambda n: pl.BlockSpec((s, n), lambda b: (b, 0))
    return pl.pallas_call(
        functools.partial(_attn_sample_kernel, lam_init=lam_init),
        grid=(nb,),
        in_specs=[row(D_QK_ALL), row(D_QK_ALL), row(D_ATT),
                  pl.BlockSpec((1, D_QK_ALL, past), lambda b: (b, 0, 0)),
                  pl.BlockSpec((1, D_ATT, past), lambda b: (b, 0, 0)),
                  full(bc), full(bn)] + [full(a) for a in lam_args] + [full(g2)],
        out_specs=row(D_ATT), out_shape=jax.ShapeDtypeStruct((T, D_ATT), bf16),
        compiler_params=_params(("parallel",)),
    )(q, k_new, v_new, kc, vc, bc, bn, *lam_args, g2)


def _ssd_kernel(z_ref, xbc_ref, dt_ref, dtT_ref, h0_ref, c0_ref, cw_ref, cb_ref, dtb_ref, dtbT_ref,
                alog_ref, alogT_ref, dsk_ref, g_ref, y_ref, hout_ref, cout_ref, xpad_sc, h_sc):
    q = xbc_ref.shape[0]
    c = pl.program_id(1)
    gw = HEADS_PER_GROUP * SSM_HEADDIM

    @pl.when(c == 0)
    def _():
        xpad_sc[0:SUBLANES] = c0_ref[0]
        h_sc[...] = h0_ref[0]

    xpad_sc[SUBLANES:SUBLANES + q] = xbc_ref[...]
    first = SUBLANES - (CONV_W - 1)
    conv = cb_ref[...]
    for i in range(CONV_W):
        conv = conv + xpad_sc[first + i:first + i + q] * cw_ref[i:i + 1, :]
    tail = xpad_sc[q:q + SUBLANES]
    xpad_sc[0:SUBLANES] = tail
    cout_ref[0] = tail
    act = conv * _sigmoid(conv)
    xs = act[:, :D_SSM]

    lane = lax.broadcasted_iota(i32, (1, LANES), 1)
    a_row = jnp.where(lane < H_S, -jnp.exp(alog_ref[...]) * LOG2E, 0.0)
    dt = _softplus(dt_ref[...] + dtb_ref[...])
    ii = lax.broadcasted_iota(i32, (q, q), 0)
    jj = lax.broadcasted_iota(i32, (q, q), 1)
    causal = jj <= ii
    acs = _dot_f32_rhs(causal.astype(bf16), dt * a_row)
    sub = lax.broadcasted_iota(i32, (BF16_ROWS, 1), 0)
    a_col = jnp.where(sub < H_S, -jnp.exp(alogT_ref[...]) * LOG2E, 0.0)
    dtT = _softplus(dtT_ref[0] + dtbT_ref[...])
    acsT = _dot_f32_lhs(dtT * a_col, (ii <= jj).astype(bf16))

    er = lax.broadcasted_iota(i32, (LANES, D_SSM), 0)
    ec = lax.broadcasted_iota(i32, (LANES, D_SSM), 1)
    expand = (ec // SSM_HEADDIM == er).astype(bf16)
    dt_x = _dot_f32_lhs(dt, expand)
    acs_x = _dot_f32_lhs(acs, expand)
    e_acs = jnp.exp2(acs_x)
    acs_last = acs_x[q - 1:q, :]
    decay = jnp.exp2(acs_last - acs_x) * dt_x
    dtx = xs * dt_x
    xd = xs * decay
    glane = lax.broadcasted_iota(i32, (1, gw), 1)

    ys = []
    for g in range(SSM_GROUPS):
        gs = slice(g * gw, (g + 1) * gw)
        bg = act[:, D_SSM + g * D_STATE:D_SSM + (g + 1) * D_STATE].astype(bf16)
        cg = act[:, D_SSM + (SSM_GROUPS + g) * D_STATE:D_SSM + (SSM_GROUPS + g + 1) * D_STATE].astype(bf16)
        cb = _dot_nt(cg, bg)
        h_old = h_sc[g]
        yg = _dot(cg, h_old.astype(bf16)) * e_acs[:, gs]
        dtx_g = dtx[:, gs]
        for e4 in range(HEADS_PER_GROUP):
            e = g * HEADS_PER_GROUP + e4
            seg = acs[:, e:e + 1] - acsT[e:e + 1, :]
            lmat = jnp.exp2(jnp.where(causal, seg, -jnp.inf))
            rhs = jnp.where(glane // SSM_HEADDIM == e4, dtx_g, 0.0).astype(bf16)
            yg = yg + _dot((cb * lmat).astype(bf16), rhs)
        h_sc[g] = h_old * jnp.exp2(acs_last[:, gs]) + _dot_tn(bg, xd[:, gs].astype(bf16))
        ys.append(yg)
    y = jnp.concatenate(ys, axis=1) + dsk_ref[...] * xs
    zz = z_ref[...]
    y = y * (zz * _sigmoid(zz))
    for g in range(SSM_GROUPS):
        gs = slice(g * gw, (g + 1) * gw)
        yg = y[:, gs]
        r = lax.rsqrt(jnp.mean(yg * yg, axis=-1, keepdims=True) + RMS_EPS)
        y_ref[:, gs] = (yg * r * g_ref[:, gs]).astype(bf16)
    hout_ref[0] = h_sc[...]


def _ssd(z, xbc, dt, dtT, h0, conv0, pw, B, L):
    q = min(SSD_TILE, L)
    nc = L // q
    gw = HEADS_PER_GROUP * SSM_HEADDIM
    dtT3 = dtT.reshape(BF16_ROWS, B, L).transpose(1, 0, 2)
    hT0 = h0.astype(f32).reshape(B, SSM_GROUPS, gw, D_STATE).transpose(0, 1, 3, 2)
    c0 = jnp.pad(conv0.astype(f32), ((0, 0), (SUBLANES - (CONV_W - 1), 0), (0, 0)))
    full = lambda a: pl.BlockSpec(a.shape, lambda b, c: (0,) * a.ndim)
    row = lambda n: pl.BlockSpec((q, n), lambda b, c: (b * nc + c, 0))
    params = [pw['conv_w'], pw['conv_b'], pw['dt_bias'], pw['dt_biasT'], pw['a_log'], pw['a_logT'],
              pw['d_skip'], pw['ssm_g']]
    y, hT, ctail = pl.pallas_call(
        _ssd_kernel, grid=(B, nc),
        in_specs=[row(D_SSM), row(CONV_DIM), row(LANES),
                  pl.BlockSpec((1, BF16_ROWS, q), lambda b, c: (b, 0, c)),
                  pl.BlockSpec((1, SSM_GROUPS, D_STATE, gw), lambda b, c: (b, 0, 0, 0)),
                  pl.BlockSpec((1, SUBLANES, CONV_DIM), lambda b, c: (b, 0, 0))] + [full(a) for a in params],
        out_specs=[row(D_SSM),
                   pl.BlockSpec((1, SSM_GROUPS, D_STATE, gw), lambda b, c: (b, 0, 0, 0)),
                   pl.BlockSpec((1, SUBLANES, CONV_DIM), lambda b, c: (b, 0, 0))],
        out_shape=[jax.ShapeDtypeStruct((B * L, D_SSM), bf16),
                   jax.ShapeDtypeStruct((B, SSM_GROUPS, D_STATE, gw), f32),
                   jax.ShapeDtypeStruct((B, SUBLANES, CONV_DIM), f32)],
        scratch_shapes=[pltpu.VMEM((q + SUBLANES, CONV_DIM), f32), pltpu.VMEM((SSM_GROUPS, D_STATE, gw), f32)],
        compiler_params=_params(("parallel", "arbitrary")),
    )(z, xbc, dt, dtT3, hT0, c0, *params)
    h_new = hT.transpose(0, 1, 3, 2).reshape(B, H_S, SSM_HEADDIM, D_STATE)
    return y, h_new, ctail[:, SUBLANES - (CONV_W - 1):, :]


def _pack_bf16_pairs(x):
    half = x.shape[1] // 2
    bits = lambda a: lax.bitcast_convert_type(a.astype(bf16).astype(f32), jnp.uint32)
    return (bits(x[:, half:]) & jnp.uint32(0xFFFF0000)) | (bits(x[:, :half]) >> 16)


def _unpack_bf16_pairs(word):
    lo = lax.bitcast_convert_type(word << 16, f32)
    hi = lax.bitcast_convert_type(word & jnp.uint32(0xFFFF0000), f32)
    return jnp.concatenate([lo, hi], axis=1)


def _mix_router_kernel(att_ref, ssm_ref, x_ref, woa_ref, wos_ref, g_ref, b_ref, wrh_ref, wrhl_ref, br_ref,
                       cnt0_ref, *rest):
    x1_ref, x1p_ref, te_ref, gate_ref, cnt_ref, run_sc, earlier_sc = rest[-7:]

    @pl.when(pl.program_id(0) == 0)
    def _():
        run_sc[...] = cnt0_ref[...]
        ii = lax.broadcasted_iota(i32, earlier_sc.shape, 0)
        jj = lax.broadcasted_iota(i32, earlier_sc.shape, 1)
        earlier_sc[...] = (jj < ii).astype(bf16)

    mix = _dot(att_ref[...], woa_ref[...]) + _dot(ssm_ref[...], wos_ref[...])
    x1 = _layer_norm(DEEPNORM_ALPHA * x_ref[...] + mix, g_ref[...], b_ref[...])
    x1_ref[...] = x1
    x1p_ref[...] = _pack_bf16_pairs(x1)
    hi = x1.astype(bf16)
    lo = (x1 - hi.astype(f32)).astype(bf16)
    both = _dot(hi, wrhl_ref[...])
    logits = both[:, :LANES] + both[:, LANES:] + _dot(lo, wrh_ref[...]) + br_ref[...]
    lane =lax.broadcasted_iota(i32, logits.shape, 1)
    vals, idxs = [], []
    for _ in range(TOP_K):
        m = jnp.max(logits, axis=-1, keepdims=True)
        idx = jnp.min(jnp.where(logits == m, lane, LANES), axis=-1, keepdims=True)
        vals.append(m)
        idxs.append(idx)
        logits = jnp.where(lane == idx, -jnp.inf, logits)
    es = [jnp.exp(v - vals[0]) for v in vals]
    tot = es[0]
    for e in es[1:]:
        tot = tot + e
    chosen = jnp.zeros(logits.shape, f32)
    for k in range(TOP_K):
        chosen = chosen + (lane == idxs[k]).astype(f32)
    before = _dot(earlier_sc[...], chosen.astype(bf16)) + run_sc[...]
    te = jnp.zeros(logits.shape, i32)
    gate = jnp.zeros(logits.shape, f32)
    for k in range(TOP_K):
        rank = jnp.sum(jnp.where(lane == idxs[k], before, 0.0), axis=-1, keepdims=True).astype(i32)
        te = jnp.where(lane == k, idxs[k], te)
        te = jnp.where(lane == TOP_K + k, rank, te)
        gate = jnp.where(lane == k, es[k] / tot, gate)
    te_ref[...] = te
    gate_ref[...] = gate
    run_sc[...] = run_sc[...] + jnp.sum(chosen, axis=0, keepdims=True)
    cnt_ref[...] = run_sc[...]


def _mix_router(att, ssm, x2d, w, counts0, packed, row0, rows_total):
    T, D = x2d.shape
    tl = min(ROW_TILE, T)
    assert row0 % tl == 0
    row = lambda n: pl.BlockSpec((tl, n), lambda i: (i, 0))
    full = lambda a: pl.BlockSpec(a.shape, lambda i: (0,) * a.ndim)
    ws = [w['wo_att'], w['wo_ssm'], w['ln1_g'], w['ln1_b'], w['wr_hi'], w['wr_hi_lo'], w['b_router'], counts0]
    ins = [att, ssm, x2d] + ws + [packed]
    in_specs = [row(D_ATT), row(D_SSM), row(D)] + [full(a) for a in ws] + [pl.BlockSpec(memory_space=pl.ANY)]
    aliases = {len(ins) - 1: 1}
    return pl.pallas_call(
        _mix_router_kernel, grid=(T // tl,), in_specs=in_specs,
        out_specs=[row(D), pl.BlockSpec((tl, D // 2), lambda i: (row0 // tl + i, 0)), row(LANES), row(LANES),
                   pl.BlockSpec((1, LANES), lambda i: (0, 0))],
        out_shape=[jax.ShapeDtypeStruct((T, D), f32), jax.ShapeDtypeStruct((rows_total, D // 2), jnp.uint32),
                   jax.ShapeDtypeStruct((T, LANES), i32),
                   jax.ShapeDtypeStruct((T, LANES), f32), jax.ShapeDtypeStruct((1, LANES), f32)],
        scratch_shapes=[pltpu.VMEM((1, LANES), f32), pltpu.VMEM((tl, tl), bf16)], input_output_aliases=aliases,
        compiler_params=_params(("arbitrary",)),
    )(*ins)


def _route(te, counts_f):
    T = te.shape[0]
    n_assign = T * TOP_K
    counts = counts_f[0, :N_EXPERTS].astype(i32)
    padded = (counts + MOE_BLOCK - 1) // MOE_BLOCK * MOE_BLOCK
    pad_end = jnp.cumsum(padded)
    pad_start = pad_end - padded
    top_e, rank = te[:, :TOP_K], te[:, TOP_K:2 * TOP_K]
    experts = jnp.arange(N_EXPERTS, dtype=i32)
    slot = rank + jnp.sum(jnp.where(top_e[..., None] == experts, pad_start, 0), axis=-1)
    n_blocks = -(-n_assign // MOE_BLOCK) + N_EXPERTS
    n_blocks += n_blocks % FFN_BLOCKS_PER_STEP
    block_start = jnp.arange(n_blocks, dtype=i32) * MOE_BLOCK
    block_e = jnp.minimum(jnp.sum(block_start[:, None] >= pad_end[None, :], axis=-1), N_EXPERTS - 1).astype(i32)
    n_used = (pad_end[-1] // MOE_BLOCK).astype(i32).reshape(1)
    row_end = jnp.sum(jnp.where(block_e[:, None] == experts, pad_start + counts, 0), axis=-1)
    rows_valid = jnp.clip(row_end - block_start, 0, MOE_BLOCK).astype(i32)
    return block_e, n_used, rows_valid, slot.astype(i32), n_blocks


SC_INDEX_WINDOW = 128
SC_ROWS = 64


def _sc_mesh():
    return plsc.VectorSubcoreMesh(core_axis_name="c", subcore_axis_name="s")


def _sc_move_rows(src, src_idx, dst_idx_list, n_out):
    M = src_idx.shape[0]
    D = src.shape[1]
    idx = [a.reshape(1, M) for a in [src_idx] + list(dst_idx_list)]

    @functools.partial(pl.kernel, out_type=jax.ShapeDtypeStruct((n_out, D), src.dtype), mesh=_sc_mesh(),
                       scratch_types=[pltpu.VMEM((2, SC_ROWS, D), src.dtype), pltpu.SemaphoreType.DMA((2,))])
    def move(s_hbm, *rest):
        i_hbm, o_hbm, buf, sem = rest[:-3], rest[-3], rest[-2], rest[-1]
        n_parts = SC_INDEX_WINDOW // SC_ROWS

        def body(si_vmem, *di_vmem):
            def fetch(j):
                part = pl.ds(j * SC_ROWS, SC_ROWS)
                return pltpu.async_copy(s_hbm.at[si_vmem.at[0, part]], buf.at[j % 2], sem.at[j % 2])

            pending = fetch(0)
            for j in range(n_parts):
                nxt = fetch(j + 1) if j + 1 < n_parts else None
                pending.wait()
                part = pl.ds(j * SC_ROWS, SC_ROWS)
                for dv in di_vmem:
                    pltpu.sync_copy(buf.at[j % 2], o_hbm.at[dv.at[0, part]])
                pending = nxt

        pltpu.emit_pipeline(
            body, grid=(M // SC_INDEX_WINDOW,),
            in_specs=[pl.BlockSpec((1, SC_INDEX_WINDOW), lambda i: (0, i))] * len(idx),
            out_specs=[], core_axis_name=("c", "s"), dimension_semantics=(pltpu.PARALLEL,),
        )(*i_hbm)

    return move(src, *idx)


def _sc_scatter_rows(x, idx_k, n_rows):
    return _sc_move_rows(x, jnp.arange(x.shape[0], dtype=i32), idx_k, n_rows)


def _sc_gather_rows(src, idx):
    M = idx.shape[0]
    return _sc_move_rows(src, idx, [jnp.arange(M, dtype=i32)], M)


def _deinterleave_kernel(w_ref, g_ref, u_ref):
    tn = 2 * LANES
    r = lax.broadcasted_iota(i32, (2 * tn, tn), 0)
    c = lax.broadcasted_iota(i32, (2 * tn, tn), 1)
    pick_g = (r == 2 * c).astype(bf16)
    pick_u = (r == 2 * c + 1).astype(bf16)
    for j in range(g_ref.shape[-1] // tn):
        wb = w_ref[0, :, j * 2 * tn:(j + 1) * 2 * tn].astype(bf16)
        g_ref[0, :, j * tn:(j + 1) * tn] = _dot(wb, pick_g).astype(bf16)
        u_ref[0, :, j * tn:(j + 1) * tn] = _dot(wb, pick_u).astype(bf16)


def _deinterleave(w_gu):
    E, D, F2 = w_gu.shape
    out = jax.ShapeDtypeStruct((E, D, F2 // 2), bf16)
    return pl.pallas_call(
        _deinterleave_kernel, grid=(E,),
        in_specs=[pl.BlockSpec((1, D, F2), lambda e: (e, 0, 0))],
        out_specs=[pl.BlockSpec((1, D, F2 // 2), lambda e: (e, 0, 0))] * 2,
        out_shape=[out, out],
        compiler_params=_params(("parallel",)),
    )(w_gu)


def _ffn_kernel(be_ref, nu_ref, rv_ref, xs_ref, *refs):
    del be_ref
    y_ref = refs[-1]
    wsets = [refs[6 * j:6 * j + 6] for j in range(FFN_BLOCKS_PER_STEP)]
    first = pl.program_id(0) * FFN_BLOCKS_PER_STEP
    n_live = jnp.clip(nu_ref[0] - first, 0, FFN_BLOCKS_PER_STEP)
    row_id = lax.broadcasted_iota(i32, (MOE_BLOCK, 1), 0)

    def block(j):
        wg_ref, wu_ref, bg_ref, bu_ref, wd_ref, bd_ref = wsets[j]
        rows = pl.ds(j * MOE_BLOCK, MOE_BLOCK)
        words = jnp.where(row_id < rv_ref[first + j], xs_ref[rows, :], jnp.uint32(0))
        xb = _unpack_bf16_pairs(words).astype(bf16)
        g = _dot(xb, wg_ref[0]) + bg_ref[0]
        u = _dot(xb, wu_ref[0]) + bu_ref[0]
        g = jnp.minimum(g, SWIGLU_LIMIT)
        u = jnp.clip(u, -SWIGLU_LIMIT, SWIGLU_LIMIT)
        act = (u + 1.0) * g * _sigmoid(SWIGLU_ALPHA * g)
        y_ref[rows, :] = _pack_bf16_pairs(_dot(act.astype(bf16), wd_ref[0].astype(bf16)) + bd_ref[0])

    for live in range(FFN_BLOCKS_PER_STEP + 1):
        @pl.when(n_live == live)
        def _(live=live):
            for j in range(live):
                block(j)
            for j in range(live, FFN_BLOCKS_PER_STEP):
                y_ref[pl.ds(j * MOE_BLOCK, MOE_BLOCK), :] = jnp.zeros((MOE_BLOCK, y_ref.shape[1]), jnp.uint32)


def _expert_ffn(xs, block_e, n_used, rows_valid, w):
    n_rows, half = xs.shape
    D = 2 * half
    nps = FFN_BLOCKS_PER_STEP
    n_steps = n_rows // (MOE_BLOCK * nps)
    F = w['w_g'].shape[2]
    wspecs, wargs = [], []
    for j in range(nps):
        pick = lambda i, be, nu, rv, j=j: (be[i * nps + j], 0, 0)
        wspecs += [pl.BlockSpec((1, D, F), pick), pl.BlockSpec((1, D, F), pick), pl.BlockSpec((1, 1, F), pick),
                   pl.BlockSpec((1, 1, F), pick), pl.BlockSpec((1, F, D), pick), pl.BlockSpec((1, 1, D), pick)]
        wargs += [w['w_g'], w['w_u'], w['b_g'], w['b_u'], w['w_d'], w['b_d']]
    grid_spec = pltpu.PrefetchScalarGridSpec(
        num_scalar_prefetch=3, grid=(n_steps,),
        in_specs=[pl.BlockSpec((nps * MOE_BLOCK, half),
                               lambda i, be, nu, rv: (jnp.minimum(i, (nu[0] - 1) // nps), 0))] + wspecs,
        out_specs=pl.BlockSpec((nps * MOE_BLOCK, half), lambda i, be, nu, rv: (i, 0)))
    return pl.pallas_call(
        _ffn_kernel, grid_spec=grid_spec,
        out_shape=jax.ShapeDtypeStruct((n_rows, half), jnp.uint32),
        compiler_params=_params(("arbitrary",)),
    )(block_e, n_used, rows_valid, xs, *wargs)


def _combine_dense_kernel(rows_ref, x1_ref, gate_ref, g_ref, b_ref, *rest):
    y_ref = rest[-1]
    gate = gate_ref[...]
    ff = gate[:, 0:1] * _unpack_bf16_pairs(rows_ref[0])
    for k in range(1, TOP_K):
        ff = ff + gate[:, k:k + 1] * _unpack_bf16_pairs(rows_ref[k])
    y_ref[...] = _layer_norm(DEEPNORM_ALPHA * x1_ref[...] + ff, g_ref[...], b_ref[...])


def _combine_dense(rows, x1, gate, ln_g, ln_b, chunk, y_partial):
    T, D = x1.shape
    tc = rows.shape[1]
    tl = min(ROW_TILE, tc)
    first = chunk * (tc // tl)
    row = lambda n: pl.BlockSpec((tl, n), lambda i: (first + i, 0))
    full = lambda a: pl.BlockSpec(a.shape, lambda i: (0,) * a.ndim)
    ins = [rows, x1, gate, ln_g, ln_b]
    in_specs = [pl.BlockSpec((TOP_K, tl, D // 2), lambda i: (0, i, 0)), row(D), row(LANES), full(ln_g), full(ln_b)]
    aliases = {}
    if y_partial is not None:
        ins.append(y_partial)
        in_specs.append(pl.BlockSpec(memory_space=pl.ANY))
        aliases = {len(ins) - 1: 0}
    return pl.pallas_call(
        _combine_dense_kernel, grid=(tc // tl,), in_specs=in_specs,
        out_specs=row(D), out_shape=jax.ShapeDtypeStruct((T, D), f32), input_output_aliases=aliases,
        compiler_params=_params(("parallel",)),
    )(*ins)


def _prep_weights(l, w_in, conv_w, conv_b, dt_bias, a_log, d_skip, ssm_norm_g, w_o, ln1_g, ln1_b,
                  w_router, b_router, w_gate_up, b_gate_up, w_down, b_down, ln2_g, ln2_b):
    wi = w_in[l]
    c0, c1, c2, c3 = D_QK_ALL, 2 * D_QK_ALL, 2 * D_QK_ALL + D_ATT, 2 * D_QK_ALL + D_ATT + D_SSM
    c4 = c3 + CONV_DIM
    wdt = wi[:, c4:c4 + H_S]
    pad_lane = lambda v, fill=0.0: jnp.pad(v.astype(f32).reshape(1, -1), ((0, 0), (0, LANES - v.shape[-1])),
                                            constant_values=fill)
    pad_col = lambda v: jnp.pad(v.astype(f32).reshape(-1, 1), ((0, BF16_ROWS - v.shape[-1]), (0, 0)))
    wr = jnp.pad(w_router[l].astype(f32), ((0, 0), (0, LANES - N_EXPERTS)))
    wr_hi = wr.astype(bf16)
    wgu = w_gate_up[l]
    return {
        'wq': wi[:, :c0].astype(bf16), 'wqT': wi[:, :c0].T.astype(bf16),
        'wk': wi[:, c0:c1].astype(bf16), 'wkT': wi[:, c0:c1].T.astype(bf16), 'wv': wi[:, c1:c2].astype(bf16), 'wvT': wi[:, c1:c2].T.astype(bf16),
        'wz': wi[:, c2:c3].astype(bf16), 'wx': wi[:, c3:c4].astype(bf16),
        'wdt': jnp.pad(wdt, ((0, 0), (0, LANES - H_S))).astype(bf16),
        'wdtT': jnp.pad(wdt.T, ((0, BF16_ROWS - H_S), (0, 0))).astype(bf16),
        'conv_w': conv_w[l].astype(f32), 'conv_b': conv_b[l].astype(f32).reshape(1, -1),
        'dt_bias': pad_lane(dt_bias[l]), 'dt_biasT': pad_col(dt_bias[l]),
        'a_log': pad_lane(a_log[l]), 'a_logT': pad_col(a_log[l]),
        'd_skip': jnp.repeat(d_skip[l].astype(f32), SSM_HEADDIM).reshape(1, -1),
        'ssm_g': ssm_norm_g[l].astype(f32).reshape(1, -1),
        'wo_att': w_o[l][:D_ATT].astype(bf16), 'wo_ssm': w_o[l][D_ATT:].astype(bf16),
        'ln1_g': ln1_g[l].astype(f32).reshape(1, -1), 'ln1_b': ln1_b[l].astype(f32).reshape(1, -1),
        'wr_hi': wr_hi, 'wr_hi_lo': jnp.concatenate([wr_hi, (wr - wr_hi.astype(f32)).astype(bf16)], axis=1),
        'b_router': pad_lane(b_router[l], NEG_INF),
        'w_gu': wgu,
        'b_g': b_gate_up[l][:, None, 0::2].astype(f32), 'b_u': b_gate_up[l][:, None, 1::2].astype(f32),
        'w_d': w_down[l], 'b_d': b_down[l][:, None, :].astype(f32),
        'ln2_g': ln2_g[l].astype(f32).reshape(1, -1), 'ln2_b': ln2_b[l].astype(f32).reshape(1, -1),
    }


def _mixers(x, l, w, rel_bias, lam_args, attn_g, k_past, v_past, h0, conv0):
    B, L, D = x.shape
    x2d = x.reshape(B * L, D)
    lam_init = 0.8 - 0.6 * math.exp(-0.3 * l)
    prompt = k_past is None
    proj = _in_proj(x2d, w, prompt, B)
    z, xbc, dt, dtT = proj[:4]
    if prompt:
        kT, vT, kb, qT, v1T = proj[4:]
        att = _attn_prompt(qT, kb, v1T, rel_bias, lam_args, attn_g, B, L, lam_init)
        k_rows = kT.reshape(B, H_A, 2 * DQK, L).transpose(0, 3, 1, 2)
        v_rows = vT.reshape(B, H_A, DV, L).transpose(0, 3, 1, 2)
    else:
        k, v, q = proj[4:]
        att = _attn_sample(q, k, v, k_past, v_past, rel_bias, lam_args, attn_g, lam_init)
        k_rows = k.reshape(B, L, H_A, 2 * DQK)
        v_rows = v.reshape(B, L, H_A, DV)
    ssm, h_new, conv_new = _ssd(z, xbc, dt, dtT, h0, conv0, w, B, L)
    return (att, ssm, x2d), (k_rows, v_rows, h_new, conv_new)


def _moe_layer(streams, w):
    rows_total = sum(x2d.shape[0] for _, _, x2d in streams)
    counts = jnp.zeros((1, LANES), f32)
    packed = jnp.zeros((rows_total, streams[0][2].shape[1] // 2), jnp.uint32)
    row0, routed = 0, []
    for att, ssm, x2d in streams:
        x1, packed, te, gate, counts = _mix_router(att, ssm, x2d, w, counts, packed, row0, rows_total)
        routed.append((x1, gate, row0))
        row0 += x2d.shape[0]
        te_all = te[:, :2 * TOP_K] if len(routed) == 1 else jnp.concatenate([te_all, te[:, :2 * TOP_K]])
    block_e, n_used, rows_valid, slot, n_blocks = _route(te_all, counts)
    slot_k = slot.T
    xs = _sc_scatter_rows(packed, [slot_k[k] for k in range(TOP_K)], n_blocks * MOE_BLOCK)
    ys = _expert_ffn(xs, block_e, n_used, rows_valid, w)
    outs = []
    for x1, gate, r0 in routed:
        T, D = x1.shape
        n_groups = COMBINE_GROUPS if T % (COMBINE_GROUPS * ROW_TILE) == 0 else 1
        tc = T // n_groups
        y = None
        for c in range(n_groups):
            idx = slot_k[:, r0 + c * tc:r0 + (c + 1) * tc].reshape(-1)
            rows = _sc_gather_rows(ys, idx).reshape(TOP_K, tc, D // 2)
            y = _combine_dense(rows, x1, gate, w['ln2_g'], w['ln2_b'], c, y)
        outs.append(y)
    return outs


def kernel(x_prompt, x_sample, cache_k, cache_v, state_ssm, state_conv, rel_bias, w_in, lambda_q1, lambda_k1, lambda_q2, lambda_k2, attn_norm_g, conv_w, conv_b, dt_bias, a_log, d_skip, ssm_norm_g, w_o, ln1_g, ln1_b, w_router, b_router, w_gate_up, b_gate_up, w_down, b_down, ln2_g, ln2_b):
    yp, ys = x_prompt, x_sample
    bp = x_prompt.shape[0]
    depth = w_in.shape[0]
    outs = [[] for _ in range(8)]
    for l in range(depth):
        w = _prep_weights(l, w_in, conv_w, conv_b, dt_bias, a_log, d_skip, ssm_norm_g, w_o, ln1_g, ln1_b,
                          w_router, b_router, w_gate_up, b_gate_up, w_down, b_down, ln2_g, ln2_b)
        w['w_g'], w['w_u'] = _deinterleave(w.pop('w_gu'))
        lam_args = [a[l].astype(f32).reshape(1, -1) for a in (lambda_q1, lambda_k1, lambda_q2, lambda_k2)]
        h0 = jnp.zeros((bp, H_S, SSM_HEADDIM, D_STATE), f32)
        c0 = jnp.zeros((bp, CONV_W - 1, CONV_DIM), f32)
        mix_p, state_p = _mixers(yp, l, w, rel_bias, lam_args, attn_norm_g[l], None, None, h0, c0)
        mix_s, state_s = _mixers(ys, l, w, rel_bias, lam_args, attn_norm_g[l], cache_k[l], cache_v[l],
                                 state_ssm[l], state_conv[l])
        y2p, y2s = _moe_layer([mix_p, mix_s], w)
        yp, ys = y2p.reshape(yp.shape), y2s.reshape(ys.shape)
        for lst, a in zip(outs, state_p + state_s):
            lst.append(a)
    return (yp, ys) + tuple(jnp.stack(o) for o in outs)
```

```python
import functools
import math

import jax
import jax.numpy as jnp
from jax import lax
from jax.experimental import pallas as pl
from jax.experimental.pallas import tpu as pltpu
from jax.experimental.pallas import tpu_sc as plsc

f32 = jnp.float32
bf16 = jnp.bfloat16
i32 = jnp.int32

CHUNK = 64
H_A = 8
DQK = 32
DV = 2 * DQK
D_ATT = H_A * DV
SSM_HEADDIM = 64
H_S = 8
D_SSM = H_S * SSM_HEADDIM
SSM_GROUPS = 2
HEADS_PER_GROUP = H_S // SSM_GROUPS
D_STATE = 128
CONV_W = 4
CONV_DIM = D_SSM + 2 * SSM_GROUPS * D_STATE
D_QK_ALL = H_A * 2 * DQK
N_BUCKETS = 32
MAX_DISTANCE = 128
N_EXPERTS = 32
TOP_K = 4
SWIGLU_LIMIT = 7.0
SWIGLU_ALPHA = 1.702
MOE_BLOCK = 256
LN_EPS = 1e-5
RMS_EPS = 1e-5
NEG_INF = -1e30
DEPTH = 1
DEEPNORM_ALPHA = (2.0 * DEPTH) ** 0.25
LOG2E = math.log2(math.e)

LANES = 128
SUBLANES = 8
BF16_ROWS = 16
VMEM_LIMIT = 48 * 1024 * 1024

ROW_TILE = 512
ATT_TILE = 512
ATT_SUB = 256
SSD_TILE = 256
V_ROWS = DV + BF16_ROWS
FFN_BLOCKS_PER_STEP = 2
COMBINE_GROUPS = 4


def _params(semantics):
    return pltpu.CompilerParams(dimension_semantics=semantics, vmem_limit_bytes=VMEM_LIMIT)


def _dot(a, b):
    return jnp.dot(a, b, preferred_element_type=f32)


def _dot_nt(a, b):
    return lax.dot_general(a, b, (((1,), (1,)), ((), ())), preferred_element_type=f32)


def _dot_tn(a, b):
    return lax.dot_general(a, b, (((0,), (0,)), ((), ())), preferred_element_type=f32)


def _split3(a):
    hi = a.astype(bf16)
    r1 = a - hi.astype(f32)
    mid = r1.astype(bf16)
    lo = (r1 - mid.astype(f32)).astype(bf16)
    return hi, mid, lo


def _dot_f32_lhs(a, b_exact):
    hi, mid, lo = _split3(a)
    return _dot(hi, b_exact) + _dot(mid, b_exact) + _dot(lo, b_exact)


def _dot_f32_rhs(a_exact, b):
    hi, mid, lo = _split3(b)
    return _dot(a_exact, hi) + _dot(a_exact, mid) + _dot(a_exact, lo)


def _softplus(x):
    return jnp.maximum(x, 0.0) + jnp.log1p(jnp.exp(-jnp.abs(x)))


def _sigmoid(x):
    return 0.5 * jnp.tanh(0.5 * x) + 0.5


def _layer_norm(y, g, b):
    mu = jnp.mean(y, axis=-1, keepdims=True)
    yc = y - mu
    var = jnp.mean(yc * yc, axis=-1, keepdims=True)
    return yc * lax.rsqrt(var + LN_EPS) * g + b


def _lambda(lq1_ref, lk1_ref, lq2_ref, lk2_ref, lam_init):
    s1 = jnp.sum(lq1_ref[...] * lk1_ref[...], axis=-1, keepdims=True)
    s2 = jnp.sum(lq2_ref[...] * lk2_ref[...], axis=-1, keepdims=True)
    return jnp.exp(s1) - jnp.exp(s2) + lam_init


def _in_proj_kernel(x_ref, wk_ref, wv_ref, wz_ref, wx_ref, wdt_ref, wdtT_ref, wq_ref, wkT_ref, wvT_ref, *outs,
                    prompt):
    xb = x_ref[...].astype(bf16)
    z_ref, xbc_ref, dt_ref, dtT_ref = outs[:4]
    z_ref[...] = _dot(xb, wz_ref[...])
    xbc_ref[...] = _dot(xb, wx_ref[...])
    dt_ref[...] = _dot(xb, wdt_ref[...])
    dtT_ref[...] = _dot_nt(wdtT_ref[...], xb)
    scale = DQK ** -0.5
    if prompt:
        kT_ref, vT_ref, kb_ref, qT_ref, v1T_ref = outs[4:]
        kT = _dot_nt(wkT_ref[...], xb)
        kT_ref[0] = kT
        kb_ref[...] = kT.T.astype(bf16)
        qT_ref[...] = (_dot_nt(wq_ref[...], xb) * (scale * LOG2E)).astype(bf16)
        vT32 = _dot_nt(wvT_ref[...], xb)
        vT_ref[0] = vT32
        vT = vT32.astype(bf16)
        ones = jnp.ones((BF16_ROWS, vT.shape[1]), bf16)
        for h in range(H_A):
            v1T_ref[h * V_ROWS:h * V_ROWS + DV, :] = vT[h * DV:(h + 1) * DV, :]
            v1T_ref[h * V_ROWS + DV:(h + 1) * V_ROWS, :] = ones
    else:
        k_ref, v_ref, q_ref = outs[4:]
        k_ref[...] = _dot(xb, wk_ref[...])
        v_ref[...] = _dot(xb, wv_ref[...])
        q_ref[...] = (_dot(xb, wq_ref[...]) * scale).astype(bf16)


def _in_proj(x2d, w, prompt, B):
    T, D = x2d.shape
    tl = min(ROW_TILE, T)
    L = T // B
    n_l = L // tl
    grid = (T // tl,)
    row = lambda n: pl.BlockSpec((tl, n), lambda i: (i, 0))
    col = lambda n: pl.BlockSpec((n, tl), lambda i: (0, i))
    per_stream = lambda n: pl.BlockSpec((1, n, tl), lambda i: (i // n_l, 0, i % n_l))
    full = lambda a: pl.BlockSpec(a.shape, lambda i: (0,) * a.ndim)
    wq = w['wqT'] if prompt else w['wq']
    ins = [x2d, w['wk'], w['wv'], w['wz'], w['wx'], w['wdt'], w['wdtT'], wq, w['wkT'], w['wvT']]
    in_specs = [row(D)] + [full(a) for a in ins[1:]]
    out_shape = [jax.ShapeDtypeStruct((T, D_SSM), f32), jax.ShapeDtypeStruct((T, CONV_DIM), f32),
                 jax.ShapeDtypeStruct((T, LANES), f32), jax.ShapeDtypeStruct((BF16_ROWS, T), f32)]
    out_specs = [row(D_SSM), row(CONV_DIM), row(LANES), col(BF16_ROWS)]
    if prompt:
        assert L % tl == 0
        out_shape += [jax.ShapeDtypeStruct((B, D_QK_ALL, L), f32), jax.ShapeDtypeStruct((B, D_ATT, L), f32),
                      jax.ShapeDtypeStruct((T, D_QK_ALL), bf16), jax.ShapeDtypeStruct((D_QK_ALL, T), bf16),
                      jax.ShapeDtypeStruct((H_A * V_ROWS, T), bf16)]
        out_specs += [per_stream(D_QK_ALL), per_stream(D_ATT), row(D_QK_ALL), col(D_QK_ALL), col(H_A * V_ROWS)]
    else:
        out_shape += [jax.ShapeDtypeStruct((T, D_QK_ALL), f32), jax.ShapeDtypeStruct((T, D_ATT), f32),
                      jax.ShapeDtypeStruct((T, D_QK_ALL), bf16)]
        out_specs += [row(D_QK_ALL), row(D_ATT), row(D_QK_ALL)]
    return pl.pallas_call(
        functools.partial(_in_proj_kernel, prompt=prompt),
        grid=grid, in_specs=in_specs, out_specs=out_specs, out_shape=out_shape,
        compiler_params=_params(("parallel",)),
    )(*ins)


def _t5_bucket(rel):
    half = N_BUCKETS // 2
    max_exact = half // 2
    n = jnp.abs(rel)
    large = max_exact + (jnp.log(jnp.maximum(n, 1).astype(f32) / max_exact)
                         / math.log(MAX_DISTANCE / max_exact) * (half - max_exact)).astype(i32)
    large = jnp.minimum(large, half - 1)
    return jnp.where(rel > 0, half, 0) + jnp.where(n < max_exact, n, large)


def _far_bucket(min_dist):
    half = N_BUCKETS // 2
    max_exact = half // 2
    v = max_exact + int(math.log(min_dist / max_exact) / math.log(MAX_DISTANCE / max_exact) * (half - max_exact))
    return half - 1 if (min_dist >= max_exact and v - 1 >= half - 1) else None


def _bias_lookup(rel_bias, bucket):
    onehot = (bucket[..., None] == jnp.arange(N_BUCKETS, dtype=i32)).astype(f32)
    return jnp.dot(onehot, rel_bias.astype(f32), precision=lax.Precision.HIGHEST)


def _prompt_bias_tiles(rel_bias, ta):
    j = jnp.arange(ta, dtype=i32)[:, None]
    i = jnp.arange(ta, dtype=i32)[None, :]
    tiles = []
    for d in range(2):
        rel = j - i - d * ta
        b = _bias_lookup(rel_bias, _t5_bucket(rel))
        allowed = ((j // CHUNK) <= (i // CHUNK)) if d == 0 else jnp.ones((ta, ta), bool)
        tiles.append(jnp.where(allowed[..., None], b, NEG_INF))
    return jnp.transpose(jnp.stack(tiles), (3, 0, 1, 2))


def _sample_bias(rel_bias, past, s):
    q_pos = past + jnp.arange(s, dtype=i32)
    k_pos = jnp.arange(past + s, dtype=i32)
    rel = k_pos[None, :] - q_pos[:, None]
    b = _bias_lookup(rel_bias, _t5_bucket(rel))
    allowed = (k_pos[None, :] // CHUNK) <= (q_pos[:, None] // CHUNK)
    return jnp.transpose(jnp.where(allowed[..., None], b, NEG_INF), (2, 0, 1))


ATT_DIAG, ATT_SUBDIAG, ATT_FAR1, ATT_FAR2 = 0, 1, 2, 3


def _attn_prompt_kernel(qi_tab, ka_tab, kb_tab, kind_tab, qT_ref, k_ref, v1T_ref, k2_ref, v2T_ref, bias_ref,
                        cfar_ref, lq1_ref, lk1_ref, lq2_ref, lk2_ref, g_ref, o_ref,
                        qm_sc, m_sc, acc_sc, oT_sc, *, lam_init):
    del qi_tab, kb_tab
    p = pl.program_id(1)
    kind = kind_tab[p]
    ta = k_ref.shape[0]
    n_hp = H_A // 2

    @pl.when(ka_tab[p] == 0)
    def _():
        m_sc[...] = jnp.full(m_sc.shape, NEG_INF, f32)
        acc_sc[...] = jnp.zeros(acc_sc.shape, f32)
        row = lax.broadcasted_iota(i32, (LANES, ta), 0)
        for hp in range(n_hp):
            qt = qT_ref[hp * LANES:(hp + 1) * LANES, :]
            for gi in range(4):
                keep = (row >= gi * DQK) & (row < (gi + 1) * DQK)
                qm_sc[hp, :, gi * ta:(gi + 1) * ta] = jnp.where(keep, qt, jnp.zeros_like(qt))

    def scores(keys_ref, hp):
        return _dot(keys_ref[:, hp * LANES:(hp + 1) * LANES], qm_sc[hp])

    nb = ta // ATT_SUB

    def near_bias(hp, step_d):
        rows = []
        for jb in range(nb):
            cols = []
            for g in range(4):
                for ib in range(nb):
                    dd = step_d * nb + ib - jb
                    if dd < 0:
                        blk = jnp.full((ATT_SUB, ATT_SUB), NEG_INF * LOG2E, f32)
                    elif dd < 2:
                        blk = bias_ref[2 * hp + g // 2, dd]
                    else:
                        lo = g * ta + ib * ATT_SUB
                        blk = jnp.broadcast_to(cfar_ref[hp][:, lo:lo + ATT_SUB], (ATT_SUB, ATT_SUB))
                    cols.append(blk)
            rows.append(jnp.concatenate(cols, axis=1))
        return jnp.concatenate(rows, axis=0)

    def update(step_d, tiles):
        chain = [(kr, vr, hp) for kr, vr in tiles for hp in range(n_hp)]
        s_next = scores(chain[0][0], chain[0][2])
        for link, (_, values_ref, hp) in enumerate(chain):
            s = s_next
            if link + 1 < len(chain):
                s_next = scores(chain[link + 1][0], chain[link + 1][2])
            m_old = m_sc[hp]
            if step_d is not None:
                s = s + near_bias(hp, step_d)
                m_new = jnp.maximum(m_old, jnp.max(s, axis=0, keepdims=True))
                shift = m_new
            else:
                c = cfar_ref[hp]
                m_new = jnp.maximum(m_old, jnp.max(s, axis=0, keepdims=True) + c)
                shift = m_new - c
            alpha = jnp.exp2(m_old - m_new)
            pT = jnp.exp2(s - shift).astype(bf16)
            for hh in range(2):
                h = 2 * hp + hh
                cols = slice(hh * 2 * ta, (hh + 1) * 2 * ta)
                pv = _dot(values_ref[h * V_ROWS:(h + 1) * V_ROWS, :], pT[:, cols])
                acc_sc[h] = alpha[:, cols] * acc_sc[h] + pv
            m_sc[hp] = m_new

    @pl.when(kind == ATT_DIAG)
    def _():
        update(0, [(k_ref, v1T_ref)])

    @pl.when(kind == ATT_SUBDIAG)
    def _():
        update(1, [(k_ref, v1T_ref)])

    @pl.when(kind == ATT_FAR1)
    def _():
        update(None, [(k_ref, v1T_ref)])

    @pl.when(kind == ATT_FAR2)
    def _():
        update(None, [(k_ref, v1T_ref), (k2_ref, v2T_ref)])

    @pl.when(kind == ATT_DIAG)
    def _():
        lam = _lambda(lq1_ref, lk1_ref, lq2_ref, lk2_ref, lam_init)
        for h in range(H_A):
            a = acc_sc[h]
            a0 = a[:, :ta]
            a1 = a[:, ta:]
            o = a0[:DV] / a0[DV:DV + 1] - lam * (a1[:DV] / a1[DV:DV + 1])
            o = o * lax.rsqrt(jnp.mean(o * o, axis=0, keepdims=True) + RMS_EPS) * g_ref[...] * (1.0 - lam_init)
            oT_sc[h * DV:(h + 1) * DV, :] = o
        o_ref[...] = oT_sc[...].T.astype(bf16)


def _attn_prompt(qT, kb, v1T, rel_bias, lam_args, attn_g, B, L, lam_init):
    ta = min(ATT_TILE, L)
    assert L % ta == 0 and ta % ATT_SUB == 0 and ATT_SUB % CHUNK == 0
    nt = L // ta
    far = _far_bucket(ATT_SUB + 1)
    assert L <= 2 * ATT_SUB or far is not None, "key blocks two or more behind must share one bias bucket"
    far = far if far is not None else N_BUCKETS // 2 - 1
    steps = []
    idle_b = 0
    for q in range(nt):
        far_tiles = list(range(max(q - 1, 0)))
        for a in range(0, len(far_tiles) - 1, 2):
            steps.append((q, a, a + 1, ATT_FAR2))
            idle_b = a + 1
        if len(far_tiles) % 2:
            steps.append((q, far_tiles[-1], idle_b, ATT_FAR1))
        if q >= 1:
            steps.append((q, q - 1, idle_b, ATT_SUBDIAG))
        steps.append((q, q, idle_b, ATT_DIAG))
    qi_tab, ka_tab, kb_tab, kind_tab = (jnp.asarray([s[j] for s in steps], i32) for j in range(4))
    bias =_prompt_bias_tiles(rel_bias, ATT_SUB) * LOG2E
    cfar = jnp.repeat((rel_bias.astype(f32)[far] * LOG2E).reshape(H_A // 2, 2), 2 * ta, axis=1)[:, None, :]
    g_col = attn_g.astype(f32).reshape(DV, 1)
    T = B * L
    n_hp = H_A // 2
    full = lambda a: pl.BlockSpec(a.shape, lambda b, p, qt, ka, kb, kd: (0,) * a.ndim)
    grid_spec = pltpu.PrefetchScalarGridSpec(
        num_scalar_prefetch=4, grid=(B, len(steps)),
        in_specs=[
            pl.BlockSpec((D_QK_ALL, ta), lambda b, p, qt, ka, kb, kd: (0, b * nt + qt[p])),
            pl.BlockSpec((ta, D_QK_ALL), lambda b, p, qt, ka, kb, kd: (b * nt + ka[p], 0)),
            pl.BlockSpec((H_A * V_ROWS, ta), lambda b, p, qt, ka, kb, kd: (0, b * nt + ka[p])),
            pl.BlockSpec((ta, D_QK_ALL), lambda b, p, qt, ka, kb, kd: (b * nt + kb[p], 0)),
            pl.BlockSpec((H_A * V_ROWS, ta), lambda b, p, qt, ka, kb, kd: (0, b * nt + kb[p])),
            full(bias), full(cfar)] + [full(a) for a in lam_args] + [full(g_col)],
        out_specs=pl.BlockSpec((ta, D_ATT), lambda b, p, qt, ka, kb, kd: (b * nt + qt[p], 0)),
        scratch_shapes=[pltpu.VMEM((n_hp, LANES, 4 * ta), bf16), pltpu.VMEM((n_hp, 1, 4 * ta), f32),
                        pltpu.VMEM((H_A, V_ROWS, 2 * ta), f32), pltpu.VMEM((D_ATT, ta), f32)])
    return pl.pallas_call(
        functools.partial(_attn_prompt_kernel, lam_init=lam_init),
        grid_spec=grid_spec, out_shape=jax.ShapeDtypeStruct((T, D_ATT), bf16),
        compiler_params=_params(("parallel", "arbitrary")),
    )(qi_tab, ka_tab, kb_tab, kind_tab, qT, kb, v1T, kb, v1T, bias, cfar, *lam_args, g_col)


def _attn_sample_kernel(q_ref, kn_ref, vn_ref, kc_ref, vc_ref, bc_ref, bn_ref,
                        lq1_ref, lk1_ref, lq2_ref, lk2_ref, g2_ref, o_ref, *, lam_init):
    s_len = q_ref.shape[0]
    lane = lax.broadcasted_iota(i32, (s_len, LANES), 1)
    lam = _lambda(lq1_ref, lk1_ref, lq2_ref, lk2_ref, lam_init)
    for hp in range(H_A // 2):
        sl = slice(hp * LANES, (hp + 1) * LANES)
        qt = q_ref[:, sl]
        kct = kc_ref[0, sl, :].astype(bf16)
        vct = vc_ref[0, sl, :].astype(bf16)
        knt = kn_ref[:, sl].astype(bf16)
        vnt = vn_ref[:, sl].astype(bf16)
        outs = []
        for hh in range(2):
            h = 2 * hp + hh
            oc = []
            for c in range(2):
                gi = 2 * hh + c
                qm = jnp.where((lane >= gi * DQK) & (lane < (gi + 1) * DQK), qt, jnp.zeros_like(qt))
                s1 = _dot(qm, kct) + bc_ref[h]
                s2 = _dot_nt(qm, knt) + bn_ref[h]
                m = jnp.maximum(jnp.max(s1, axis=-1, keepdims=True), jnp.max(s2, axis=-1, keepdims=True))
                p1 = jnp.exp(s1 - m)
                p2 = jnp.exp(s2 - m)
                l = jnp.sum(p1, axis=-1, keepdims=True) + jnp.sum(p2, axis=-1, keepdims=True)
                pv = _dot_nt(p1.astype(bf16), vct) + _dot(p2.astype(bf16), vnt)
                oc.append(pv / l)
            o = oc[0] - lam * oc[1]
            in_head = (lane >= hh * DV) & (lane < (hh + 1) * DV)
            ms = jnp.sum(jnp.where(in_head, o * o, 0.0), axis=-1, keepdims=True) * (1.0 / DV)
            outs.append(o * lax.rsqrt(ms + RMS_EPS))
        ot = jnp.where(lane < DV, outs[0], outs[1]) * g2_ref[...] * (1.0 - lam_init)
        o_ref[:, sl] = ot.astype(bf16)


def _attn_sample(q, k_new, v_new, cache_k, cache_v, rel_bias, lam_args, attn_g, lam_init):
    nb, past = cache_k.shape[0], cache_k.shape[1]
    T = q.shape[0]
    s = T // nb
    kc = cache_k.transpose(0, 2, 3, 1).reshape(nb, D_QK_ALL, past)
    vc = cache_v.transpose(0, 2, 3, 1).reshape(nb, D_ATT, past)
    bias = _sample_bias(rel_bias, past, s)
    bc, bn = bias[:, :, :past], bias[:, :, past:]
    g2 = jnp.tile(attn_g.astype(f32), 2).reshape(1, LANES)
    full = lambda a: pl.BlockSpec(a.shape, lambda b: (0,) * a.ndim)
    row = lambda n: pl.BlockSpec((s, n), lambda b: (b, 0))
    return pl.pallas_call(
        functools.partial(_attn_sample_kernel, lam_init=lam_init),
        grid=(nb,),
        in_specs=[row(D_QK_ALL), row(D_QK_ALL), row(D_ATT),
                  pl.BlockSpec((1, D_QK_ALL, past), lambda b: (b, 0, 0)),
                  pl.BlockSpec((1, D_ATT, past), lambda b: (b, 0, 0)),
                  full(bc), full(bn)] + [full(a) for a in lam_args] + [full(g2)],
        out_specs=row(D_ATT), out_shape=jax.ShapeDtypeStruct((T, D_ATT), bf16),
        compiler_params=_params(("parallel",)),
    )(q, k_new, v_new, kc, vc, bc, bn, *lam_args, g2)


def _ssd_kernel(z_ref, xbc_ref, dt_ref, dtT_ref, h0_ref, c0_ref, cw_ref, cb_ref, dtb_ref, dtbT_ref,
                alog_ref, alogT_ref, dsk_ref, g_ref, y_ref, hout_ref, cout_ref, xpad_sc, h_sc):
    q = xbc_ref.shape[0]
    c = pl.program_id(1)
    gw = HEADS_PER_GROUP * SSM_HEADDIM

    @pl.when(c == 0)
    def _():
        xpad_sc[0:SUBLANES] = c0_ref[0]
        h_sc[...] = h0_ref[0]

    xpad_sc[SUBLANES:SUBLANES + q] = xbc_ref[...]
    first = SUBLANES - (CONV_W - 1)
    conv = cb_ref[...]
    for i in range(CONV_W):
        conv = conv + xpad_sc[first + i:first + i + q] * cw_ref[i:i + 1, :]
    tail = xpad_sc[q:q + SUBLANES]
    xpad_sc[0:SUBLANES] = tail
    cout_ref[0] = tail
    act = conv * _sigmoid(conv)
    xs = act[:, :D_SSM]

    lane = lax.broadcasted_iota(i32, (1, LANES), 1)
    a_row = jnp.where(lane < H_S, -jnp.exp(alog_ref[...]) * LOG2E, 0.0)
    dt = _softplus(dt_ref[...] + dtb_ref[...])
    ii = lax.broadcasted_iota(i32, (q, q), 0)
    jj = lax.broadcasted_iota(i32, (q, q), 1)
    causal = jj <= ii
    acs = _dot_f32_rhs(causal.astype(bf16), dt * a_row)
    sub = lax.broadcasted_iota(i32, (BF16_ROWS, 1), 0)
    a_col = jnp.where(sub < H_S, -jnp.exp(alogT_ref[...]) * LOG2E, 0.0)
    dtT = _softplus(dtT_ref[0] + dtbT_ref[...])
    acsT = _dot_f32_lhs(dtT * a_col, (ii <= jj).astype(bf16))

    er = lax.broadcasted_iota(i32, (LANES, D_SSM), 0)
    ec = lax.broadcasted_iota(i32, (LANES, D_SSM), 1)
    expand = (ec // SSM_HEADDIM == er).astype(bf16)
    dt_x = _dot_f32_lhs(dt, expand)
    acs_x = _dot_f32_lhs(acs, expand)
    e_acs = jnp.exp2(acs_x)
    acs_last = acs_x[q - 1:q, :]
    decay = jnp.exp2(acs_last - acs_x) * dt_x
    dtx = xs * dt_x
    xd = xs * decay
    glane = lax.broadcasted_iota(i32, (1, gw), 1)

    ys = []
    for g in range(SSM_GROUPS):
        gs = slice(g * gw, (g + 1) * gw)
        bg = act[:, D_SSM + g * D_STATE:D_SSM + (g + 1) * D_STATE].astype(bf16)
        cg = act[:, D_SSM + (SSM_GROUPS + g) * D_STATE:D_SSM + (SSM_GROUPS + g + 1) * D_STATE].astype(bf16)
        cb = _dot_nt(cg, bg)
        h_old = h_sc[g]
        yg = _dot(cg, h_old.astype(bf16)) * e_acs[:, gs]
        dtx_g = dtx[:, gs]
        for e4 in range(HEADS_PER_GROUP):
            e = g * HEADS_PER_GROUP + e4
            seg = acs[:, e:e + 1] - acsT[e:e + 1, :]
            lmat = jnp.exp2(jnp.where(causal, seg, -jnp.inf))
            rhs = jnp.where(glane // SSM_HEADDIM == e4, dtx_g, 0.0).astype(bf16)
            yg = yg + _dot((cb * lmat).astype(bf16), rhs)
        h_sc[g] = h_old * jnp.exp2(acs_last[:, gs]) + _dot_tn(bg, xd[:, gs].astype(bf16))
        ys.append(yg)
    y = jnp.concatenate(ys, axis=1) + dsk_ref[...] * xs
    zz = z_ref[...]
    y = y * (zz * _sigmoid(zz))
    for g in range(SSM_GROUPS):
        gs = slice(g * gw, (g + 1) * gw)
        yg = y[:, gs]
        r = lax.rsqrt(jnp.mean(yg * yg, axis=-1, keepdims=True) + RMS_EPS)
        y_ref[:, gs] = (yg * r * g_ref[:, gs]).astype(bf16)
    hout_ref[0] = h_sc[...]


def _ssd(z, xbc, dt, dtT, h0, conv0, pw, B, L):
    q = min(SSD_TILE, L)
    nc = L // q
    gw = HEADS_PER_GROUP * SSM_HEADDIM
    dtT3 = dtT.reshape(BF16_ROWS, B, L).transpose(1, 0, 2)
    hT0 = h0.astype(f32).reshape(B, SSM_GROUPS, gw, D_STATE).transpose(0, 1, 3, 2)
    c0 = jnp.pad(conv0.astype(f32), ((0, 0), (SUBLANES - (CONV_W - 1), 0), (0, 0)))
    full = lambda a: pl.BlockSpec(a.shape, lambda b, c: (0,) * a.ndim)
    row = lambda n: pl.BlockSpec((q, n), lambda b, c: (b * nc + c, 0))
    params = [pw['conv_w'], pw['conv_b'], pw['dt_bias'], pw['dt_biasT'], pw['a_log'], pw['a_logT'],
              pw['d_skip'], pw['ssm_g']]
    y, hT, ctail = pl.pallas_call(
        _ssd_kernel, grid=(B, nc),
        in_specs=[row(D_SSM), row(CONV_DIM), row(LANES),
                  pl.BlockSpec((1, BF16_ROWS, q), lambda b, c: (b, 0, c)),
                  pl.BlockSpec((1, SSM_GROUPS, D_STATE, gw), lambda b, c: (b, 0, 0, 0)),
                  pl.BlockSpec((1, SUBLANES, CONV_DIM), lambda b, c: (b, 0, 0))] + [full(a) for a in params],
        out_specs=[row(D_SSM),
                   pl.BlockSpec((1, SSM_GROUPS, D_STATE, gw), lambda b, c: (b, 0, 0, 0)),
                   pl.BlockSpec((1, SUBLANES, CONV_DIM), lambda b, c: (b, 0, 0))],
        out_shape=[jax.ShapeDtypeStruct((B * L, D_SSM), bf16),
                   jax.ShapeDtypeStruct((B, SSM_GROUPS, D_STATE, gw), f32),
                   jax.ShapeDtypeStruct((B, SUBLANES, CONV_DIM), f32)],
        scratch_shapes=[pltpu.VMEM((q + SUBLANES, CONV_DIM), f32), pltpu.VMEM((SSM_GROUPS, D_STATE, gw), f32)],
        compiler_params=_params(("parallel", "arbitrary")),
    )(z, xbc, dt, dtT3, hT0, c0, *params)
    h_new = hT.transpose(0, 1, 3, 2).reshape(B, H_S, SSM_HEADDIM, D_STATE)
    return y, h_new, ctail[:, SUBLANES - (CONV_W - 1):, :]


def _pack_bf16_pairs(x):
    half = x.shape[1] // 2
    bits = lambda a: lax.bitcast_convert_type(a.astype(bf16).astype(f32), jnp.uint32)
    return (bits(x[:, half:]) & jnp.uint32(0xFFFF0000)) | (bits(x[:, :half]) >> 16)


def _unpack_bf16_pairs(word):
    lo = lax.bitcast_convert_type(word << 16, f32)
    hi = lax.bitcast_convert_type(word & jnp.uint32(0xFFFF0000), f32)
    return jnp.concatenate([lo, hi], axis=1)


def _mix_router_kernel(att_ref, ssm_ref, x_ref, woa_ref, wos_ref, g_ref, b_ref, wrh_ref, wrhl_ref, br_ref,
                       cnt0_ref, *rest):
    x1_ref, x1p_ref, te_ref, gate_ref, cnt_ref, run_sc, earlier_sc = rest[-7:]

    @pl.when(pl.program_id(0) == 0)
    def _():
        run_sc[...] = cnt0_ref[...]
        ii = lax.broadcasted_iota(i32, earlier_sc.shape, 0)
        jj = lax.broadcasted_iota(i32, earlier_sc.shape, 1)
        earlier_sc[...] = (jj < ii).astype(bf16)

    mix = _dot(att_ref[...], woa_ref[...]) + _dot(ssm_ref[...], wos_ref[...])
    x1 = _layer_norm(DEEPNORM_ALPHA * x_ref[...] + mix, g_ref[...], b_ref[...])
    x1_ref[...] = x1
    x1p_ref[...] = _pack_bf16_pairs(x1)
    hi = x1.astype(bf16)
    lo = (x1 - hi.astype(f32)).astype(bf16)
    both = _dot(hi, wrhl_ref[...])
    logits = both[:, :LANES] + both[:, LANES:] + _dot(lo, wrh_ref[...]) + br_ref[...]
    lane =lax.broadcasted_iota(i32, logits.shape, 1)
    vals, idxs = [], []
    for _ in range(TOP_K):
        m = jnp.max(logits, axis=-1, keepdims=True)
        idx = jnp.min(jnp.where(logits == m, lane, LANES), axis=-1, keepdims=True)
        vals.append(m)
        idxs.append(idx)
        logits = jnp.where(lane == idx, -jnp.inf, logits)
    es = [jnp.exp(v - vals[0]) for v in vals]
    tot = es[0]
    for e in es[1:]:
        tot = tot + e
    chosen = jnp.zeros(logits.shape, f32)
    for k in range(TOP_K):
        chosen = chosen + (lane == idxs[k]).astype(f32)
    before = _dot(earlier_sc[...], chosen.astype(bf16)) + run_sc[...]
    te = jnp.zeros(logits.shape, i32)
    gate = jnp.zeros(logits.shape, f32)
    for k in range(TOP_K):
        rank = jnp.sum(jnp.where(lane == idxs[k], before, 0.0), axis=-1, keepdims=True).astype(i32)
        te = jnp.where(lane == k, idxs[k], te)
        te = jnp.where(lane == TOP_K + k, rank, te)
        gate = jnp.where(lane == k, es[k] / tot, gate)
    te_ref[...] = te
    gate_ref[...] = gate
    run_sc[...] = run_sc[...] + jnp.sum(chosen, axis=0, keepdims=True)
    cnt_ref[...] = run_sc[...]


def _mix_router(att, ssm, x2d, w, counts0, packed, row0, rows_total):
    T, D = x2d.shape
    tl = min(ROW_TILE, T)
    assert row0 % tl == 0
    row = lambda n: pl.BlockSpec((tl, n), lambda i: (i, 0))
    full = lambda a: pl.BlockSpec(a.shape, lambda i: (0,) * a.ndim)
    ws = [w['wo_att'], w['wo_ssm'], w['ln1_g'], w['ln1_b'], w['wr_hi'], w['wr_hi_lo'], w['b_router'], counts0]
    ins = [att, ssm, x2d] + ws + [packed]
    in_specs = [row(D_ATT), row(D_SSM), row(D)] + [full(a) for a in ws] + [pl.BlockSpec(memory_space=pl.ANY)]
    aliases = {len(ins) - 1: 1}
    return pl.pallas_call(
        _mix_router_kernel, grid=(T // tl,), in_specs=in_specs,
        out_specs=[row(D), pl.BlockSpec((tl, D // 2), lambda i: (row0 // tl + i, 0)), row(LANES), row(LANES),
                   pl.BlockSpec((1, LANES), lambda i: (0, 0))],
        out_shape=[jax.ShapeDtypeStruct((T, D), f32), jax.ShapeDtypeStruct((rows_total, D // 2), jnp.uint32),
                   jax.ShapeDtypeStruct((T, LANES), i32),
                   jax.ShapeDtypeStruct((T, LANES), f32), jax.ShapeDtypeStruct((1, LANES), f32)],
        scratch_shapes=[pltpu.VMEM((1, LANES), f32), pltpu.VMEM((tl, tl), bf16)], input_output_aliases=aliases,
        compiler_params=_params(("arbitrary",)),
    )(*ins)


def _route(te, counts_f):
    T = te.shape[0]
    n_assign = T * TOP_K
    counts = counts_f[0, :N_EXPERTS].astype(i32)
    padded = (counts + MOE_BLOCK - 1) // MOE_BLOCK * MOE_BLOCK
    pad_end = jnp.cumsum(padded)
    pad_start = pad_end - padded
    top_e, rank = te[:, :TOP_K], te[:, TOP_K:2 * TOP_K]
    experts = jnp.arange(N_EXPERTS, dtype=i32)
    slot = rank + jnp.sum(jnp.where(top_e[..., None] == experts, pad_start, 0), axis=-1)
    n_blocks = -(-n_assign // MOE_BLOCK) + N_EXPERTS
    n_blocks += n_blocks % FFN_BLOCKS_PER_STEP
    block_start = jnp.arange(n_blocks, dtype=i32) * MOE_BLOCK
    block_e = jnp.minimum(jnp.sum(block_start[:, None] >= pad_end[None, :], axis=-1), N_EXPERTS - 1).astype(i32)
    n_used = (pad_end[-1] // MOE_BLOCK).astype(i32).reshape(1)
    row_end = jnp.sum(jnp.where(block_e[:, None] == experts, pad_start + counts, 0), axis=-1)
    rows_valid = jnp.clip(row_end - block_start, 0, MOE_BLOCK).astype(i32)
    return block_e, n_used, rows_valid, slot.astype(i32), n_blocks


SC_INDEX_WINDOW = 128
SC_ROWS = 64


def _sc_mesh():
    return plsc.VectorSubcoreMesh(core_axis_name="c", subcore_axis_name="s")


def _sc_move_rows(src, src_idx, dst_idx_list, n_out):
    M = src_idx.shape[0]
    D = src.shape[1]
    idx = [a.reshape(1, M) for a in [src_idx] + list(dst_idx_list)]

    @functools.partial(pl.kernel, out_type=jax.ShapeDtypeStruct((n_out, D), src.dtype), mesh=_sc_mesh(),
                       scratch_types=[pltpu.VMEM((2, SC_ROWS, D), src.dtype), pltpu.SemaphoreType.DMA((2,))])
    def move(s_hbm, *rest):
        i_hbm, o_hbm, buf, sem = rest[:-3], rest[-3], rest[-2], rest[-1]
        n_parts = SC_INDEX_WINDOW // SC_ROWS

        def body(si_vmem, *di_vmem):
            def fetch(j):
                part = pl.ds(j * SC_ROWS, SC_ROWS)
                return pltpu.async_copy(s_hbm.at[si_vmem.at[0, part]], buf.at[j % 2], sem.at[j % 2])

            pending = fetch(0)
            for j in range(n_parts):
                nxt = fetch(j + 1) if j + 1 < n_parts else None
                pending.wait()
                part = pl.ds(j * SC_ROWS, SC_ROWS)
                for dv in di_vmem:
                    pltpu.sync_copy(buf.at[j % 2], o_hbm.at[dv.at[0, part]])
                pending = nxt

        pltpu.emit_pipeline(
            body, grid=(M // SC_INDEX_WINDOW,),
            in_specs=[pl.BlockSpec((1, SC_INDEX_WINDOW), lambda i: (0, i))] * len(idx),
            out_specs=[], core_axis_name=("c", "s"), dimension_semantics=(pltpu.PARALLEL,),
        )(*i_hbm)

    return move(src, *idx)


def _sc_scatter_rows(x, idx_k, n_rows):
    return _sc_move_rows(x, jnp.arange(x.shape[0], dtype=i32), idx_k, n_rows)


def _sc_gather_rows(src, idx):
    M = idx.shape[0]
    return _sc_move_rows(src, idx, [jnp.arange(M, dtype=i32)], M)


def _deinterleave_kernel(w_ref, g_ref, u_ref):
    tn = 2 * LANES
    r = lax.broadcasted_iota(i32, (2 * tn, tn), 0)
    c = lax.broadcasted_iota(i32, (2 * tn, tn), 1)
    pick_g = (r == 2 * c).astype(bf16)
    pick_u = (r == 2 * c + 1).astype(bf16)
    for j in range(g_ref.shape[-1] // tn):
        wb = w_ref[0, :, j * 2 * tn:(j + 1) * 2 * tn].astype(bf16)
        g_ref[0, :, j * tn:(j + 1) * tn] = _dot(wb, pick_g).astype(bf16)
        u_ref[0, :, j * tn:(j + 1) * tn] = _dot(wb, pick_u).astype(bf16)


def _deinterleave(w_gu):
    E, D, F2 = w_gu.shape
    out = jax.ShapeDtypeStruct((E, D, F2 // 2), bf16)
    return pl.pallas_call(
        _deinterleave_kernel, grid=(E,),
        in_specs=[pl.BlockSpec((1, D, F2), lambda e: (e, 0, 0))],
        out_specs=[pl.BlockSpec((1, D, F2 // 2), lambda e: (e, 0, 0))] * 2,
        out_shape=[out, out],
        compiler_params=_params(("parallel",)),
    )(w_gu)


def _ffn_kernel(be_ref, nu_ref, rv_ref, xs_ref, *refs):
    del be_ref
    y_ref = refs[-1]
    wsets = [refs[6 * j:6 * j + 6] for j in range(FFN_BLOCKS_PER_STEP)]
    first = pl.program_id(0) * FFN_BLOCKS_PER_STEP
    n_live = jnp.clip(nu_ref[0] - first, 0, FFN_BLOCKS_PER_STEP)
    row_id = lax.broadcasted_iota(i32, (MOE_BLOCK, 1), 0)

    def block(j):
        wg_ref, wu_ref, bg_ref, bu_ref, wd_ref, bd_ref = wsets[j]
        rows = pl.ds(j * MOE_BLOCK, MOE_BLOCK)
        words = jnp.where(row_id < rv_ref[first + j], xs_ref[rows, :], jnp.uint32(0))
        xb = _unpack_bf16_pairs(words).astype(bf16)
        g = _dot(xb, wg_ref[0]) + bg_ref[0]
        u = _dot(xb, wu_ref[0]) + bu_ref[0]
        g = jnp.minimum(g, SWIGLU_LIMIT)
        u = jnp.clip(u, -SWIGLU_LIMIT, SWIGLU_LIMIT)
        act = (u + 1.0) * g * _sigmoid(SWIGLU_ALPHA * g)
        y_ref[rows, :] = _pack_bf16_pairs(_dot(act.astype(bf16), wd_ref[0].astype(bf16)) + bd_ref[0])

    for live in range(FFN_BLOCKS_PER_STEP + 1):
        @pl.when(n_live == live)
        def _(live=live):
            for j in range(live):
                block(j)
            for j in range(live, FFN_BLOCKS_PER_STEP):
                y_ref[pl.ds(j * MOE_BLOCK, MOE_BLOCK), :] = jnp.zeros((MOE_BLOCK, y_ref.shape[1]), jnp.uint32)


def _expert_ffn(xs, block_e, n_used, rows_valid, w):
    n_rows, half = xs.shape
    D = 2 * half
    nps = FFN_BLOCKS_PER_STEP
    n_steps = n_rows // (MOE_BLOCK * nps)
    F = w['w_g'].shape[2]
    wspecs, wargs = [], []
    for j in range(nps):
        pick = lambda i, be, nu, rv, j=j: (be[i * nps + j], 0, 0)
        wspecs += [pl.BlockSpec((1, D, F), pick), pl.BlockSpec((1, D, F), pick), pl.BlockSpec((1, 1, F), pick),
                   pl.BlockSpec((1, 1, F), pick), pl.BlockSpec((1, F, D), pick), pl.BlockSpec((1, 1, D), pick)]
        wargs += [w['w_g'], w['w_u'], w['b_g'], w['b_u'], w['w_d'], w['b_d']]
    grid_spec = pltpu.PrefetchScalarGridSpec(
        num_scalar_prefetch=3, grid=(n_steps,),
        in_specs=[pl.BlockSpec((nps * MOE_BLOCK, half),
                               lambda i, be, nu, rv: (jnp.minimum(i, (nu[0] - 1) // nps), 0))] + wspecs,
        out_specs=pl.BlockSpec((nps * MOE_BLOCK, half), lambda i, be, nu, rv: (i, 0)))
    return pl.pallas_call(
        _ffn_kernel, grid_spec=grid_spec,
        out_shape=jax.ShapeDtypeStruct((n_rows, half), jnp.uint32),
        compiler_params=_params(("arbitrary",)),
    )(block_e, n_used, rows_valid, xs, *wargs)


def _combine_dense_kernel(rows_ref, x1_ref, gate_ref, g_ref, b_ref, *rest):
    y_ref = rest[-1]
    gate = gate_ref[...]
    ff = gate[:, 0:1] * _unpack_bf16_pairs(rows_ref[0])
    for k in range(1, TOP_K):
        ff = ff + gate[:, k:k + 1] * _unpack_bf16_pairs(rows_ref[k])
    y_ref[...] = _layer_norm(DEEPNORM_ALPHA * x1_ref[...] + ff, g_ref[...], b_ref[...])


def _combine_dense(rows, x1, gate, ln_g, ln_b, chunk, y_partial):
    T, D = x1.shape
    tc = rows.shape[1]
    tl = min(ROW_TILE, tc)
    first = chunk * (tc // tl)
    row = lambda n: pl.BlockSpec((tl, n), lambda i: (first + i, 0))
    full = lambda a: pl.BlockSpec(a.shape, lambda i: (0,) * a.ndim)
    ins = [rows, x1, gate, ln_g, ln_b]
    in_specs = [pl.BlockSpec((TOP_K, tl, D // 2), lambda i: (0, i, 0)), row(D), row(LANES), full(ln_g), full(ln_b)]
    aliases = {}
    if y_partial is not None:
        ins.append(y_partial)
        in_specs.append(pl.BlockSpec(memory_space=pl.ANY))
        aliases = {len(ins) - 1: 0}
    return pl.pallas_call(
        _combine_dense_kernel, grid=(tc // tl,), in_specs=in_specs,
        out_specs=row(D), out_shape=jax.ShapeDtypeStruct((T, D), f32), input_output_aliases=aliases,
        compiler_params=_params(("parallel",)),
    )(*ins)


def _prep_weights(l, w_in, conv_w, conv_b, dt_bias, a_log, d_skip, ssm_norm_g, w_o, ln1_g, ln1_b,
                  w_router, b_router, w_gate_up, b_gate_up, w_down, b_down, ln2_g, ln2_b):
    wi = w_in[l]
    c0, c1, c2, c3 = D_QK_ALL, 2 * D_QK_ALL, 2 * D_QK_ALL + D_ATT, 2 * D_QK_ALL + D_ATT + D_SSM
    c4 = c3 + CONV_DIM
    wdt = wi[:, c4:c4 + H_S]
    pad_lane = lambda v, fill=0.0: jnp.pad(v.astype(f32).reshape(1, -1), ((0, 0), (0, LANES - v.shape[-1])),
                                            constant_values=fill)
    pad_col = lambda v: jnp.pad(v.astype(f32).reshape(-1, 1), ((0, BF16_ROWS - v.shape[-1]), (0, 0)))
    wr = jnp.pad(w_router[l].astype(f32), ((0, 0), (0, LANES - N_EXPERTS)))
    wr_hi = wr.astype(bf16)
    wgu = w_gate_up[l]
    return {
        'wq': wi[:, :c0].astype(bf16), 'wqT': wi[:, :c0].T.astype(bf16),
        'wk': wi[:, c0:c1].astype(bf16), 'wkT': wi[:, c0:c1].T.astype(bf16), 'wv': wi[:, c1:c2].astype(bf16), 'wvT': wi[:, c1:c2].T.astype(bf16),
        'wz': wi[:, c2:c3].astype(bf16), 'wx': wi[:, c3:c4].astype(bf16),
        'wdt': jnp.pad(wdt, ((0, 0), (0, LANES - H_S))).astype(bf16),
        'wdtT': jnp.pad(wdt.T, ((0, BF16_ROWS - H_S), (0, 0))).astype(bf16),
        'conv_w': conv_w[l].astype(f32), 'conv_b': conv_b[l].astype(f32).reshape(1, -1),
        'dt_bias': pad_lane(dt_bias[l]), 'dt_biasT': pad_col(dt_bias[l]),
        'a_log': pad_lane(a_log[l]), 'a_logT': pad_col(a_log[l]),
        'd_skip': jnp.repeat(d_skip[l].astype(f32), SSM_HEADDIM).reshape(1, -1),
        'ssm_g': ssm_norm_g[l].astype(f32).reshape(1, -1),
        'wo_att': w_o[l][:D_ATT].astype(bf16), 'wo_ssm': w_o[l][D_ATT:].astype(bf16),
        'ln1_g': ln1_g[l].astype(f32).reshape(1, -1), 'ln1_b': ln1_b[l].astype(f32).reshape(1, -1),
        'wr_hi': wr_hi, 'wr_hi_lo': jnp.concatenate([wr_hi, (wr - wr_hi.astype(f32)).astype(bf16)], axis=1),
        'b_router': pad_lane(b_router[l], NEG_INF),
        'w_gu': wgu,
        'b_g': b_gate_up[l][:, None, 0::2].astype(f32), 'b_u': b_gate_up[l][:, None, 1::2].astype(f32),
        'w_d': w_down[l], 'b_d': b_down[l][:, None, :].astype(f32),
        'ln2_g': ln2_g[l].astype(f32).reshape(1, -1), 'ln2_b': ln2_b[l].astype(f32).reshape(1, -1),
    }


def _mixers(x, l, w, rel_bias, lam_args, attn_g, k_past, v_past, h0, conv0):
    B, L, D = x.shape
    x2d = x.reshape(B * L, D)
    lam_init = 0.8 - 0.6 * math.exp(-0.3 * l)
    prompt = k_past is None
    proj = _in_proj(x2d, w, prompt, B)
    z, xbc, dt, dtT = proj[:4]
    if prompt:
        kT, vT, kb, qT, v1T = proj[4:]
        att = _attn_prompt(qT, kb, v1T, rel_bias, lam_args, attn_g, B, L, lam_init)
        k_rows = kT.reshape(B, H_A, 2 * DQK, L).transpose(0, 3, 1, 2)
        v_rows = vT.reshape(B, H_A, DV, L).transpose(0, 3, 1, 2)
    else:
        k, v, q = proj[4:]
        att = _attn_sample(q, k, v, k_past, v_past, rel_bias, lam_args, attn_g, lam_init)
        k_rows = k.reshape(B, L, H_A, 2 * DQK)
        v_rows = v.reshape(B, L, H_A, DV)
    ssm, h_new, conv_new = _ssd(z, xbc, dt, dtT, h0, conv0, w, B, L)
    return (att, ssm, x2d), (k_rows, v_rows, h_new, conv_new)


def _moe_layer(streams, w):
    rows_total = sum(x2d.shape[0] for _, _, x2d in streams)
    counts = jnp.zeros((1, LANES), f32)
    packed = jnp.zeros((rows_total, streams[0][2].shape[1] // 2), jnp.uint32)
    row0, routed = 0, []
    for att, ssm, x2d in streams:
        x1, packed, te, gate, counts = _mix_router(att, ssm, x2d, w, counts, packed, row0, rows_total)
        routed.append((x1, gate, row0))
        row0 += x2d.shape[0]
        te_all = te[:, :2 * TOP_K] if len(routed) == 1 else jnp.concatenate([te_all, te[:, :2 * TOP_K]])
    block_e, n_used, rows_valid, slot, n_blocks = _route(te_all, counts)
    slot_k = slot.T
    xs = _sc_scatter_rows(packed, [slot_k[k] for k in range(TOP_K)], n_blocks * MOE_BLOCK)
    ys = _expert_ffn(xs, block_e, n_used, rows_valid, w)
    outs = []
    for x1, gate, r0 in routed:
        T, D = x1.shape
        n_groups = COMBINE_GROUPS if T % (COMBINE_GROUPS * ROW_TILE) == 0 else 1
        tc = T // n_groups
        y = None
        for c in range(n_groups):
            idx = slot_k[:, r0 + c * tc:r0 + (c + 1) * tc].reshape(-1)
            rows = _sc_gather_rows(ys, idx).reshape(TOP_K, tc, D // 2)
            y = _combine_dense(rows, x1, gate, w['ln2_g'], w['ln2_b'], c, y)
        outs.append(y)
    return outs


def kernel(x_prompt, x_sample, cache_k, cache_v, state_ssm, state_conv, rel_bias, w_in, lambda_q1, lambda_k1, lambda_q2, lambda_k2, attn_norm_g, conv_w, conv_b, dt_bias, a_log, d_skip, ssm_norm_g, w_o, ln1_g, ln1_b, w_router, b_router, w_gate_up, b_gate_up, w_down, b_down, ln2_g, ln2_b):
    yp, ys = x_prompt, x_sample
    bp = x_prompt.shape[0]
    depth = w_in.shape[0]
    outs = [[] for _ in range(8)]
    for l in range(depth):
        w = _prep_weights(l, w_in, conv_w, conv_b, dt_bias, a_log, d_skip, ssm_norm_g, w_o, ln1_g, ln1_b,
                          w_router, b_router, w_gate_up, b_gate_up, w_down, b_down, ln2_g, ln2_b)
        w['w_g'], w['w_u'] = _deinterleave(w.pop('w_gu'))
        lam_args = [a[l].astype(f32).reshape(1, -1) for a in (lambda_q1, lambda_k1, lambda_q2, lambda_k2)]
        h0 = jnp.zeros((bp, H_S, SSM_HEADDIM, D_STATE), f32)
        c0 = jnp.zeros((bp, CONV_W - 1, CONV_DIM), f32)
        mix_p, state_p = _mixers(yp, l, w, rel_bias, lam_args, attn_norm_g[l], None, None, h0, c0)
        mix_s, state_s = _mixers(ys, l, w, rel_bias, lam_args, attn_norm_g[l], cache_k[l], cache_v[l],
                                 state_ssm[l], state_conv[l])
        y2p, y2s = _moe_layer([mix_p, mix_s], w)
        yp, ys = y2p.reshape(yp.shape), y2s.reshape(ys.shape)
        for lst, a in zip(outs, state_p + state_s):
            lst.append(a)
    return (yp, ys) + tuple(jnp.stack(o) for o in outs)
```
